```python
import jax, jax.numpy as jnp
from jax import lax
import numpy as np

D_MODEL = 2048
BATCH = 8
SEQ = 4096
DEPTH = 4

LRU_WIDTH = D_MODEL // 2
LRU_HEADS = 8
LRU_HEAD_DIM = LRU_WIDTH // LRU_HEADS
LRU_CONV = 4
LRU_C = 8.0
HGRN_HEADS = 8
HGRN_DK = (D_MODEL // 2) // HGRN_HEADS
HGRN_DV = (D_MODEL // 2) // HGRN_HEADS
HGRN_WIDTH = HGRN_HEADS * HGRN_DK
HGRN_CHUNK = 64
SC_WIDTH = D_MODEL // 2
SC_CONV = 3
CF_WIDTH = D_MODEL // 2
CF_CONV = 31
FFN_HIDDEN = -(-8 * D_MODEL // (3 * 256)) * 256
N_EVEN = (DEPTH + 1) // 2
N_ODD = DEPTH // 2
EV_IN = 2 * LRU_WIDTH + 4 * HGRN_WIDTH
EV_OUT = LRU_WIDTH + HGRN_HEADS * HGRN_DV
OD_IN = 3 * SC_WIDTH + 2 * CF_WIDTH
OD_OUT = SC_WIDTH + CF_WIDTH
EPS = 1e-6
F_FLOOR = 1e-30

kernel_name = "hybrid_rglru_hgrn2_shortconv_conformer_trunk"


def rmsnorm(x, g):
    xf = x.astype(jnp.float32)
    y = xf * lax.rsqrt(jnp.mean(xf * xf, axis=-1, keepdims=True) + EPS)
    return (y * g.astype(jnp.float32)).astype(x.dtype)


def layernorm(x, g, b):
    xf = x.astype(jnp.float32)
    mu = jnp.mean(xf, axis=-1, keepdims=True)
    xc = xf - mu
    y = xc * lax.rsqrt(jnp.mean(xc * xc, axis=-1, keepdims=True) + EPS)
    return (y * g.astype(jnp.float32) + b.astype(jnp.float32)).astype(x.dtype)


def split_cols(t, sizes):
    outs, off = [], 0
    for s in sizes:
        outs.append(t[..., off:off + s])
        off += s
    return outs


def causal_dwconv(x, w, b=None):
    K, C = w.shape
    y = lax.conv_general_dilated(
        x, w[:, None, :].astype(x.dtype), window_strides=(1,), padding=[(K - 1, 0)],
        dimension_numbers=("NWC", "WIO", "NWC"), feature_group_count=C)
    if b is not None:
        y = y + b.astype(x.dtype)
    return y


def rg_lru(x, wa, ba, wx, bx, lam):
    Bsz, S, W = x.shape
    xh = x.reshape(Bsz, S, LRU_HEADS, LRU_HEAD_DIM)
    r = jax.nn.sigmoid((jnp.einsum('bshd,hde->bshe', xh, wa) + ba).astype(jnp.float32)).reshape(Bsz, S, W)
    i = jax.nn.sigmoid((jnp.einsum('bshd,hde->bshe', xh, wx) + bx).astype(jnp.float32)).reshape(Bsz, S, W)
    log_a = -LRU_C * r * jax.nn.softplus(-lam.astype(jnp.float32))
    a = jnp.exp(log_a)
    mult = jnp.sqrt(jnp.maximum(-jnp.expm1(2.0 * log_a), 0.0))
    mult = mult.at[:, 0].set(1.0)
    u = mult * i * x.astype(jnp.float32)

    def combine(left, right):
        a_l, b_l = left
        a_r, b_r = right
        return a_l * a_r, a_r * b_l + b_r

    _, h = lax.associative_scan(combine, (a, u), axis=1)
    return h


def hgrn2_chunked(q, k, v, log_f):
    Bsz, S, H, DK = q.shape
    DV = v.shape[-1]
    C = HGRN_CHUNK
    NC = S // C

    def to_chunks(t):
        return t.reshape(Bsz, NC, C, H, t.shape[-1]).transpose(1, 0, 3, 2, 4)

    qc, kc, vc, gc = to_chunks(q), to_chunks(k), to_chunks(v), to_chunks(log_f)
    bc = jnp.cumsum(gc, axis=3)
    causal = jnp.tril(jnp.ones((C, C), dtype=bool))[:, :, None]

    def step(state, inp):
        q_, k_, v_, b_ = inp
        inter = jnp.einsum('bhtk,bhkv->bhtv', q_ * jnp.exp(b_), state)
        diff = b_[:, :, :, None, :] - b_[:, :, None, :, :]
        decay = jnp.where(causal, jnp.exp(jnp.where(causal, diff, 0.0)), 0.0)
        scores = jnp.einsum('bhtk,bhtsk,bhsk->bhts', q_, decay, k_)
        intra = jnp.einsum('bhts,bhsv->bhtv', scores, v_)
        b_last = b_[:, :, -1:, :]
        new_state = (jnp.exp(b_last[:, :, 0, :])[..., None] * state
                     + jnp.einsum('bhsk,bhsv->bhkv', k_ * jnp.exp(b_last - b_), v_))
        return new_state, inter + intra

    state0 = jnp.zeros((Bsz, H, DK, DV), jnp.float32)
    _, o = lax.scan(step, state0, (qc, kc, vc, bc))
    return o.transpose(1, 0, 3, 2, 4).reshape(Bsz, S, H, DV)


def even_mixer(h, w_in, b_in, conv_w, conv_b, wa, ba, wx, bx, lam, lb, norm_g, w_out):
    Bsz, S, _ = h.shape
    proj = h @ w_in + b_in
    x_a, gate_a, q, f, i, g = split_cols(
        proj, [LRU_WIDTH, LRU_WIDTH, HGRN_WIDTH, HGRN_WIDTH, HGRN_HEADS * HGRN_DV, HGRN_HEADS * HGRN_DV])
    x_a = causal_dwconv(x_a, conv_w, conv_b)
    h_a = rg_lru(x_a, wa, ba, wx, bx, lam)
    y_a = (h_a * jax.nn.gelu(gate_a.astype(jnp.float32))).astype(h.dtype)
    z = f.astype(jnp.float32).reshape(Bsz, S, HGRN_HEADS, HGRN_DK)
    lb = lb.astype(jnp.float32).reshape(HGRN_HEADS, HGRN_DK)
    sig = jax.nn.sigmoid(z)
    f_gate = lb + (1.0 - lb) * sig
    log_f = jnp.log(jnp.maximum(f_gate, F_FLOOR))
    k = (1.0 - lb) * (1.0 - sig)
    qf = jax.nn.silu(q.astype(jnp.float32)).reshape(Bsz, S, HGRN_HEADS, HGRN_DK)
    vf = i.astype(jnp.float32).reshape(Bsz, S, HGRN_HEADS, HGRN_DV)
    o = hgrn2_chunked(qf, k, vf, log_f)
    o = o * lax.rsqrt(jnp.mean(o * o, axis=-1, keepdims=True) + EPS)
    o = o.reshape(Bsz, S, HGRN_HEADS * HGRN_DV) * norm_g.astype(jnp.float32)
    y_b = (o * jax.nn.silu(g.astype(jnp.float32))).astype(h.dtype)
    return jnp.concatenate([y_a, y_b], axis=-1) @ w_out


def odd_mixer(h, w_in, b_in, sc_w, cf_w, cf_b, cf_g, cf_beta, w_out):
    proj = h @ w_in + b_in
    sb, sc, sv, cu, cg = split_cols(proj, [SC_WIDTH, SC_WIDTH, SC_WIDTH, CF_WIDTH, CF_WIDTH])
    y_c = sb * causal_dwconv(sc * sv, sc_w)
    glu = cu * jax.nn.sigmoid(cg)
    d = causal_dwconv(glu, cf_w, cf_b)
    y_d = jax.nn.silu(layernorm(d, cf_g, cf_beta))
    return jnp.concatenate([y_c, y_d], axis=-1) @ w_out


def swiglu(h, wg, wu, wd):
    return (jax.nn.silu(h @ wg) * (h @ wu)) @ wd


def _fwd_setup_inputs(seed: int = 0) -> dict:
    key = jax.random.key(seed)
    ks = iter(jax.random.split(key, 40))
    f32 = jnp.float32

    def nrm(shape, scale):
        return jax.random.normal(next(ks), shape, f32) * scale

    def gain(shape):
        return 1.0 + 0.02 * jax.random.normal(next(ks), shape, f32)

    u = jax.random.uniform(next(ks), (N_EVEN, LRU_WIDTH), f32, 0.9, 0.999)
    a_base = u ** (1.0 / LRU_C)
    lru_lambda = jnp.log(a_base) - jnp.log1p(-a_base)
    return {
        "x": jax.random.normal(next(ks), (BATCH, SEQ, D_MODEL), f32),
        "ln_mix_g": gain((DEPTH, D_MODEL)),
        "ln_ffn_g": gain((DEPTH, D_MODEL)),
        "ln_final_g": gain((D_MODEL,)),
        "ev_w_in": nrm((N_EVEN, D_MODEL, EV_IN), D_MODEL ** -0.5),
        "ev_b_in": nrm((N_EVEN, EV_IN), 0.01),
        "lru_conv_w": nrm((N_EVEN, LRU_CONV, LRU_WIDTH), LRU_CONV ** -0.5),
        "lru_conv_b": nrm((N_EVEN, LRU_WIDTH), 0.01),
        "lru_wa": nrm((N_EVEN, LRU_HEADS, LRU_HEAD_DIM, LRU_HEAD_DIM), LRU_HEAD_DIM ** -0.5),
        "lru_ba": nrm((N_EVEN, LRU_HEADS, LRU_HEAD_DIM), 0.01),
        "lru_wx": nrm((N_EVEN, LRU_HEADS, LRU_HEAD_DIM, LRU_HEAD_DIM), LRU_HEAD_DIM ** -0.5),
        "lru_bx": nrm((N_EVEN, LRU_HEADS, LRU_HEAD_DIM), 0.01),
        "lru_lambda": lru_lambda,
        "hgrn_lb_logits": nrm((N_EVEN, HGRN_WIDTH), 0.1),
        "hgrn_norm_g": gain((N_EVEN, HGRN_HEADS * HGRN_DV)),
        "ev_w_out": nrm((N_EVEN, EV_OUT, D_MODEL), EV_OUT ** -0.5),
        "od_w_in": nrm((N_ODD, D_MODEL, OD_IN), D_MODEL ** -0.5),
        "od_b_in": nrm((N_ODD, OD_IN), 0.01),
        "sc_conv_w": nrm((N_ODD, SC_CONV, SC_WIDTH), SC_CONV ** -0.5),
        "cf_conv_w": nrm((N_ODD, CF_CONV, CF_WIDTH), CF_CONV ** -0.5),
        "cf_conv_b": nrm((N_ODD, CF_WIDTH), 0.01),
        "cf_ln_g": gain((N_ODD, CF_WIDTH)),
        "cf_ln_b": nrm((N_ODD, CF_WIDTH), 0.01),
        "od_w_out": nrm((N_ODD, OD_OUT, D_MODEL), OD_OUT ** -0.5),
        "ffn_w_gate": nrm((DEPTH, D_MODEL, FFN_HIDDEN), D_MODEL ** -0.5),
        "ffn_w_up": nrm((DEPTH, D_MODEL, FFN_HIDDEN), D_MODEL ** -0.5),
        "ffn_w_down": nrm((DEPTH, FFN_HIDDEN, D_MODEL), FFN_HIDDEN ** -0.5),
    }


def _fwd_reference(x, ln_mix_g, ln_ffn_g, ln_final_g, ev_w_in, ev_b_in, lru_conv_w, lru_conv_b,
              lru_wa, lru_ba, lru_wx, lru_bx, lru_lambda, hgrn_lb_logits, hgrn_norm_g, ev_w_out,
              od_w_in, od_b_in, sc_conv_w, cf_conv_w, cf_conv_b, cf_ln_g, cf_ln_b, od_w_out,
              ffn_w_gate, ffn_w_up, ffn_w_down):
    sm = jax.nn.softmax(hgrn_lb_logits.astype(jnp.float32), axis=0)
    lower_bounds = jnp.cumsum(sm, axis=0) - sm[0]
    for layer in range(DEPTH):
        h = rmsnorm(x, ln_mix_g[layer])
        if layer % 2 == 0:
            j = layer // 2
            x = x + even_mixer(h, ev_w_in[j], ev_b_in[j], lru_conv_w[j], lru_conv_b[j],
                               lru_wa[j], lru_ba[j], lru_wx[j], lru_bx[j], lru_lambda[j],
                               lower_bounds[j], hgrn_norm_g[j], ev_w_out[j])
        else:
            j = layer // 2
            x = x + odd_mixer(h, od_w_in[j], od_b_in[j], sc_conv_w[j], cf_conv_w[j], cf_conv_b[j],
                              cf_ln_g[j], cf_ln_b[j], od_w_out[j])
        h = rmsnorm(x, ln_ffn_g[layer])
        x = x + swiglu(h, ffn_w_gate[layer], ffn_w_up[layer], ffn_w_down[layer])
    return rmsnorm(x, ln_final_g)


import jax as _jax
import jax.numpy as _jnp

TWIN_FORMAT = 'train_step'
FWD_PARAMS = ['x', 'ln_mix_g', 'ln_ffn_g', 'ln_final_g', 'ev_w_in', 'ev_b_in', 'lru_conv_w', 'lru_conv_b', 'lru_wa', 'lru_ba', 'lru_wx', 'lru_bx', 'lru_lambda', 'hgrn_lb_logits', 'hgrn_norm_g', 'ev_w_out', 'od_w_in', 'od_b_in', 'sc_conv_w', 'cf_conv_w', 'cf_conv_b', 'cf_ln_g', 'cf_ln_b', 'od_w_out', 'ffn_w_gate', 'ffn_w_up', 'ffn_w_down']
TWIN_WEIGHTS = ['ln_mix_g', 'ln_ffn_g', 'ln_final_g', 'ev_w_in', 'ev_b_in', 'lru_conv_w', 'lru_conv_b', 'lru_wa', 'lru_ba', 'lru_wx', 'lru_bx', 'lru_lambda', 'hgrn_lb_logits', 'hgrn_norm_g', 'ev_w_out', 'od_w_in', 'od_b_in', 'sc_conv_w', 'cf_conv_w', 'cf_conv_b', 'cf_ln_g', 'cf_ln_b', 'od_w_out', 'ffn_w_gate', 'ffn_w_up', 'ffn_w_down']
TWIN_DIFF_INPUT = 'x'
TWIN_INPUTS = ['x', 'ln_mix_g', 'ln_ffn_g', 'ln_final_g', 'ev_w_in', 'ev_b_in', 'lru_conv_w', 'lru_conv_b', 'lru_wa', 'lru_ba', 'lru_wx', 'lru_bx', 'lru_lambda', 'hgrn_lb_logits', 'hgrn_norm_g', 'ev_w_out', 'od_w_in', 'od_b_in', 'sc_conv_w', 'cf_conv_w', 'cf_conv_b', 'cf_ln_g', 'cf_ln_b', 'od_w_out', 'ffn_w_gate', 'ffn_w_up', 'ffn_w_down', 'loss_target', 'm_ln_mix_g', 'm_ln_ffn_g', 'm_ln_final_g', 'm_ev_w_in', 'm_ev_b_in', 'm_lru_conv_w', 'm_lru_conv_b', 'm_lru_wa', 'm_lru_ba', 'm_lru_wx', 'm_lru_bx', 'm_lru_lambda', 'm_hgrn_lb_logits', 'm_hgrn_norm_g', 'm_ev_w_out', 'm_od_w_in', 'm_od_b_in', 'm_sc_conv_w', 'm_cf_conv_w', 'm_cf_conv_b', 'm_cf_ln_g', 'm_cf_ln_b', 'm_od_w_out', 'm_ffn_w_gate', 'm_ffn_w_up', 'm_ffn_w_down', 'v_ln_mix_g', 'v_ln_ffn_g', 'v_ln_final_g', 'v_ev_w_in', 'v_ev_b_in', 'v_lru_conv_w', 'v_lru_conv_b', 'v_lru_wa', 'v_lru_ba', 'v_lru_wx', 'v_lru_bx', 'v_lru_lambda', 'v_hgrn_lb_logits', 'v_hgrn_norm_g', 'v_ev_w_out', 'v_od_w_in', 'v_od_b_in', 'v_sc_conv_w', 'v_cf_conv_w', 'v_cf_conv_b', 'v_cf_ln_g', 'v_cf_ln_b', 'v_od_w_out', 'v_ffn_w_gate', 'v_ffn_w_up', 'v_ffn_w_down']
TWIN_OUTPUTS = ['loss', 'grad_x', 'grad_ln_mix_g', 'grad_ln_ffn_g', 'grad_ln_final_g', 'grad_ev_w_in', 'grad_ev_b_in', 'grad_lru_conv_w', 'grad_lru_conv_b', 'grad_lru_wa', 'grad_lru_ba', 'grad_lru_wx', 'grad_lru_bx', 'grad_lru_lambda', 'grad_hgrn_lb_logits', 'grad_hgrn_norm_g', 'grad_ev_w_out', 'grad_od_w_in', 'grad_od_b_in', 'grad_sc_conv_w', 'grad_cf_conv_w', 'grad_cf_conv_b', 'grad_cf_ln_g', 'grad_cf_ln_b', 'grad_od_w_out', 'grad_ffn_w_gate', 'grad_ffn_w_up', 'grad_ffn_w_down', 'delta_ln_mix_g', 'delta_ln_ffn_g', 'delta_ln_final_g', 'delta_ev_w_in', 'delta_ev_b_in', 'delta_lru_conv_w', 'delta_lru_conv_b', 'delta_lru_wa', 'delta_lru_ba', 'delta_lru_wx', 'delta_lru_bx', 'delta_lru_lambda', 'delta_hgrn_lb_logits', 'delta_hgrn_norm_g', 'delta_ev_w_out', 'delta_od_w_in', 'delta_od_b_in', 'delta_sc_conv_w', 'delta_cf_conv_w', 'delta_cf_conv_b', 'delta_cf_ln_g', 'delta_cf_ln_b', 'delta_od_w_out', 'delta_ffn_w_gate', 'delta_ffn_w_up', 'delta_ffn_w_down', 'new_m_ln_mix_g', 'new_m_ln_ffn_g', 'new_m_ln_final_g', 'new_m_ev_w_in', 'new_m_ev_b_in', 'new_m_lru_conv_w', 'new_m_lru_conv_b', 'new_m_lru_wa', 'new_m_lru_ba', 'new_m_lru_wx', 'new_m_lru_bx', 'new_m_lru_lambda', 'new_m_hgrn_lb_logits', 'new_m_hgrn_norm_g', 'new_m_ev_w_out', 'new_m_od_w_in', 'new_m_od_b_in', 'new_m_sc_conv_w', 'new_m_cf_conv_w', 'new_m_cf_conv_b', 'new_m_cf_ln_g', 'new_m_cf_ln_b', 'new_m_od_w_out', 'new_m_ffn_w_gate', 'new_m_ffn_w_up', 'new_m_ffn_w_down', 'new_v_ln_mix_g', 'new_v_ln_ffn_g', 'new_v_ln_final_g', 'new_v_ev_w_in', 'new_v_ev_b_in', 'new_v_lru_conv_w', 'new_v_lru_conv_b', 'new_v_lru_wa', 'new_v_lru_ba', 'new_v_lru_wx', 'new_v_lru_bx', 'new_v_lru_lambda', 'new_v_hgrn_lb_logits', 'new_v_hgrn_norm_g', 'new_v_ev_w_out', 'new_v_od_w_in', 'new_v_od_b_in', 'new_v_sc_conv_w', 'new_v_cf_conv_w', 'new_v_cf_conv_b', 'new_v_cf_ln_g', 'new_v_cf_ln_b', 'new_v_od_w_out', 'new_v_ffn_w_gate', 'new_v_ffn_w_up', 'new_v_ffn_w_down']
TWIN_LEAF_KINDS = {'loss': 'loss', 'grad_x': 'grad_x', 'grad_ln_mix_g': 'grad_w', 'grad_ln_ffn_g': 'grad_w', 'grad_ln_final_g': 'grad_w', 'grad_ev_w_in': 'grad_w', 'grad_ev_b_in': 'grad_w', 'grad_lru_conv_w': 'grad_w', 'grad_lru_conv_b': 'grad_w', 'grad_lru_wa': 'grad_w', 'grad_lru_ba': 'grad_w', 'grad_lru_wx': 'grad_w', 'grad_lru_bx': 'grad_w', 'grad_lru_lambda': 'grad_w', 'grad_hgrn_lb_logits': 'grad_w', 'grad_hgrn_norm_g': 'grad_w', 'grad_ev_w_out': 'grad_w', 'grad_od_w_in': 'grad_w', 'grad_od_b_in': 'grad_w', 'grad_sc_conv_w': 'grad_w', 'grad_cf_conv_w': 'grad_w', 'grad_cf_conv_b': 'grad_w', 'grad_cf_ln_g': 'grad_w', 'grad_cf_ln_b': 'grad_w', 'grad_od_w_out': 'grad_w', 'grad_ffn_w_gate': 'grad_w', 'grad_ffn_w_up': 'grad_w', 'grad_ffn_w_down': 'grad_w', 'delta_ln_mix_g': 'delta_w', 'delta_ln_ffn_g': 'delta_w', 'delta_ln_final_g': 'delta_w', 'delta_ev_w_in': 'delta_w', 'delta_ev_b_in': 'delta_w', 'delta_lru_conv_w': 'delta_w', 'delta_lru_conv_b': 'delta_w', 'delta_lru_wa': 'delta_w', 'delta_lru_ba': 'delta_w', 'delta_lru_wx': 'delta_w', 'delta_lru_bx': 'delta_w', 'delta_lru_lambda': 'delta_w', 'delta_hgrn_lb_logits': 'delta_w', 'delta_hgrn_norm_g': 'delta_w', 'delta_ev_w_out': 'delta_w', 'delta_od_w_in': 'delta_w', 'delta_od_b_in': 'delta_w', 'delta_sc_conv_w': 'delta_w', 'delta_cf_conv_w': 'delta_w', 'delta_cf_conv_b': 'delta_w', 'delta_cf_ln_g': 'delta_w', 'delta_cf_ln_b': 'delta_w', 'delta_od_w_out': 'delta_w', 'delta_ffn_w_gate': 'delta_w', 'delta_ffn_w_up': 'delta_w', 'delta_ffn_w_down': 'delta_w', 'new_m_ln_mix_g': 'new_m', 'new_m_ln_ffn_g': 'new_m', 'new_m_ln_final_g': 'new_m', 'new_m_ev_w_in': 'new_m', 'new_m_ev_b_in': 'new_m', 'new_m_lru_conv_w': 'new_m', 'new_m_lru_conv_b': 'new_m', 'new_m_lru_wa': 'new_m', 'new_m_lru_ba': 'new_m', 'new_m_lru_wx': 'new_m', 'new_m_lru_bx': 'new_m', 'new_m_lru_lambda': 'new_m', 'new_m_hgrn_lb_logits': 'new_m', 'new_m_hgrn_norm_g': 'new_m', 'new_m_ev_w_out': 'new_m', 'new_m_od_w_in': 'new_m', 'new_m_od_b_in': 'new_m', 'new_m_sc_conv_w': 'new_m', 'new_m_cf_conv_w': 'new_m', 'new_m_cf_conv_b': 'new_m', 'new_m_cf_ln_g': 'new_m', 'new_m_cf_ln_b': 'new_m', 'new_m_od_w_out': 'new_m', 'new_m_ffn_w_gate': 'new_m', 'new_m_ffn_w_up': 'new_m', 'new_m_ffn_w_down': 'new_m', 'new_v_ln_mix_g': 'new_v', 'new_v_ln_ffn_g': 'new_v', 'new_v_ln_final_g': 'new_v', 'new_v_ev_w_in': 'new_v', 'new_v_ev_b_in': 'new_v', 'new_v_lru_conv_w': 'new_v', 'new_v_lru_conv_b': 'new_v', 'new_v_lru_wa': 'new_v', 'new_v_lru_ba': 'new_v', 'new_v_lru_wx': 'new_v', 'new_v_lru_bx': 'new_v', 'new_v_lru_lambda': 'new_v', 'new_v_hgrn_lb_logits': 'new_v', 'new_v_hgrn_norm_g': 'new_v', 'new_v_ev_w_out': 'new_v', 'new_v_od_w_in': 'new_v', 'new_v_od_b_in': 'new_v', 'new_v_sc_conv_w': 'new_v', 'new_v_cf_conv_w': 'new_v', 'new_v_cf_conv_b': 'new_v', 'new_v_cf_ln_g': 'new_v', 'new_v_cf_ln_b': 'new_v', 'new_v_od_w_out': 'new_v', 'new_v_ffn_w_gate': 'new_v', 'new_v_ffn_w_up': 'new_v', 'new_v_ffn_w_down': 'new_v'}


def _forward(args):
    return _fwd_reference(*[args[k] for k in FWD_PARAMS])


def _output_shape():
    def fwd():
        inp = _fwd_setup_inputs(0)
        return _fwd_reference(*[inp[k] for k in FWD_PARAMS])
    out = _jax.eval_shape(fwd)
    return out.shape, out.dtype

N_MICROBATCH = 1
ADAM_LR = 0.001
ADAM_B1 = 0.9
ADAM_B2 = 0.999
ADAM_EPS = 1e-08
ADAM_WD = 0.01
ADAM_STEP = 10
PER_EXAMPLE_BATCH_AXIS = {'x': 0, 'loss_target': 0}
SHARED_INPUTS = []
_WEIGHT_DTYPES = {'ln_mix_g': _jnp.float32, 'ln_ffn_g': _jnp.float32, 'ln_final_g': _jnp.float32, 'ev_w_in': _jnp.float32, 'ev_b_in': _jnp.float32, 'lru_conv_w': _jnp.float32, 'lru_conv_b': _jnp.float32, 'lru_wa': _jnp.float32, 'lru_ba': _jnp.float32, 'lru_wx': _jnp.float32, 'lru_bx': _jnp.float32, 'lru_lambda': _jnp.float32, 'hgrn_lb_logits': _jnp.float32, 'hgrn_norm_g': _jnp.float32, 'ev_w_out': _jnp.float32, 'od_w_in': _jnp.float32, 'od_b_in': _jnp.float32, 'sc_conv_w': _jnp.float32, 'cf_conv_w': _jnp.float32, 'cf_conv_b': _jnp.float32, 'cf_ln_g': _jnp.float32, 'cf_ln_b': _jnp.float32, 'od_w_out': _jnp.float32, 'ffn_w_gate': _jnp.float32, 'ffn_w_up': _jnp.float32, 'ffn_w_down': _jnp.float32}
MOMENT_SCALE = {'ln_mix_g': 8.905067e-02, 'ln_ffn_g': 6.528647e-02, 'ln_final_g': 1.599757e+01, 'ev_w_in': 4.919494e-02, 'ev_b_in': 2.037310e-01, 'lru_conv_w': 5.389064e-02, 'lru_conv_b': 5.587808e-01, 'lru_wa': 1.615341e-02, 'lru_ba': 1.449397e-02, 'lru_wx': 2.879838e-02, 'lru_bx': 1.771955e-02, 'lru_lambda': 2.800056e-02, 'hgrn_lb_logits': 3.537642e-03, 'hgrn_norm_g': 6.683587e-02, 'ev_w_out': 5.940873e-02, 'od_w_in': 5.466180e-02, 'od_b_in': 5.746972e-02, 'sc_conv_w': 6.787663e-02, 'cf_conv_w': 4.141755e-02, 'cf_conv_b': 1.051903e-01, 'cf_ln_g': 5.655662e-02, 'cf_ln_b': 6.165779e-02, 'od_w_out': 5.571378e-02, 'ffn_w_gate': 2.855851e-02, 'ffn_w_up': 2.768169e-02, 'ffn_w_down': 4.591519e-02}


def _to_microbatches(a, axis):
    t = _jnp.moveaxis(a, axis, 0)
    t = t.reshape((N_MICROBATCH, t.shape[0] // N_MICROBATCH) + t.shape[1:])
    return _jnp.moveaxis(t, 1, axis + 1)


def setup_inputs(seed: int = 0) -> dict:
    inp = _fwd_setup_inputs(seed)
    key = _jax.random.fold_in(_jax.random.key(seed), 7919)
    shape, _ = _output_shape()
    out = dict(inp)
    out["loss_target"] = _jax.random.normal(_jax.random.fold_in(key, 0), shape, _jnp.float32)
    for i, name in enumerate(TWIN_WEIGHTS):
        w = inp[name].astype(_jnp.float32)
        if MOMENT_SCALE is None:
            s = _jnp.sqrt(_jnp.mean(_jnp.square(w)) + 1e-30)
        else:
            s = MOMENT_SCALE[name]
        km, kv = _jax.random.split(_jax.random.fold_in(key, i + 1))
        out[name] = w
        out["m_" + name] = s * _jax.random.normal(km, w.shape, _jnp.float32)
        out["v_" + name] = (s * s) * _jax.random.uniform(kv, w.shape, _jnp.float32, 0.5, 1.5)
    if N_MICROBATCH > 1:
        for name, axis in PER_EXAMPLE_BATCH_AXIS.items():
            out[name] = _to_microbatches(out[name], axis)
    return {'x': out['x'], 'ln_mix_g': out['ln_mix_g'], 'ln_ffn_g': out['ln_ffn_g'], 'ln_final_g': out['ln_final_g'], 'ev_w_in': out['ev_w_in'], 'ev_b_in': out['ev_b_in'], 'lru_conv_w': out['lru_conv_w'], 'lru_conv_b': out['lru_conv_b'], 'lru_wa': out['lru_wa'], 'lru_ba': out['lru_ba'], 'lru_wx': out['lru_wx'], 'lru_bx': out['lru_bx'], 'lru_lambda': out['lru_lambda'], 'hgrn_lb_logits': out['hgrn_lb_logits'], 'hgrn_norm_g': out['hgrn_norm_g'], 'ev_w_out': out['ev_w_out'], 'od_w_in': out['od_w_in'], 'od_b_in': out['od_b_in'], 'sc_conv_w': out['sc_conv_w'], 'cf_conv_w': out['cf_conv_w'], 'cf_conv_b': out['cf_conv_b'], 'cf_ln_g': out['cf_ln_g'], 'cf_ln_b': out['cf_ln_b'], 'od_w_out': out['od_w_out'], 'ffn_w_gate': out['ffn_w_gate'], 'ffn_w_up': out['ffn_w_up'], 'ffn_w_down': out['ffn_w_down'], 'loss_target': out['loss_target'], 'm_ln_mix_g': out['m_ln_mix_g'], 'm_ln_ffn_g': out['m_ln_ffn_g'], 'm_ln_final_g': out['m_ln_final_g'], 'm_ev_w_in': out['m_ev_w_in'], 'm_ev_b_in': out['m_ev_b_in'], 'm_lru_conv_w': out['m_lru_conv_w'], 'm_lru_conv_b': out['m_lru_conv_b'], 'm_lru_wa': out['m_lru_wa'], 'm_lru_ba': out['m_lru_ba'], 'm_lru_wx': out['m_lru_wx'], 'm_lru_bx': out['m_lru_bx'], 'm_lru_lambda': out['m_lru_lambda'], 'm_hgrn_lb_logits': out['m_hgrn_lb_logits'], 'm_hgrn_norm_g': out['m_hgrn_norm_g'], 'm_ev_w_out': out['m_ev_w_out'], 'm_od_w_in': out['m_od_w_in'], 'm_od_b_in': out['m_od_b_in'], 'm_sc_conv_w': out['m_sc_conv_w'], 'm_cf_conv_w': out['m_cf_conv_w'], 'm_cf_conv_b': out['m_cf_conv_b'], 'm_cf_ln_g': out['m_cf_ln_g'], 'm_cf_ln_b': out['m_cf_ln_b'], 'm_od_w_out': out['m_od_w_out'], 'm_ffn_w_gate': out['m_ffn_w_gate'], 'm_ffn_w_up': out['m_ffn_w_up'], 'm_ffn_w_down': out['m_ffn_w_down'], 'v_ln_mix_g': out['v_ln_mix_g'], 'v_ln_ffn_g': out['v_ln_ffn_g'], 'v_ln_final_g': out['v_ln_final_g'], 'v_ev_w_in': out['v_ev_w_in'], 'v_ev_b_in': out['v_ev_b_in'], 'v_lru_conv_w': out['v_lru_conv_w'], 'v_lru_conv_b': out['v_lru_conv_b'], 'v_lru_wa': out['v_lru_wa'], 'v_lru_ba': out['v_lru_ba'], 'v_lru_wx': out['v_lru_wx'], 'v_lru_bx': out['v_lru_bx'], 'v_lru_lambda': out['v_lru_lambda'], 'v_hgrn_lb_logits': out['v_hgrn_lb_logits'], 'v_hgrn_norm_g': out['v_hgrn_norm_g'], 'v_ev_w_out': out['v_ev_w_out'], 'v_od_w_in': out['v_od_w_in'], 'v_od_b_in': out['v_od_b_in'], 'v_sc_conv_w': out['v_sc_conv_w'], 'v_cf_conv_w': out['v_cf_conv_w'], 'v_cf_conv_b': out['v_cf_conv_b'], 'v_cf_ln_g': out['v_cf_ln_g'], 'v_cf_ln_b': out['v_cf_ln_b'], 'v_od_w_out': out['v_od_w_out'], 'v_ffn_w_gate': out['v_ffn_w_gate'], 'v_ffn_w_up': out['v_ffn_w_up'], 'v_ffn_w_down': out['v_ffn_w_down']}


def _loss(weights, diff, rest, loss_target):
    with _jax.named_scope("forward"):
        args = {**rest, TWIN_DIFF_INPUT: diff, **{k: w.astype(_WEIGHT_DTYPES[k]) for k, w in weights.items()}}
        y = _forward(args)
    with _jax.named_scope("loss_head"):
        err = _jnp.square(y.astype(_jnp.float32) - loss_target)
        return 0.5 * _jnp.sum(_jnp.mean(err, axis=-1)) if err.ndim else 0.5 * err


def _adamw(w, g, m, v):
    m = ADAM_B1 * m + (1.0 - ADAM_B1) * g
    v = ADAM_B2 * v + (1.0 - ADAM_B2) * _jnp.square(g)
    m_hat = m / (1.0 - ADAM_B1 ** ADAM_STEP)
    v_hat = v / (1.0 - ADAM_B2 ** ADAM_STEP)
    delta = -ADAM_LR * (m_hat / (_jnp.sqrt(v_hat) + ADAM_EPS) + ADAM_WD * w)
    return delta, m, v


def reference(x, ln_mix_g, ln_ffn_g, ln_final_g, ev_w_in, ev_b_in, lru_conv_w, lru_conv_b, lru_wa, lru_ba, lru_wx, lru_bx, lru_lambda, hgrn_lb_logits, hgrn_norm_g, ev_w_out, od_w_in, od_b_in, sc_conv_w, cf_conv_w, cf_conv_b, cf_ln_g, cf_ln_b, od_w_out, ffn_w_gate, ffn_w_up, ffn_w_down, loss_target, m_ln_mix_g, m_ln_ffn_g, m_ln_final_g, m_ev_w_in, m_ev_b_in, m_lru_conv_w, m_lru_conv_b, m_lru_wa, m_lru_ba, m_lru_wx, m_lru_bx, m_lru_lambda, m_hgrn_lb_logits, m_hgrn_norm_g, m_ev_w_out, m_od_w_in, m_od_b_in, m_sc_conv_w, m_cf_conv_w, m_cf_conv_b, m_cf_ln_g, m_cf_ln_b, m_od_w_out, m_ffn_w_gate, m_ffn_w_up, m_ffn_w_down, v_ln_mix_g, v_ln_ffn_g, v_ln_final_g, v_ev_w_in, v_ev_b_in, v_lru_conv_w, v_lru_conv_b, v_lru_wa, v_lru_ba, v_lru_wx, v_lru_bx, v_lru_lambda, v_hgrn_lb_logits, v_hgrn_norm_g, v_ev_w_out, v_od_w_in, v_od_b_in, v_sc_conv_w, v_cf_conv_w, v_cf_conv_b, v_cf_ln_g, v_cf_ln_b, v_od_w_out, v_ffn_w_gate, v_ffn_w_up, v_ffn_w_down):
    given = dict(x=x, ln_mix_g=ln_mix_g, ln_ffn_g=ln_ffn_g, ln_final_g=ln_final_g, ev_w_in=ev_w_in, ev_b_in=ev_b_in, lru_conv_w=lru_conv_w, lru_conv_b=lru_conv_b, lru_wa=lru_wa, lru_ba=lru_ba, lru_wx=lru_wx, lru_bx=lru_bx, lru_lambda=lru_lambda, hgrn_lb_logits=hgrn_lb_logits, hgrn_norm_g=hgrn_norm_g, ev_w_out=ev_w_out, od_w_in=od_w_in, od_b_in=od_b_in, sc_conv_w=sc_conv_w, cf_conv_w=cf_conv_w, cf_conv_b=cf_conv_b, cf_ln_g=cf_ln_g, cf_ln_b=cf_ln_b, od_w_out=od_w_out, ffn_w_gate=ffn_w_gate, ffn_w_up=ffn_w_up, ffn_w_down=ffn_w_down, loss_target=loss_target, m_ln_mix_g=m_ln_mix_g, m_ln_ffn_g=m_ln_ffn_g, m_ln_final_g=m_ln_final_g, m_ev_w_in=m_ev_w_in, m_ev_b_in=m_ev_b_in, m_lru_conv_w=m_lru_conv_w, m_lru_conv_b=m_lru_conv_b, m_lru_wa=m_lru_wa, m_lru_ba=m_lru_ba, m_lru_wx=m_lru_wx, m_lru_bx=m_lru_bx, m_lru_lambda=m_lru_lambda, m_hgrn_lb_logits=m_hgrn_lb_logits, m_hgrn_norm_g=m_hgrn_norm_g, m_ev_w_out=m_ev_w_out, m_od_w_in=m_od_w_in, m_od_b_in=m_od_b_in, m_sc_conv_w=m_sc_conv_w, m_cf_conv_w=m_cf_conv_w, m_cf_conv_b=m_cf_conv_b, m_cf_ln_g=m_cf_ln_g, m_cf_ln_b=m_cf_ln_b, m_od_w_out=m_od_w_out, m_ffn_w_gate=m_ffn_w_gate, m_ffn_w_up=m_ffn_w_up, m_ffn_w_down=m_ffn_w_down, v_ln_mix_g=v_ln_mix_g, v_ln_ffn_g=v_ln_ffn_g, v_ln_final_g=v_ln_final_g, v_ev_w_in=v_ev_w_in, v_ev_b_in=v_ev_b_in, v_lru_conv_w=v_lru_conv_w, v_lru_conv_b=v_lru_conv_b, v_lru_wa=v_lru_wa, v_lru_ba=v_lru_ba, v_lru_wx=v_lru_wx, v_lru_bx=v_lru_bx, v_lru_lambda=v_lru_lambda, v_hgrn_lb_logits=v_hgrn_lb_logits, v_hgrn_norm_g=v_hgrn_norm_g, v_ev_w_out=v_ev_w_out, v_od_w_in=v_od_w_in, v_od_b_in=v_od_b_in, v_sc_conv_w=v_sc_conv_w, v_cf_conv_w=v_cf_conv_w, v_cf_conv_b=v_cf_conv_b, v_cf_ln_g=v_cf_ln_g, v_cf_ln_b=v_cf_ln_b, v_od_w_out=v_od_w_out, v_ffn_w_gate=v_ffn_w_gate, v_ffn_w_up=v_ffn_w_up, v_ffn_w_down=v_ffn_w_down)
    weights = {n: given[n] for n in TWIN_WEIGHTS}
    shared = {n: given[n] for n in SHARED_INPUTS}
    per_example = {n: given[n] for n in ['x']}
    grad_fn = _jax.value_and_grad(_loss, argnums=(0, 1))

    def one_microbatch(ex, loss_target):
        ex = dict(ex)
        diff = ex.pop(TWIN_DIFF_INPUT)
        return grad_fn(weights, diff, {**shared, **ex}, loss_target)

    if N_MICROBATCH == 1:
        loss, (grad_w, grad_x) = one_microbatch(per_example, given["loss_target"])
    else:
        def body(carry, xs):
            loss_sum, grad_sum = carry
            l_k, (gw_k, gx_k) = one_microbatch(xs[0], xs[1])
            with _jax.named_scope("update"):
                return (loss_sum + l_k, _jax.tree.map(_jnp.add, grad_sum, gw_k)), gx_k

        init = (_jnp.zeros((), _jnp.float32), _jax.tree.map(_jnp.zeros_like, weights))
        (loss, grad_w), grad_x = _jax.lax.scan(body, init, (per_example, given["loss_target"]))
    with _jax.named_scope("update"):
        delta_w, new_m, new_v = {}, {}, {}
        for n in TWIN_WEIGHTS:
            delta_w[n], new_m[n], new_v[n] = _adamw(weights[n], grad_w[n], given["m_" + n], given["v_" + n])
    return (loss, grad_x, *[grad_w[n] for n in TWIN_WEIGHTS], *[delta_w[n] for n in TWIN_WEIGHTS],
            *[new_m[n] for n in TWIN_WEIGHTS], *[new_v[n] for n in TWIN_WEIGHTS])
```

```python
import functools
import math

import jax
import jax.numpy as jnp
from jax import lax
from jax.experimental import pallas as pl
from jax.experimental.pallas import tpu as pltpu

F32 = jnp.float32
MXU_DTYPE = jnp.bfloat16
WIRE_DTYPE = jnp.bfloat16
N_DEV = 8
EPS = 1e-6
F_FLOOR = 1e-30
LRU_C = 8.0
HGRN_HEADS = 8
HGRN_SUB = 16
ADAM_LR, ADAM_B1, ADAM_B2, ADAM_EPS, ADAM_WD, ADAM_STEP = 0.001, 0.9, 0.999, 1e-08, 0.01, 10
V7X_VMEM_LIMIT = 48 * 1024 * 1024
MESH = pl.DeviceIdType.MESH
ANY = pl.BlockSpec(memory_space=pl.ANY)


def _cp(*sem):
    return pltpu.CompilerParams(dimension_semantics=sem or None, vmem_limit_bytes=V7X_VMEM_LIMIT)


def _sigmoid(x):
    return 1.0 / (1.0 + jnp.exp(-x))


def _silu(x):
    return x * _sigmoid(x)


def _dsilu(x):
    s = _sigmoid(x)
    return s * (1.0 + x * (1.0 - s))


_GELU_C = math.sqrt(2.0 / math.pi)


def _gelu(x):
    return 0.5 * x * (1.0 + jnp.tanh(_GELU_C * (x + 0.044715 * x * x * x)))


def _dgelu(x):
    t = jnp.tanh(_GELU_C * (x + 0.044715 * x * x * x))
    return 0.5 * (1.0 + t) + 0.5 * x * (1.0 - t * t) * _GELU_C * (1.0 + 3.0 * 0.044715 * x * x)


def _log1p(e):
    return jnp.where(e < 1e-2, e * (1.0 - e * (0.5 - e * (1.0 / 3.0))), jnp.log(1.0 + e))


def _softplus(x):
    return jnp.maximum(x, 0.0) + _log1p(jnp.exp(-jnp.abs(x)))


def _one_minus_exp(x):
    series = -x * (1.0 + x * (0.5 + x * (1.0 / 6.0 + x * (1.0 / 24.0))))
    return jnp.where(x > -0.05, series, 1.0 - jnp.exp(x))


def _rows(n, d=1):
    return lax.broadcasted_iota(jnp.int32, (n, d), 0)


def _dot(a, b):
    return jnp.dot(a.astype(MXU_DTYPE), b.astype(MXU_DTYPE), preferred_element_type=F32)


def _dot_nt(a, b):
    return lax.dot_general(a.astype(MXU_DTYPE), b.astype(MXU_DTYPE), (((1,), (1,)), ((), ())),
                           preferred_element_type=F32)


def _dot_tn(a, b):
    return lax.dot_general(a.astype(MXU_DTYPE), b.astype(MXU_DTYPE), (((0,), (0,)), ((), ())),
                           preferred_element_type=F32)


def _tile(n, want):
    if n <= want:
        return n
    t = want - want % 8
    while n % t:
        t -= 8
    assert t > 0, (n, want)
    return t


def _my_place():
    x, y, c = lax.axis_index("x"), lax.axis_index("y"), lax.axis_index("c")
    return x, y, c


def _all_gather(name, srcs):
    n = len(srcs)

    def body(*refs):
        src_refs, out_refs = refs[:n], refs[n:2 * n]
        send_sems, recv_sems, local_sems = refs[2 * n:]
        x, y, c = _my_place()
        sibling = (x, y, 1 - c)
        chips = [(1 - x, y), (x, 1 - y), (1 - x, 1 - y)]

        def slot(px, py, pc):
            return 4 * px + 2 * py + pc

        def copy(i, k, block, to, src=None):
            dst = out_refs[i].at[slot(*block)]
            return pltpu.make_async_remote_copy(
                src_ref=dst if src is None else src, dst_ref=dst,
                send_sem=send_sems.at[i, k], recv_sem=recv_sems.at[i, k],
                device_id=to, device_id_type=MESH)

        me = (x, y, c)
        sends, own = [], []
        for i in range(n):
            mine = pltpu.make_async_copy(src_refs[i], out_refs[i].at[slot(*me)], local_sems.at[i])
            mine.start()
            own.append(mine)
            first = [copy(i, 0, me, sibling, src=src_refs[i])]
            first += [copy(i, 1 + j, me, (*chip, c), src=src_refs[i]) for j, chip in enumerate(chips)]
            for cp in first:
                cp.start()
            sends += first
        for i in range(n):
            for j, chip in enumerate(chips):
                copy(i, 1 + j, (*chip, c), me).wait_recv()
                passed = copy(i, 4 + j, (*chip, c), sibling)
                passed.start()
                sends.append(passed)
        for i in range(n):
            copy(i, 0, sibling, me).wait_recv()
            for j, chip in enumerate(chips):
                copy(i, 4 + j, (*chip, 1 - c), me).wait_recv()
        for cp in sends:
            cp.wait_send()
        for cp in own:
            cp.wait()

    outs = pl.pallas_call(
        body, name=name,
        out_shape=[jax.ShapeDtypeStruct((N_DEV,) + s.shape, s.dtype) for s in srcs],
        in_specs=[ANY] * n, out_specs=[ANY] * n,
        scratch_shapes=[pltpu.SemaphoreType.DMA((n, 7)), pltpu.SemaphoreType.DMA((n, 7)),
                        pltpu.SemaphoreType.DMA((n,))],
    )(*srcs)
    return list(outs)


def _pair_exchange(name, srcs):
    n = len(srcs)

    def body(*refs):
        src_refs, out_refs = refs[:n], refs[n:2 * n]
        send_sems, recv_sems = refs[2 * n:]
        x, y, c = _my_place()
        copies = []
        for i in range(n):
            for j in range(4):
                cp = pltpu.make_async_remote_copy(
                    src_ref=src_refs[i].at[2 * j + (1 - c)], dst_ref=out_refs[i].at[j],
                    send_sem=send_sems.at[i, j], recv_sem=recv_sems.at[i, j],
                    device_id=(x, y, 1 - c), device_id_type=MESH)
                cp.start()
                copies.append(cp)
        for cp in copies:
            cp.wait()

    outs = pl.pallas_call(
        body, name=name,
        out_shape=[jax.ShapeDtypeStruct((4,) + s.shape[1:], s.dtype) for s in srcs],
        in_specs=[ANY] * n, out_specs=[ANY] * n,
        scratch_shapes=[pltpu.SemaphoreType.DMA((n, 4)), pltpu.SemaphoreType.DMA((n, 4))],
    )(*srcs)
    return list(outs)


def _chip_exchange(name, srcs):
    n = len(srcs)

    def body(*refs):
        src_refs, out_refs = refs[:n], refs[n:2 * n]
        send_sems, recv_sems, local_sems = refs[2 * n:]
        x, y, c = _my_place()
        chip = 2 * x + y
        copies = []
        for i in range(n):
            mine = pltpu.make_async_copy(src_refs[i].at[chip], out_refs[i].at[3], local_sems.at[i])
            mine.start()
            copies.append(mine)
            for k, (fx, fy) in enumerate([(1, 0), (0, 1), (1, 1)]):
                px = x + fx - 2 * x * fx
                py = y + fy - 2 * y * fy
                cp = pltpu.make_async_remote_copy(
                    src_ref=src_refs[i].at[2 * px + py], dst_ref=out_refs[i].at[k],
                    send_sem=send_sems.at[i, k], recv_sem=recv_sems.at[i, k],
                    device_id=(px, py, c), device_id_type=MESH)
                cp.start()
                copies.append(cp)
        for cp in copies:
            cp.wait()

    outs = pl.pallas_call(
        body, name=name,
        out_shape=[jax.ShapeDtypeStruct(s.shape, s.dtype) for s in srcs],
        in_specs=[ANY] * n, out_specs=[ANY] * n,
        scratch_shapes=[pltpu.SemaphoreType.DMA((n, 3)), pltpu.SemaphoreType.DMA((n, 3)),
                        pltpu.SemaphoreType.DMA((n,))],
    )(*srcs)
    return list(outs)


def _pair_add(name, mine, got):
    assert mine.shape[0] == N_DEV and got.shape[0] == 4
    cdim = mine.shape[-1]
    m4 = mine.reshape(4, 2, -1, cdim)
    g3 = got.reshape(4, -1, cdim)
    rows = m4.shape[2]
    tr = _tile(rows, 512)

    def body(m_ref, g_ref, o_ref):
        c = lax.axis_index("c")
        o_ref[...] = (m_ref[c].astype(F32) + g_ref[...].astype(F32)).astype(o_ref.dtype)

    out = pl.pallas_call(
        body, name=name, grid=(4, rows // tr),
        in_specs=[pl.BlockSpec((None, 2, tr, cdim), lambda j, i: (j, 0, i, 0)),
                  pl.BlockSpec((None, tr, cdim), lambda j, i: (j, i, 0))],
        out_specs=pl.BlockSpec((None, tr, cdim), lambda j, i: (j, i, 0)),
        out_shape=jax.ShapeDtypeStruct(g3.shape, got.dtype),
        compiler_params=_cp("parallel", "parallel"),
    )(m4, g3)
    return out.reshape(got.shape)


def _rmsnorm_fwd(name, x, g):
    T, D = x.shape
    tm = _tile(T, 256)

    def body(x_ref, g_ref, o_ref):
        xv = x_ref[...]
        r = lax.rsqrt(jnp.mean(xv * xv, axis=-1, keepdims=True) + EPS)
        o_ref[...] = ((xv * r) * g_ref[...]).astype(o_ref.dtype)

    return pl.pallas_call(
        body, name=name, grid=(T // tm,),
        in_specs=[pl.BlockSpec((tm, D), lambda i: (i, 0)), pl.BlockSpec((1, D), lambda i: (0, 0))],
        out_specs=pl.BlockSpec((tm, D), lambda i: (i, 0)),
        out_shape=jax.ShapeDtypeStruct((T, D), MXU_DTYPE), compiler_params=_cp("parallel"),
    )(x, g)


def _rmsnorm_bwd(name, x, g, dh, dres):
    T, D = x.shape
    tm = _tile(T, 256)

    def body(x_ref, g_ref, dh_ref, dres_ref, dx_ref, dg_ref):
        xv = x_ref[...]
        r = lax.rsqrt(jnp.mean(xv * xv, axis=-1, keepdims=True) + EPS)
        xh = xv * r
        dhv = dh_ref[...]

        @pl.when(pl.program_id(0) == 0)
        def _():
            dg_ref[...] = jnp.zeros_like(dg_ref)

        dg_ref[...] += jnp.sum(dhv * xh, axis=0, keepdims=True)
        dxh = dhv * g_ref[...]
        dx_ref[...] = dres_ref[...] + r * (dxh - xh * jnp.mean(dxh * xh, axis=-1, keepdims=True))

    return pl.pallas_call(
        body, name=name, grid=(T // tm,),
        in_specs=[pl.BlockSpec((tm, D), lambda i: (i, 0)), pl.BlockSpec((1, D), lambda i: (0, 0)),
                  pl.BlockSpec((tm, D), lambda i: (i, 0)), pl.BlockSpec((tm, D), lambda i: (i, 0))],
        out_specs=[pl.BlockSpec((tm, D), lambda i: (i, 0)), pl.BlockSpec((1, D), lambda i: (0, 0))],
        out_shape=[jax.ShapeDtypeStruct((T, D), F32), jax.ShapeDtypeStruct((1, D), F32)],
        compiler_params=_cp("arbitrary"),
    )(x, g, dh, dres)


def _loss_head(name, x, g, tgt):
    T, D = x.shape
    tm = _tile(T, 256)

    def body(x_ref, g_ref, t_ref, loss_ref, dx_ref, dg_ref):
        xv = x_ref[...]
        r = lax.rsqrt(jnp.mean(xv * xv, axis=-1, keepdims=True) + EPS)
        xh = xv * r
        gv = g_ref[...]
        diff = xh * gv - t_ref[...]

        @pl.when(pl.program_id(0) == 0)
        def _():
            dg_ref[...] = jnp.zeros_like(dg_ref)
            loss_ref[...] = jnp.zeros_like(loss_ref)

        part = 0.5 * jnp.sum(jnp.mean(diff * diff, axis=-1, keepdims=True), axis=0, keepdims=True)
        loss_ref[...] += jnp.broadcast_to(part, loss_ref.shape)
        dy = diff * (1.0 / D)
        dg_ref[...] += jnp.sum(dy * xh, axis=0, keepdims=True)
        dxh = dy * gv
        dx_ref[...] = r * (dxh - xh * jnp.mean(dxh * xh, axis=-1, keepdims=True))

    return pl.pallas_call(
        body, name=name, grid=(T // tm,),
        in_specs=[pl.BlockSpec((tm, D), lambda i: (i, 0)), pl.BlockSpec((1, D), lambda i: (0, 0)),
                  pl.BlockSpec((tm, D), lambda i: (i, 0))],
        out_specs=[pl.BlockSpec((1, 128), lambda i: (0, 0)), pl.BlockSpec((tm, D), lambda i: (i, 0)),
                   pl.BlockSpec((1, D), lambda i: (0, 0))],
        out_shape=[jax.ShapeDtypeStruct((1, 128), F32), jax.ShapeDtypeStruct((T, D), F32),
                   jax.ShapeDtypeStruct((1, D), F32)],
        compiler_params=_cp("arbitrary"),
    )(x, g, tgt)


def _adamw(name, w, m, v, parts):
    R, C = w.shape
    P = parts.shape[0]
    tr = _tile(R, 128)
    c1 = 1.0 / (1.0 - ADAM_B1 ** ADAM_STEP)
    c2 = 1.0 / (1.0 - ADAM_B2 ** ADAM_STEP)

    def body(w_ref, m_ref, v_ref, p_ref, g_ref, d_ref, nm_ref, nv_ref):
        g = p_ref[0].astype(F32)
        for s in range(1, P):
            g = g + p_ref[s].astype(F32)
        nm = ADAM_B1 * m_ref[...] + (1.0 - ADAM_B1) * g
        nv = ADAM_B2 * v_ref[...] + (1.0 - ADAM_B2) * (g * g)
        g_ref[...] = g
        nm_ref[...] = nm
        nv_ref[...] = nv
        d_ref[...] = -ADAM_LR * ((nm * c1) / (jnp.sqrt(nv * c2) + ADAM_EPS) + ADAM_WD * w_ref[...])

    blk = pl.BlockSpec((tr, C), lambda i: (i, 0))
    return pl.pallas_call(
        body, name=name, grid=(R // tr,),
        in_specs=[blk, blk, blk, pl.BlockSpec((P, tr, C), lambda i: (0, i, 0))],
        out_specs=[blk, blk, blk, blk],
        out_shape=[jax.ShapeDtypeStruct((R, C), F32)] * 4, compiler_params=_cp("parallel"),
    )(w, m, v, parts)


def _colsum(name, a):
    T, N = a.shape
    tm = _tile(T, 256)

    def body(a_ref, o_ref):
        @pl.when(pl.program_id(0) == 0)
        def _():
            o_ref[...] = jnp.zeros_like(o_ref)

        o_ref[...] += jnp.sum(a_ref[...].astype(F32), axis=0, keepdims=True)

    return pl.pallas_call(
        body, name=name, grid=(T // tm,),
        in_specs=[pl.BlockSpec((tm, N), lambda i: (i, 0))],
        out_specs=pl.BlockSpec((1, N), lambda i: (0, 0)),
        out_shape=jax.ShapeDtypeStruct((1, N), F32), compiler_params=_cp("arbitrary"),
    )(a)


def _proj_in(name, h, wg, j, bias):
    T, K = h.shape
    n = wg.shape[-1]
    tm = _tile(T, 512)

    def body(a_ref, w_ref, b_ref, o_ref):
        o_ref[...] = _dot(a_ref[...], w_ref[...]) + b_ref[...]

    return pl.pallas_call(
        body, name=name, grid=(N_DEV, T // tm),
        in_specs=[pl.BlockSpec((tm, K), lambda s, i: (i, 0)),
                  pl.BlockSpec((None, None, K, n), lambda s, i: (s, j, 0, 0)),
                  pl.BlockSpec((1, n), lambda s, i: (0, s))],
        out_specs=pl.BlockSpec((tm, n), lambda s, i: (i, s)),
        out_shape=jax.ShapeDtypeStruct((T, N_DEV * n), F32), compiler_params=_cp("parallel", "parallel"),
    )(h, wg, bias)


def _ffn_in(name, h, wg_gate, wg_up, l):
    T, K = h.shape
    n = wg_gate.shape[-1]
    tm = _tile(T, 512)

    def body(a_ref, wgt_ref, wup_ref, g_ref, u_ref, hid_ref):
        a = a_ref[...]
        g = _dot(a, wgt_ref[...])
        u = _dot(a, wup_ref[...])
        g_ref[...] = g
        u_ref[...] = u
        hid_ref[...] = (_silu(g) * u).astype(hid_ref.dtype)

    wspec = pl.BlockSpec((None, None, K, n), lambda s, i: (s, l, 0, 0))
    ospec = pl.BlockSpec((None, tm, n), lambda s, i: (s, i, 0))
    return pl.pallas_call(
        body, name=name, grid=(N_DEV, T // tm),
        in_specs=[pl.BlockSpec((tm, K), lambda s, i: (i, 0)), wspec, wspec],
        out_specs=[ospec, ospec, ospec],
        out_shape=[jax.ShapeDtypeStruct((N_DEV, T, n), F32), jax.ShapeDtypeStruct((N_DEV, T, n), F32),
                   jax.ShapeDtypeStruct((N_DEV, T, n), MXU_DTYPE)],
        compiler_params=_cp("parallel", "parallel"),
    )(h, wg_gate, wg_up)


def _a_spec(a, tm, k):
    if a.ndim == 2:
        return pl.BlockSpec((tm, k), lambda i, s: (i, s))
    return pl.BlockSpec((None, tm, k), lambda i, s: (s, i, 0))


def _proj_out(name, a, wg, j, res):
    k, N = wg.shape[-2:]
    T = res.shape[0]
    tm = _tile(T, 512)

    def body(a_ref, w_ref, r_ref, o_ref):
        p = _dot(a_ref[...], w_ref[...])

        @pl.when(pl.program_id(1) == 0)
        def _():
            o_ref[...] = r_ref[...] + p

        @pl.when(pl.program_id(1) > 0)
        def _():
            o_ref[...] += p

    return pl.pallas_call(
        body, name=name, grid=(T // tm, N_DEV),
        in_specs=[_a_spec(a, tm, k), pl.BlockSpec((None, None, k, N), lambda i, s: (s, j, 0, 0)),
                  pl.BlockSpec((tm, N), lambda i, s: (i, 0))],
        out_specs=pl.BlockSpec((tm, N), lambda i, s: (i, 0)),
        out_shape=jax.ShapeDtypeStruct((T, N), F32), compiler_params=_cp("parallel", "arbitrary"),
    )(a, wg, res)


def _bwd_in(name, das, wgs, j):
    K, n = wgs[0].shape[-2:]
    T = das[0].shape[-2]
    tm = _tile(T, 512)
    npair = len(das)

    def body(*refs):
        o_ref = refs[-1]
        p = _dot_nt(refs[0][...], refs[npair][...])
        for q in range(1, npair):
            p = p + _dot_nt(refs[q][...], refs[npair + q][...])

        @pl.when(pl.program_id(1) == 0)
        def _():
            o_ref[...] = p

        @pl.when(pl.program_id(1) > 0)
        def _():
            o_ref[...] += p

    return pl.pallas_call(
        body, name=name, grid=(T // tm, N_DEV),
        in_specs=[_a_spec(a, tm, n) for a in das]
        + [pl.BlockSpec((None, None, K, n), lambda i, s: (s, j, 0, 0)) for _ in wgs],
        out_specs=pl.BlockSpec((tm, K), lambda i, s: (i, 0)),
        out_shape=jax.ShapeDtypeStruct((T, K), F32), compiler_params=_cp("parallel", "arbitrary"),
    )(*das, *wgs)


def _bwd_out(name, dx, wg, j):
    k, N = wg.shape[-2:]
    T = dx.shape[0]
    tm = _tile(T, 512)

    def body(a_ref, w_ref, o_ref):
        o_ref[...] = _dot_nt(a_ref[...], w_ref[...])

    return pl.pallas_call(
        body, name=name, grid=(N_DEV, T // tm),
        in_specs=[pl.BlockSpec((tm, N), lambda s, i: (i, 0)),
                  pl.BlockSpec((None, None, k, N), lambda s, i: (s, j, 0, 0))],
        out_specs=pl.BlockSpec((tm, k), lambda s, i: (i, s)),
        out_shape=jax.ShapeDtypeStruct((T, N_DEV * k), F32), compiler_params=_cp("parallel", "parallel"),
    )(dx, wg)


def _ffn_bwd_hidden(name, dx, wg_down, l, gate, up):
    n, N = wg_down.shape[-2:]
    T = dx.shape[0]
    tm = _tile(T, 512)

    def body(a_ref, w_ref, g_ref, u_ref, dg_ref, du_ref):
        dh = _dot_nt(a_ref[...], w_ref[...])
        g = g_ref[...]
        dg_ref[...] = (dh * u_ref[...] * _dsilu(g)).astype(dg_ref.dtype)
        du_ref[...] = (dh * _silu(g)).astype(du_ref.dtype)

    sm = pl.BlockSpec((None, tm, n), lambda s, i: (s, i, 0))
    return pl.pallas_call(
        body, name=name, grid=(N_DEV, T // tm),
        in_specs=[pl.BlockSpec((tm, N), lambda s, i: (i, 0)),
                  pl.BlockSpec((None, None, n, N), lambda s, i: (s, l, 0, 0)), sm, sm],
        out_specs=[sm, sm],
        out_shape=[jax.ShapeDtypeStruct((N_DEV, T, n), MXU_DTYPE)] * 2,
        compiler_params=_cp("parallel", "parallel"),
    )(dx, wg_down, gate, up)


def _wgrad(name, a, c, buf, j):
    _, _, rows, cols = buf.shape
    T = a.shape[-2]
    tk = _tile(T, 512)
    nk = T // tk

    def spec(z, w):
        if z.ndim == 3:
            return pl.BlockSpec((None, tk, w), lambda s, k: (s, k, 0))
        if z.shape[1] == w:
            return pl.BlockSpec((tk, w), lambda s, k: (k, 0))
        return pl.BlockSpec((tk, w), lambda s, k: (k, s))

    def body(a_ref, c_ref, buf_ref, o_ref, acc_ref):
        del buf_ref
        k = pl.program_id(1)
        p = _dot_tn(a_ref[...], c_ref[...])

        @pl.when(k == 0)
        def _():
            acc_ref[...] = p

        @pl.when(k > 0)
        def _():
            acc_ref[...] += p

        @pl.when(k == nk - 1)
        def _():
            o_ref[...] = acc_ref[...].astype(o_ref.dtype)

    return pl.pallas_call(
        body, name=name, grid=(N_DEV, nk),
        in_specs=[spec(a, rows), spec(c, cols), ANY],
        out_specs=pl.BlockSpec((None, None, rows, cols), lambda s, k: (s, j, 0, 0)),
        out_shape=jax.ShapeDtypeStruct(buf.shape, buf.dtype),
        scratch_shapes=[pltpu.VMEM((rows, cols), F32)],
        input_output_aliases={2: 0}, compiler_params=_cp("parallel", "arbitrary"),
    )(a, c, buf)


def _shift_down(cur, prev8, sh):
    n = cur.shape[0]
    rolled = pltpu.roll(cur, sh, 0)
    top = jnp.where(_rows(8) < sh, pltpu.roll(prev8, sh, 0), rolled[0:8])
    return jnp.concatenate([top, rolled[8:n]], axis=0)


def _shift_up(cur, next8, sh):
    n = cur.shape[0]
    rolled = pltpu.roll(cur, n - sh, 0)
    bot = jnp.where(_rows(8) >= 8 - sh, pltpu.roll(next8, 8 - sh, 0), rolled[n - 8:n])
    return jnp.concatenate([rolled[0:n - 8], bot], axis=0)


def _lru_gate_terms(r, lam):
    sp = _softplus(-lam)
    la = -LRU_C * r * sp
    a = jnp.exp(la)
    m2 = _one_minus_exp(2.0 * la)
    return sp, la, a, m2


def _lru_fwd(name, proj, conv_w, conv_b, wa, ba, wx, bx, lam):
    T = proj.shape[0]
    H, hd, _ = wa.shape
    W = H * hd
    K = conv_w.shape[0]
    tb = _tile(T, 256)

    def body(xin_ref, gate_ref, cw_ref, cb_ref, wa_ref, ba_ref, wx_ref, bx_ref, lam_ref,
             ya_ref, xc_ref, r_ref, i_ref, hs_ref, tail_ref, hprev_ref):
        blk = pl.program_id(0)

        @pl.when(blk == 0)
        def _():
            tail_ref[...] = jnp.zeros_like(tail_ref)
            hprev_ref[...] = jnp.zeros_like(hprev_ref)

        xin = xin_ref[...]
        prev8 = tail_ref[...]
        xc = cw_ref[K - 1:K, :] * xin
        for sh in range(1, K):
            xc = xc + cw_ref[K - 1 - sh:K - sh, :] * _shift_down(xin, prev8, sh)
        xc = xc + cb_ref[...]
        tail_ref[...] = xin[tb - 8:tb]
        xc_ref[...] = xc
        for h in range(H):
            cs = slice(h * hd, (h + 1) * hd)
            xh = xc[:, cs]
            r_ref[:, cs] = _sigmoid(_dot(xh, wa_ref[h]) + ba_ref[:, cs])
            i_ref[:, cs] = _sigmoid(_dot(xh, wx_ref[h]) + bx_ref[:, cs])
        r = r_ref[...]
        _, _, a, m2 = _lru_gate_terms(r, lam_ref[...])
        row = _rows(tb)
        mult = jnp.where((row == 0) & (blk == 0), 1.0, jnp.sqrt(jnp.maximum(m2, 0.0)))
        u = mult * i_ref[...] * xc
        d = 1
        while d < tb:
            keep = row >= d
            u = a * jnp.where(keep, pltpu.roll(u, d, 0), 0.0) + u
            a = a * jnp.where(keep, pltpu.roll(a, d, 0), 1.0)
            d *= 2
        hs = u + a * hprev_ref[...]
        hprev_ref[...] = hs[tb - 1:tb]
        hs_ref[...] = hs
        ya_ref[...] = (hs * _gelu(gate_ref[...])).astype(ya_ref.dtype)

    full = lambda shape: pl.BlockSpec(shape, lambda i: tuple(0 for _ in shape))
    blk = pl.BlockSpec((tb, W), lambda i: (i, 0))
    return pl.pallas_call(
        body, name=name, grid=(T // tb,),
        in_specs=[pl.BlockSpec((tb, W), lambda i: (i, 0)), pl.BlockSpec((tb, W), lambda i: (i, 1)),
                  full((K, W)), full((1, W)), full((H, hd, hd)), full((1, W)), full((H, hd, hd)),
                  full((1, W)), full((1, W))],
        out_specs=[blk] * 5,
        out_shape=[jax.ShapeDtypeStruct((T, W), MXU_DTYPE)] + [jax.ShapeDtypeStruct((T, W), F32)] * 4,
        scratch_shapes=[pltpu.VMEM((8, W), F32), pltpu.VMEM((1, W), F32)],
        compiler_params=_cp("arbitrary"),
    )(proj, proj, conv_w, conv_b, wa, ba, wx, bx, lam)


def _lru_bwd(name, proj, dy, xc, r, ig, hs, conv_w, wa, wx, lam):
    T = proj.shape[0]
    H, hd, _ = wa.shape
    W = H * hd
    K = conv_w.shape[0]
    tb = _tile(T, 256)
    nb = T // tb
    t8 = tb // 8

    def body(xin_ref, xprev_ref, gate_ref, dy_ref, xc_ref, r_ref, i_ref, hs_ref, hsprev_ref,
             cw_ref, wa_ref, wx_ref, lam_ref,
             dp_ref, dcw_ref, dcb_ref, dwa_ref, dba_ref, dwx_ref, dbx_ref, dlam_ref,
             cdh_ref, ca_ref, cdxc_ref, dxc_ref):
        step = pl.program_id(0)
        blk = nb - 1 - step

        @pl.when(step == 0)
        def _():
            for ref in (dcw_ref, dcb_ref, dwa_ref, dba_ref, dwx_ref, dbx_ref, dlam_ref,
                        cdh_ref, ca_ref, cdxc_ref):
                ref[...] = jnp.zeros_like(ref)

        row = _rows(tb)
        first = blk == 0
        gate = gate_ref[...]
        dy_a = dy_ref[...]
        hsv = hs_ref[...]
        dp_ref[:, W:2 * W] = (dy_a * hsv * _dgelu(gate)).astype(dp_ref.dtype)
        d_hs = dy_a * _gelu(gate)
        lam = lam_ref[...]
        rv = r_ref[...]
        sp, la, a, m2 = _lru_gate_terms(rv, lam)
        an = jnp.where(row == tb - 1, ca_ref[...], pltpu.roll(a, tb - 1, 0))
        u = d_hs
        d = 1
        while d < tb:
            keep = row < tb - d
            u = an * jnp.where(keep, pltpu.roll(u, tb - d, 0), 0.0) + u
            an = an * jnp.where(keep, pltpu.roll(an, tb - d, 0), 1.0)
            d *= 2
        dh = u + an * cdh_ref[...]
        cdh_ref[...] = dh[0:1]
        ca_ref[...] = a[0:1]
        hlast = jnp.where(first, 0.0, hsprev_ref[7:8, :])
        hprev = jnp.where(row == 0, hlast, pltpu.roll(hsv, 1, 0))
        da = dh * hprev
        xcv = xc_ref[...]
        iv = i_ref[...]
        t0 = (row == 0) & first
        mult = jnp.sqrt(jnp.maximum(m2, 0.0))
        mult_eff = jnp.where(t0, 1.0, mult)
        d_mult = dh * iv * xcv
        d_i = dh * mult_eff * xcv
        dxc = dh * mult_eff * iv
        e2 = 1.0 - m2
        d_la = da * a + jnp.where(t0 | (m2 <= 0.0), 0.0, -d_mult * e2 / jnp.where(m2 > 0.0, mult, 1.0))
        d_r = d_la * (-LRU_C * sp)
        dlam_ref[...] += jnp.sum(d_la * (-LRU_C * rv), axis=0, keepdims=True) * (-_sigmoid(-lam))
        d_zr = d_r * rv * (1.0 - rv)
        d_zi = d_i * iv * (1.0 - iv)
        dba_ref[...] += jnp.sum(d_zr, axis=0, keepdims=True)
        dbx_ref[...] += jnp.sum(d_zi, axis=0, keepdims=True)
        for h in range(H):
            cs = slice(h * hd, (h + 1) * hd)
            xh = xcv[:, cs]
            zr, zi = d_zr[:, cs], d_zi[:, cs]
            dwa_ref[h] += _dot_tn(xh, zr)
            dwx_ref[h] += _dot_tn(xh, zi)
            dxc_ref[:, cs] = dxc[:, cs] + _dot_nt(zr, wa_ref[h]) + _dot_nt(zi, wx_ref[h])
        dxc = dxc_ref[...]
        dcb_ref[...] += jnp.sum(dxc, axis=0, keepdims=True)
        xin = xin_ref[...]
        prev8 = jnp.where(first, 0.0, xprev_ref[...])
        next8 = cdxc_ref[...]
        dxin = cw_ref[K - 1:K, :] * dxc
        dcw_ref[K - 1:K, :] += jnp.sum(dxc * xin, axis=0, keepdims=True)
        for sh in range(1, K):
            dxin = dxin + cw_ref[K - 1 - sh:K - sh, :] * _shift_up(dxc, next8, sh)
            dcw_ref[K - 1 - sh:K - sh, :] += jnp.sum(dxc * _shift_down(xin, prev8, sh), axis=0, keepdims=True)
        cdxc_ref[...] = dxc[0:8]
        dp_ref[:, 0:W] = dxin.astype(dp_ref.dtype)

    full = lambda shape: pl.BlockSpec(shape, lambda i: tuple(0 for _ in shape))
    cur = lambda col: pl.BlockSpec((tb, W), lambda i: (nb - 1 - i, col))
    prev = pl.BlockSpec((8, W), lambda i: (jnp.maximum((nb - 1 - i) * t8 - 1, 0), 0))
    return pl.pallas_call(
        body, name=name, grid=(nb,),
        in_specs=[cur(0), prev, cur(1), cur(0), cur(0), cur(0), cur(0), cur(0), prev,
                  full((K, W)), full((H, hd, hd)), full((H, hd, hd)), full((1, W))],
        out_specs=[pl.BlockSpec((tb, 2 * W), lambda i: (nb - 1 - i, 0)), full((K, W)), full((1, W)),
                   full((H, hd, hd)), full((1, W)), full((H, hd, hd)), full((1, W)), full((1, W))],
        out_shape=[jax.ShapeDtypeStruct((T, 2 * W), MXU_DTYPE), jax.ShapeDtypeStruct((K, W), F32),
                   jax.ShapeDtypeStruct((1, W), F32), jax.ShapeDtypeStruct((H, hd, hd), F32),
                   jax.ShapeDtypeStruct((1, W), F32), jax.ShapeDtypeStruct((H, hd, hd), F32),
                   jax.ShapeDtypeStruct((1, W), F32), jax.ShapeDtypeStruct((1, W), F32)],
        scratch_shapes=[pltpu.VMEM((1, W), F32), pltpu.VMEM((1, W), F32), pltpu.VMEM((8, W), F32),
                        pltpu.VMEM((tb, W), F32)],
        compiler_params=_cp("arbitrary"),
    )(proj, proj, proj, dy, xc, r, ig, hs, hs, conv_w, wa, wx, lam)


def _chunk_cumsum(g, c):
    n = g.shape[0]
    rc = _rows(n) & (c - 1)
    d = 1
    while d < c:
        g = g + jnp.where(rc >= d, pltpu.roll(g, d, 0), 0.0)
        d *= 2
    return g


def _chunk_rcumsum(g, c):
    n = g.shape[0]
    rc = _rows(n) & (c - 1)
    d = 1
    while d < c:
        g = g + jnp.where(rc < c - d, pltpu.roll(g, n - d, 0), 0.0)
        d *= 2
    return g


def _hgrn_pointwise(qr, fr, lb):
    qf = _silu(qr)
    sig = _sigmoid(fr)
    fg = lb + (1.0 - lb) * sig
    gl = jnp.log(jnp.maximum(fg, F_FLOOR))
    kk = (1.0 - lb) * (1.0 - sig)
    return qf, sig, fg, gl, kk


def _hgrn_fwd(name, proj, lb, norm_g):
    T = proj.shape[0]
    W = lb.shape[1]
    H = HGRN_HEADS
    dk = W // H
    c = HGRN_SUB
    R = _tile(T, 128)
    nck = R // c

    def body(q_ref, f_ref, v_ref, g_ref, lb_ref, ng_ref, yb_ref, o_ref, sall_ref,
             st_ref, qe_ref, ke_ref, acc_ref):
        @pl.when(pl.program_id(0) == 0)
        def _():
            st_ref[...] = jnp.zeros_like(st_ref)

        qf, _, _, gl, kk = _hgrn_pointwise(q_ref[...], f_ref[...], lb_ref[...])
        b = _chunk_cumsum(gl, c)
        rc = _rows(R) & (c - 1)
        for h in range(H):
            cs = slice(h * dk, (h + 1) * dk)
            qh, kh, bh, vh = qf[:, cs], kk[:, cs], b[:, cs], v_ref[:, cs]
            acc = jnp.sum(qh * kh, axis=1, keepdims=True) * vh
            for d in range(1, c):
                ok = rc >= d
                e = jnp.exp(jnp.where(ok, bh - pltpu.roll(bh, d, 0), 0.0))
                s = jnp.sum(qh * pltpu.roll(kh, d, 0) * e, axis=1, keepdims=True)
                acc = acc + jnp.where(ok, s, 0.0) * pltpu.roll(vh, d, 0)
            acc_ref[:, cs] = acc
        qe_ref[...] = qf * jnp.exp(b)
        for ci in range(nck):
            rs = slice(ci * c, (ci + 1) * c)
            bl = b[ci * c + c - 1:ci * c + c, :]
            ke_ref[rs, :] = kk[rs, :] * jnp.exp(bl - b[rs, :])
            ebl = jnp.exp(bl)
            for h in range(H):
                cs = slice(h * dk, (h + 1) * dk)
                st = st_ref[h]
                sall_ref[ci, h] = st
                o_ref[rs, cs] = acc_ref[rs, cs] + _dot_nt(qe_ref[rs, cs], st)
                st_ref[h] = st * ebl[:, cs] + _dot_tn(v_ref[rs, cs], ke_ref[rs, cs])
        ng = ng_ref[...]
        gg = g_ref[...]
        for h in range(H):
            cs = slice(h * dk, (h + 1) * dk)
            oh = o_ref[:, cs]
            rr = lax.rsqrt(jnp.mean(oh * oh, axis=1, keepdims=True) + EPS)
            yb_ref[:, cs] = ((oh * rr) * ng[:, cs] * _silu(gg[:, cs])).astype(yb_ref.dtype)

    full = lambda shape: pl.BlockSpec(shape, lambda i: tuple(0 for _ in shape))
    col = lambda k: pl.BlockSpec((R, W), lambda i: (i, k))
    blk = pl.BlockSpec((R, W), lambda i: (i, 0))
    return pl.pallas_call(
        body, name=name, grid=(T // R,),
        in_specs=[col(2), col(3), col(4), col(5), full((1, W)), full((1, W))],
        out_specs=[blk, blk, pl.BlockSpec((nck, H, dk, dk), lambda i: (i, 0, 0, 0))],
        out_shape=[jax.ShapeDtypeStruct((T, W), MXU_DTYPE), jax.ShapeDtypeStruct((T, W), F32),
                   jax.ShapeDtypeStruct((T // c, H, dk, dk), F32)],
        scratch_shapes=[pltpu.VMEM((H, dk, dk), F32), pltpu.VMEM((R, W), F32), pltpu.VMEM((R, W), F32),
                        pltpu.VMEM((R, W), F32)],
        compiler_params=_cp("arbitrary"),
    )(proj, proj, proj, proj, lb, norm_g)


def _hgrn_bwd(name, proj, dy, o, sall, lb, norm_g):
    T = proj.shape[0]
    W = lb.shape[1]
    H = HGRN_HEADS
    dk = W // H
    c = HGRN_SUB
    R = _tile(T, 128)
    nck = R // c
    nb = T // R

    def body(q_ref, f_ref, v_ref, g_ref, dy_ref, o_ref, sall_ref, lb_ref, ng_ref,
             dp_ref, dlb_ref, dng_ref,
             dst_ref, do_ref, dq_ref, dk_ref, dv_ref, ex_ref, qe_ref, ke_ref):
        @pl.when(pl.program_id(0) == 0)
        def _():
            dst_ref[...] = jnp.zeros_like(dst_ref)
            dlb_ref[...] = jnp.zeros_like(dlb_ref)
            dng_ref[...] = jnp.zeros_like(dng_ref)

        lbv = lb_ref[...]
        qr = q_ref[...]
        qf, sig, fg, gl, kk = _hgrn_pointwise(qr, f_ref[...], lbv)
        b = _chunk_cumsum(gl, c)
        rc = _rows(R) & (c - 1)
        ng = ng_ref[...]
        gg = g_ref[...]
        dyv = dy_ref[...]
        sg = _silu(gg)
        for h in range(H):
            cs = slice(h * dk, (h + 1) * dk)
            oh = o_ref[:, cs]
            rr = lax.rsqrt(jnp.mean(oh * oh, axis=1, keepdims=True) + EPS)
            ohat = oh * rr
            dyh = dyv[:, cs]
            dp_ref[:, 3 * W + h * dk:3 * W + (h + 1) * dk] = (
                dyh * ohat * ng[:, cs] * _dsilu(gg[:, cs])).astype(dp_ref.dtype)
            t = dyh * sg[:, cs]
            dng_ref[:, cs] += jnp.sum(t * ohat, axis=0, keepdims=True)
            dohat = t * ng[:, cs]
            do_ref[:, cs] = rr * (dohat - ohat * jnp.mean(dohat * ohat, axis=1, keepdims=True))
        for h in range(H):
            cs = slice(h * dk, (h + 1) * dk)
            qh, kh, bh, vh, doh = qf[:, cs], kk[:, cs], b[:, cs], v_ref[:, cs], do_ref[:, cs]
            da0 = jnp.sum(doh * vh, axis=1, keepdims=True)
            a0 = jnp.sum(qh * kh, axis=1, keepdims=True)
            dq = da0 * kh
            dkk = da0 * qh
            dv = a0 * doh
            for d in range(1, c):
                ok = rc >= d
                e = jnp.exp(jnp.where(ok, bh - pltpu.roll(bh, d, 0), 0.0))
                kr = pltpu.roll(kh, d, 0)
                da = jnp.where(ok, jnp.sum(doh * pltpu.roll(vh, d, 0), axis=1, keepdims=True), 0.0)
                aa = jnp.where(ok, jnp.sum(qh * kr * e, axis=1, keepdims=True), 0.0)
                dq = dq + da * kr * e
                dkk = dkk + pltpu.roll(da * qh * e, R - d, 0)
                dv = dv + pltpu.roll(aa * doh, R - d, 0)
            dq_ref[:, cs] = dq
            dk_ref[:, cs] = dkk
            dv_ref[:, cs] = dv
        eb = jnp.exp(b)
        qe_ref[...] = qf * eb
        ex_ref[...] = jnp.zeros_like(ex_ref)
        for ci in reversed(range(nck)):
            rs = slice(ci * c, (ci + 1) * c)
            bl = b[ci * c + c - 1:ci * c + c, :]
            ebl_rows = jnp.exp(bl - b[rs, :])
            ke_ref[rs, :] = kk[rs, :] * ebl_rows
            ebl = jnp.exp(bl)
            for h in range(H):
                cs = slice(h * dk, (h + 1) * dk)
                st0 = sall_ref[ci, h]
                dst1 = dst_ref[h]
                doc = do_ref[rs, cs]
                vc = v_ref[rs, cs]
                dq_ref[rs, cs] += _dot(doc, st0) * eb[rs, cs]
                dv_ref[rs, cs] += _dot_nt(ke_ref[rs, cs], dst1)
                dki = _dot(vc, dst1) * ebl_rows[:, cs]
                dk_ref[rs, cs] += dki
                ex_ref[ci * c + c - 1:ci * c + c, cs] = (
                    jnp.sum(dki * kk[rs, cs], axis=0, keepdims=True)
                    + ebl[:, cs] * jnp.sum(st0 * dst1, axis=0, keepdims=True))
                dst_ref[h] = dst1 * ebl[:, cs] + _dot_tn(doc, qe_ref[rs, cs])
        dq = dq_ref[...]
        dkk = dk_ref[...]
        db = qf * dq - kk * dkk + ex_ref[...]
        dgl = _chunk_rcumsum(db, c)
        dfg = jnp.where(fg > F_FLOOR, dgl / jnp.maximum(fg, F_FLOOR), 0.0)
        dsig = (dfg - dkk) * (1.0 - lbv)
        dlb_ref[...] += jnp.sum((dfg - dkk) * (1.0 - sig), axis=0, keepdims=True)
        dp_ref[:, 0:W] = (dq * _dsilu(qr)).astype(dp_ref.dtype)
        dp_ref[:, W:2 * W] = (dsig * sig * (1.0 - sig)).astype(dp_ref.dtype)
        dp_ref[:, 2 * W:3 * W] = dv_ref[...].astype(dp_ref.dtype)

    full = lambda shape: pl.BlockSpec(shape, lambda i: tuple(0 for _ in shape))
    col = lambda k: pl.BlockSpec((R, W), lambda i: (nb - 1 - i, k))
    scr = pltpu.VMEM((R, W), F32)
    return pl.pallas_call(
        body, name=name, grid=(nb,),
        in_specs=[col(2), col(3), col(4), col(5), col(1), col(0),
                  pl.BlockSpec((nck, H, dk, dk), lambda i: (nb - 1 - i, 0, 0, 0)), full((1, W)), full((1, W))],
        out_specs=[pl.BlockSpec((R, 4 * W), lambda i: (nb - 1 - i, 0)), full((1, W)), full((1, W))],
        out_shape=[jax.ShapeDtypeStruct((T, 4 * W), MXU_DTYPE), jax.ShapeDtypeStruct((1, W), F32),
                   jax.ShapeDtypeStruct((1, W), F32)],
        scratch_shapes=[pltpu.VMEM((H, dk, dk), F32), scr, scr, scr, scr, scr, scr, scr],
        compiler_params=_cp("arbitrary"),
    )(proj, proj, proj, proj, dy, o, sall, lb, norm_g)


ODD_HALO = 32


def _odd_fwd(name, proj, sc_w, cf_w, cf_b, ln_g, ln_b):
    T = proj.shape[0]
    W = sc_w.shape[1]
    K3, K31 = sc_w.shape[0], cf_w.shape[0]
    tb = _tile(T, 256)
    hb = tb // ODD_HALO
    n = tb + ODD_HALO

    def body(cur_ref, prev_ref, w3_ref, w31_ref, cb_ref, lg_ref, lbeta_ref, y_ref, d_ref):
        keep = (pl.program_id(0) > 0).astype(F32)
        sb = cur_ref[:, 0:W]
        p = cur_ref[:, W:2 * W] * cur_ref[:, 2 * W:3 * W]
        glu = cur_ref[:, 3 * W:4 * W] * _sigmoid(cur_ref[:, 4 * W:5 * W])
        p_prev = prev_ref[:, W:2 * W] * prev_ref[:, 2 * W:3 * W] * keep
        glu_prev = prev_ref[:, 3 * W:4 * W] * _sigmoid(prev_ref[:, 4 * W:5 * W]) * keep
        ext = jnp.concatenate([p_prev, p], axis=0)
        cp = w3_ref[K3 - 1:K3, :] * p
        for sh in range(1, K3):
            ext = pltpu.roll(ext, 1, 0)
            cp = cp + w3_ref[K3 - 1 - sh:K3 - sh, :] * ext[ODD_HALO:n]
        y_ref[:, 0:W] = (sb * cp).astype(y_ref.dtype)
        ext = jnp.concatenate([glu_prev, glu], axis=0)
        d = cb_ref[...] + w31_ref[K31 - 1:K31, :] * glu
        for sh in range(1, K31):
            ext = pltpu.roll(ext, 1, 0)
            d = d + w31_ref[K31 - 1 - sh:K31 - sh, :] * ext[ODD_HALO:n]
        d_ref[...] = d
        mu = jnp.mean(d, axis=1, keepdims=True)
        xc = d - mu
        rstd = lax.rsqrt(jnp.mean(xc * xc, axis=1, keepdims=True) + EPS)
        ln = (xc * rstd) * lg_ref[...] + lbeta_ref[...]
        y_ref[:, W:2 * W] = _silu(ln).astype(y_ref.dtype)

    full = lambda shape: pl.BlockSpec(shape, lambda i: tuple(0 for _ in shape))
    return pl.pallas_call(
        body, name=name, grid=(T // tb,),
        in_specs=[pl.BlockSpec((tb, 5 * W), lambda i: (i, 0)),
                  pl.BlockSpec((ODD_HALO, 5 * W), lambda i: (jnp.maximum(i * hb - 1, 0), 0)),
                  full((K3, W)), full((K31, W)), full((1, W)), full((1, W)), full((1, W))],
        out_specs=[pl.BlockSpec((tb, 2 * W), lambda i: (i, 0)), pl.BlockSpec((tb, W), lambda i: (i, 0))],
        out_shape=[jax.ShapeDtypeStruct((T, 2 * W), MXU_DTYPE), jax.ShapeDtypeStruct((T, W), F32)],
        compiler_params=_cp("parallel"),
    )(proj, proj, sc_w, cf_w, cf_b, ln_g, ln_b)


def _odd_bwd(name, proj, dy, dsave, sc_w, cf_w, ln_g, ln_b):
    T = proj.shape[0]
    W = sc_w.shape[1]
    K3, K31 = sc_w.shape[0], cf_w.shape[0]
    tb = _tile(T, 128)
    nb = T // tb
    hb = tb // ODD_HALO
    nh = T // ODD_HALO
    n = tb + ODD_HALO

    def body(cur_ref, prev_ref, next_ref, dy_ref, dyn_ref, d_ref, dn_ref,
             w3_ref, w31_ref, lg_ref, lbeta_ref,
             dp_ref, dw3_ref, dw31_ref, dcb_ref, dlg_ref, dlb_ref):
        i = pl.program_id(0)

        @pl.when(i == 0)
        def _():
            for ref in (dw3_ref, dw31_ref, dcb_ref, dlg_ref, dlb_ref):
                ref[...] = jnp.zeros_like(ref)

        keep_prev = (i > 0).astype(F32)
        keep_next = (i < nb - 1).astype(F32)
        sb = cur_ref[:, 0:W]
        scv = cur_ref[:, W:2 * W]
        svv = cur_ref[:, 2 * W:3 * W]
        cu = cur_ref[:, 3 * W:4 * W]
        sg = _sigmoid(cur_ref[:, 4 * W:5 * W])
        p = scv * svv
        glu = cu * sg
        p_prev = prev_ref[:, W:2 * W] * prev_ref[:, 2 * W:3 * W] * keep_prev
        glu_prev = prev_ref[:, 3 * W:4 * W] * _sigmoid(prev_ref[:, 4 * W:5 * W]) * keep_prev
        dext = jnp.concatenate([d_ref[...], dn_ref[...]], axis=0)
        dyd = jnp.concatenate([dy_ref[:, W:2 * W], dyn_ref[:, W:2 * W] * keep_next], axis=0)
        mu = jnp.mean(dext, axis=1, keepdims=True)
        xc = dext - mu
        rstd = lax.rsqrt(jnp.mean(xc * xc, axis=1, keepdims=True) + EPS)
        xh = xc * rstd
        lg = lg_ref[...]
        dln = dyd * _dsilu(xh * lg + lbeta_ref[...])
        dxh = dln * lg
        dd = rstd * (dxh - jnp.mean(dxh, axis=1, keepdims=True)
                     - xh * jnp.mean(dxh * xh, axis=1, keepdims=True))
        dlg_ref[...] += jnp.sum((dln * xh)[0:tb], axis=0, keepdims=True)
        dlb_ref[...] += jnp.sum(dln[0:tb], axis=0, keepdims=True)
        ddc = dd[0:tb]
        dcb_ref[...] += jnp.sum(ddc, axis=0, keepdims=True)
        dglu = w31_ref[K31 - 1:K31, :] * ddc
        ext = jnp.concatenate([glu_prev, glu], axis=0)
        dw31_ref[K31 - 1:K31, :] += jnp.sum(ddc * glu, axis=0, keepdims=True)
        up = dd
        for sh in range(1, K31):
            up = pltpu.roll(up, n - 1, 0)
            ext = pltpu.roll(ext, 1, 0)
            dglu = dglu + w31_ref[K31 - 1 - sh:K31 - sh, :] * up[0:tb]
            dw31_ref[K31 - 1 - sh:K31 - sh, :] += jnp.sum(ddc * ext[ODD_HALO:n], axis=0, keepdims=True)
        dp_ref[:, 3 * W:4 * W] = (dglu * sg).astype(dp_ref.dtype)
        dp_ref[:, 4 * W:5 * W] = (dglu * cu * sg * (1.0 - sg)).astype(dp_ref.dtype)
        dyc = dy_ref[:, 0:W]
        dcp = jnp.concatenate([dyc * sb, dyn_ref[:, 0:W] * next_ref[:, 0:W] * keep_next], axis=0)
        dcpc = dcp[0:tb]
        ext = jnp.concatenate([p_prev, p], axis=0)
        cp = w3_ref[K3 - 1:K3, :] * p
        dpp = w3_ref[K3 - 1:K3, :] * dcpc
        dw3_ref[K3 - 1:K3, :] += jnp.sum(dcpc * p, axis=0, keepdims=True)
        up = dcp
        for sh in range(1, K3):
            up = pltpu.roll(up, n - 1, 0)
            ext = pltpu.roll(ext, 1, 0)
            shifted = ext[ODD_HALO:n]
            cp = cp + w3_ref[K3 - 1 - sh:K3 - sh, :] * shifted
            dpp = dpp + w3_ref[K3 - 1 - sh:K3 - sh, :] * up[0:tb]
            dw3_ref[K3 - 1 - sh:K3 - sh, :] += jnp.sum(dcpc * shifted, axis=0, keepdims=True)
        dp_ref[:, 0:W] = (dyc * cp).astype(dp_ref.dtype)
        dp_ref[:, W:2 * W] = (dpp * svv).astype(dp_ref.dtype)
        dp_ref[:, 2 * W:3 * W] = (dpp * scv).astype(dp_ref.dtype)

    full = lambda shape: pl.BlockSpec(shape, lambda i: tuple(0 for _ in shape))
    prev_map = lambda i: (jnp.maximum(i * hb - 1, 0), 0)
    next_map = lambda i: (jnp.minimum((i + 1) * hb, nh - 1), 0)
    return pl.pallas_call(
        body, name=name, grid=(nb,),
        in_specs=[pl.BlockSpec((tb, 5 * W), lambda i: (i, 0)),
                  pl.BlockSpec((ODD_HALO, 5 * W), prev_map), pl.BlockSpec((ODD_HALO, 5 * W), next_map),
                  pl.BlockSpec((tb, 2 * W), lambda i: (i, 0)), pl.BlockSpec((ODD_HALO, 2 * W), next_map),
                  pl.BlockSpec((tb, W), lambda i: (i, 0)), pl.BlockSpec((ODD_HALO, W), next_map),
                  full((K3, W)), full((K31, W)), full((1, W)), full((1, W))],
        out_specs=[pl.BlockSpec((tb, 5 * W), lambda i: (i, 0)), full((K3, W)), full((K31, W)),
                   full((1, W)), full((1, W)), full((1, W))],
        out_shape=[jax.ShapeDtypeStruct((T, 5 * W), MXU_DTYPE), jax.ShapeDtypeStruct((K3, W), F32),
                   jax.ShapeDtypeStruct((K31, W), F32)] + [jax.ShapeDtypeStruct((1, W), F32)] * 3,
        compiler_params=_cp("arbitrary"),
    )(proj, proj, proj, dy, dy, dsave, dsave, sc_w, cf_w, ln_g, ln_b)


def _lower_bounds(logits):
    sm = jax.nn.softmax(logits.astype(F32), axis=0)
    return jnp.cumsum(sm, axis=0) - sm[0]


def _pack_rows(arrays):
    flat = jnp.concatenate([a.reshape(-1) for a in arrays])
    pad = (-flat.shape[0]) % (8 * 128)
    return jnp.pad(flat, (0, pad)).reshape(-1, 128)


def _unpack_rows(packed, shapes):
    flat = packed.reshape(-1)
    out, off = [], 0
    for s in shapes:
        sz = math.prod(s)
        out.append(flat[off:off + sz].reshape(s))
        off += sz
    return out


def _shards_last(a):
    n = a.shape[-1] // N_DEV
    return jnp.moveaxis(a.reshape(a.shape[:-1] + (N_DEV, n)), -2, 0)


def _unshard_last(a):
    a = jnp.moveaxis(a, 0, -2)
    return a.reshape(a.shape[:-2] + (a.shape[-2] * a.shape[-1],))


BIG = ("ev_w_in", "ev_w_out", "od_w_in", "od_w_out", "ffn_w_gate", "ffn_w_up", "ffn_w_down")
SMALL_SHARDED = ("lru_conv_w", "od_b_in", "sc_conv_w", "cf_conv_w", "cf_conv_b", "cf_ln_g", "cf_ln_b")
SMALL_REPL = ("ln_mix_g", "ln_ffn_g", "ln_final_g", "ev_b_in", "lru_conv_b", "lru_wa", "lru_ba", "lru_wx",
              "lru_bx", "lru_lambda", "hgrn_lb_logits", "hgrn_norm_g")
WEIGHTS = ("ln_mix_g", "ln_ffn_g", "ln_final_g", "ev_w_in", "ev_b_in", "lru_conv_w", "lru_conv_b", "lru_wa",
           "lru_ba", "lru_wx", "lru_bx", "lru_lambda", "hgrn_lb_logits", "hgrn_norm_g", "ev_w_out", "od_w_in",
           "od_b_in", "sc_conv_w", "cf_conv_w", "cf_conv_b", "cf_ln_g", "cf_ln_b", "od_w_out", "ffn_w_gate",
           "ffn_w_up", "ffn_w_down")


def _local_step(x, tgt, p, wg):
    T, D = x.shape
    depth = p["ln_mix_g"].shape[0]
    lbs = _lower_bounds(p["hgrn_lb_logits"])
    row = lambda a: a.reshape(1, -1)
    saved = []
    for l in range(depth):
        j = l // 2
        s = {"x": x}
        h = _rmsnorm_fwd(f"norm_mix{l}", x, row(p["ln_mix_g"][l]))
        s["h"] = h
        if l % 2 == 0:
            proj = _proj_in(f"ev_in{l}", h, wg["ev_w_in"], j, row(p["ev_b_in"][j]))
            wa = p["lru_wa"][j].astype(MXU_DTYPE)
            wx = p["lru_wx"][j].astype(MXU_DTYPE)
            ya, xc, r, ig, hs = _lru_fwd(f"lru_fwd{l}", proj, p["lru_conv_w"][j], row(p["lru_conv_b"][j]),
                                         wa, row(p["lru_ba"][j]), wx, row(p["lru_bx"][j]),
                                         row(p["lru_lambda"][j]))
            yb, o, sall = _hgrn_fwd(f"hgrn_fwd{l}", proj, row(lbs[j]), row(p["hgrn_norm_g"][j]))
            y = jnp.concatenate([ya, yb], axis=1)
            s.update(proj=proj, xc=xc, r=r, ig=ig, hs=hs, o=o, sall=sall, wa=wa, wx=wx)
            x = _proj_out(f"ev_out{l}", y, wg["ev_w_out"], j, x)
        else:
            proj = _proj_in(f"od_in{l}", h, wg["od_w_in"], j, row(p["od_b_in"][j]))
            y, dsave = _odd_fwd(f"odd_fwd{l}", proj, p["sc_conv_w"][j], p["cf_conv_w"][j],
                                row(p["cf_conv_b"][j]), row(p["cf_ln_g"][j]), row(p["cf_ln_b"][j]))
            s.update(proj=proj, dsave=dsave)
            x = _proj_out(f"od_out{l}", y, wg["od_w_out"], j, x)
        s["y"] = y
        s["xmid"] = x
        h2 = _rmsnorm_fwd(f"norm_ffn{l}", x, row(p["ln_ffn_g"][l]))
        gate, up, hid = _ffn_in(f"ffn_in{l}", h2, wg["ffn_w_gate"], wg["ffn_w_up"], l)
        x = _proj_out(f"ffn_out{l}", hid, wg["ffn_w_down"], l, x)
        s.update(h2=h2, gate=gate, up=up, hid=hid)
        saved.append(s)

    loss, dx, dg_final = _loss_head("loss_head", x, row(p["ln_final_g"]), tgt)

    gbuf = {k: lax.empty(wg[k].shape, WIRE_DTYPE) for k in BIG}
    gs = {k: [None] * p[k].shape[0] for k in SMALL_REPL + SMALL_SHARDED if k not in ("ln_final_g", "hgrn_lb_logits")}
    d_lb = [None] * (depth // 2 + depth % 2)
    for l in reversed(range(depth)):
        j = l // 2
        s = saved[l]
        gbuf["ffn_w_down"] = _wgrad(f"ffn_dwd{l}", s["hid"], dx, gbuf["ffn_w_down"], l)
        dgate, dup = _ffn_bwd_hidden(f"ffn_bwd_hid{l}", dx, wg["ffn_w_down"], l, s["gate"], s["up"])
        gbuf["ffn_w_gate"] = _wgrad(f"ffn_dwg{l}", s["h2"], dgate, gbuf["ffn_w_gate"], l)
        gbuf["ffn_w_up"] = _wgrad(f"ffn_dwu{l}", s["h2"], dup, gbuf["ffn_w_up"], l)
        dh2 = _bwd_in(f"ffn_dh{l}", [dgate, dup], [wg["ffn_w_gate"], wg["ffn_w_up"]], l)
        dx, dg = _rmsnorm_bwd(f"norm_ffn_bwd{l}", s["xmid"], row(p["ln_ffn_g"][l]), dh2, dx)
        gs["ln_ffn_g"][l] = dg[0]
        if l % 2 == 0:
            gbuf["ev_w_out"] = _wgrad(f"ev_dwo{l}", s["y"], dx, gbuf["ev_w_out"], j)
            dy = _bwd_out(f"ev_dy{l}", dx, wg["ev_w_out"], j)
            dpa, d_cw, d_cb, d_wa, d_ba, d_wx, d_bx, d_lam = _lru_bwd(
                f"lru_bwd{l}", s["proj"], dy, s["xc"], s["r"], s["ig"], s["hs"], p["lru_conv_w"][j],
                s["wa"], s["wx"], row(p["lru_lambda"][j]))
            dph, dlb, dng = _hgrn_bwd(f"hgrn_bwd{l}", s["proj"], dy, s["o"], s["sall"], row(lbs[j]),
                                      row(p["hgrn_norm_g"][j]))
            dproj = jnp.concatenate([dpa, dph], axis=1)
            gs["lru_conv_w"][j], gs["lru_conv_b"][j] = d_cw, d_cb[0]
            gs["lru_wa"][j], gs["lru_ba"][j] = d_wa, d_ba.reshape(p["lru_ba"].shape[1:])
            gs["lru_wx"][j], gs["lru_bx"][j] = d_wx, d_bx.reshape(p["lru_bx"].shape[1:])
            gs["lru_lambda"][j], gs["hgrn_norm_g"][j] = d_lam[0], dng[0]
            d_lb[j] = dlb[0]
            gs["ev_b_in"][j] = _colsum(f"ev_db{l}", dproj)[0]
            gbuf["ev_w_in"] = _wgrad(f"ev_dwi{l}", s["h"], dproj, gbuf["ev_w_in"], j)
            dh = _bwd_in(f"ev_dh{l}", [dproj], [wg["ev_w_in"]], j)
        else:
            gbuf["od_w_out"] = _wgrad(f"od_dwo{l}", s["y"], dx, gbuf["od_w_out"], j)
            dy = _bwd_out(f"od_dy{l}", dx, wg["od_w_out"], j)
            dproj, d_w3, d_w31, d_cfb, d_lg, d_lbeta = _odd_bwd(
                f"odd_bwd{l}", s["proj"], dy, s["dsave"], p["sc_conv_w"][j], p["cf_conv_w"][j],
                row(p["cf_ln_g"][j]), row(p["cf_ln_b"][j]))
            gs["sc_conv_w"][j], gs["cf_conv_w"][j] = d_w3, d_w31
            gs["cf_conv_b"][j], gs["cf_ln_g"][j], gs["cf_ln_b"][j] = d_cfb[0], d_lg[0], d_lbeta[0]
            gs["od_b_in"][j] = _colsum(f"od_db{l}", dproj)[0]
            gbuf["od_w_in"] = _wgrad(f"od_dwi{l}", s["h"], dproj, gbuf["od_w_in"], j)
            dh = _bwd_in(f"od_dh{l}", [dproj], [wg["od_w_in"]], j)
        dx, dg = _rmsnorm_bwd(f"norm_mix_bwd{l}", s["x"], row(p["ln_mix_g"][l]), dh, dx)
        gs["ln_mix_g"][l] = dg[0]

    small = {k: jnp.stack(v) for k, v in gs.items()}
    small["ln_final_g"] = dg_final[0]
    _, lb_vjp = jax.vjp(_lower_bounds, p["hgrn_lb_logits"])
    small["hgrn_lb_logits"] = lb_vjp(jnp.stack(d_lb))[0]
    return loss, dx, gbuf, small


def kernel(x, ln_mix_g, ln_ffn_g, ln_final_g, ev_w_in, ev_b_in, lru_conv_w, lru_conv_b, lru_wa, lru_ba, lru_wx, lru_bx, lru_lambda, hgrn_lb_logits, hgrn_norm_g, ev_w_out, od_w_in, od_b_in, sc_conv_w, cf_conv_w, cf_conv_b, cf_ln_g, cf_ln_b, od_w_out, ffn_w_gate, ffn_w_up, ffn_w_down, loss_target, m_ln_mix_g, m_ln_ffn_g, m_ln_final_g, m_ev_w_in, m_ev_b_in, m_lru_conv_w, m_lru_conv_b, m_lru_wa, m_lru_ba, m_lru_wx, m_lru_bx, m_lru_lambda, m_hgrn_lb_logits, m_hgrn_norm_g, m_ev_w_out, m_od_w_in, m_od_b_in, m_sc_conv_w, m_cf_conv_w, m_cf_conv_b, m_cf_ln_g, m_cf_ln_b, m_od_w_out, m_ffn_w_gate, m_ffn_w_up, m_ffn_w_down, v_ln_mix_g, v_ln_ffn_g, v_ln_final_g, v_ev_w_in, v_ev_b_in, v_lru_conv_w, v_lru_conv_b, v_lru_wa, v_lru_ba, v_lru_wx, v_lru_bx, v_lru_lambda, v_hgrn_lb_logits, v_hgrn_norm_g, v_ev_w_out, v_od_w_in, v_od_b_in, v_sc_conv_w, v_cf_conv_w, v_cf_conv_b, v_cf_ln_g, v_cf_ln_b, v_od_w_out, v_ffn_w_gate, v_ffn_w_up, v_ffn_w_down):
    args = locals()
    w = {k: args[k] for k in WEIGHTS}
    m = {k: args["m_" + k] for k in WEIGHTS}
    v = {k: args["v_" + k] for k in WEIGHTS}
    assert x.shape[0] == 1
    T, D = x.shape[1:]

    local_shapes = [w[k].shape for k in SMALL_SHARDED]
    gathered = _all_gather("gather_weights",
                           [w[k].astype(MXU_DTYPE) for k in BIG] + [_pack_rows([w[k] for k in SMALL_SHARDED])])
    wg = dict(zip(BIG, gathered[:len(BIG)]))
    p = {k: w[k] for k in SMALL_REPL}
    per_dev = [_unpack_rows(gathered[-1][s], local_shapes) for s in range(N_DEV)]
    for i, k in enumerate(SMALL_SHARDED):
        p[k] = _unshard_last(jnp.stack([per_dev[s][i] for s in range(N_DEV)]))

    loss_part, dx, gbuf, small = _local_step(x[0], loss_target[0], p, wg)
    loss = lax.psum(loss_part[0, 0], ("x", "y", "c"))

    small_sh = jnp.stack([_pack_rows([_shards_last(small[k])[s] for k in SMALL_SHARDED]) for s in range(N_DEV)])
    parts = [gbuf[k] for k in BIG] + [small_sh]
    got = _pair_exchange("grads_to_sibling", parts)
    chip_parts = [_pair_add(f"pair_add{i}", a, b) for i, (a, b) in enumerate(zip(parts, got))]
    final_parts = _chip_exchange("grads_to_chips", chip_parts)
    repl_parts = _all_gather("gather_small_grads", [_pack_rows([small[k] for k in SMALL_REPL])])[0]

    out_g, out_d, out_m, out_v = {}, {}, {}, {}
    for i, k in enumerate(BIG):
        cols = w[k].shape[-1]
        res = _adamw(f"adamw_{k}", w[k].reshape(-1, cols), m[k].reshape(-1, cols), v[k].reshape(-1, cols),
                     final_parts[i].reshape(4, -1, cols))
        out_g[k], out_d[k], out_m[k], out_v[k] = [r.reshape(w[k].shape) for r in res]
    res = _adamw("adamw_small_sharded", *[_pack_rows([t[k] for k in SMALL_SHARDED]) for t in (w, m, v)],
                 final_parts[-1])
    for o, r in zip((out_g, out_d, out_m, out_v), res):
        o.update(zip(SMALL_SHARDED, _unpack_rows(r, local_shapes)))
    res = _adamw("adamw_small_repl", *[_pack_rows([t[k] for k in SMALL_REPL]) for t in (w, m, v)], repl_parts)
    for o, r in zip((out_g, out_d, out_m, out_v), res):
        o.update(zip(SMALL_REPL, _unpack_rows(r, [w[k].shape for k in SMALL_REPL])))

    return (loss, dx[None], *[out_g[k] for k in WEIGHTS], *[out_d[k] for k in WEIGHTS],
            *[out_m[k] for k in WEIGHTS], *[out_v[k] for k in WEIGHTS])
```

```python
import functools
import math

import jax
import jax.numpy as jnp
from jax import lax
from jax.experimental import pallas as pl
from jax.experimental.pallas import tpu as pltpu

F32 = jnp.float32
MXU_DTYPE = jnp.bfloat16
WIRE_DTYPE = jnp.bfloat16
N_DEV = 8
EPS = 1e-6
F_FLOOR = 1e-30
LRU_C = 8.0
HGRN_HEADS = 8
HGRN_SUB = 16
ADAM_LR, ADAM_B1, ADAM_B2, ADAM_EPS, ADAM_WD, ADAM_STEP = 0.001, 0.9, 0.999, 1e-08, 0.01, 10
V7X_VMEM_LIMIT = 48 * 1024 * 1024
MESH = pl.DeviceIdType.MESH
ANY = pl.BlockSpec(memory_space=pl.ANY)


def _cp(*sem):
    return pltpu.CompilerParams(dimension_semantics=sem or None, vmem_limit_bytes=V7X_VMEM_LIMIT)


def _sigmoid(x):
    return 1.0 / (1.0 + jnp.exp(-x))


def _silu(x):
    return x * _sigmoid(x)


def _dsilu(x):
    s = _sigmoid(x)
    return s * (1.0 + x * (1.0 - s))


_GELU_C = math.sqrt(2.0 / math.pi)


def _gelu(x):
    return 0.5 * x * (1.0 + jnp.tanh(_GELU_C * (x + 0.044715 * x * x * x)))


def _dgelu(x):
    t = jnp.tanh(_GELU_C * (x + 0.044715 * x * x * x))
    return 0.5 * (1.0 + t) + 0.5 * x * (1.0 - t * t) * _GELU_C * (1.0 + 3.0 * 0.044715 * x * x)


def _log1p(e):
    return jnp.where(e < 1e-2, e * (1.0 - e * (0.5 - e * (1.0 / 3.0))), jnp.log(1.0 + e))


def _softplus(x):
    return jnp.maximum(x, 0.0) + _log1p(jnp.exp(-jnp.abs(x)))


def _one_minus_exp(x):
    series = -x * (1.0 + x * (0.5 + x * (1.0 / 6.0 + x * (1.0 / 24.0))))
    return jnp.where(x > -0.05, series, 1.0 - jnp.exp(x))


def _rows(n, d=1):
    return lax.broadcasted_iota(jnp.int32, (n, d), 0)


def _dot(a, b):
    return jnp.dot(a.astype(MXU_DTYPE), b.astype(MXU_DTYPE), preferred_element_type=F32)


def _dot_nt(a, b):
    return lax.dot_general(a.astype(MXU_DTYPE), b.astype(MXU_DTYPE), (((1,), (1,)), ((), ())),
                           preferred_element_type=F32)


def _dot_tn(a, b):
    return lax.dot_general(a.astype(MXU_DTYPE), b.astype(MXU_DTYPE), (((0,), (0,)), ((), ())),
                           preferred_element_type=F32)


def _tile(n, want):
    if n <= want:
        return n
    t = want - want % 8
    while n % t:
        t -= 8
    assert t > 0, (n, want)
    return t


def _my_place():
    x, y, c = lax.axis_index("x"), lax.axis_index("y"), lax.axis_index("c")
    return x, y, c


def _all_gather(name, srcs):
    n = len(srcs)

    def body(*refs):
        src_refs, out_refs = refs[:n], refs[n:2 * n]
        send_sems, recv_sems, local_sems = refs[2 * n:]
        x, y, c = _my_place()
        sibling = (x, y, 1 - c)
        chips = [(1 - x, y), (x, 1 - y), (1 - x, 1 - y)]

        def slot(px, py, pc):
            return 4 * px + 2 * py + pc

        def copy(i, k, block, to, src=None):
            dst = out_refs[i].at[slot(*block)]
            return pltpu.make_async_remote_copy(
                src_ref=dst if src is None else src, dst_ref=dst,
                send_sem=send_sems.at[i, k], recv_sem=recv_sems.at[i, k],
                device_id=to, device_id_type=MESH)

        me = (x, y, c)
        sends, own = [], []
        for i in range(n):
            mine = pltpu.make_async_copy(src_refs[i], out_refs[i].at[slot(*me)], local_sems.at[i])
            mine.start()
            own.append(mine)
            first = [copy(i, 0, me, sibling, src=src_refs[i])]
            first += [copy(i, 1 + j, me, (*chip, c), src=src_refs[i]) for j, chip in enumerate(chips)]
            for cp in first:
                cp.start()
            sends += first
        for i in range(n):
            for j, chip in enumerate(chips):
                copy(i, 1 + j, (*chip, c), me).wait_recv()
                passed = copy(i, 4 + j, (*chip, c), sibling)
                passed.start()
                sends.append(passed)
        for i in range(n):
            copy(i, 0, sibling, me).wait_recv()
            for j, chip in enumerate(chips):
                copy(i, 4 + j, (*chip, 1 - c), me).wait_recv()
        for cp in sends:
            cp.wait_send()
        for cp in own:
            cp.wait()

    outs = pl.pallas_call(
        body, name=name,
        out_shape=[jax.ShapeDtypeStruct((N_DEV,) + s.shape, s.dtype) for s in srcs],
        in_specs=[ANY] * n, out_specs=[ANY] * n,
        scratch_shapes=[pltpu.SemaphoreType.DMA((n, 7)), pltpu.SemaphoreType.DMA((n, 7)),
                        pltpu.SemaphoreType.DMA((n,))],
    )(*srcs)
    return list(outs)


def _pair_exchange(name, srcs):
    n = len(srcs)

    def body(*refs):
        src_refs, out_refs = refs[:n], refs[n:2 * n]
        send_sems, recv_sems = refs[2 * n:]
        x, y, c = _my_place()
        copies = []
        for i in range(n):
            for j in range(4):
                cp = pltpu.make_async_remote_copy(
                    src_ref=src_refs[i].at[2 * j + (1 - c)], dst_ref=out_refs[i].at[j],
                    send_sem=send_sems.at[i, j], recv_sem=recv_sems.at[i, j],
                    device_id=(x, y, 1 - c), device_id_type=MESH)
                cp.start()
                copies.append(cp)
        for cp in copies:
            cp.wait()

    outs = pl.pallas_call(
        body, name=name,
        out_shape=[jax.ShapeDtypeStruct((4,) + s.shape[1:], s.dtype) for s in srcs],
        in_specs=[ANY] * n, out_specs=[ANY] * n,
        scratch_shapes=[pltpu.SemaphoreType.DMA((n, 4)), pltpu.SemaphoreType.DMA((n, 4))],
    )(*srcs)
    return list(outs)


def _chip_exchange(name, srcs):
    n = len(srcs)

    def body(*refs):
        src_refs, out_refs = refs[:n], refs[n:2 * n]
        send_sems, recv_sems, local_sems = refs[2 * n:]
        x, y, c = _my_place()
        chip = 2 * x + y
        copies = []
        for i in range(n):
            mine = pltpu.make_async_copy(src_refs[i].at[chip], out_refs[i].at[3], local_sems.at[i])
            mine.start()
            copies.append(mine)
            for k, (fx, fy) in enumerate([(1, 0), (0, 1), (1, 1)]):
                px = x + fx - 2 * x * fx
                py = y + fy - 2 * y * fy
                cp = pltpu.make_async_remote_copy(
                    src_ref=src_refs[i].at[2 * px + py], dst_ref=out_refs[i].at[k],
                    send_sem=send_sems.at[i, k], recv_sem=recv_sems.at[i, k],
                    device_id=(px, py, c), device_id_type=MESH)
                cp.start()
                copies.append(cp)
        for cp in copies:
            cp.wait()

    outs = pl.pallas_call(
        body, name=name,
        out_shape=[jax.ShapeDtypeStruct(s.shape, s.dtype) for s in srcs],
        in_specs=[ANY] * n, out_specs=[ANY] * n,
        scratch_shapes=[pltpu.SemaphoreType.DMA((n, 3)), pltpu.SemaphoreType.DMA((n, 3)),
                        pltpu.SemaphoreType.DMA((n,))],
    )(*srcs)
    return list(outs)


def _pair_add(name, mine, got):
    assert mine.shape[0] == N_DEV and got.shape[0] == 4
    cdim = mine.shape[-1]
    m4 = mine.reshape(4, 2, -1, cdim)
    g3 = got.reshape(4, -1, cdim)
    rows = m4.shape[2]
    tr = _tile(rows, 512)

    def body(m_ref, g_ref, o_ref, fin_ref):
        x, y, c = _my_place()
        s = (m_ref[c].astype(F32) + g_ref[...].astype(F32)).astype(o_ref.dtype)
        o_ref[...] = s

        @pl.when(pl.program_id(1) == 2 * x + y)
        def _():
            fin_ref[...] = s

    out, fin = pl.pallas_call(
        body, name=name, grid=(rows // tr, 4),
        in_specs=[pl.BlockSpec((None, 2, tr, cdim), lambda i, j: (j, 0, i, 0)),
                  pl.BlockSpec((None, tr, cdim), lambda i, j: (j, i, 0))],
        out_specs=[pl.BlockSpec((None, tr, cdim), lambda i, j: (j, i, 0)),
                   pl.BlockSpec((None, tr, cdim), lambda i, j: (3, i, 0))],
        out_shape=[jax.ShapeDtypeStruct(g3.shape, got.dtype)] * 2,
        compiler_params=_cp("parallel", "arbitrary"),
    )(m4, g3)
    return out.reshape(got.shape), fin.reshape(got.shape)


HBM = pl.BlockSpec(memory_space=pltpu.HBM)
SEM = pl.BlockSpec(memory_space=pltpu.SEMAPHORE)
EFFECT = pltpu.SideEffectType.DATAFLOW_SIDE_EFFECTING
SLOTS = 4


def _hbm(a):
    return pltpu.with_memory_space_constraint(a, pltpu.HBM)


def _remote(src, dst, sems, i, k, to):
    return pltpu.make_async_remote_copy(src_ref=src, dst_ref=dst, send_sem=sems[0].at[i * SLOTS + k],
                                        recv_sem=sems[1].at[i * SLOTS + k], device_id=to, device_id_type=MESH)


def _slot(px, py, pc):
    return 4 * px + 2 * py + pc


def _other_chips(x, y):
    return [(1 - x, y), (x, 1 - y), (1 - x, 1 - y)]


def _plan_gather_own(layers):
    def plan(srcs, lands, sems):
        x, y, c = _my_place()
        out = []
        for i, j in enumerate(layers):
            dst = lands[i].at[_slot(x, y, c)]
            out.append(_remote(srcs[i].at[j], dst, sems, i, 0, (x, y, 1 - c)))
            for k, (px, py) in enumerate(_other_chips(x, y)):
                out.append(_remote(srcs[i].at[j], dst, sems, i, 1 + k, (px, py, c)))
        return out
    return plan


def _plan_gather_pass(n):
    def plan(srcs, lands, sems):
        x, y, c = _my_place()
        out = []
        for i in range(n):
            for k, (px, py) in enumerate(_other_chips(x, y)):
                blk = lands[i].at[_slot(px, py, c)]
                out.append(_remote(blk, blk, sems, i, k, (x, y, 1 - c)))
        return out
    return plan


def _plan_scatter_pair(n):
    def plan(srcs, lands, sems):
        x, y, c = _my_place()
        return [_remote(srcs[i].at[2 * j + (1 - c)], lands[i].at[j], sems, i, j, (x, y, 1 - c))
                for i in range(n) for j in range(4)]
    return plan


def _plan_scatter_chips(n):
    def plan(srcs, lands, sems):
        x, y, c = _my_place()
        return [_remote(srcs[i].at[2 * px + py], lands[i].at[k], sems, i, k, (px, py, c))
                for i in range(n) for k, (px, py) in enumerate(_other_chips(x, y))]
    return plan


def _split_start(name, plan, srcs, lands, deps=()):
    ns, nl, nd = len(srcs), len(lands), len(deps)
    n = max(ns, nl)

    def body(*refs):
        sems = refs[ns + nl + nd:ns + nl + nd + 2]
        for cp in plan(refs[:ns], refs[ns:ns + nl], sems):
            cp.start()
        refs[-1][...] = jnp.zeros_like(refs[-1])

    outs = pl.pallas_call(
        body, name=name,
        out_shape=(pltpu.SemaphoreType.DMA((n * SLOTS,)), pltpu.SemaphoreType.DMA((n * SLOTS,)),
                   *[pltpu.HBM(a.shape, a.dtype) for a in lands], jax.ShapeDtypeStruct((8, 128), F32)),
        in_specs=[HBM] * (ns + nl) + [ANY] * nd,
        out_specs=(SEM, SEM, *[HBM] * nl, pl.BlockSpec(memory_space=pltpu.VMEM)),
        input_output_aliases={ns + i: 2 + i for i in range(nl)},
        compiler_params=pltpu.CompilerParams(has_side_effects=EFFECT),
    )(*[_hbm(a) for a in srcs], *[_hbm(a) for a in lands], *deps)
    return (outs[0], outs[1]), list(outs[2:2 + nl]), outs[-1]


def _split_wait(name, plan, sems, srcs, lands, deps=()):
    ns, nl, nd = len(srcs), len(lands), len(deps)

    def body(*refs):
        for cp in plan(refs[:ns], refs[ns:ns + nl], refs[ns + nl:ns + nl + 2]):
            cp.wait_send()
            cp.wait_recv()

    outs = pl.pallas_call(
        body, name=name,
        out_shape=tuple(pltpu.HBM(a.shape, a.dtype) for a in lands),
        in_specs=[HBM] * (ns + nl) + [SEM, SEM] + [ANY] * nd,
        out_specs=tuple([HBM] * nl),
        input_output_aliases={ns + i: i for i in range(nl)},
        compiler_params=pltpu.CompilerParams(has_side_effects=EFFECT),
    )(*srcs, *lands, *sems, *deps)
    return list(outs)


def _place_own(name, srcs, layers):
    n = len(srcs)

    def body(*refs):
        x, y, c = _my_place()
        cps = [pltpu.make_async_copy(refs[i].at[j], refs[n + i].at[_slot(x, y, c)], refs[2 * n].at[i])
               for i, j in enumerate(layers)]
        for cp in cps:
            cp.start()
        for cp in cps:
            cp.wait()

    return list(pl.pallas_call(
        body, name=name,
        out_shape=[jax.ShapeDtypeStruct((N_DEV,) + a.shape[1:], a.dtype) for a in srcs],
        in_specs=[ANY] * n, out_specs=[ANY] * n,
        scratch_shapes=[pltpu.SemaphoreType.DMA((n,))],
    )(*srcs))


def _with_deps(body, n_in, deps):
    nd = len(deps)
    if not nd:
        return body

    def wrapped(*refs):
        body(*refs[:n_in], *refs[n_in + nd:])

    return wrapped


def _rmsnorm_fwd(name, x, g, deps=()):
    T, D = x.shape
    tm = _tile(T, 256)

    def body(x_ref, g_ref, o_ref):
        xv = x_ref[...]
        r = lax.rsqrt(jnp.mean(xv * xv, axis=-1, keepdims=True) + EPS)
        o_ref[...] = ((xv * r) * g_ref[...]).astype(o_ref.dtype)

    return pl.pallas_call(
        _with_deps(body, 2, deps), name=name, grid=(T // tm,),
        in_specs=[pl.BlockSpec((tm, D), lambda i: (i, 0)), pl.BlockSpec((1, D), lambda i: (0, 0))]
        + [ANY] * len(deps),
        out_specs=pl.BlockSpec((tm, D), lambda i: (i, 0)),
        out_shape=jax.ShapeDtypeStruct((T, D), MXU_DTYPE), compiler_params=_cp("parallel"),
    )(x, g, *deps)


def _rmsnorm_bwd(name, x, g, dh, dres):
    T, D = x.shape
    tm = _tile(T, 256)

    def body(x_ref, g_ref, dh_ref, dres_ref, dx_ref, dxb_ref, dg_ref):
        xv = x_ref[...]
        r = lax.rsqrt(jnp.mean(xv * xv, axis=-1, keepdims=True) + EPS)
        xh = xv * r
        dhv = dh_ref[...]

        @pl.when(pl.program_id(0) == 0)
        def _():
            dg_ref[...] = jnp.zeros_like(dg_ref)

        dg_ref[...] += jnp.sum(dhv * xh, axis=0, keepdims=True)
        dxh = dhv * g_ref[...]
        dx = dres_ref[...] + r * (dxh - xh * jnp.mean(dxh * xh, axis=-1, keepdims=True))
        dx_ref[...] = dx
        dxb_ref[...] = dx.astype(dxb_ref.dtype)

    return pl.pallas_call(
        body, name=name, grid=(T // tm,),
        in_specs=[pl.BlockSpec((tm, D), lambda i: (i, 0)), pl.BlockSpec((1, D), lambda i: (0, 0)),
                  pl.BlockSpec((tm, D), lambda i: (i, 0)), pl.BlockSpec((tm, D), lambda i: (i, 0))],
        out_specs=[pl.BlockSpec((tm, D), lambda i: (i, 0)), pl.BlockSpec((tm, D), lambda i: (i, 0)),
                   pl.BlockSpec((1, D), lambda i: (0, 0))],
        out_shape=[jax.ShapeDtypeStruct((T, D), F32), jax.ShapeDtypeStruct((T, D), MXU_DTYPE),
                   jax.ShapeDtypeStruct((1, D), F32)],
        compiler_params=_cp("arbitrary"),
    )(x, g, dh, dres)


def _loss_head(name, x, g, tgt):
    T, D = x.shape
    tm = _tile(T, 256)

    def body(x_ref, g_ref, t_ref, loss_ref, dx_ref, dxb_ref, dg_ref):
        xv = x_ref[...]
        r = lax.rsqrt(jnp.mean(xv * xv, axis=-1, keepdims=True) + EPS)
        xh = xv * r
        gv = g_ref[...]
        diff = xh * gv - t_ref[...]

        @pl.when(pl.program_id(0) == 0)
        def _():
            dg_ref[...] = jnp.zeros_like(dg_ref)
            loss_ref[...] = jnp.zeros_like(loss_ref)

        part = 0.5 * jnp.sum(jnp.mean(diff * diff, axis=-1, keepdims=True), axis=0, keepdims=True)
        loss_ref[...] += jnp.broadcast_to(part, loss_ref.shape)
        dy = diff * (1.0 / D)
        dg_ref[...] += jnp.sum(dy * xh, axis=0, keepdims=True)
        dxh = dy * gv
        dx = r * (dxh - xh * jnp.mean(dxh * xh, axis=-1, keepdims=True))
        dx_ref[...] = dx
        dxb_ref[...] = dx.astype(dxb_ref.dtype)

    return pl.pallas_call(
        body, name=name, grid=(T // tm,),
        in_specs=[pl.BlockSpec((tm, D), lambda i: (i, 0)), pl.BlockSpec((1, D), lambda i: (0, 0)),
                  pl.BlockSpec((tm, D), lambda i: (i, 0))],
        out_specs=[pl.BlockSpec((1, 128), lambda i: (0, 0)), pl.BlockSpec((tm, D), lambda i: (i, 0)),
                   pl.BlockSpec((tm, D), lambda i: (i, 0)), pl.BlockSpec((1, D), lambda i: (0, 0))],
        out_shape=[jax.ShapeDtypeStruct((1, 128), F32), jax.ShapeDtypeStruct((T, D), F32),
                   jax.ShapeDtypeStruct((T, D), MXU_DTYPE), jax.ShapeDtypeStruct((1, D), F32)],
        compiler_params=_cp("arbitrary"),
    )(x, g, tgt)


def _adamw(name, w, m, v, parts, off=0, prev=None):
    Rtot, C = w.shape
    P, R = parts.shape[:2]
    tr = _tile(R, 128)
    assert off % tr == 0
    ob = off // tr
    c1 = 1.0 / (1.0 - ADAM_B1 ** ADAM_STEP)
    c2 = 1.0 / (1.0 - ADAM_B2 ** ADAM_STEP)
    chained = R != Rtot
    if chained and prev is None:
        prev = [lax.empty((Rtot, C), F32) for _ in range(4)]
    prev = list(prev) if chained else []

    def body(w_ref, m_ref, v_ref, p_ref, *rest):
        g_ref, d_ref, nm_ref, nv_ref = rest[len(prev):]
        g = p_ref[0].astype(F32)
        for s in range(1, P):
            g = g + p_ref[s].astype(F32)
        nm = ADAM_B1 * m_ref[...] + (1.0 - ADAM_B1) * g
        nv = ADAM_B2 * v_ref[...] + (1.0 - ADAM_B2) * (g * g)
        g_ref[...] = g
        nm_ref[...] = nm
        nv_ref[...] = nv
        d_ref[...] = -ADAM_LR * ((nm * c1) / (jnp.sqrt(nv * c2) + ADAM_EPS) + ADAM_WD * w_ref[...])

    blk = pl.BlockSpec((tr, C), lambda i: (i + ob, 0))
    return pl.pallas_call(
        body, name=name, grid=(R // tr,),
        in_specs=[blk, blk, blk, pl.BlockSpec((P, tr, C), lambda i: (0, i, 0))] + [ANY] * len(prev),
        out_specs=[blk, blk, blk, blk],
        out_shape=[jax.ShapeDtypeStruct((Rtot, C), F32)] * 4,
        input_output_aliases={4 + i: i for i in range(len(prev))},
        compiler_params=_cp("parallel"),
    )(w, m, v, parts, *prev)


def _colsum(name, a):
    T, N = a.shape
    tm = _tile(T, 256)

    def body(a_ref, o_ref):
        @pl.when(pl.program_id(0) == 0)
        def _():
            o_ref[...] = jnp.zeros_like(o_ref)

        o_ref[...] += jnp.sum(a_ref[...].astype(F32), axis=0, keepdims=True)

    return pl.pallas_call(
        body, name=name, grid=(T // tm,),
        in_specs=[pl.BlockSpec((tm, N), lambda i: (i, 0))],
        out_specs=pl.BlockSpec((1, N), lambda i: (0, 0)),
        out_shape=jax.ShapeDtypeStruct((1, N), F32), compiler_params=_cp("arbitrary"),
    )(a)


def _proj_in(name, h, wg, bias):
    T, K = h.shape
    n = wg.shape[-1]
    tm = _tile(T, 512)

    def body(a_ref, w_ref, b_ref, o_ref):
        o_ref[...] = _dot(a_ref[...], w_ref[...]) + b_ref[...]

    return pl.pallas_call(
        body, name=name, grid=(N_DEV, T // tm),
        in_specs=[pl.BlockSpec((tm, K), lambda s, i: (i, 0)),
                  pl.BlockSpec((None, K, n), lambda s, i: (s, 0, 0)),
                  pl.BlockSpec((1, n), lambda s, i: (0, s))],
        out_specs=pl.BlockSpec((tm, n), lambda s, i: (i, s)),
        out_shape=jax.ShapeDtypeStruct((T, N_DEV * n), F32), compiler_params=_cp("parallel", "parallel"),
    )(h, wg, bias)


def _ffn_in(name, h, wg_gate, wg_up):
    T, K = h.shape
    n = wg_gate.shape[-1]
    tm = _tile(T, 512)

    def body(a_ref, wgt_ref, wup_ref, g_ref, u_ref, hid_ref):
        a = a_ref[...]
        g = _dot(a, wgt_ref[...])
        u = _dot(a, wup_ref[...])
        g_ref[...] = g
        u_ref[...] = u
        hid_ref[...] = (_silu(g) * u).astype(hid_ref.dtype)

    wspec = pl.BlockSpec((None, K, n), lambda s, i: (s, 0, 0))
    ospec = pl.BlockSpec((None, tm, n), lambda s, i: (s, i, 0))
    return pl.pallas_call(
        body, name=name, grid=(N_DEV, T // tm),
        in_specs=[pl.BlockSpec((tm, K), lambda s, i: (i, 0)), wspec, wspec],
        out_specs=[ospec, ospec, ospec],
        out_shape=[jax.ShapeDtypeStruct((N_DEV, T, n), F32), jax.ShapeDtypeStruct((N_DEV, T, n), F32),
                   jax.ShapeDtypeStruct((N_DEV, T, n), MXU_DTYPE)],
        compiler_params=_cp("parallel", "parallel"),
    )(h, wg_gate, wg_up)


def _a_spec(a, tm, k):
    if a.ndim == 2:
        return pl.BlockSpec((tm, k), lambda i, s: (i, s))
    return pl.BlockSpec((None, tm, k), lambda i, s: (s, i, 0))


def _proj_out(name, a, wg, res, deps=()):
    k, N = wg.shape[-2:]
    T = res.shape[0]
    tm = _tile(T, 512)

    def body(a_ref, w_ref, r_ref, o_ref):
        p = _dot(a_ref[...], w_ref[...])

        @pl.when(pl.program_id(1) == 0)
        def _():
            o_ref[...] = r_ref[...] + p

        @pl.when(pl.program_id(1) > 0)
        def _():
            o_ref[...] += p

    return pl.pallas_call(
        _with_deps(body, 3, deps), name=name, grid=(T // tm, N_DEV),
        in_specs=[_a_spec(a, tm, k), pl.BlockSpec((None, k, N), lambda i, s: (s, 0, 0)),
                  pl.BlockSpec((tm, N), lambda i, s: (i, 0))] + [ANY] * len(deps),
        out_specs=pl.BlockSpec((tm, N), lambda i, s: (i, 0)),
        out_shape=jax.ShapeDtypeStruct((T, N), F32), compiler_params=_cp("parallel", "arbitrary"),
    )(a, wg, res, *deps)


def _bwd_in(name, das, wgs, deps=()):
    K, n = wgs[0].shape[-2:]
    T = das[0].shape[-2]
    tm = _tile(T, 512)
    npair = len(das)

    def body(*refs):
        o_ref = refs[-1]
        p = _dot_nt(refs[0][...], refs[npair][...])
        for q in range(1, npair):
            p = p + _dot_nt(refs[q][...], refs[npair + q][...])

        @pl.when(pl.program_id(1) == 0)
        def _():
            o_ref[...] = p

        @pl.when(pl.program_id(1) > 0)
        def _():
            o_ref[...] += p

    return pl.pallas_call(
        _with_deps(body, 2 * npair, deps), name=name, grid=(T // tm, N_DEV),
        in_specs=[_a_spec(a, tm, n) for a in das]
        + [pl.BlockSpec((None, K, n), lambda i, s: (s, 0, 0)) for _ in wgs] + [ANY] * len(deps),
        out_specs=pl.BlockSpec((tm, K), lambda i, s: (i, 0)),
        out_shape=jax.ShapeDtypeStruct((T, K), F32), compiler_params=_cp("parallel", "arbitrary"),
    )(*das, *wgs, *deps)


def _bwd_out(name, dx, wg):
    k, N = wg.shape[-2:]
    T = dx.shape[0]
    tm = _tile(T, 512)

    def body(a_ref, w_ref, o_ref):
        o_ref[...] = _dot_nt(a_ref[...], w_ref[...])

    return pl.pallas_call(
        body, name=name, grid=(N_DEV, T // tm),
        in_specs=[pl.BlockSpec((tm, N), lambda s, i: (i, 0)),
                  pl.BlockSpec((None, k, N), lambda s, i: (s, 0, 0))],
        out_specs=pl.BlockSpec((tm, k), lambda s, i: (i, s)),
        out_shape=jax.ShapeDtypeStruct((T, N_DEV * k), F32), compiler_params=_cp("parallel", "parallel"),
    )(dx, wg)


def _ffn_bwd_hidden(name, dx, wg_down, gate, up):
    n, N = wg_down.shape[-2:]
    T = dx.shape[0]
    tm = _tile(T, 512)

    def body(a_ref, w_ref, g_ref, u_ref, dg_ref, du_ref):
        dh = _dot_nt(a_ref[...], w_ref[...])
        g = g_ref[...]
        dg_ref[...] = (dh * u_ref[...] * _dsilu(g)).astype(dg_ref.dtype)
        du_ref[...] = (dh * _silu(g)).astype(du_ref.dtype)

    sm = pl.BlockSpec((None, tm, n), lambda s, i: (s, i, 0))
    return pl.pallas_call(
        body, name=name, grid=(N_DEV, T // tm),
        in_specs=[pl.BlockSpec((tm, N), lambda s, i: (i, 0)),
                  pl.BlockSpec((None, n, N), lambda s, i: (s, 0, 0)), sm, sm],
        out_specs=[sm, sm],
        out_shape=[jax.ShapeDtypeStruct((N_DEV, T, n), MXU_DTYPE)] * 2,
        compiler_params=_cp("parallel", "parallel"),
    )(dx, wg_down, gate, up)


def _wgrad(name, a, c, rows, cols, deps=()):
    T = a.shape[-2]
    tk = _tile(T, 1024)
    nk = T // tk

    def spec(z, w):
        if z.ndim == 3:
            return pl.BlockSpec((None, tk, w), lambda s, k: (s, k, 0))
        if z.shape[1] == w:
            return pl.BlockSpec((tk, w), lambda s, k: (k, 0))
        return pl.BlockSpec((tk, w), lambda s, k: (k, s))

    def body(a_ref, c_ref, o_ref, acc_ref):
        k = pl.program_id(1)
        p = _dot_tn(a_ref[...], c_ref[...])

        @pl.when(k == 0)
        def _():
            acc_ref[...] = p

        @pl.when(k > 0)
        def _():
            acc_ref[...] += p

        @pl.when(k == nk - 1)
        def _():
            o_ref[...] = acc_ref[...].astype(o_ref.dtype)

    return pl.pallas_call(
        _with_deps(body, 2, deps), name=name, grid=(N_DEV, nk),
        in_specs=[spec(a, rows), spec(c, cols)] + [ANY] * len(deps),
        out_specs=pl.BlockSpec((None, rows, cols), lambda s, k: (s, 0, 0)),
        out_shape=jax.ShapeDtypeStruct((N_DEV, rows, cols), WIRE_DTYPE),
        scratch_shapes=[pltpu.VMEM((rows, cols), F32)],
        compiler_params=_cp("parallel", "arbitrary"),
    )(a, c, *deps)


def _shift_down(cur, prev8, sh):
    n = cur.shape[0]
    rolled = pltpu.roll(cur, sh, 0)
    top = jnp.where(_rows(8) < sh, pltpu.roll(prev8, sh, 0), rolled[0:8])
    return jnp.concatenate([top, rolled[8:n]], axis=0)


def _shift_up(cur, next8, sh):
    n = cur.shape[0]
    rolled = pltpu.roll(cur, n - sh, 0)
    bot = jnp.where(_rows(8) >= 8 - sh, pltpu.roll(next8, 8 - sh, 0), rolled[n - 8:n])
    return jnp.concatenate([rolled[0:n - 8], bot], axis=0)


def _lru_gate_terms(r, lam):
    sp = _softplus(-lam)
    la = -LRU_C * r * sp
    a = jnp.exp(la)
    m2 = _one_minus_exp(2.0 * la)
    return sp, la, a, m2


def _lru_fwd(name, proj, conv_w, conv_b, wa, ba, wx, bx, lam):
    T = proj.shape[0]
    H, hd, _ = wa.shape
    W = H * hd
    K = conv_w.shape[0]
    tb = _tile(T, 256)

    def body(xin_ref, gate_ref, cw_ref, cb_ref, wa_ref, ba_ref, wx_ref, bx_ref, lam_ref,
             ya_ref, xc_ref, r_ref, i_ref, hs_ref, tail_ref, hprev_ref):
        blk = pl.program_id(0)

        @pl.when(blk == 0)
        def _():
            tail_ref[...] = jnp.zeros_like(tail_ref)
            hprev_ref[...] = jnp.zeros_like(hprev_ref)

        xin = xin_ref[...]
        prev8 = tail_ref[...]
        xc = cw_ref[K - 1:K, :] * xin
        for sh in range(1, K):
            xc = xc + cw_ref[K - 1 - sh:K - sh, :] * _shift_down(xin, prev8, sh)
        xc = xc + cb_ref[...]
        tail_ref[...] = xin[tb - 8:tb]
        xc_ref[...] = xc
        for h in range(H):
            cs = slice(h * hd, (h + 1) * hd)
            xh = xc[:, cs]
            r_ref[:, cs] = _sigmoid(_dot(xh, wa_ref[h]) + ba_ref[:, cs])
            i_ref[:, cs] = _sigmoid(_dot(xh, wx_ref[h]) + bx_ref[:, cs])
        r = r_ref[...]
        _, _, a, m2 = _lru_gate_terms(r, lam_ref[...])
        row = _rows(tb)
        mult = jnp.where((row == 0) & (blk == 0), 1.0, jnp.sqrt(jnp.maximum(m2, 0.0)))
        u = mult * i_ref[...] * xc
        d = 1
        while d < tb:
            keep = row >= d
            u = a * jnp.where(keep, pltpu.roll(u, d, 0), 0.0) + u
            a = a * jnp.where(keep, pltpu.roll(a, d, 0), 1.0)
            d *= 2
        hs = u + a * hprev_ref[...]
        hprev_ref[...] = hs[tb - 1:tb]
        hs_ref[...] = hs
        ya_ref[...] = (hs * _gelu(gate_ref[...])).astype(ya_ref.dtype)

    full = lambda shape: pl.BlockSpec(shape, lambda i: tuple(0 for _ in shape))
    blk = pl.BlockSpec((tb, W), lambda i: (i, 0))
    return pl.pallas_call(
        body, name=name, grid=(T // tb,),
        in_specs=[pl.BlockSpec((tb, W), lambda i: (i, 0)), pl.BlockSpec((tb, W), lambda i: (i, 1)),
                  full((K, W)), full((1, W)), full((H, hd, hd)), full((1, W)), full((H, hd, hd)),
                  full((1, W)), full((1, W))],
        out_specs=[blk] * 5,
        out_shape=[jax.ShapeDtypeStruct((T, W), MXU_DTYPE)] + [jax.ShapeDtypeStruct((T, W), F32)] * 4,
        scratch_shapes=[pltpu.VMEM((8, W), F32), pltpu.VMEM((1, W), F32)],
        compiler_params=_cp("arbitrary"),
    )(proj, proj, conv_w, conv_b, wa, ba, wx, bx, lam)


def _lru_bwd(name, proj, dy, xc, r, ig, hs, conv_w, wa, wx, lam):
    T = proj.shape[0]
    H, hd, _ = wa.shape
    W = H * hd
    K = conv_w.shape[0]
    tb = _tile(T, 256)
    nb = T // tb
    t8 = tb // 8

    def body(xin_ref, xprev_ref, gate_ref, dy_ref, xc_ref, r_ref, i_ref, hs_ref, hsprev_ref,
             cw_ref, wa_ref, wx_ref, lam_ref,
             dp_ref, dcw_ref, dcb_ref, dwa_ref, dba_ref, dwx_ref, dbx_ref, dlam_ref,
             cdh_ref, ca_ref, cdxc_ref, dxc_ref):
        step = pl.program_id(0)
        blk = nb - 1 - step

        @pl.when(step == 0)
        def _():
            for ref in (dcw_ref, dcb_ref, dwa_ref, dba_ref, dwx_ref, dbx_ref, dlam_ref,
                        cdh_ref, ca_ref, cdxc_ref):
                ref[...] = jnp.zeros_like(ref)

        row = _rows(tb)
        first = blk == 0
        gate = gate_ref[...]
        dy_a = dy_ref[...]
        hsv = hs_ref[...]
        dp_ref[:, W:2 * W] = (dy_a * hsv * _dgelu(gate)).astype(dp_ref.dtype)
        d_hs = dy_a * _gelu(gate)
        lam = lam_ref[...]
        rv = r_ref[...]
        sp, la, a, m2 = _lru_gate_terms(rv, lam)
        an = jnp.where(row == tb - 1, ca_ref[...], pltpu.roll(a, tb - 1, 0))
        u = d_hs
        d = 1
        while d < tb:
            keep = row < tb - d
            u = an * jnp.where(keep, pltpu.roll(u, tb - d, 0), 0.0) + u
            an = an * jnp.where(keep, pltpu.roll(an, tb - d, 0), 1.0)
            d *= 2
        dh = u + an * cdh_ref[...]
        cdh_ref[...] = dh[0:1]
        ca_ref[...] = a[0:1]
        hlast = jnp.where(first, 0.0, hsprev_ref[7:8, :])
        hprev = jnp.where(row == 0, hlast, pltpu.roll(hsv, 1, 0))
        da = dh * hprev
        xcv = xc_ref[...]
        iv = i_ref[...]
        t0 = (row == 0) & first
        mult = jnp.sqrt(jnp.maximum(m2, 0.0))
        mult_eff = jnp.where(t0, 1.0, mult)
        d_mult = dh * iv * xcv
        d_i = dh * mult_eff * xcv
        dxc = dh * mult_eff * iv
        e2 = 1.0 - m2
        d_la = da * a + jnp.where(t0 | (m2 <= 0.0), 0.0, -d_mult * e2 / jnp.where(m2 > 0.0, mult, 1.0))
        d_r = d_la * (-LRU_C * sp)
        dlam_ref[...] += jnp.sum(d_la * (-LRU_C * rv), axis=0, keepdims=True) * (-_sigmoid(-lam))
        d_zr = d_r * rv * (1.0 - rv)
        d_zi = d_i * iv * (1.0 - iv)
        dba_ref[...] += jnp.sum(d_zr, axis=0, keepdims=True)
        dbx_ref[...] += jnp.sum(d_zi, axis=0, keepdims=True)
        for h in range(H):
            cs = slice(h * hd, (h + 1) * hd)
            xh = xcv[:, cs]
            zr, zi = d_zr[:, cs], d_zi[:, cs]
            dwa_ref[h] += _dot_tn(xh, zr)
            dwx_ref[h] += _dot_tn(xh, zi)
            dxc_ref[:, cs] = dxc[:, cs] + _dot_nt(zr, wa_ref[h]) + _dot_nt(zi, wx_ref[h])
        dxc = dxc_ref[...]
        dcb_ref[...] += jnp.sum(dxc, axis=0, keepdims=True)
        xin = xin_ref[...]
        prev8 = jnp.where(first, 0.0, xprev_ref[...])
        next8 = cdxc_ref[...]
        dxin = cw_ref[K - 1:K, :] * dxc
        dcw_ref[K - 1:K, :] += jnp.sum(dxc * xin, axis=0, keepdims=True)
        for sh in range(1, K):
            dxin = dxin + cw_ref[K - 1 - sh:K - sh, :] * _shift_up(dxc, next8, sh)
            dcw_ref[K - 1 - sh:K - sh, :] += jnp.sum(dxc * _shift_down(xin, prev8, sh), axis=0, keepdims=True)
        cdxc_ref[...] = dxc[0:8]
        dp_ref[:, 0:W] = dxin.astype(dp_ref.dtype)

    full = lambda shape: pl.BlockSpec(shape, lambda i: tuple(0 for _ in shape))
    cur = lambda col: pl.BlockSpec((tb, W), lambda i: (nb - 1 - i, col))
    prev = pl.BlockSpec((8, W), lambda i: (jnp.maximum((nb - 1 - i) * t8 - 1, 0), 0))
    return pl.pallas_call(
        body, name=name, grid=(nb,),
        in_specs=[cur(0), prev, cur(1), cur(0), cur(0), cur(0), cur(0), cur(0), prev,
                  full((K, W)), full((H, hd, hd)), full((H, hd, hd)), full((1, W))],
        out_specs=[pl.BlockSpec((tb, 2 * W), lambda i: (nb - 1 - i, 0)), full((K, W)), full((1, W)),
                   full((H, hd, hd)), full((1, W)), full((H, hd, hd)), full((1, W)), full((1, W))],
        out_shape=[jax.ShapeDtypeStruct((T, 2 * W), MXU_DTYPE), jax.ShapeDtypeStruct((K, W), F32),
                   jax.ShapeDtypeStruct((1, W), F32), jax.ShapeDtypeStruct((H, hd, hd), F32),
                   jax.ShapeDtypeStruct((1, W), F32), jax.ShapeDtypeStruct((H, hd, hd), F32),
                   jax.ShapeDtypeStruct((1, W), F32), jax.ShapeDtypeStruct((1, W), F32)],
        scratch_shapes=[pltpu.VMEM((1, W), F32), pltpu.VMEM((1, W), F32), pltpu.VMEM((8, W), F32),
                        pltpu.VMEM((tb, W), F32)],
        compiler_params=_cp("arbitrary"),
    )(proj, proj, proj, dy, xc, r, ig, hs, hs, conv_w, wa, wx, lam)


def _chunk_cumsum(g, c):
    n = g.shape[0]
    rc = _rows(n) & (c - 1)
    d = 1
    while d < c:
        g = g + jnp.where(rc >= d, pltpu.roll(g, d, 0), 0.0)
        d *= 2
    return g


def _chunk_rcumsum(g, c):
    n = g.shape[0]
    rc = _rows(n) & (c - 1)
    d = 1
    while d < c:
        g = g + jnp.where(rc < c - d, pltpu.roll(g, n - d, 0), 0.0)
        d *= 2
    return g


def _hgrn_pointwise(qr, fr, lb):
    qf = _silu(qr)
    sig = _sigmoid(fr)
    fg = lb + (1.0 - lb) * sig
    gl = jnp.log(jnp.maximum(fg, F_FLOOR))
    kk = (1.0 - lb) * (1.0 - sig)
    return qf, sig, fg, gl, kk


def _hgrn_fwd(name, proj, lb, norm_g):
    T = proj.shape[0]
    W = lb.shape[1]
    H = HGRN_HEADS
    dk = W // H
    c = HGRN_SUB
    R = _tile(T, 128)
    nck = R // c

    def body(q_ref, f_ref, v_ref, g_ref, lb_ref, ng_ref, yb_ref, o_ref, sall_ref,
             st_ref, qe_ref, ke_ref, acc_ref):
        @pl.when(pl.program_id(0) == 0)
        def _():
            st_ref[...] = jnp.zeros_like(st_ref)

        qf, _, _, gl, kk = _hgrn_pointwise(q_ref[...], f_ref[...], lb_ref[...])
        b = _chunk_cumsum(gl, c)
        rc = _rows(R) & (c - 1)
        for h in range(H):
            cs = slice(h * dk, (h + 1) * dk)
            qh, kh, bh, vh = qf[:, cs], kk[:, cs], b[:, cs], v_ref[:, cs]
            acc = jnp.sum(qh * kh, axis=1, keepdims=True) * vh
            for d in range(1, c):
                ok = rc >= d
                e = jnp.exp(jnp.where(ok, bh - pltpu.roll(bh, d, 0), 0.0))
                s = jnp.sum(qh * pltpu.roll(kh, d, 0) * e, axis=1, keepdims=True)
                acc = acc + jnp.where(ok, s, 0.0) * pltpu.roll(vh, d, 0)
            acc_ref[:, cs] = acc
        qe_ref[...] = qf * jnp.exp(b)
        for ci in range(nck):
            rs = slice(ci * c, (ci + 1) * c)
            bl = b[ci * c + c - 1:ci * c + c, :]
            ke_ref[rs, :] = kk[rs, :] * jnp.exp(bl - b[rs, :])
            ebl = jnp.exp(bl)
            for h in range(H):
                cs = slice(h * dk, (h + 1) * dk)
                st = st_ref[h]
                sall_ref[ci, h] = st
                o_ref[rs, cs] = acc_ref[rs, cs] + _dot_nt(qe_ref[rs, cs], st)
                st_ref[h] = st * ebl[:, cs] + _dot_tn(v_ref[rs, cs], ke_ref[rs, cs])
        ng = ng_ref[...]
        gg = g_ref[...]
        for h in range(H):
            cs = slice(h * dk, (h + 1) * dk)
            oh = o_ref[:, cs]
            rr = lax.rsqrt(jnp.mean(oh * oh, axis=1, keepdims=True) + EPS)
            yb_ref[:, cs] = ((oh * rr) * ng[:, cs] * _silu(gg[:, cs])).astype(yb_ref.dtype)

    full = lambda shape: pl.BlockSpec(shape, lambda i: tuple(0 for _ in shape))
    col = lambda k: pl.BlockSpec((R, W), lambda i: (i, k))
    blk = pl.BlockSpec((R, W), lambda i: (i, 0))
    return pl.pallas_call(
        body, name=name, grid=(T // R,),
        in_specs=[col(2), col(3), col(4), col(5), full((1, W)), full((1, W))],
        out_specs=[blk, blk, pl.BlockSpec((nck, H, dk, dk), lambda i: (i, 0, 0, 0))],
        out_shape=[jax.ShapeDtypeStruct((T, W), MXU_DTYPE), jax.ShapeDtypeStruct((T, W), F32),
                   jax.ShapeDtypeStruct((T // c, H, dk, dk), F32)],
        scratch_shapes=[pltpu.VMEM((H, dk, dk), F32), pltpu.VMEM((R, W), F32), pltpu.VMEM((R, W), F32),
                        pltpu.VMEM((R, W), F32)],
        compiler_params=_cp("arbitrary"),
    )(proj, proj, proj, proj, lb, norm_g)


def _hgrn_bwd(name, proj, dy, o, sall, lb, norm_g):
    T = proj.shape[0]
    W = lb.shape[1]
    H = HGRN_HEADS
    dk = W // H
    c = HGRN_SUB
    R = _tile(T, 128)
    nck = R // c
    nb = T // R

    def body(q_ref, f_ref, v_ref, g_ref, dy_ref, o_ref, sall_ref, lb_ref, ng_ref,
             dp_ref, dlb_ref, dng_ref,
             dst_ref, do_ref, dq_ref, dk_ref, dv_ref, ex_ref, qe_ref, ke_ref):
        @pl.when(pl.program_id(0) == 0)
        def _():
            dst_ref[...] = jnp.zeros_like(dst_ref)
            dlb_ref[...] = jnp.zeros_like(dlb_ref)
            dng_ref[...] = jnp.zeros_like(dng_ref)

        lbv = lb_ref[...]
        qr = q_ref[...]
        qf, sig, fg, gl, kk = _hgrn_pointwise(qr, f_ref[...], lbv)
        b = _chunk_cumsum(gl, c)
        rc = _rows(R) & (c - 1)
        ng = ng_ref[...]
        gg = g_ref[...]
        dyv = dy_ref[...]
        sg = _silu(gg)
        for h in range(H):
            cs = slice(h * dk, (h + 1) * dk)
            oh = o_ref[:, cs]
            rr = lax.rsqrt(jnp.mean(oh * oh, axis=1, keepdims=True) + EPS)
            ohat = oh * rr
            dyh = dyv[:, cs]
            dp_ref[:, 3 * W + h * dk:3 * W + (h + 1) * dk] = (
                dyh * ohat * ng[:, cs] * _dsilu(gg[:, cs])).astype(dp_ref.dtype)
            t = dyh * sg[:, cs]
            dng_ref[:, cs] += jnp.sum(t * ohat, axis=0, keepdims=True)
            dohat = t * ng[:, cs]
            do_ref[:, cs] = rr * (dohat - ohat * jnp.mean(dohat * ohat, axis=1, keepdims=True))
        for h in range(H):
            cs = slice(h * dk, (h + 1) * dk)
            qh, kh, bh, vh, doh = qf[:, cs], kk[:, cs], b[:, cs], v_ref[:, cs], do_ref[:, cs]
            da0 = jnp.sum(doh * vh, axis=1, keepdims=True)
            a0 = jnp.sum(qh * kh, axis=1, keepdims=True)
            dq = da0 * kh
            dkk = da0 * qh
            dv = a0 * doh
            for d in range(1, c):
                ok = rc >= d
                e = jnp.exp(jnp.where(ok, bh - pltpu.roll(bh, d, 0), 0.0))
                kr = pltpu.roll(kh, d, 0)
                da = jnp.where(ok, jnp.sum(doh * pltpu.roll(vh, d, 0), axis=1, keepdims=True), 0.0)
                aa = jnp.where(ok, jnp.sum(qh * kr * e, axis=1, keepdims=True), 0.0)
                dq = dq + da * kr * e
                dkk = dkk + pltpu.roll(da * qh * e, R - d, 0)
                dv = dv + pltpu.roll(aa * doh, R - d, 0)
            dq_ref[:, cs] = dq
            dk_ref[:, cs] = dkk
            dv_ref[:, cs] = dv
        eb = jnp.exp(b)
        qe_ref[...] = qf * eb
        ex_ref[...] = jnp.zeros_like(ex_ref)
        for ci in reversed(range(nck)):
            rs = slice(ci * c, (ci + 1) * c)
            bl = b[ci * c + c - 1:ci * c + c, :]
            ebl_rows = jnp.exp(bl - b[rs, :])
            ke_ref[rs, :] = kk[rs, :] * ebl_rows
            ebl = jnp.exp(bl)
            for h in range(H):
                cs = slice(h * dk, (h + 1) * dk)
                st0 = sall_ref[ci, h]
                dst1 = dst_ref[h]
                doc = do_ref[rs, cs]
                vc = v_ref[rs, cs]
                dq_ref[rs, cs] += _dot(doc, st0) * eb[rs, cs]
                dv_ref[rs, cs] += _dot_nt(ke_ref[rs, cs], dst1)
                dki = _dot(vc, dst1) * ebl_rows[:, cs]
                dk_ref[rs, cs] += dki
                ex_ref[ci * c + c - 1:ci * c + c, cs] = (
                    jnp.sum(dki * kk[rs, cs], axis=0, keepdims=True)
                    + ebl[:, cs] * jnp.sum(st0 * dst1, axis=0, keepdims=True))
                dst_ref[h] = dst1 * ebl[:, cs] + _dot_tn(doc, qe_ref[rs, cs])
        dq = dq_ref[...]
        dkk = dk_ref[...]
        db = qf * dq - kk * dkk + ex_ref[...]
        dgl = _chunk_rcumsum(db, c)
        dfg = jnp.where(fg > F_FLOOR, dgl / jnp.maximum(fg, F_FLOOR), 0.0)
        dsig = (dfg - dkk) * (1.0 - lbv)
        dlb_ref[...] += jnp.sum((dfg - dkk) * (1.0 - sig), axis=0, keepdims=True)
        dp_ref[:, 0:W] = (dq * _dsilu(qr)).astype(dp_ref.dtype)
        dp_ref[:, W:2 * W] = (dsig * sig * (1.0 - sig)).astype(dp_ref.dtype)
        dp_ref[:, 2 * W:3 * W] = dv_ref[...].astype(dp_ref.dtype)

    full = lambda shape: pl.BlockSpec(shape, lambda i: tuple(0 for _ in shape))
    col = lambda k: pl.BlockSpec((R, W), lambda i: (nb - 1 - i, k))
    scr = pltpu.VMEM((R, W), F32)
    return pl.pallas_call(
        body, name=name, grid=(nb,),
        in_specs=[col(2), col(3), col(4), col(5), col(1), col(0),
                  pl.BlockSpec((nck, H, dk, dk), lambda i: (nb - 1 - i, 0, 0, 0)), full((1, W)), full((1, W))],
        out_specs=[pl.BlockSpec((R, 4 * W), lambda i: (nb - 1 - i, 0)), full((1, W)), full((1, W))],
        out_shape=[jax.ShapeDtypeStruct((T, 4 * W), MXU_DTYPE), jax.ShapeDtypeStruct((1, W), F32),
                   jax.ShapeDtypeStruct((1, W), F32)],
        scratch_shapes=[pltpu.VMEM((H, dk, dk), F32), scr, scr, scr, scr, scr, scr, scr],
        compiler_params=_cp("arbitrary"),
    )(proj, proj, proj, proj, dy, o, sall, lb, norm_g)


ODD_HALO = 32


def _odd_fwd(name, proj, sc_w, cf_w, cf_b, ln_g, ln_b):
    T = proj.shape[0]
    W = sc_w.shape[1]
    K3, K31 = sc_w.shape[0], cf_w.shape[0]
    tb = _tile(T, 256)
    hb = tb // ODD_HALO
    n = tb + ODD_HALO

    def body(cur_ref, prev_ref, w3_ref, w31_ref, cb_ref, lg_ref, lbeta_ref, y_ref, d_ref):
        keep = (pl.program_id(0) > 0).astype(F32)
        sb = cur_ref[:, 0:W]
        p = cur_ref[:, W:2 * W] * cur_ref[:, 2 * W:3 * W]
        glu = cur_ref[:, 3 * W:4 * W] * _sigmoid(cur_ref[:, 4 * W:5 * W])
        p_prev = prev_ref[:, W:2 * W] * prev_ref[:, 2 * W:3 * W] * keep
        glu_prev = prev_ref[:, 3 * W:4 * W] * _sigmoid(prev_ref[:, 4 * W:5 * W]) * keep
        ext = jnp.concatenate([p_prev, p], axis=0)
        cp = w3_ref[K3 - 1:K3, :] * p
        for sh in range(1, K3):
            ext = pltpu.roll(ext, 1, 0)
            cp = cp + w3_ref[K3 - 1 - sh:K3 - sh, :] * ext[ODD_HALO:n]
        y_ref[:, 0:W] = (sb * cp).astype(y_ref.dtype)
        ext = jnp.concatenate([glu_prev, glu], axis=0)
        d = cb_ref[...] + w31_ref[K31 - 1:K31, :] * glu
        for sh in range(1, K31):
            ext = pltpu.roll(ext, 1, 0)
            d = d + w31_ref[K31 - 1 - sh:K31 - sh, :] * ext[ODD_HALO:n]
        d_ref[...] = d
        mu = jnp.mean(d, axis=1, keepdims=True)
        xc = d - mu
        rstd = lax.rsqrt(jnp.mean(xc * xc, axis=1, keepdims=True) + EPS)
        ln = (xc * rstd) * lg_ref[...] + lbeta_ref[...]
        y_ref[:, W:2 * W] = _silu(ln).astype(y_ref.dtype)

    full = lambda shape: pl.BlockSpec(shape, lambda i: tuple(0 for _ in shape))
    return pl.pallas_call(
        body, name=name, grid=(T // tb,),
        in_specs=[pl.BlockSpec((tb, 5 * W), lambda i: (i, 0)),
                  pl.BlockSpec((ODD_HALO, 5 * W), lambda i: (jnp.maximum(i * hb - 1, 0), 0)),
                  full((K3, W)), full((K31, W)), full((1, W)), full((1, W)), full((1, W))],
        out_specs=[pl.BlockSpec((tb, 2 * W), lambda i: (i, 0)), pl.BlockSpec((tb, W), lambda i: (i, 0))],
        out_shape=[jax.ShapeDtypeStruct((T, 2 * W), MXU_DTYPE), jax.ShapeDtypeStruct((T, W), F32)],
        compiler_params=_cp("parallel"),
    )(proj, proj, sc_w, cf_w, cf_b, ln_g, ln_b)


def _odd_bwd(name, proj, dy, dsave, sc_w, cf_w, ln_g, ln_b):
    T = proj.shape[0]
    W = sc_w.shape[1]
    K3, K31 = sc_w.shape[0], cf_w.shape[0]
    tb = _tile(T, 128)
    nb = T // tb
    hb = tb // ODD_HALO
    nh = T // ODD_HALO
    n = tb + ODD_HALO

    def body(cur_ref, prev_ref, next_ref, dy_ref, dyn_ref, d_ref, dn_ref,
             w3_ref, w31_ref, lg_ref, lbeta_ref,
             dp_ref, dw3_ref, dw31_ref, dcb_ref, dlg_ref, dlb_ref):
        i = pl.program_id(0)

        @pl.when(i == 0)
        def _():
            for ref in (dw3_ref, dw31_ref, dcb_ref, dlg_ref, dlb_ref):
                ref[...] = jnp.zeros_like(ref)

        keep_prev = (i > 0).astype(F32)
        keep_next = (i < nb - 1).astype(F32)
        sb = cur_ref[:, 0:W]
        scv = cur_ref[:, W:2 * W]
        svv = cur_ref[:, 2 * W:3 * W]
        cu = cur_ref[:, 3 * W:4 * W]
        sg = _sigmoid(cur_ref[:, 4 * W:5 * W])
        p = scv * svv
        glu = cu * sg
        p_prev = prev_ref[:, W:2 * W] * prev_ref[:, 2 * W:3 * W] * keep_prev
        glu_prev = prev_ref[:, 3 * W:4 * W] * _sigmoid(prev_ref[:, 4 * W:5 * W]) * keep_prev
        dext = jnp.concatenate([d_ref[...], dn_ref[...]], axis=0)
        dyd = jnp.concatenate([dy_ref[:, W:2 * W], dyn_ref[:, W:2 * W] * keep_next], axis=0)
        mu = jnp.mean(dext, axis=1, keepdims=True)
        xc = dext - mu
        rstd = lax.rsqrt(jnp.mean(xc * xc, axis=1, keepdims=True) + EPS)
        xh = xc * rstd
        lg = lg_ref[...]
        dln = dyd * _dsilu(xh * lg + lbeta_ref[...])
        dxh = dln * lg
        dd = rstd * (dxh - jnp.mean(dxh, axis=1, keepdims=True)
                     - xh * jnp.mean(dxh * xh, axis=1, keepdims=True))
        dlg_ref[...] += jnp.sum((dln * xh)[0:tb], axis=0, keepdims=True)
        dlb_ref[...] += jnp.sum(dln[0:tb], axis=0, keepdims=True)
        ddc = dd[0:tb]
        dcb_ref[...] += jnp.sum(ddc, axis=0, keepdims=True)
        dglu = w31_ref[K31 - 1:K31, :] * ddc
        ext = jnp.concatenate([glu_prev, glu], axis=0)
        dw31_ref[K31 - 1:K31, :] += jnp.sum(ddc * glu, axis=0, keepdims=True)
        up = dd
        for sh in range(1, K31):
            up = pltpu.roll(up, n - 1, 0)
            ext = pltpu.roll(ext, 1, 0)
            dglu = dglu + w31_ref[K31 - 1 - sh:K31 - sh, :] * up[0:tb]
            dw31_ref[K31 - 1 - sh:K31 - sh, :] += jnp.sum(ddc * ext[ODD_HALO:n], axis=0, keepdims=True)
        dp_ref[:, 3 * W:4 * W] = (dglu * sg).astype(dp_ref.dtype)
        dp_ref[:, 4 * W:5 * W] = (dglu * cu * sg * (1.0 - sg)).astype(dp_ref.dtype)
        dyc = dy_ref[:, 0:W]
        dcp = jnp.concatenate([dyc * sb, dyn_ref[:, 0:W] * next_ref[:, 0:W] * keep_next], axis=0)
        dcpc = dcp[0:tb]
        ext = jnp.concatenate([p_prev, p], axis=0)
        cp = w3_ref[K3 - 1:K3, :] * p
        dpp = w3_ref[K3 - 1:K3, :] * dcpc
        dw3_ref[K3 - 1:K3, :] += jnp.sum(dcpc * p, axis=0, keepdims=True)
        up = dcp
        for sh in range(1, K3):
            up = pltpu.roll(up, n - 1, 0)
            ext = pltpu.roll(ext, 1, 0)
            shifted = ext[ODD_HALO:n]
            cp = cp + w3_ref[K3 - 1 - sh:K3 - sh, :] * shifted
            dpp = dpp + w3_ref[K3 - 1 - sh:K3 - sh, :] * up[0:tb]
            dw3_ref[K3 - 1 - sh:K3 - sh, :] += jnp.sum(dcpc * shifted, axis=0, keepdims=True)
        dp_ref[:, 0:W] = (dyc * cp).astype(dp_ref.dtype)
        dp_ref[:, W:2 * W] = (dpp * svv).astype(dp_ref.dtype)
        dp_ref[:, 2 * W:3 * W] = (dpp * scv).astype(dp_ref.dtype)

    full = lambda shape: pl.BlockSpec(shape, lambda i: tuple(0 for _ in shape))
    prev_map = lambda i: (jnp.maximum(i * hb - 1, 0), 0)
    next_map = lambda i: (jnp.minimum((i + 1) * hb, nh - 1), 0)
    return pl.pallas_call(
        body, name=name, grid=(nb,),
        in_specs=[pl.BlockSpec((tb, 5 * W), lambda i: (i, 0)),
                  pl.BlockSpec((ODD_HALO, 5 * W), prev_map), pl.BlockSpec((ODD_HALO, 5 * W), next_map),
                  pl.BlockSpec((tb, 2 * W), lambda i: (i, 0)), pl.BlockSpec((ODD_HALO, 2 * W), next_map),
                  pl.BlockSpec((tb, W), lambda i: (i, 0)), pl.BlockSpec((ODD_HALO, W), next_map),
                  full((K3, W)), full((K31, W)), full((1, W)), full((1, W))],
        out_specs=[pl.BlockSpec((tb, 5 * W), lambda i: (i, 0)), full((K3, W)), full((K31, W)),
                   full((1, W)), full((1, W)), full((1, W))],
        out_shape=[jax.ShapeDtypeStruct((T, 5 * W), MXU_DTYPE), jax.ShapeDtypeStruct((K3, W), F32),
                   jax.ShapeDtypeStruct((K31, W), F32)] + [jax.ShapeDtypeStruct((1, W), F32)] * 3,
        compiler_params=_cp("arbitrary"),
    )(proj, proj, proj, dy, dy, dsave, dsave, sc_w, cf_w, ln_g, ln_b)


def _lower_bounds(logits):
    sm = jax.nn.softmax(logits.astype(F32), axis=0)
    return jnp.cumsum(sm, axis=0) - sm[0]


def _pack_rows(arrays):
    flat = jnp.concatenate([a.reshape(-1) for a in arrays])
    pad = (-flat.shape[0]) % (8 * 128)
    return jnp.pad(flat, (0, pad)).reshape(-1, 128)


def _unpack_rows(packed, shapes):
    flat = packed.reshape(-1)
    out, off = [], 0
    for s in shapes:
        sz = math.prod(s)
        out.append(flat[off:off + sz].reshape(s))
        off += sz
    return out


def _shards_last(a):
    n = a.shape[-1] // N_DEV
    return jnp.moveaxis(a.reshape(a.shape[:-1] + (N_DEV, n)), -2, 0)


def _unshard_last(a):
    a = jnp.moveaxis(a, 0, -2)
    return a.reshape(a.shape[:-2] + (a.shape[-2] * a.shape[-1],))


BIG = ("ev_w_in", "ev_w_out", "od_w_in", "od_w_out", "ffn_w_gate", "ffn_w_up", "ffn_w_down")
SMALL_SHARDED = ("lru_conv_w", "od_b_in", "sc_conv_w", "cf_conv_w", "cf_conv_b", "cf_ln_g", "cf_ln_b")
SMALL_REPL = ("ln_mix_g", "ln_ffn_g", "ln_final_g", "ev_b_in", "lru_conv_b", "lru_wa", "lru_ba", "lru_wx",
              "lru_bx", "lru_lambda", "hgrn_lb_logits", "hgrn_norm_g")
WEIGHTS = ("ln_mix_g", "ln_ffn_g", "ln_final_g", "ev_w_in", "ev_b_in", "lru_conv_w", "lru_conv_b", "lru_wa",
           "lru_ba", "lru_wx", "lru_bx", "lru_lambda", "hgrn_lb_logits", "hgrn_norm_g", "ev_w_out", "od_w_in",
           "od_b_in", "sc_conv_w", "cf_conv_w", "cf_conv_b", "cf_ln_g", "cf_ln_b", "od_w_out", "ffn_w_gate",
           "ffn_w_up", "ffn_w_down")


def _layer_weights(l):
    mix = ("ev_w_in", "ev_w_out") if l % 2 == 0 else ("od_w_in", "od_w_out")
    return [(mix[0], l // 2), (mix[1], l // 2), ("ffn_w_gate", l), ("ffn_w_up", l), ("ffn_w_down", l)]


class _MeshExchange:
    def __init__(self, w_bf, w, m, v, depth):
        self.w_bf, self.w, self.m, self.v, self.depth = w_bf, w, m, v, depth
        self.ready, self.flight, self.rs, self.adam = {}, {}, {}, {}

    def _own_start(self, l, deps):
        names = _layer_weights(l)
        srcs = [self.w_bf[k] for k, _ in names]
        layers = [j for _, j in names]
        lands = _place_own(f"ag_place{l}", srcs, layers)
        plan = _plan_gather_own(layers)
        sems, lands, tok = _split_start(f"ag_own_start{l}", plan, srcs, lands, deps)
        self.flight[l] = (plan, sems, srcs, lands)
        return tok

    def _own_wait_pass_start(self, l, deps):
        plan, sems, srcs, lands = self.flight[l]
        lands = _split_wait(f"ag_own_wait{l}", plan, sems, srcs, lands, deps)
        plan = _plan_gather_pass(len(lands))
        sems, lands, tok = _split_start(f"ag_pass_start{l}", plan, [], lands)
        self.flight[l] = (plan, sems, [], lands)
        return tok

    def _pass_wait(self, l, deps):
        plan, sems, srcs, lands = self.flight.pop(l)
        lands = _split_wait(f"ag_pass_wait{l}", plan, sems, srcs, lands, deps)
        self.ready[l] = dict(zip([k for k, _ in _layer_weights(l)], lands))

    def weights(self, l):
        if l == 0:
            self._own_start(0, ())
            self._own_wait_pass_start(0, ())
            self._pass_wait(0, ())
        wl = self.ready.pop(l)
        tok = ()
        if l + 1 < self.depth:
            tok = (self._own_start(l + 1, (wl["ffn_w_down"],)),)
        return wl, tok

    def forward_mid(self, l, after):
        if l + 1 < self.depth:
            return (self._own_wait_pass_start(l + 1, (after,)),)
        return ()

    def forward_end(self, l, after):
        if l + 1 < self.depth:
            self._pass_wait(l + 1, (after,))

    def grads(self, tag, named):
        srcs = [g for _, _, g in named]
        lands = [lax.empty((4,) + g.shape[1:], g.dtype) for g in srcs]
        plan = _plan_scatter_pair(len(srcs))
        sems, lands, tok = _split_start(f"rs_pair_start_{tag}", plan, srcs, lands)
        self.rs[tag] = (named, plan, sems, srcs, lands)
        return (tok,)

    def grads_mid(self, tag, after):
        named, plan, sems, srcs, lands = self.rs[tag]
        got = _split_wait(f"rs_pair_wait_{tag}", plan, sems, srcs, lands, (after,))
        both = [_pair_add(f"pair_add_{tag}_{i}", a, b) for i, (a, b) in enumerate(zip(srcs, got))]
        self.rs[tag] = (named, [p for p, _ in both], [f for _, f in both])

    def grads_send(self, tag):
        named, parts, fins = self.rs[tag]
        plan = _plan_scatter_chips(len(parts))
        sems, fins, tok = _split_start(f"rs_chip_start_{tag}", plan, parts, fins)
        self.rs[tag] = (named, plan, sems, parts, fins)
        return (tok,)

    def grads_end(self, tag, after):
        named, plan, sems, parts, fins = self.rs.pop(tag)
        fins = _split_wait(f"rs_chip_wait_{tag}", plan, sems, parts, fins, (after,))
        for (k, j, _), fin in zip(named, fins):
            cols = fin.shape[-1]
            rows = fin.shape[-2]
            two_d = lambda a: a.reshape(-1, cols)
            self.adam[k] = _adamw(f"adamw_{k}_{j}", two_d(self.w[k]), two_d(self.m[k]), two_d(self.v[k]),
                                  fin, off=j * rows, prev=self.adam.get(k))

    def results(self, k):
        return [r.reshape(self.w[k].shape) for r in self.adam[k]]


def _local_step(x, tgt, p, ex):
    T, D = x.shape
    depth = p["ln_mix_g"].shape[0]
    lbs = _lower_bounds(p["hgrn_lb_logits"])
    row = lambda a: a.reshape(1, -1)
    saved = []
    for l in range(depth):
        j = l // 2
        wl, tok = ex.weights(l)
        s = {"x": x, "w": wl}
        h = _rmsnorm_fwd(f"norm_mix{l}", x, row(p["ln_mix_g"][l]), tok)
        s["h"] = h
        if l % 2 == 0:
            proj = _proj_in(f"ev_in{l}", h, wl["ev_w_in"], row(p["ev_b_in"][j]))
            wa = p["lru_wa"][j].astype(MXU_DTYPE)
            wx = p["lru_wx"][j].astype(MXU_DTYPE)
            ya, xc, r, ig, hs = _lru_fwd(f"lru_fwd{l}", proj, p["lru_conv_w"][j], row(p["lru_conv_b"][j]),
                                         wa, row(p["lru_ba"][j]), wx, row(p["lru_bx"][j]),
                                         row(p["lru_lambda"][j]))
            yb, o, sall = _hgrn_fwd(f"hgrn_fwd{l}", proj, row(lbs[j]), row(p["hgrn_norm_g"][j]))
            y = jnp.concatenate([ya, yb], axis=1)
            s.update(proj=proj, xc=xc, r=r, ig=ig, hs=hs, o=o, sall=sall, wa=wa, wx=wx)
            x = _proj_out(f"ev_out{l}", y, wl["ev_w_out"], x)
        else:
            proj = _proj_in(f"od_in{l}", h, wl["od_w_in"], row(p["od_b_in"][j]))
            y, dsave = _odd_fwd(f"odd_fwd{l}", proj, p["sc_conv_w"][j], p["cf_conv_w"][j],
                                row(p["cf_conv_b"][j]), row(p["cf_ln_g"][j]), row(p["cf_ln_b"][j]))
            s.update(proj=proj, dsave=dsave)
            x = _proj_out(f"od_out{l}", y, wl["od_w_out"], x)
        s["y"] = y
        s["xmid"] = x
        h2 = _rmsnorm_fwd(f"norm_ffn{l}", x, row(p["ln_ffn_g"][l]))
        gate, up, hid = _ffn_in(f"ffn_in{l}", h2, wl["ffn_w_gate"], wl["ffn_w_up"])
        tok = ex.forward_mid(l, hid)
        x = _proj_out(f"ffn_out{l}", hid, wl["ffn_w_down"], x, tok)
        ex.forward_end(l, x)
        s.update(h2=h2, gate=gate, up=up, hid=hid)
        saved.append(s)

    loss, dx, dxb, dg_final = _loss_head("loss_head", x, row(p["ln_final_g"]), tgt)

    gs = {k: [None] * p[k].shape[0] for k in SMALL_REPL + SMALL_SHARDED if k not in ("ln_final_g", "hgrn_lb_logits")}
    d_lb = [None] * (depth // 2 + depth % 2)
    tok = ()
    pending = None
    for l in reversed(range(depth)):
        j = l // 2
        s = saved[l]
        wl = s["w"]
        ffn_shape = wl["ffn_w_gate"].shape[1:]
        dwd = _wgrad(f"ffn_dwd{l}", s["hid"], dxb, ffn_shape[1], ffn_shape[0], tok)
        dgate, dup = _ffn_bwd_hidden(f"ffn_bwd_hid{l}", dxb, wl["ffn_w_down"], s["gate"], s["up"])
        dwg = _wgrad(f"ffn_dwg{l}", s["h2"], dgate, *ffn_shape)
        dwu = _wgrad(f"ffn_dwu{l}", s["h2"], dup, *ffn_shape)
        tok = ex.grads(f"ffn{l}", [("ffn_w_down", l, dwd), ("ffn_w_gate", l, dwg), ("ffn_w_up", l, dwu)])
        dh2 = _bwd_in(f"ffn_dh{l}", [dgate, dup], [wl["ffn_w_gate"], wl["ffn_w_up"]], tok)
        dx, dxb, dg = _rmsnorm_bwd(f"norm_ffn_bwd{l}", s["xmid"], row(p["ln_ffn_g"][l]), dh2, dx)
        gs["ln_ffn_g"][l] = dg[0]
        ex.grads_mid(f"ffn{l}", dxb)
        if pending is not None:
            ex.grads_end(pending, dxb)
        tok = ex.grads_send(f"ffn{l}")
        w_in, w_out = ("ev_w_in", "ev_w_out") if l % 2 == 0 else ("od_w_in", "od_w_out")
        dwo = _wgrad(f"mix_dwo{l}", s["y"], dxb, *wl[w_out].shape[1:], tok)
        if l % 2 == 0:
            dy = _bwd_out(f"ev_dy{l}", dxb, wl["ev_w_out"])
            dpa, d_cw, d_cb, d_wa, d_ba, d_wx, d_bx, d_lam = _lru_bwd(
                f"lru_bwd{l}", s["proj"], dy, s["xc"], s["r"], s["ig"], s["hs"], p["lru_conv_w"][j],
                s["wa"], s["wx"], row(p["lru_lambda"][j]))
            dph, dlb, dng = _hgrn_bwd(f"hgrn_bwd{l}", s["proj"], dy, s["o"], s["sall"], row(lbs[j]),
                                      row(p["hgrn_norm_g"][j]))
            dproj = jnp.concatenate([dpa, dph], axis=1)
            gs["lru_conv_w"][j], gs["lru_conv_b"][j] = d_cw, d_cb[0]
            gs["lru_wa"][j], gs["lru_ba"][j] = d_wa, d_ba.reshape(p["lru_ba"].shape[1:])
            gs["lru_wx"][j], gs["lru_bx"][j] = d_wx, d_bx.reshape(p["lru_bx"].shape[1:])
            gs["lru_lambda"][j], gs["hgrn_norm_g"][j] = d_lam[0], dng[0]
            d_lb[j] = dlb[0]
            gs["ev_b_in"][j] = _colsum(f"ev_db{l}", dproj)[0]
        else:
            dy = _bwd_out(f"od_dy{l}", dxb, wl["od_w_out"])
            dproj, d_w3, d_w31, d_cfb, d_lg, d_lbeta = _odd_bwd(
                f"odd_bwd{l}", s["proj"], dy, s["dsave"], p["sc_conv_w"][j], p["cf_conv_w"][j],
                row(p["cf_ln_g"][j]), row(p["cf_ln_b"][j]))
            gs["sc_conv_w"][j], gs["cf_conv_w"][j] = d_w3, d_w31
            gs["cf_conv_b"][j], gs["cf_ln_g"][j], gs["cf_ln_b"][j] = d_cfb[0], d_lg[0], d_lbeta[0]
            gs["od_b_in"][j] = _colsum(f"od_db{l}", dproj)[0]
        dwi = _wgrad(f"mix_dwi{l}", s["h"], dproj, *wl[w_in].shape[1:])
        tok = ex.grads(f"mix{l}", [(w_out, j, dwo), (w_in, j, dwi)])
        dh = _bwd_in(f"mix_dh{l}", [dproj], [wl[w_in]], tok)
        dx, dxb, dg = _rmsnorm_bwd(f"norm_mix_bwd{l}", s["x"], row(p["ln_mix_g"][l]), dh, dx)
        gs["ln_mix_g"][l] = dg[0]
        ex.grads_mid(f"mix{l}", dxb)
        ex.grads_end(f"ffn{l}", dxb)
        tok = ex.grads_send(f"mix{l}")
        pending = f"mix{l}"
    ex.grads_end(pending, dxb)

    small = {k: jnp.stack(v) for k, v in gs.items()}
    small["ln_final_g"] = dg_final[0]
    _, lb_vjp = jax.vjp(_lower_bounds, p["hgrn_lb_logits"])
    small["hgrn_lb_logits"] = lb_vjp(jnp.stack(d_lb))[0]
    return loss, dx, small


def kernel(x, ln_mix_g, ln_ffn_g, ln_final_g, ev_w_in, ev_b_in, lru_conv_w, lru_conv_b, lru_wa, lru_ba, lru_wx, lru_bx, lru_lambda, hgrn_lb_logits, hgrn_norm_g, ev_w_out, od_w_in, od_b_in, sc_conv_w, cf_conv_w, cf_conv_b, cf_ln_g, cf_ln_b, od_w_out, ffn_w_gate, ffn_w_up, ffn_w_down, loss_target, m_ln_mix_g, m_ln_ffn_g, m_ln_final_g, m_ev_w_in, m_ev_b_in, m_lru_conv_w, m_lru_conv_b, m_lru_wa, m_lru_ba, m_lru_wx, m_lru_bx, m_lru_lambda, m_hgrn_lb_logits, m_hgrn_norm_g, m_ev_w_out, m_od_w_in, m_od_b_in, m_sc_conv_w, m_cf_conv_w, m_cf_conv_b, m_cf_ln_g, m_cf_ln_b, m_od_w_out, m_ffn_w_gate, m_ffn_w_up, m_ffn_w_down, v_ln_mix_g, v_ln_ffn_g, v_ln_final_g, v_ev_w_in, v_ev_b_in, v_lru_conv_w, v_lru_conv_b, v_lru_wa, v_lru_ba, v_lru_wx, v_lru_bx, v_lru_lambda, v_hgrn_lb_logits, v_hgrn_norm_g, v_ev_w_out, v_od_w_in, v_od_b_in, v_sc_conv_w, v_cf_conv_w, v_cf_conv_b, v_cf_ln_g, v_cf_ln_b, v_od_w_out, v_ffn_w_gate, v_ffn_w_up, v_ffn_w_down):
    args = locals()
    w = {k: args[k] for k in WEIGHTS}
    m = {k: args["m_" + k] for k in WEIGHTS}
    v = {k: args["v_" + k] for k in WEIGHTS}
    assert x.shape[0] == 1
    T, D = x.shape[1:]

    local_shapes = [w[k].shape for k in SMALL_SHARDED]
    gathered = _all_gather("gather_small_params", [_pack_rows([w[k] for k in SMALL_SHARDED])])[0]
    p = {k: w[k] for k in SMALL_REPL}
    per_dev = [_unpack_rows(gathered[s], local_shapes) for s in range(N_DEV)]
    for i, k in enumerate(SMALL_SHARDED):
        p[k] = _unshard_last(jnp.stack([per_dev[s][i] for s in range(N_DEV)]))

    ex = _MeshExchange({k: w[k].astype(MXU_DTYPE) for k in BIG}, w, m, v, ln_mix_g.shape[0])
    loss_part, dx, small = _local_step(x[0], loss_target[0], p, ex)
    loss = lax.psum(loss_part[0, 0], ("x", "y", "c"))

    small_sh = jnp.stack([_pack_rows([_shards_last(small[k])[s] for k in SMALL_SHARDED]) for s in range(N_DEV)])
    got = _pair_exchange("small_grads_to_sibling", [small_sh])[0]
    final_small = _chip_exchange("small_grads_to_chips", [_pair_add("pair_add_small", small_sh, got)[0]])[0]
    repl_parts = _all_gather("gather_small_grads", [_pack_rows([small[k] for k in SMALL_REPL])])[0]

    out_g, out_d, out_m, out_v = {}, {}, {}, {}
    for k in BIG:
        out_g[k], out_d[k], out_m[k], out_v[k] = ex.results(k)
    res = _adamw("adamw_small_sharded", *[_pack_rows([t[k] for k in SMALL_SHARDED]) for t in (w, m, v)],
                 final_small)
    for o, r in zip((out_g, out_d, out_m, out_v), res):
        o.update(zip(SMALL_SHARDED, _unpack_rows(r, local_shapes)))
    res = _adamw("adamw_small_repl", *[_pack_rows([t[k] for k in SMALL_REPL]) for t in (w, m, v)], repl_parts)
    for o, r in zip((out_g, out_d, out_m, out_v), res):
        o.update(zip(SMALL_REPL, _unpack_rows(r, [w[k].shape for k in SMALL_REPL])))

    return (loss, dx[None], *[out_g[k] for k in WEIGHTS], *[out_d[k] for k in WEIGHTS],
            *[out_m[k] for k in WEIGHTS], *[out_v[k] for k in WEIGHTS])
```

```python
import functools
import math

import jax
import jax.numpy as jnp
from jax import lax
from jax.experimental import pallas as pl
from jax.experimental.pallas import tpu as pltpu

F32 = jnp.float32
MXU_DTYPE = jnp.bfloat16
WIRE_DTYPE = jnp.bfloat16
N_DEV = 8
EPS = 1e-6
F_FLOOR = 1e-30
LRU_C = 8.0
HGRN_HEADS = 8
HGRN_SUB = 16
ADAM_LR, ADAM_B1, ADAM_B2, ADAM_EPS, ADAM_WD, ADAM_STEP = 0.001, 0.9, 0.999, 1e-08, 0.01, 10
V7X_VMEM_LIMIT = 48 * 1024 * 1024
MESH = pl.DeviceIdType.MESH
ANY = pl.BlockSpec(memory_space=pl.ANY)


def _cp(*sem):
    return pltpu.CompilerParams(dimension_semantics=sem or None, vmem_limit_bytes=V7X_VMEM_LIMIT)


def _sigmoid(x):
    return 1.0 / (1.0 + jnp.exp(-x))


def _silu(x):
    return x * _sigmoid(x)


def _dsilu(x):
    s = _sigmoid(x)
    return s * (1.0 + x * (1.0 - s))


_GELU_C = math.sqrt(2.0 / math.pi)


def _gelu(x):
    return 0.5 * x * (1.0 + jnp.tanh(_GELU_C * (x + 0.044715 * x * x * x)))


def _dgelu(x):
    t = jnp.tanh(_GELU_C * (x + 0.044715 * x * x * x))
    return 0.5 * (1.0 + t) + 0.5 * x * (1.0 - t * t) * _GELU_C * (1.0 + 3.0 * 0.044715 * x * x)


def _log1p(e):
    return jnp.where(e < 1e-2, e * (1.0 - e * (0.5 - e * (1.0 / 3.0))), jnp.log(1.0 + e))


def _softplus(x):
    return jnp.maximum(x, 0.0) + _log1p(jnp.exp(-jnp.abs(x)))


def _one_minus_exp(x):
    series = -x * (1.0 + x * (0.5 + x * (1.0 / 6.0 + x * (1.0 / 24.0))))
    return jnp.where(x > -0.05, series, 1.0 - jnp.exp(x))


def _rows(n, d=1):
    return lax.broadcasted_iota(jnp.int32, (n, d), 0)


def _dot(a, b):
    return jnp.dot(a.astype(MXU_DTYPE), b.astype(MXU_DTYPE), preferred_element_type=F32)


def _dot_nt(a, b):
    return lax.dot_general(a.astype(MXU_DTYPE), b.astype(MXU_DTYPE), (((1,), (1,)), ((), ())),
                           preferred_element_type=F32)


def _dot_tn(a, b):
    return lax.dot_general(a.astype(MXU_DTYPE), b.astype(MXU_DTYPE), (((0,), (0,)), ((), ())),
                           preferred_element_type=F32)


def _tile(n, want):
    if n <= want:
        return n
    t = want - want % 8
    while n % t:
        t -= 8
    assert t > 0, (n, want)
    return t


def _my_place():
    x, y, c = lax.axis_index("x"), lax.axis_index("y"), lax.axis_index("c")
    return x, y, c


def _all_gather(name, srcs):
    n = len(srcs)

    def body(*refs):
        src_refs, out_refs = refs[:n], refs[n:2 * n]
        send_sems, recv_sems, local_sems = refs[2 * n:]
        x, y, c = _my_place()
        sibling = (x, y, 1 - c)
        chips = [(1 - x, y), (x, 1 - y), (1 - x, 1 - y)]

        def slot(px, py, pc):
            return 4 * px + 2 * py + pc

        def copy(i, k, block, to, src=None):
            dst = out_refs[i].at[slot(*block)]
            return pltpu.make_async_remote_copy(
                src_ref=dst if src is None else src, dst_ref=dst,
                send_sem=send_sems.at[i, k], recv_sem=recv_sems.at[i, k],
                device_id=to, device_id_type=MESH)

        me = (x, y, c)
        sends, own = [], []
        for i in range(n):
            mine = pltpu.make_async_copy(src_refs[i], out_refs[i].at[slot(*me)], local_sems.at[i])
            mine.start()
            own.append(mine)
            first = [copy(i, 0, me, sibling, src=src_refs[i])]
            first += [copy(i, 1 + j, me, (*chip, c), src=src_refs[i]) for j, chip in enumerate(chips)]
            for cp in first:
                cp.start()
            sends += first
        for i in range(n):
            for j, chip in enumerate(chips):
                copy(i, 1 + j, (*chip, c), me).wait_recv()
                passed = copy(i, 4 + j, (*chip, c), sibling)
                passed.start()
                sends.append(passed)
        for i in range(n):
            copy(i, 0, sibling, me).wait_recv()
            for j, chip in enumerate(chips):
                copy(i, 4 + j, (*chip, 1 - c), me).wait_recv()
        for cp in sends:
            cp.wait_send()
        for cp in own:
            cp.wait()

    outs = pl.pallas_call(
        body, name=name,
        out_shape=[jax.ShapeDtypeStruct((N_DEV,) + s.shape, s.dtype) for s in srcs],
        in_specs=[ANY] * n, out_specs=[ANY] * n,
        scratch_shapes=[pltpu.SemaphoreType.DMA((n, 7)), pltpu.SemaphoreType.DMA((n, 7)),
                        pltpu.SemaphoreType.DMA((n,))],
    )(*srcs)
    return list(outs)


def _pair_exchange(name, srcs):
    n = len(srcs)

    def body(*refs):
        src_refs, out_refs = refs[:n], refs[n:2 * n]
        send_sems, recv_sems = refs[2 * n:]
        x, y, c = _my_place()
        copies = []
        for i in range(n):
            for j in range(4):
                cp = pltpu.make_async_remote_copy(
                    src_ref=src_refs[i].at[2 * j + (1 - c)], dst_ref=out_refs[i].at[j],
                    send_sem=send_sems.at[i, j], recv_sem=recv_sems.at[i, j],
                    device_id=(x, y, 1 - c), device_id_type=MESH)
                cp.start()
                copies.append(cp)
        for cp in copies:
            cp.wait()

    outs = pl.pallas_call(
        body, name=name,
        out_shape=[jax.ShapeDtypeStruct((4,) + s.shape[1:], s.dtype) for s in srcs],
        in_specs=[ANY] * n, out_specs=[ANY] * n,
        scratch_shapes=[pltpu.SemaphoreType.DMA((n, 4)), pltpu.SemaphoreType.DMA((n, 4))],
    )(*srcs)
    return list(outs)


def _chip_exchange(name, srcs):
    n = len(srcs)

    def body(*refs):
        src_refs, out_refs = refs[:n], refs[n:2 * n]
        send_sems, recv_sems, local_sems = refs[2 * n:]
        x, y, c = _my_place()
        chip = 2 * x + y
        copies = []
        for i in range(n):
            mine = pltpu.make_async_copy(src_refs[i].at[chip], out_refs[i].at[3], local_sems.at[i])
            mine.start()
            copies.append(mine)
            for k, (fx, fy) in enumerate([(1, 0), (0, 1), (1, 1)]):
                px = x + fx - 2 * x * fx
                py = y + fy - 2 * y * fy
                cp = pltpu.make_async_remote_copy(
                    src_ref=src_refs[i].at[2 * px + py], dst_ref=out_refs[i].at[k],
                    send_sem=send_sems.at[i, k], recv_sem=recv_sems.at[i, k],
                    device_id=(px, py, c), device_id_type=MESH)
                cp.start()
                copies.append(cp)
        for cp in copies:
            cp.wait()

    outs = pl.pallas_call(
        body, name=name,
        out_shape=[jax.ShapeDtypeStruct(s.shape, s.dtype) for s in srcs],
        in_specs=[ANY] * n, out_specs=[ANY] * n,
        scratch_shapes=[pltpu.SemaphoreType.DMA((n, 3)), pltpu.SemaphoreType.DMA((n, 3)),
                        pltpu.SemaphoreType.DMA((n,))],
    )(*srcs)
    return list(outs)


def _pair_add(name, mine, got):
    assert mine.shape[0] == N_DEV and got.shape[0] == 4
    cdim = mine.shape[-1]
    m4 = mine.reshape(4, 2, -1, cdim)
    g3 = got.reshape(4, -1, cdim)
    rows = m4.shape[2]
    tr = _tile(rows, 512)

    def body(m_ref, g_ref, o_ref, fin_ref):
        x, y, c = _my_place()
        s = (m_ref[c].astype(F32) + g_ref[...].astype(F32)).astype(o_ref.dtype)
        o_ref[...] = s

        @pl.when(pl.program_id(1) == 2 * x + y)
        def _():
            fin_ref[...] = s

    out, fin = pl.pallas_call(
        body, name=name, grid=(rows // tr, 4),
        in_specs=[pl.BlockSpec((None, 2, tr, cdim), lambda i, j: (j, 0, i, 0)),
                  pl.BlockSpec((None, tr, cdim), lambda i, j: (j, i, 0))],
        out_specs=[pl.BlockSpec((None, tr, cdim), lambda i, j: (j, i, 0)),
                   pl.BlockSpec((None, tr, cdim), lambda i, j: (3, i, 0))],
        out_shape=[jax.ShapeDtypeStruct(g3.shape, got.dtype)] * 2,
        compiler_params=_cp("parallel", "arbitrary"),
    )(m4, g3)
    return out.reshape(got.shape), fin.reshape(got.shape)


HBM = pl.BlockSpec(memory_space=pltpu.HBM)
SEM = pl.BlockSpec(memory_space=pltpu.SEMAPHORE)
EFFECT = pltpu.SideEffectType.DATAFLOW_SIDE_EFFECTING
SLOTS = 4


def _hbm(a):
    return pltpu.with_memory_space_constraint(a, pltpu.HBM)


def _remote(src, dst, sems, i, k, to):
    return pltpu.make_async_remote_copy(src_ref=src, dst_ref=dst, send_sem=sems[0].at[i * SLOTS + k],
                                        recv_sem=sems[1].at[i * SLOTS + k], device_id=to, device_id_type=MESH)


def _slot(px, py, pc):
    return 4 * px + 2 * py + pc


def _other_chips(x, y):
    return [(1 - x, y), (x, 1 - y), (1 - x, 1 - y)]


def _plan_gather_own(layers):
    def plan(srcs, lands, sems):
        x, y, c = _my_place()
        out = []
        for i, j in enumerate(layers):
            dst = lands[i].at[_slot(x, y, c)]
            out.append(_remote(srcs[i].at[j], dst, sems, i, 0, (x, y, 1 - c)))
            for k, (px, py) in enumerate(_other_chips(x, y)):
                out.append(_remote(srcs[i].at[j], dst, sems, i, 1 + k, (px, py, c)))
        return out
    return plan


def _plan_gather_pass(n):
    def plan(srcs, lands, sems):
        x, y, c = _my_place()
        out = []
        for i in range(n):
            for k, (px, py) in enumerate(_other_chips(x, y)):
                blk = lands[i].at[_slot(px, py, c)]
                out.append(_remote(blk, blk, sems, i, k, (x, y, 1 - c)))
        return out
    return plan


def _plan_scatter_pair(n):
    def plan(srcs, lands, sems):
        x, y, c = _my_place()
        return [_remote(srcs[i].at[2 * j + (1 - c)], lands[i].at[j], sems, i, j, (x, y, 1 - c))
                for i in range(n) for j in range(4)]
    return plan


def _plan_scatter_chips(n):
    def plan(srcs, lands, sems):
        x, y, c = _my_place()
        return [_remote(srcs[i].at[2 * px + py], lands[i].at[k], sems, i, k, (px, py, c))
                for i in range(n) for k, (px, py) in enumerate(_other_chips(x, y))]
    return plan


def _split_start(name, plan, srcs, lands, deps=()):
    ns, nl, nd = len(srcs), len(lands), len(deps)
    n = max(ns, nl)

    def body(*refs):
        sems = refs[ns + nl + nd:ns + nl + nd + 2]
        for cp in plan(refs[:ns], refs[ns:ns + nl], sems):
            cp.start()
        refs[-1][...] = jnp.zeros_like(refs[-1])

    outs = pl.pallas_call(
        body, name=name,
        out_shape=(pltpu.SemaphoreType.DMA((n * SLOTS,)), pltpu.SemaphoreType.DMA((n * SLOTS,)),
                   *[pltpu.HBM(a.shape, a.dtype) for a in lands], jax.ShapeDtypeStruct((8, 128), F32)),
        in_specs=[HBM] * (ns + nl) + [ANY] * nd,
        out_specs=(SEM, SEM, *[HBM] * nl, pl.BlockSpec(memory_space=pltpu.VMEM)),
        input_output_aliases={ns + i: 2 + i for i in range(nl)},
        compiler_params=pltpu.CompilerParams(has_side_effects=EFFECT),
    )(*[_hbm(a) for a in srcs], *[_hbm(a) for a in lands], *deps)
    return (outs[0], outs[1]), list(outs[2:2 + nl]), outs[-1]


def _split_wait(name, plan, sems, srcs, lands, deps=()):
    ns, nl, nd = len(srcs), len(lands), len(deps)

    def body(*refs):
        for cp in plan(refs[:ns], refs[ns:ns + nl], refs[ns + nl:ns + nl + 2]):
            cp.wait_send()
            cp.wait_recv()

    outs = pl.pallas_call(
        body, name=name,
        out_shape=tuple(pltpu.HBM(a.shape, a.dtype) for a in lands),
        in_specs=[HBM] * (ns + nl) + [SEM, SEM] + [ANY] * nd,
        out_specs=tuple([HBM] * nl),
        input_output_aliases={ns + i: i for i in range(nl)},
        compiler_params=pltpu.CompilerParams(has_side_effects=EFFECT),
    )(*srcs, *lands, *sems, *deps)
    return list(outs)


def _place_own(srcs, layers):
    x, y, c = _my_place()
    zero = jnp.zeros((), jnp.int32)
    return [lax.dynamic_update_slice(lax.empty((N_DEV,) + a.shape[1:], a.dtype), a[j][None],
                                     (_slot(x, y, c),) + (zero,) * (a.ndim - 1))
            for a, j in zip(srcs, layers)]


def _with_deps(body, n_in, deps):
    nd = len(deps)
    if not nd:
        return body

    def wrapped(*refs):
        body(*refs[:n_in], *refs[n_in + nd:])

    return wrapped


def _rmsnorm_fwd(name, x, g, deps=()):
    T, D = x.shape
    tm = _tile(T, 256)

    def body(x_ref, g_ref, o_ref):
        xv = x_ref[...]
        r = lax.rsqrt(jnp.mean(xv * xv, axis=-1, keepdims=True) + EPS)
        o_ref[...] = ((xv * r) * g_ref[...]).astype(o_ref.dtype)

    return pl.pallas_call(
        _with_deps(body, 2, deps), name=name, grid=(T // tm,),
        in_specs=[pl.BlockSpec((tm, D), lambda i: (i, 0)), pl.BlockSpec((1, D), lambda i: (0, 0))]
        + [ANY] * len(deps),
        out_specs=pl.BlockSpec((tm, D), lambda i: (i, 0)),
        out_shape=jax.ShapeDtypeStruct((T, D), MXU_DTYPE), compiler_params=_cp("parallel"),
    )(x, g, *deps)


def _rmsnorm_bwd(name, x, g, dh, dres):
    T, D = x.shape
    tm = _tile(T, 256)

    def body(x_ref, g_ref, dh_ref, dres_ref, dx_ref, dxb_ref, dg_ref):
        xv = x_ref[...]
        r = lax.rsqrt(jnp.mean(xv * xv, axis=-1, keepdims=True) + EPS)
        xh = xv * r
        dhv = dh_ref[...]

        @pl.when(pl.program_id(0) == 0)
        def _():
            dg_ref[...] = jnp.zeros_like(dg_ref)

        dg_ref[...] += jnp.sum(dhv * xh, axis=0, keepdims=True)
        dxh = dhv * g_ref[...]
        dx = dres_ref[...] + r * (dxh - xh * jnp.mean(dxh * xh, axis=-1, keepdims=True))
        dx_ref[...] = dx
        dxb_ref[...] = dx.astype(dxb_ref.dtype)

    return pl.pallas_call(
        body, name=name, grid=(T // tm,),
        in_specs=[pl.BlockSpec((tm, D), lambda i: (i, 0)), pl.BlockSpec((1, D), lambda i: (0, 0)),
                  pl.BlockSpec((tm, D), lambda i: (i, 0)), pl.BlockSpec((tm, D), lambda i: (i, 0))],
        out_specs=[pl.BlockSpec((tm, D), lambda i: (i, 0)), pl.BlockSpec((tm, D), lambda i: (i, 0)),
                   pl.BlockSpec((1, D), lambda i: (0, 0))],
        out_shape=[jax.ShapeDtypeStruct((T, D), F32), jax.ShapeDtypeStruct((T, D), MXU_DTYPE),
                   jax.ShapeDtypeStruct((1, D), F32)],
        compiler_params=_cp("arbitrary"),
    )(x, g, dh, dres)


def _loss_head(name, x, g, tgt):
    T, D = x.shape
    tm = _tile(T, 256)

    def body(x_ref, g_ref, t_ref, loss_ref, dx_ref, dxb_ref, dg_ref):
        xv = x_ref[...]
        r = lax.rsqrt(jnp.mean(xv * xv, axis=-1, keepdims=True) + EPS)
        xh = xv * r
        gv = g_ref[...]
        diff = xh * gv - t_ref[...]

        @pl.when(pl.program_id(0) == 0)
        def _():
            dg_ref[...] = jnp.zeros_like(dg_ref)
            loss_ref[...] = jnp.zeros_like(loss_ref)

        part = 0.5 * jnp.sum(jnp.mean(diff * diff, axis=-1, keepdims=True), axis=0, keepdims=True)
        loss_ref[...] += jnp.broadcast_to(part, loss_ref.shape)
        dy = diff * (1.0 / D)
        dg_ref[...] += jnp.sum(dy * xh, axis=0, keepdims=True)
        dxh = dy * gv
        dx = r * (dxh - xh * jnp.mean(dxh * xh, axis=-1, keepdims=True))
        dx_ref[...] = dx
        dxb_ref[...] = dx.astype(dxb_ref.dtype)

    return pl.pallas_call(
        body, name=name, grid=(T // tm,),
        in_specs=[pl.BlockSpec((tm, D), lambda i: (i, 0)), pl.BlockSpec((1, D), lambda i: (0, 0)),
                  pl.BlockSpec((tm, D), lambda i: (i, 0))],
        out_specs=[pl.BlockSpec((1, 128), lambda i: (0, 0)), pl.BlockSpec((tm, D), lambda i: (i, 0)),
                   pl.BlockSpec((tm, D), lambda i: (i, 0)), pl.BlockSpec((1, D), lambda i: (0, 0))],
        out_shape=[jax.ShapeDtypeStruct((1, 128), F32), jax.ShapeDtypeStruct((T, D), F32),
                   jax.ShapeDtypeStruct((T, D), MXU_DTYPE), jax.ShapeDtypeStruct((1, D), F32)],
        compiler_params=_cp("arbitrary"),
    )(x, g, tgt)


def _adamw(name, w, m, v, parts, off=0, prev=None):
    Rtot, C = w.shape
    P, R = parts.shape[:2]
    tr = _tile(R, 256)
    assert off % tr == 0
    ob = off // tr
    c1 = 1.0 / (1.0 - ADAM_B1 ** ADAM_STEP)
    c2 = 1.0 / (1.0 - ADAM_B2 ** ADAM_STEP)
    chained = R != Rtot
    if chained and prev is None:
        prev = [lax.empty((Rtot, C), F32) for _ in range(4)]
    prev = list(prev) if chained else []

    def body(w_ref, m_ref, v_ref, p_ref, *rest):
        g_ref, d_ref, nm_ref, nv_ref = rest[len(prev):]
        g = p_ref[0].astype(F32)
        for s in range(1, P):
            g = g + p_ref[s].astype(F32)
        nm = ADAM_B1 * m_ref[...] + (1.0 - ADAM_B1) * g
        nv = ADAM_B2 * v_ref[...] + (1.0 - ADAM_B2) * (g * g)
        g_ref[...] = g
        nm_ref[...] = nm
        nv_ref[...] = nv
        d_ref[...] = -ADAM_LR * ((nm * c1) / (jnp.sqrt(nv * c2) + ADAM_EPS) + ADAM_WD * w_ref[...])

    blk = pl.BlockSpec((tr, C), lambda i: (i + ob, 0))
    return pl.pallas_call(
        body, name=name, grid=(R // tr,),
        in_specs=[blk, blk, blk, pl.BlockSpec((P, tr, C), lambda i: (0, i, 0))] + [ANY] * len(prev),
        out_specs=[blk, blk, blk, blk],
        out_shape=[jax.ShapeDtypeStruct((Rtot, C), F32)] * 4,
        input_output_aliases={4 + i: i for i in range(len(prev))},
        compiler_params=_cp("parallel"),
    )(w, m, v, parts, *prev)


def _colsum(name, a):
    T, N = a.shape
    tm = _tile(T, 256)

    def body(a_ref, o_ref):
        @pl.when(pl.program_id(0) == 0)
        def _():
            o_ref[...] = jnp.zeros_like(o_ref)

        o_ref[...] += jnp.sum(a_ref[...].astype(F32), axis=0, keepdims=True)

    return pl.pallas_call(
        body, name=name, grid=(T // tm,),
        in_specs=[pl.BlockSpec((tm, N), lambda i: (i, 0))],
        out_specs=pl.BlockSpec((1, N), lambda i: (0, 0)),
        out_shape=jax.ShapeDtypeStruct((1, N), F32), compiler_params=_cp("arbitrary"),
    )(a)


def _proj_in(name, h, wg, bias):
    T, K = h.shape
    n = wg.shape[-1]
    tm = _tile(T, 512)

    def body(a_ref, w_ref, b_ref, o_ref):
        o_ref[...] = _dot(a_ref[...], w_ref[...]) + b_ref[...]

    return pl.pallas_call(
        body, name=name, grid=(N_DEV, T // tm),
        in_specs=[pl.BlockSpec((tm, K), lambda s, i: (i, 0)),
                  pl.BlockSpec((None, K, n), lambda s, i: (s, 0, 0)),
                  pl.BlockSpec((1, n), lambda s, i: (0, s))],
        out_specs=pl.BlockSpec((tm, n), lambda s, i: (i, s)),
        out_shape=jax.ShapeDtypeStruct((T, N_DEV * n), F32), compiler_params=_cp("parallel", "parallel"),
    )(h, wg, bias)


def _ffn_in(name, h, wg_gate, wg_up):
    T, K = h.shape
    n = wg_gate.shape[-1]
    tm = _tile(T, 512)

    def body(a_ref, wgt_ref, wup_ref, g_ref, u_ref, hid_ref):
        a = a_ref[...]
        g = _dot(a, wgt_ref[...])
        u = _dot(a, wup_ref[...])
        g_ref[...] = g
        u_ref[...] = u
        hid_ref[...] = (_silu(g) * u).astype(hid_ref.dtype)

    wspec = pl.BlockSpec((None, K, n), lambda s, i: (s, 0, 0))
    ospec = pl.BlockSpec((None, tm, n), lambda s, i: (s, i, 0))
    return pl.pallas_call(
        body, name=name, grid=(N_DEV, T // tm),
        in_specs=[pl.BlockSpec((tm, K), lambda s, i: (i, 0)), wspec, wspec],
        out_specs=[ospec, ospec, ospec],
        out_shape=[jax.ShapeDtypeStruct((N_DEV, T, n), F32), jax.ShapeDtypeStruct((N_DEV, T, n), F32),
                   jax.ShapeDtypeStruct((N_DEV, T, n), MXU_DTYPE)],
        compiler_params=_cp("parallel", "parallel"),
    )(h, wg_gate, wg_up)


def _a_spec(a, tm, k):
    if a.ndim == 2:
        return pl.BlockSpec((tm, k), lambda i, s: (i, s))
    return pl.BlockSpec((None, tm, k), lambda i, s: (s, i, 0))


def _proj_out(name, a, wg, res, deps=()):
    k, N = wg.shape[-2:]
    T = res.shape[0]
    tm = _tile(T, 512)

    def body(a_ref, w_ref, r_ref, o_ref):
        p = _dot(a_ref[...], w_ref[...])

        @pl.when(pl.program_id(1) == 0)
        def _():
            o_ref[...] = r_ref[...] + p

        @pl.when(pl.program_id(1) > 0)
        def _():
            o_ref[...] += p

    return pl.pallas_call(
        _with_deps(body, 3, deps), name=name, grid=(T // tm, N_DEV),
        in_specs=[_a_spec(a, tm, k), pl.BlockSpec((None, k, N), lambda i, s: (s, 0, 0)),
                  pl.BlockSpec((tm, N), lambda i, s: (i, 0))] + [ANY] * len(deps),
        out_specs=pl.BlockSpec((tm, N), lambda i, s: (i, 0)),
        out_shape=jax.ShapeDtypeStruct((T, N), F32), compiler_params=_cp("parallel", "arbitrary"),
    )(a, wg, res, *deps)


def _bwd_in(name, das, wgs, deps=()):
    K, n = wgs[0].shape[-2:]
    T = das[0].shape[-2]
    tm = _tile(T, 1024)
    npair = len(das)

    def body(*refs):
        o_ref = refs[-1]
        p = _dot_nt(refs[0][...], refs[npair][...])
        for q in range(1, npair):
            p = p + _dot_nt(refs[q][...], refs[npair + q][...])

        @pl.when(pl.program_id(1) == 0)
        def _():
            o_ref[...] = p

        @pl.when(pl.program_id(1) > 0)
        def _():
            o_ref[...] += p

    return pl.pallas_call(
        _with_deps(body, 2 * npair, deps), name=name, grid=(T // tm, N_DEV),
        in_specs=[_a_spec(a, tm, n) for a in das]
        + [pl.BlockSpec((None, K, n), lambda i, s: (s, 0, 0)) for _ in wgs] + [ANY] * len(deps),
        out_specs=pl.BlockSpec((tm, K), lambda i, s: (i, 0)),
        out_shape=jax.ShapeDtypeStruct((T, K), F32), compiler_params=_cp("parallel", "arbitrary"),
    )(*das, *wgs, *deps)


def _bwd_out(name, dx, wg):
    k, N = wg.shape[-2:]
    T = dx.shape[0]
    tm = _tile(T, 512)

    def body(a_ref, w_ref, o_ref):
        o_ref[...] = _dot_nt(a_ref[...], w_ref[...])

    return pl.pallas_call(
        body, name=name, grid=(N_DEV, T // tm),
        in_specs=[pl.BlockSpec((tm, N), lambda s, i: (i, 0)),
                  pl.BlockSpec((None, k, N), lambda s, i: (s, 0, 0))],
        out_specs=pl.BlockSpec((tm, k), lambda s, i: (i, s)),
        out_shape=jax.ShapeDtypeStruct((T, N_DEV * k), F32), compiler_params=_cp("parallel", "parallel"),
    )(dx, wg)


def _ffn_bwd_hidden(name, dx, wg_down, gate, up):
    n, N = wg_down.shape[-2:]
    T = dx.shape[0]
    tm = _tile(T, 512)

    def body(a_ref, w_ref, g_ref, u_ref, dg_ref, du_ref):
        dh = _dot_nt(a_ref[...], w_ref[...])
        g = g_ref[...]
        dg_ref[...] = (dh * u_ref[...] * _dsilu(g)).astype(dg_ref.dtype)
        du_ref[...] = (dh * _silu(g)).astype(du_ref.dtype)

    sm = pl.BlockSpec((None, tm, n), lambda s, i: (s, i, 0))
    return pl.pallas_call(
        body, name=name, grid=(N_DEV, T // tm),
        in_specs=[pl.BlockSpec((tm, N), lambda s, i: (i, 0)),
                  pl.BlockSpec((None, n, N), lambda s, i: (s, 0, 0)), sm, sm],
        out_specs=[sm, sm],
        out_shape=[jax.ShapeDtypeStruct((N_DEV, T, n), MXU_DTYPE)] * 2,
        compiler_params=_cp("parallel", "parallel"),
    )(dx, wg_down, gate, up)


def _wgrad(name, a, c, rows, cols, deps=()):
    T = a.shape[-2]
    tk = _tile(T, 1024)
    nk = T // tk

    def spec(z, w):
        if z.ndim == 3:
            return pl.BlockSpec((None, tk, w), lambda s, k: (s, k, 0))
        if z.shape[1] == w:
            return pl.BlockSpec((tk, w), lambda s, k: (k, 0))
        return pl.BlockSpec((tk, w), lambda s, k: (k, s))

    def body(a_ref, c_ref, o_ref, acc_ref):
        k = pl.program_id(1)
        p = _dot_tn(a_ref[...], c_ref[...])

        @pl.when(k == 0)
        def _():
            acc_ref[...] = p

        @pl.when(k > 0)
        def _():
            acc_ref[...] += p

        @pl.when(k == nk - 1)
        def _():
            o_ref[...] = acc_ref[...].astype(o_ref.dtype)

    return pl.pallas_call(
        _with_deps(body, 2, deps), name=name, grid=(N_DEV, nk),
        in_specs=[spec(a, rows), spec(c, cols)] + [ANY] * len(deps),
        out_specs=pl.BlockSpec((None, rows, cols), lambda s, k: (s, 0, 0)),
        out_shape=jax.ShapeDtypeStruct((N_DEV, rows, cols), WIRE_DTYPE),
        scratch_shapes=[pltpu.VMEM((rows, cols), F32)],
        compiler_params=_cp("parallel", "arbitrary"),
    )(a, c, *deps)


def _shift_down(cur, prev8, sh):
    n = cur.shape[0]
    rolled = pltpu.roll(cur, sh, 0)
    top = jnp.where(_rows(8) < sh, pltpu.roll(prev8, sh, 0), rolled[0:8])
    return jnp.concatenate([top, rolled[8:n]], axis=0)


def _shift_up(cur, next8, sh):
    n = cur.shape[0]
    rolled = pltpu.roll(cur, n - sh, 0)
    bot = jnp.where(_rows(8) >= 8 - sh, pltpu.roll(next8, 8 - sh, 0), rolled[n - 8:n])
    return jnp.concatenate([rolled[0:n - 8], bot], axis=0)


def _lru_gate_terms(r, lam):
    sp = _softplus(-lam)
    la = -LRU_C * r * sp
    a = jnp.exp(la)
    m2 = _one_minus_exp(2.0 * la)
    return sp, la, a, m2


def _lru_fwd(name, proj, conv_w, conv_b, wa, ba, wx, bx, lam):
    T = proj.shape[0]
    H, hd, _ = wa.shape
    W = H * hd
    K = conv_w.shape[0]
    tb = _tile(T, 256)

    def body(xin_ref, gate_ref, cw_ref, cb_ref, wa_ref, ba_ref, wx_ref, bx_ref, lam_ref,
             ya_ref, xc_ref, r_ref, i_ref, hs_ref, tail_ref, hprev_ref):
        blk = pl.program_id(0)

        @pl.when(blk == 0)
        def _():
            tail_ref[...] = jnp.zeros_like(tail_ref)
            hprev_ref[...] = jnp.zeros_like(hprev_ref)

        xin = xin_ref[...]
        prev8 = tail_ref[...]
        xc = cw_ref[K - 1:K, :] * xin
        for sh in range(1, K):
            xc = xc + cw_ref[K - 1 - sh:K - sh, :] * _shift_down(xin, prev8, sh)
        xc = xc + cb_ref[...]
        tail_ref[...] = xin[tb - 8:tb]
        xc_ref[...] = xc
        for h in range(H):
            cs = slice(h * hd, (h + 1) * hd)
            xh = xc[:, cs]
            r_ref[:, cs] = _sigmoid(_dot(xh, wa_ref[h]) + ba_ref[:, cs])
            i_ref[:, cs] = _sigmoid(_dot(xh, wx_ref[h]) + bx_ref[:, cs])
        r = r_ref[...]
        _, _, a, m2 = _lru_gate_terms(r, lam_ref[...])
        row = _rows(tb)
        mult = jnp.where((row == 0) & (blk == 0), 1.0, jnp.sqrt(jnp.maximum(m2, 0.0)))
        u = mult * i_ref[...] * xc
        d = 1
        while d < tb:
            keep = row >= d
            u = a * jnp.where(keep, pltpu.roll(u, d, 0), 0.0) + u
            a = a * jnp.where(keep, pltpu.roll(a, d, 0), 1.0)
            d *= 2
        hs = u + a * hprev_ref[...]
        hprev_ref[...] = hs[tb - 1:tb]
        hs_ref[...] = hs
        ya_ref[...] = (hs * _gelu(gate_ref[...])).astype(ya_ref.dtype)

    full = lambda shape: pl.BlockSpec(shape, lambda i: tuple(0 for _ in shape))
    blk = pl.BlockSpec((tb, W), lambda i: (i, 0))
    return pl.pallas_call(
        body, name=name, grid=(T // tb,),
        in_specs=[pl.BlockSpec((tb, W), lambda i: (i, 0)), pl.BlockSpec((tb, W), lambda i: (i, 1)),
                  full((K, W)), full((1, W)), full((H, hd, hd)), full((1, W)), full((H, hd, hd)),
                  full((1, W)), full((1, W))],
        out_specs=[blk] * 5,
        out_shape=[jax.ShapeDtypeStruct((T, W), MXU_DTYPE)] + [jax.ShapeDtypeStruct((T, W), F32)] * 4,
        scratch_shapes=[pltpu.VMEM((8, W), F32), pltpu.VMEM((1, W), F32)],
        compiler_params=_cp("arbitrary"),
    )(proj, proj, conv_w, conv_b, wa, ba, wx, bx, lam)


def _lru_bwd(name, proj, dy, xc, r, ig, hs, conv_w, wa, wx, lam):
    T = proj.shape[0]
    H, hd, _ = wa.shape
    W = H * hd
    K = conv_w.shape[0]
    tb = _tile(T, 256)
    nb = T // tb
    t8 = tb // 8

    def body(xin_ref, xprev_ref, gate_ref, dy_ref, xc_ref, r_ref, i_ref, hs_ref, hsprev_ref,
             cw_ref, wa_ref, wx_ref, lam_ref,
             dp_ref, dcw_ref, dcb_ref, dwa_ref, dba_ref, dwx_ref, dbx_ref, dlam_ref,
             cdh_ref, ca_ref, cdxc_ref, dxc_ref):
        step = pl.program_id(0)
        blk = nb - 1 - step

        @pl.when(step == 0)
        def _():
            for ref in (dcw_ref, dcb_ref, dwa_ref, dba_ref, dwx_ref, dbx_ref, dlam_ref,
                        cdh_ref, ca_ref, cdxc_ref):
                ref[...] = jnp.zeros_like(ref)

        row = _rows(tb)
        first = blk == 0
        gate = gate_ref[...]
        dy_a = dy_ref[...]
        hsv = hs_ref[...]
        dp_ref[:, W:2 * W] = (dy_a * hsv * _dgelu(gate)).astype(dp_ref.dtype)
        d_hs = dy_a * _gelu(gate)
        lam = lam_ref[...]
        rv = r_ref[...]
        sp, la, a, m2 = _lru_gate_terms(rv, lam)
        an = jnp.where(row == tb - 1, ca_ref[...], pltpu.roll(a, tb - 1, 0))
        u = d_hs
        d = 1
        while d < tb:
            keep = row < tb - d
            u = an * jnp.where(keep, pltpu.roll(u, tb - d, 0), 0.0) + u
            an = an * jnp.where(keep, pltpu.roll(an, tb - d, 0), 1.0)
            d *= 2
        dh = u + an * cdh_ref[...]
        cdh_ref[...] = dh[0:1]
        ca_ref[...] = a[0:1]
        hlast = jnp.where(first, 0.0, hsprev_ref[7:8, :])
        hprev = jnp.where(row == 0, hlast, pltpu.roll(hsv, 1, 0))
        da = dh * hprev
        xcv = xc_ref[...]
        iv = i_ref[...]
        t0 = (row == 0) & first
        mult = jnp.sqrt(jnp.maximum(m2, 0.0))
        mult_eff = jnp.where(t0, 1.0, mult)
        d_mult = dh * iv * xcv
        d_i = dh * mult_eff * xcv
        dxc = dh * mult_eff * iv
        e2 = 1.0 - m2
        d_la = da * a + jnp.where(t0 | (m2 <= 0.0), 0.0, -d_mult * e2 / jnp.where(m2 > 0.0, mult, 1.0))
        d_r = d_la * (-LRU_C * sp)
        dlam_ref[...] += jnp.sum(d_la * (-LRU_C * rv), axis=0, keepdims=True) * (-_sigmoid(-lam))
        d_zr = d_r * rv * (1.0 - rv)
        d_zi = d_i * iv * (1.0 - iv)
        dba_ref[...] += jnp.sum(d_zr, axis=0, keepdims=True)
        dbx_ref[...] += jnp.sum(d_zi, axis=0, keepdims=True)
        for h in range(H):
            cs = slice(h * hd, (h + 1) * hd)
            xh = xcv[:, cs]
            zr, zi = d_zr[:, cs], d_zi[:, cs]
            dwa_ref[h] += _dot_tn(xh, zr)
            dwx_ref[h] += _dot_tn(xh, zi)
            dxc_ref[:, cs] = dxc[:, cs] + _dot_nt(zr, wa_ref[h]) + _dot_nt(zi, wx_ref[h])
        dxc = dxc_ref[...]
        dcb_ref[...] += jnp.sum(dxc, axis=0, keepdims=True)
        xin = xin_ref[...]
        prev8 = jnp.where(first, 0.0, xprev_ref[...])
        next8 = cdxc_ref[...]
        dxin = cw_ref[K - 1:K, :] * dxc
        dcw_ref[K - 1:K, :] += jnp.sum(dxc * xin, axis=0, keepdims=True)
        for sh in range(1, K):
            dxin = dxin + cw_ref[K - 1 - sh:K - sh, :] * _shift_up(dxc, next8, sh)
            dcw_ref[K - 1 - sh:K - sh, :] += jnp.sum(dxc * _shift_down(xin, prev8, sh), axis=0, keepdims=True)
        cdxc_ref[...] = dxc[0:8]
        dp_ref[:, 0:W] = dxin.astype(dp_ref.dtype)

    full = lambda shape: pl.BlockSpec(shape, lambda i: tuple(0 for _ in shape))
    cur = lambda col: pl.BlockSpec((tb, W), lambda i: (nb - 1 - i, col))
    prev = pl.BlockSpec((8, W), lambda i: (jnp.maximum((nb - 1 - i) * t8 - 1, 0), 0))
    return pl.pallas_call(
        body, name=name, grid=(nb,),
        in_specs=[cur(0), prev, cur(1), cur(0), cur(0), cur(0), cur(0), cur(0), prev,
                  full((K, W)), full((H, hd, hd)), full((H, hd, hd)), full((1, W))],
        out_specs=[pl.BlockSpec((tb, 2 * W), lambda i: (nb - 1 - i, 0)), full((K, W)), full((1, W)),
                   full((H, hd, hd)), full((1, W)), full((H, hd, hd)), full((1, W)), full((1, W))],
        out_shape=[jax.ShapeDtypeStruct((T, 2 * W), MXU_DTYPE), jax.ShapeDtypeStruct((K, W), F32),
                   jax.ShapeDtypeStruct((1, W), F32), jax.ShapeDtypeStruct((H, hd, hd), F32),
                   jax.ShapeDtypeStruct((1, W), F32), jax.ShapeDtypeStruct((H, hd, hd), F32),
                   jax.ShapeDtypeStruct((1, W), F32), jax.ShapeDtypeStruct((1, W), F32)],
        scratch_shapes=[pltpu.VMEM((1, W), F32), pltpu.VMEM((1, W), F32), pltpu.VMEM((8, W), F32),
                        pltpu.VMEM((tb, W), F32)],
        compiler_params=_cp("arbitrary"),
    )(proj, proj, proj, dy, xc, r, ig, hs, hs, conv_w, wa, wx, lam)


def _chunk_cumsum(g, c):
    n = g.shape[0]
    rc = _rows(n) & (c - 1)
    d = 1
    while d < c:
        g = g + jnp.where(rc >= d, pltpu.roll(g, d, 0), 0.0)
        d *= 2
    return g


def _chunk_rcumsum(g, c):
    n = g.shape[0]
    rc = _rows(n) & (c - 1)
    d = 1
    while d < c:
        g = g + jnp.where(rc < c - d, pltpu.roll(g, n - d, 0), 0.0)
        d *= 2
    return g


def _hgrn_pointwise(qr, fr, lb):
    qf = _silu(qr)
    sig = _sigmoid(fr)
    fg = lb + (1.0 - lb) * sig
    gl = jnp.log(jnp.maximum(fg, F_FLOOR))
    kk = (1.0 - lb) * (1.0 - sig)
    return qf, sig, fg, gl, kk


def _hgrn_fwd(name, proj, lb, norm_g):
    T = proj.shape[0]
    W = lb.shape[1]
    H = HGRN_HEADS
    dk = W // H
    c = HGRN_SUB
    R = _tile(T, 128)
    nck = R // c

    def body(q_ref, f_ref, v_ref, g_ref, lb_ref, ng_ref, yb_ref, o_ref, sall_ref,
             st_ref, qe_ref, ke_ref, acc_ref):
        @pl.when(pl.program_id(0) == 0)
        def _():
            st_ref[...] = jnp.zeros_like(st_ref)

        qf, _, _, gl, kk = _hgrn_pointwise(q_ref[...], f_ref[...], lb_ref[...])
        b = _chunk_cumsum(gl, c)
        rc = _rows(R) & (c - 1)
        for h in range(H):
            cs = slice(h * dk, (h + 1) * dk)
            qh, kh, bh, vh = qf[:, cs], kk[:, cs], b[:, cs], v_ref[:, cs]
            acc = jnp.sum(qh * kh, axis=1, keepdims=True) * vh
            for d in range(1, c):
                ok = rc >= d
                e = jnp.exp(jnp.where(ok, bh - pltpu.roll(bh, d, 0), 0.0))
                s = jnp.sum(qh * pltpu.roll(kh, d, 0) * e, axis=1, keepdims=True)
                acc = acc + jnp.where(ok, s, 0.0) * pltpu.roll(vh, d, 0)
            acc_ref[:, cs] = acc
        qe_ref[...] = qf * jnp.exp(b)
        for ci in range(nck):
            rs = slice(ci * c, (ci + 1) * c)
            bl = b[ci * c + c - 1:ci * c + c, :]
            ke_ref[rs, :] = kk[rs, :] * jnp.exp(bl - b[rs, :])
            ebl = jnp.exp(bl)
            for h in range(H):
                cs = slice(h * dk, (h + 1) * dk)
                st = st_ref[h]
                sall_ref[ci, h] = st
                o_ref[rs, cs] = acc_ref[rs, cs] + _dot_nt(qe_ref[rs, cs], st)
                st_ref[h] = st * ebl[:, cs] + _dot_tn(v_ref[rs, cs], ke_ref[rs, cs])
        ng = ng_ref[...]
        gg = g_ref[...]
        for h in range(H):
            cs = slice(h * dk, (h + 1) * dk)
            oh = o_ref[:, cs]
            rr = lax.rsqrt(jnp.mean(oh * oh, axis=1, keepdims=True) + EPS)
            yb_ref[:, cs] = ((oh * rr) * ng[:, cs] * _silu(gg[:, cs])).astype(yb_ref.dtype)

    full = lambda shape: pl.BlockSpec(shape, lambda i: tuple(0 for _ in shape))
    col = lambda k: pl.BlockSpec((R, W), lambda i: (i, k))
    blk = pl.BlockSpec((R, W), lambda i: (i, 0))
    return pl.pallas_call(
        body, name=name, grid=(T // R,),
        in_specs=[col(2), col(3), col(4), col(5), full((1, W)), full((1, W))],
        out_specs=[blk, blk, pl.BlockSpec((nck, H, dk, dk), lambda i: (i, 0, 0, 0))],
        out_shape=[jax.ShapeDtypeStruct((T, W), MXU_DTYPE), jax.ShapeDtypeStruct((T, W), F32),
                   jax.ShapeDtypeStruct((T // c, H, dk, dk), F32)],
        scratch_shapes=[pltpu.VMEM((H, dk, dk), F32), pltpu.VMEM((R, W), F32), pltpu.VMEM((R, W), F32),
                        pltpu.VMEM((R, W), F32)],
        compiler_params=_cp("arbitrary"),
    )(proj, proj, proj, proj, lb, norm_g)


def _hgrn_bwd(name, proj, dy, o, sall, lb, norm_g):
    T = proj.shape[0]
    W = lb.shape[1]
    H = HGRN_HEADS
    dk = W // H
    c = HGRN_SUB
    R = _tile(T, 128)
    nck = R // c
    nb = T // R

    def body(q_ref, f_ref, v_ref, g_ref, dy_ref, o_ref, sall_ref, lb_ref, ng_ref,
             dp_ref, dlb_ref, dng_ref,
             dst_ref, do_ref, dq_ref, dk_ref, dv_ref, ex_ref, qe_ref, ke_ref):
        @pl.when(pl.program_id(0) == 0)
        def _():
            dst_ref[...] = jnp.zeros_like(dst_ref)
            dlb_ref[...] = jnp.zeros_like(dlb_ref)
            dng_ref[...] = jnp.zeros_like(dng_ref)

        lbv = lb_ref[...]
        qr = q_ref[...]
        qf, sig, fg, gl, kk = _hgrn_pointwise(qr, f_ref[...], lbv)
        b = _chunk_cumsum(gl, c)
        rc = _rows(R) & (c - 1)
        ng = ng_ref[...]
        gg = g_ref[...]
        dyv = dy_ref[...]
        sg = _silu(gg)
        for h in range(H):
            cs = slice(h * dk, (h + 1) * dk)
            oh = o_ref[:, cs]
            rr = lax.rsqrt(jnp.mean(oh * oh, axis=1, keepdims=True) + EPS)
            ohat = oh * rr
            dyh = dyv[:, cs]
            dp_ref[:, 3 * W + h * dk:3 * W + (h + 1) * dk] = (
                dyh * ohat * ng[:, cs] * _dsilu(gg[:, cs])).astype(dp_ref.dtype)
            t = dyh * sg[:, cs]
            dng_ref[:, cs] += jnp.sum(t * ohat, axis=0, keepdims=True)
            dohat = t * ng[:, cs]
            do_ref[:, cs] = rr * (dohat - ohat * jnp.mean(dohat * ohat, axis=1, keepdims=True))
        for h in range(H):
            cs = slice(h * dk, (h + 1) * dk)
            qh, kh, bh, vh, doh = qf[:, cs], kk[:, cs], b[:, cs], v_ref[:, cs], do_ref[:, cs]
            da0 = jnp.sum(doh * vh, axis=1, keepdims=True)
            a0 = jnp.sum(qh * kh, axis=1, keepdims=True)
            dq = da0 * kh
            dkk = da0 * qh
            dv = a0 * doh
            for d in range(1, c):
                ok = rc >= d
                e = jnp.exp(jnp.where(ok, bh - pltpu.roll(bh, d, 0), 0.0))
                kr = pltpu.roll(kh, d, 0)
                da = jnp.where(ok, jnp.sum(doh * pltpu.roll(vh, d, 0), axis=1, keepdims=True), 0.0)
                aa = jnp.where(ok, jnp.sum(qh * kr * e, axis=1, keepdims=True), 0.0)
                dq = dq + da * kr * e
                dkk = dkk + pltpu.roll(da * qh * e, R - d, 0)
                dv = dv + pltpu.roll(aa * doh, R - d, 0)
            dq_ref[:, cs] = dq
            dk_ref[:, cs] = dkk
            dv_ref[:, cs] = dv
        eb = jnp.exp(b)
        qe_ref[...] = qf * eb
        ex_ref[...] = jnp.zeros_like(ex_ref)
        for ci in reversed(range(nck)):
            rs = slice(ci * c, (ci + 1) * c)
            bl = b[ci * c + c - 1:ci * c + c, :]
            ebl_rows = jnp.exp(bl - b[rs, :])
            ke_ref[rs, :] = kk[rs, :] * ebl_rows
            ebl = jnp.exp(bl)
            for h in range(H):
                cs = slice(h * dk, (h + 1) * dk)
                st0 = sall_ref[ci, h]
                dst1 = dst_ref[h]
                doc = do_ref[rs, cs]
                vc = v_ref[rs, cs]
                dq_ref[rs, cs] += _dot(doc, st0) * eb[rs, cs]
                dv_ref[rs, cs] += _dot_nt(ke_ref[rs, cs], dst1)
                dki = _dot(vc, dst1) * ebl_rows[:, cs]
                dk_ref[rs, cs] += dki
                ex_ref[ci * c + c - 1:ci * c + c, cs] = (
                    jnp.sum(dki * kk[rs, cs], axis=0, keepdims=True)
                    + ebl[:, cs] * jnp.sum(st0 * dst1, axis=0, keepdims=True))
                dst_ref[h] = dst1 * ebl[:, cs] + _dot_tn(doc, qe_ref[rs, cs])
        dq = dq_ref[...]
        dkk = dk_ref[...]
        db = qf * dq - kk * dkk + ex_ref[...]
        dgl = _chunk_rcumsum(db, c)
        dfg = jnp.where(fg > F_FLOOR, dgl / jnp.maximum(fg, F_FLOOR), 0.0)
        dsig = (dfg - dkk) * (1.0 - lbv)
        dlb_ref[...] += jnp.sum((dfg - dkk) * (1.0 - sig), axis=0, keepdims=True)
        dp_ref[:, 0:W] = (dq * _dsilu(qr)).astype(dp_ref.dtype)
        dp_ref[:, W:2 * W] = (dsig * sig * (1.0 - sig)).astype(dp_ref.dtype)
        dp_ref[:, 2 * W:3 * W] = dv_ref[...].astype(dp_ref.dtype)

    full = lambda shape: pl.BlockSpec(shape, lambda i: tuple(0 for _ in shape))
    col = lambda k: pl.BlockSpec((R, W), lambda i: (nb - 1 - i, k))
    scr = pltpu.VMEM((R, W), F32)
    return pl.pallas_call(
        body, name=name, grid=(nb,),
        in_specs=[col(2), col(3), col(4), col(5), col(1), col(0),
                  pl.BlockSpec((nck, H, dk, dk), lambda i: (nb - 1 - i, 0, 0, 0)), full((1, W)), full((1, W))],
        out_specs=[pl.BlockSpec((R, 4 * W), lambda i: (nb - 1 - i, 0)), full((1, W)), full((1, W))],
        out_shape=[jax.ShapeDtypeStruct((T, 4 * W), MXU_DTYPE), jax.ShapeDtypeStruct((1, W), F32),
                   jax.ShapeDtypeStruct((1, W), F32)],
        scratch_shapes=[pltpu.VMEM((H, dk, dk), F32), scr, scr, scr, scr, scr, scr, scr],
        compiler_params=_cp("arbitrary"),
    )(proj, proj, proj, proj, dy, o, sall, lb, norm_g)


ODD_HALO = 32


def _odd_fwd(name, proj, sc_w, cf_w, cf_b, ln_g, ln_b):
    T = proj.shape[0]
    W = sc_w.shape[1]
    K3, K31 = sc_w.shape[0], cf_w.shape[0]
    tb = _tile(T, 256)
    hb = tb // ODD_HALO
    n = tb + ODD_HALO

    def body(cur_ref, prev_ref, w3_ref, w31_ref, cb_ref, lg_ref, lbeta_ref, y_ref, d_ref):
        keep = (pl.program_id(0) > 0).astype(F32)
        sb = cur_ref[:, 0:W]
        p = cur_ref[:, W:2 * W] * cur_ref[:, 2 * W:3 * W]
        glu = cur_ref[:, 3 * W:4 * W] * _sigmoid(cur_ref[:, 4 * W:5 * W])
        p_prev = prev_ref[:, W:2 * W] * prev_ref[:, 2 * W:3 * W] * keep
        glu_prev = prev_ref[:, 3 * W:4 * W] * _sigmoid(prev_ref[:, 4 * W:5 * W]) * keep
        ext = jnp.concatenate([p_prev, p], axis=0)
        cp = w3_ref[K3 - 1:K3, :] * p
        for sh in range(1, K3):
            ext = pltpu.roll(ext, 1, 0)
            cp = cp + w3_ref[K3 - 1 - sh:K3 - sh, :] * ext[ODD_HALO:n]
        y_ref[:, 0:W] = (sb * cp).astype(y_ref.dtype)
        ext = jnp.concatenate([glu_prev, glu], axis=0)
        d = cb_ref[...] + w31_ref[K31 - 1:K31, :] * glu
        for sh in range(1, K31):
            ext = pltpu.roll(ext, 1, 0)
            d = d + w31_ref[K31 - 1 - sh:K31 - sh, :] * ext[ODD_HALO:n]
        d_ref[...] = d
        mu = jnp.mean(d, axis=1, keepdims=True)
        xc = d - mu
        rstd = lax.rsqrt(jnp.mean(xc * xc, axis=1, keepdims=True) + EPS)
        ln = (xc * rstd) * lg_ref[...] + lbeta_ref[...]
        y_ref[:, W:2 * W] = _silu(ln).astype(y_ref.dtype)

    full = lambda shape: pl.BlockSpec(shape, lambda i: tuple(0 for _ in shape))
    return pl.pallas_call(
        body, name=name, grid=(T // tb,),
        in_specs=[pl.BlockSpec((tb, 5 * W), lambda i: (i, 0)),
                  pl.BlockSpec((ODD_HALO, 5 * W), lambda i: (jnp.maximum(i * hb - 1, 0), 0)),
                  full((K3, W)), full((K31, W)), full((1, W)), full((1, W)), full((1, W))],
        out_specs=[pl.BlockSpec((tb, 2 * W), lambda i: (i, 0)), pl.BlockSpec((tb, W), lambda i: (i, 0))],
        out_shape=[jax.ShapeDtypeStruct((T, 2 * W), MXU_DTYPE), jax.ShapeDtypeStruct((T, W), F32)],
        compiler_params=_cp("parallel"),
    )(proj, proj, sc_w, cf_w, cf_b, ln_g, ln_b)


def _odd_bwd(name, proj, dy, dsave, sc_w, cf_w, ln_g, ln_b):
    T = proj.shape[0]
    W = sc_w.shape[1]
    K3, K31 = sc_w.shape[0], cf_w.shape[0]
    tb = _tile(T, 128)
    nb = T // tb
    hb = tb // ODD_HALO
    nh = T // ODD_HALO
    n = tb + ODD_HALO

    def body(cur_ref, prev_ref, next_ref, dy_ref, dyn_ref, d_ref, dn_ref,
             w3_ref, w31_ref, lg_ref, lbeta_ref,
             dp_ref, dw3_ref, dw31_ref, dcb_ref, dlg_ref, dlb_ref):
        i = pl.program_id(0)

        @pl.when(i == 0)
        def _():
            for ref in (dw3_ref, dw31_ref, dcb_ref, dlg_ref, dlb_ref):
                ref[...] = jnp.zeros_like(ref)

        keep_prev = (i > 0).astype(F32)
        keep_next = (i < nb - 1).astype(F32)
        sb = cur_ref[:, 0:W]
        scv = cur_ref[:, W:2 * W]
        svv = cur_ref[:, 2 * W:3 * W]
        cu = cur_ref[:, 3 * W:4 * W]
        sg = _sigmoid(cur_ref[:, 4 * W:5 * W])
        p = scv * svv
        glu = cu * sg
        p_prev = prev_ref[:, W:2 * W] * prev_ref[:, 2 * W:3 * W] * keep_prev
        glu_prev = prev_ref[:, 3 * W:4 * W] * _sigmoid(prev_ref[:, 4 * W:5 * W]) * keep_prev
        dext = jnp.concatenate([d_ref[...], dn_ref[...]], axis=0)
        dyd = jnp.concatenate([dy_ref[:, W:2 * W], dyn_ref[:, W:2 * W] * keep_next], axis=0)
        mu = jnp.mean(dext, axis=1, keepdims=True)
        xc = dext - mu
        rstd = lax.rsqrt(jnp.mean(xc * xc, axis=1, keepdims=True) + EPS)
        xh = xc * rstd
        lg = lg_ref[...]
        dln = dyd * _dsilu(xh * lg + lbeta_ref[...])
        dxh = dln * lg
        dd = rstd * (dxh - jnp.mean(dxh, axis=1, keepdims=True)
                     - xh * jnp.mean(dxh * xh, axis=1, keepdims=True))
        dlg_ref[...] += jnp.sum((dln * xh)[0:tb], axis=0, keepdims=True)
        dlb_ref[...] += jnp.sum(dln[0:tb], axis=0, keepdims=True)
        ddc = dd[0:tb]
        dcb_ref[...] += jnp.sum(ddc, axis=0, keepdims=True)
        dglu = w31_ref[K31 - 1:K31, :] * ddc
        ext = jnp.concatenate([glu_prev, glu], axis=0)
        dw31_ref[K31 - 1:K31, :] += jnp.sum(ddc * glu, axis=0, keepdims=True)
        up = dd
        for sh in range(1, K31):
            up = pltpu.roll(up, n - 1, 0)
            ext = pltpu.roll(ext, 1, 0)
            dglu = dglu + w31_ref[K31 - 1 - sh:K31 - sh, :] * up[0:tb]
            dw31_ref[K31 - 1 - sh:K31 - sh, :] += jnp.sum(ddc * ext[ODD_HALO:n], axis=0, keepdims=True)
        dp_ref[:, 3 * W:4 * W] = (dglu * sg).astype(dp_ref.dtype)
        dp_ref[:, 4 * W:5 * W] = (dglu * cu * sg * (1.0 - sg)).astype(dp_ref.dtype)
        dyc = dy_ref[:, 0:W]
        dcp = jnp.concatenate([dyc * sb, dyn_ref[:, 0:W] * next_ref[:, 0:W] * keep_next], axis=0)
        dcpc = dcp[0:tb]
        ext = jnp.concatenate([p_prev, p], axis=0)
        cp = w3_ref[K3 - 1:K3, :] * p
        dpp = w3_ref[K3 - 1:K3, :] * dcpc
        dw3_ref[K3 - 1:K3, :] += jnp.sum(dcpc * p, axis=0, keepdims=True)
        up = dcp
        for sh in range(1, K3):
            up = pltpu.roll(up, n - 1, 0)
            ext = pltpu.roll(ext, 1, 0)
            shifted = ext[ODD_HALO:n]
            cp = cp + w3_ref[K3 - 1 - sh:K3 - sh, :] * shifted
            dpp = dpp + w3_ref[K3 - 1 - sh:K3 - sh, :] * up[0:tb]
            dw3_ref[K3 - 1 - sh:K3 - sh, :] += jnp.sum(dcpc * shifted, axis=0, keepdims=True)
        dp_ref[:, 0:W] = (dyc * cp).astype(dp_ref.dtype)
        dp_ref[:, W:2 * W] = (dpp * svv).astype(dp_ref.dtype)
        dp_ref[:, 2 * W:3 * W] = (dpp * scv).astype(dp_ref.dtype)

    full = lambda shape: pl.BlockSpec(shape, lambda i: tuple(0 for _ in shape))
    prev_map = lambda i: (jnp.maximum(i * hb - 1, 0), 0)
    next_map = lambda i: (jnp.minimum((i + 1) * hb, nh - 1), 0)
    return pl.pallas_call(
        body, name=name, grid=(nb,),
        in_specs=[pl.BlockSpec((tb, 5 * W), lambda i: (i, 0)),
                  pl.BlockSpec((ODD_HALO, 5 * W), prev_map), pl.BlockSpec((ODD_HALO, 5 * W), next_map),
                  pl.BlockSpec((tb, 2 * W), lambda i: (i, 0)), pl.BlockSpec((ODD_HALO, 2 * W), next_map),
                  pl.BlockSpec((tb, W), lambda i: (i, 0)), pl.BlockSpec((ODD_HALO, W), next_map),
                  full((K3, W)), full((K31, W)), full((1, W)), full((1, W))],
        out_specs=[pl.BlockSpec((tb, 5 * W), lambda i: (i, 0)), full((K3, W)), full((K31, W)),
                   full((1, W)), full((1, W)), full((1, W))],
        out_shape=[jax.ShapeDtypeStruct((T, 5 * W), MXU_DTYPE), jax.ShapeDtypeStruct((K3, W), F32),
                   jax.ShapeDtypeStruct((K31, W), F32)] + [jax.ShapeDtypeStruct((1, W), F32)] * 3,
        compiler_params=_cp("arbitrary"),
    )(proj, proj, proj, dy, dy, dsave, dsave, sc_w, cf_w, ln_g, ln_b)


PACK_WIDTH = 1024


def _lower_bounds(logits):
    sm = jax.nn.softmax(logits.astype(F32), axis=0)
    return jnp.cumsum(sm, axis=0) - sm[0]


def _pack_rows(arrays):
    flat = jnp.concatenate([a.reshape(-1) for a in arrays])
    pad = (-flat.shape[0]) % (8 * PACK_WIDTH)
    return jnp.pad(flat, (0, pad)).reshape(-1, PACK_WIDTH)


def _unpack_rows(packed, shapes):
    flat = packed.reshape(-1)
    out, off = [], 0
    for s in shapes:
        sz = math.prod(s)
        out.append(flat[off:off + sz].reshape(s))
        off += sz
    return out


def _shards_last(a):
    n = a.shape[-1] // N_DEV
    return jnp.moveaxis(a.reshape(a.shape[:-1] + (N_DEV, n)), -2, 0)


def _unshard_last(a):
    a = jnp.moveaxis(a, 0, -2)
    return a.reshape(a.shape[:-2] + (a.shape[-2] * a.shape[-1],))


BIG = ("ev_w_in", "ev_w_out", "od_w_in", "od_w_out", "ffn_w_gate", "ffn_w_up", "ffn_w_down")
SMALL_SHARDED = ("lru_conv_w", "od_b_in", "sc_conv_w", "cf_conv_w", "cf_conv_b", "cf_ln_g", "cf_ln_b")
SMALL_REPL = ("ln_mix_g", "ln_ffn_g", "ln_final_g", "ev_b_in", "lru_conv_b", "lru_wa", "lru_ba", "lru_wx",
              "lru_bx", "lru_lambda", "hgrn_lb_logits", "hgrn_norm_g")
WEIGHTS = ("ln_mix_g", "ln_ffn_g", "ln_final_g", "ev_w_in", "ev_b_in", "lru_conv_w", "lru_conv_b", "lru_wa",
           "lru_ba", "lru_wx", "lru_bx", "lru_lambda", "hgrn_lb_logits", "hgrn_norm_g", "ev_w_out", "od_w_in",
           "od_b_in", "sc_conv_w", "cf_conv_w", "cf_conv_b", "cf_ln_g", "cf_ln_b", "od_w_out", "ffn_w_gate",
           "ffn_w_up", "ffn_w_down")


def _layer_weights(l):
    mix = ("ev_w_in", "ev_w_out") if l % 2 == 0 else ("od_w_in", "od_w_out")
    return [(mix[0], l // 2), (mix[1], l // 2), ("ffn_w_gate", l), ("ffn_w_up", l), ("ffn_w_down", l)]


class _MeshExchange:
    def __init__(self, w_bf, w, m, v, depth):
        self.w_bf, self.w, self.m, self.v, self.depth = w_bf, w, m, v, depth
        self.ready, self.flight, self.rs, self.adam = {}, {}, {}, {}

    @staticmethod
    def _names(l, grp):
        names = _layer_weights(l)
        return names[:2] if grp == "mix" else names[2:]

    def _own_start(self, l, grp, deps):
        names = self._names(l, grp)
        srcs = [self.w_bf[k] for k, _ in names]
        layers = [j for _, j in names]
        plan = _plan_gather_own(layers)
        sems, lands, tok = _split_start(f"ag_own_start_{grp}{l}", plan, srcs, _place_own(srcs, layers), deps)
        self.flight[l, grp] = (plan, sems, srcs, lands)
        return tok

    def _turn(self, l, grp, deps):
        plan, sems, srcs, lands = self.flight[l, grp]
        lands = _split_wait(f"ag_own_wait_{grp}{l}", plan, sems, srcs, lands, deps)
        plan = _plan_gather_pass(len(lands))
        sems, passed, tok = _split_start(f"ag_pass_start_{grp}{l}", plan, [], lands)
        self.flight[l, grp] = (plan, sems, [], passed)
        toks = [tok]
        nl, ng = (l, "ffn") if grp == "mix" else (l + 1, "mix")
        if nl < self.depth:
            toks.append(self._own_start(nl, ng, (tok,)))
        return tuple(toks)

    def _pass_wait(self, l, grp, deps):
        plan, sems, srcs, lands = self.flight.pop((l, grp))
        lands = _split_wait(f"ag_pass_wait_{grp}{l}", plan, sems, srcs, lands, deps)
        self.ready[l, grp] = dict(zip([k for k, _ in self._names(l, grp)], lands))

    def layer_begin(self, l):
        toks = ()
        if l == 0:
            self._own_start(0, "mix", ())
            toks = self._turn(0, "mix", ())
            self._pass_wait(0, "mix", ())
        return self.ready.pop((l, "mix")), toks

    def tick(self, l, t, after):
        if t == 2:
            return self._turn(l, "ffn", (after,))
        if t == 3:
            self._pass_wait(l, "ffn", (after,))
        if t == 4 and l + 1 < self.depth:
            return self._turn(l + 1, "mix", (after,))
        if t == 5 and l + 1 < self.depth:
            self._pass_wait(l + 1, "mix", (after,))
        return ()

    def ffn_weights(self, l):
        return self.ready.pop((l, "ffn"))

    def grads(self, tag, named):
        srcs = [g for _, _, g in named]
        lands = [lax.empty((4,) + g.shape[1:], g.dtype) for g in srcs]
        plan = _plan_scatter_pair(len(srcs))
        sems, lands, tok = _split_start(f"rs_pair_start_{tag}", plan, srcs, lands)
        self.rs[tag] = (named, plan, sems, srcs, lands)
        return (tok,)

    def grads_mid(self, tag, after):
        named, plan, sems, srcs, lands = self.rs[tag]
        got = _split_wait(f"rs_pair_wait_{tag}", plan, sems, srcs, lands, (after,))
        both = [_pair_add(f"pair_add_{tag}_{i}", a, b) for i, (a, b) in enumerate(zip(srcs, got))]
        self.rs[tag] = (named, [p for p, _ in both], [f for _, f in both])

    def grads_send(self, tag):
        named, parts, fins = self.rs[tag]
        plan = _plan_scatter_chips(len(parts))
        sems, fins, tok = _split_start(f"rs_chip_start_{tag}", plan, parts, fins)
        self.rs[tag] = (named, plan, sems, parts, fins)
        return (tok,)

    def grads_end(self, tag, after):
        named, plan, sems, parts, fins = self.rs.pop(tag)
        fins = _split_wait(f"rs_chip_wait_{tag}", plan, sems, parts, fins, (after,))
        for (k, j, _), fin in zip(named, fins):
            cols = fin.shape[-1]
            rows = fin.shape[-2]
            two_d = lambda a: a.reshape(-1, cols)
            self.adam[k] = _adamw(f"adamw_{k}_{j}", two_d(self.w[k]), two_d(self.m[k]), two_d(self.v[k]),
                                  fin, off=j * rows, prev=self.adam.get(k))

    def results(self, k):
        return [r.reshape(self.w[k].shape) for r in self.adam[k]]


def _local_step(x, tgt, p, ex):
    T, D = x.shape
    depth = p["ln_mix_g"].shape[0]
    lbs = _lower_bounds(p["hgrn_lb_logits"])
    row = lambda a: a.reshape(1, -1)
    saved = []
    for l in range(depth):
        j = l // 2
        wl, tok = ex.layer_begin(l)
        s = {"x": x}
        h = _rmsnorm_fwd(f"norm_mix{l}", x, row(p["ln_mix_g"][l]), tok)
        s["h"] = h
        if l % 2 == 0:
            proj = _proj_in(f"ev_in{l}", h, wl["ev_w_in"], row(p["ev_b_in"][j]))
            wa = p["lru_wa"][j].astype(MXU_DTYPE)
            wx = p["lru_wx"][j].astype(MXU_DTYPE)
            ya, xc, r, ig, hs = _lru_fwd(f"lru_fwd{l}", proj, p["lru_conv_w"][j], row(p["lru_conv_b"][j]),
                                         wa, row(p["lru_ba"][j]), wx, row(p["lru_bx"][j]),
                                         row(p["lru_lambda"][j]))
            yb, o, sall = _hgrn_fwd(f"hgrn_fwd{l}", proj, row(lbs[j]), row(p["hgrn_norm_g"][j]))
            y = jnp.concatenate([ya, yb], axis=1)
            s.update(proj=proj, xc=xc, r=r, ig=ig, hs=hs, o=o, sall=sall, wa=wa, wx=wx)
            x = _proj_out(f"ev_out{l}", y, wl["ev_w_out"], x, ex.tick(l, 2, y))
        else:
            proj = _proj_in(f"od_in{l}", h, wl["od_w_in"], row(p["od_b_in"][j]))
            y, dsave = _odd_fwd(f"odd_fwd{l}", proj, p["sc_conv_w"][j], p["cf_conv_w"][j],
                                row(p["cf_conv_b"][j]), row(p["cf_ln_g"][j]), row(p["cf_ln_b"][j]))
            s.update(proj=proj, dsave=dsave)
            x = _proj_out(f"od_out{l}", y, wl["od_w_out"], x, ex.tick(l, 2, y))
        s["y"] = y
        s["xmid"] = x
        ex.tick(l, 3, x)
        wl = {**wl, **ex.ffn_weights(l)}
        h2 = _rmsnorm_fwd(f"norm_ffn{l}", x, row(p["ln_ffn_g"][l]))
        gate, up, hid = _ffn_in(f"ffn_in{l}", h2, wl["ffn_w_gate"], wl["ffn_w_up"])
        x = _proj_out(f"ffn_out{l}", hid, wl["ffn_w_down"], x, ex.tick(l, 4, hid))
        ex.tick(l, 5, x)
        s.update(h2=h2, gate=gate, up=up, hid=hid, w=wl)
        saved.append(s)

    loss, dx, dxb, dg_final = _loss_head("loss_head", x, row(p["ln_final_g"]), tgt)

    gs = {k: [None] * p[k].shape[0] for k in SMALL_REPL + SMALL_SHARDED if k not in ("ln_final_g", "hgrn_lb_logits")}
    d_lb = [None] * (depth // 2 + depth % 2)
    tok = ()
    pending = None
    for l in reversed(range(depth)):
        j = l // 2
        s = saved[l]
        wl = s["w"]
        ffn_shape = wl["ffn_w_gate"].shape[1:]
        dwd = _wgrad(f"ffn_dwd{l}", s["hid"], dxb, ffn_shape[1], ffn_shape[0], tok)
        dgate, dup = _ffn_bwd_hidden(f"ffn_bwd_hid{l}", dxb, wl["ffn_w_down"], s["gate"], s["up"])
        dwg = _wgrad(f"ffn_dwg{l}", s["h2"], dgate, *ffn_shape)
        dwu = _wgrad(f"ffn_dwu{l}", s["h2"], dup, *ffn_shape)
        tok = ex.grads(f"ffn{l}", [("ffn_w_down", l, dwd), ("ffn_w_gate", l, dwg), ("ffn_w_up", l, dwu)])
        dh2 = _bwd_in(f"ffn_dh{l}", [dgate, dup], [wl["ffn_w_gate"], wl["ffn_w_up"]], tok)
        dx, dxb, dg = _rmsnorm_bwd(f"norm_ffn_bwd{l}", s["xmid"], row(p["ln_ffn_g"][l]), dh2, dx)
        gs["ln_ffn_g"][l] = dg[0]
        ex.grads_mid(f"ffn{l}", dxb)
        if pending is not None:
            ex.grads_end(pending, dxb)
        tok = ex.grads_send(f"ffn{l}")
        w_in, w_out = ("ev_w_in", "ev_w_out") if l % 2 == 0 else ("od_w_in", "od_w_out")
        dwo = _wgrad(f"mix_dwo{l}", s["y"], dxb, *wl[w_out].shape[1:], tok)
        if l % 2 == 0:
            dy = _bwd_out(f"ev_dy{l}", dxb, wl["ev_w_out"])
            dpa, d_cw, d_cb, d_wa, d_ba, d_wx, d_bx, d_lam = _lru_bwd(
                f"lru_bwd{l}", s["proj"], dy, s["xc"], s["r"], s["ig"], s["hs"], p["lru_conv_w"][j],
                s["wa"], s["wx"], row(p["lru_lambda"][j]))
            dph, dlb, dng = _hgrn_bwd(f"hgrn_bwd{l}", s["proj"], dy, s["o"], s["sall"], row(lbs[j]),
                                      row(p["hgrn_norm_g"][j]))
            dproj = jnp.concatenate([dpa, dph], axis=1)
            gs["lru_conv_w"][j], gs["lru_conv_b"][j] = d_cw, d_cb[0]
            gs["lru_wa"][j], gs["lru_ba"][j] = d_wa, d_ba.reshape(p["lru_ba"].shape[1:])
            gs["lru_wx"][j], gs["lru_bx"][j] = d_wx, d_bx.reshape(p["lru_bx"].shape[1:])
            gs["lru_lambda"][j], gs["hgrn_norm_g"][j] = d_lam[0], dng[0]
            d_lb[j] = dlb[0]
            gs["ev_b_in"][j] = _colsum(f"ev_db{l}", dproj)[0]
        else:
            dy = _bwd_out(f"od_dy{l}", dxb, wl["od_w_out"])
            dproj, d_w3, d_w31, d_cfb, d_lg, d_lbeta = _odd_bwd(
                f"odd_bwd{l}", s["proj"], dy, s["dsave"], p["sc_conv_w"][j], p["cf_conv_w"][j],
                row(p["cf_ln_g"][j]), row(p["cf_ln_b"][j]))
            gs["sc_conv_w"][j], gs["cf_conv_w"][j] = d_w3, d_w31
            gs["cf_conv_b"][j], gs["cf_ln_g"][j], gs["cf_ln_b"][j] = d_cfb[0], d_lg[0], d_lbeta[0]
            gs["od_b_in"][j] = _colsum(f"od_db{l}", dproj)[0]
        dwi = _wgrad(f"mix_dwi{l}", s["h"], dproj, *wl[w_in].shape[1:])
        tok = ex.grads(f"mix{l}", [(w_out, j, dwo), (w_in, j, dwi)])
        dh = _bwd_in(f"mix_dh{l}", [dproj], [wl[w_in]], tok)
        dx, dxb, dg = _rmsnorm_bwd(f"norm_mix_bwd{l}", s["x"], row(p["ln_mix_g"][l]), dh, dx)
        gs["ln_mix_g"][l] = dg[0]
        ex.grads_mid(f"mix{l}", dxb)
        ex.grads_end(f"ffn{l}", dxb)
        tok = ex.grads_send(f"mix{l}")
        pending = f"mix{l}"
    ex.grads_end(pending, dxb)

    small = {k: jnp.stack(v) for k, v in gs.items()}
    small["ln_final_g"] = dg_final[0]
    _, lb_vjp = jax.vjp(_lower_bounds, p["hgrn_lb_logits"])
    small["hgrn_lb_logits"] = lb_vjp(jnp.stack(d_lb))[0]
    return loss, dx, small


def kernel(x, ln_mix_g, ln_ffn_g, ln_final_g, ev_w_in, ev_b_in, lru_conv_w, lru_conv_b, lru_wa, lru_ba, lru_wx, lru_bx, lru_lambda, hgrn_lb_logits, hgrn_norm_g, ev_w_out, od_w_in, od_b_in, sc_conv_w, cf_conv_w, cf_conv_b, cf_ln_g, cf_ln_b, od_w_out, ffn_w_gate, ffn_w_up, ffn_w_down, loss_target, m_ln_mix_g, m_ln_ffn_g, m_ln_final_g, m_ev_w_in, m_ev_b_in, m_lru_conv_w, m_lru_conv_b, m_lru_wa, m_lru_ba, m_lru_wx, m_lru_bx, m_lru_lambda, m_hgrn_lb_logits, m_hgrn_norm_g, m_ev_w_out, m_od_w_in, m_od_b_in, m_sc_conv_w, m_cf_conv_w, m_cf_conv_b, m_cf_ln_g, m_cf_ln_b, m_od_w_out, m_ffn_w_gate, m_ffn_w_up, m_ffn_w_down, v_ln_mix_g, v_ln_ffn_g, v_ln_final_g, v_ev_w_in, v_ev_b_in, v_lru_conv_w, v_lru_conv_b, v_lru_wa, v_lru_ba, v_lru_wx, v_lru_bx, v_lru_lambda, v_hgrn_lb_logits, v_hgrn_norm_g, v_ev_w_out, v_od_w_in, v_od_b_in, v_sc_conv_w, v_cf_conv_w, v_cf_conv_b, v_cf_ln_g, v_cf_ln_b, v_od_w_out, v_ffn_w_gate, v_ffn_w_up, v_ffn_w_down):
    args = locals()
    w = {k: args[k] for k in WEIGHTS}
    m = {k: args["m_" + k] for k in WEIGHTS}
    v = {k: args["v_" + k] for k in WEIGHTS}
    assert x.shape[0] == 1
    T, D = x.shape[1:]

    local_shapes = [w[k].shape for k in SMALL_SHARDED]
    gathered = _all_gather("gather_small_params", [_pack_rows([w[k] for k in SMALL_SHARDED])])[0]
    p = {k: w[k] for k in SMALL_REPL}
    per_dev = [_unpack_rows(gathered[s], local_shapes) for s in range(N_DEV)]
    for i, k in enumerate(SMALL_SHARDED):
        p[k] = _unshard_last(jnp.stack([per_dev[s][i] for s in range(N_DEV)]))

    ex = _MeshExchange({k: w[k].astype(MXU_DTYPE) for k in BIG}, w, m, v, ln_mix_g.shape[0])
    loss_part, dx, small = _local_step(x[0], loss_target[0], p, ex)
    loss = lax.psum(loss_part[0, 0], ("x", "y", "c"))

    small_sh = jnp.stack([_pack_rows([_shards_last(small[k])[s] for k in SMALL_SHARDED]) for s in range(N_DEV)])
    got = _pair_exchange("small_grads_to_sibling", [small_sh])[0]
    final_small = _chip_exchange("small_grads_to_chips", [_pair_add("pair_add_small", small_sh, got)[0]])[0]
    repl_parts = _all_gather("gather_small_grads", [_pack_rows([small[k] for k in SMALL_REPL])])[0]

    out_g, out_d, out_m, out_v = {}, {}, {}, {}
    for k in BIG:
        out_g[k], out_d[k], out_m[k], out_v[k] = ex.results(k)
    res = _adamw("adamw_small_sharded", *[_pack_rows([t[k] for k in SMALL_SHARDED]) for t in (w, m, v)],
                 final_small)
    for o, r in zip((out_g, out_d, out_m, out_v), res):
        o.update(zip(SMALL_SHARDED, _unpack_rows(r, local_shapes)))
    res = _adamw("adamw_small_repl", *[_pack_rows([t[k] for k in SMALL_REPL]) for t in (w, m, v)], repl_parts)
    for o, r in zip((out_g, out_d, out_m, out_v), res):
        o.update(zip(SMALL_REPL, _unpack_rows(r, [w[k].shape for k in SMALL_REPL])))

    return (loss, dx[None], *[out_g[k] for k in WEIGHTS], *[out_d[k] for k in WEIGHTS],
            *[out_m[k] for k in WEIGHTS], *[out_v[k] for k in WEIGHTS])
```

```python
import functools
import math

import jax
import jax.numpy as jnp
from jax import lax
from jax.experimental import pallas as pl
from jax.experimental.pallas import tpu as pltpu

F32 = jnp.float32
MXU_DTYPE = jnp.bfloat16
WIRE_DTYPE = jnp.bfloat16
N_DEV = 8
EPS = 1e-6
F_FLOOR = 1e-30
LRU_C = 8.0
HGRN_HEADS = 8
HGRN_SUB = 16
ADAM_LR, ADAM_B1, ADAM_B2, ADAM_EPS, ADAM_WD, ADAM_STEP = 0.001, 0.9, 0.999, 1e-08, 0.01, 10
V7X_VMEM_LIMIT = 48 * 1024 * 1024
MM_ROWS = 1024
WGRAD_ROWS = 2048
MESH = pl.DeviceIdType.MESH
ANY = pl.BlockSpec(memory_space=pl.ANY)


def _cp(*sem):
    return pltpu.CompilerParams(dimension_semantics=sem or None, vmem_limit_bytes=V7X_VMEM_LIMIT)


def _sigmoid(x):
    return 1.0 / (1.0 + jnp.exp(-x))


def _silu(x):
    return x * _sigmoid(x)


def _dsilu(x):
    s = _sigmoid(x)
    return s * (1.0 + x * (1.0 - s))


_GELU_C = math.sqrt(2.0 / math.pi)


def _gelu(x):
    return 0.5 * x * (1.0 + jnp.tanh(_GELU_C * (x + 0.044715 * x * x * x)))


def _dgelu(x):
    t = jnp.tanh(_GELU_C * (x + 0.044715 * x * x * x))
    return 0.5 * (1.0 + t) + 0.5 * x * (1.0 - t * t) * _GELU_C * (1.0 + 3.0 * 0.044715 * x * x)


def _log1p(e):
    return jnp.where(e < 1e-2, e * (1.0 - e * (0.5 - e * (1.0 / 3.0))), jnp.log(1.0 + e))


def _softplus(x):
    return jnp.maximum(x, 0.0) + _log1p(jnp.exp(-jnp.abs(x)))


def _one_minus_exp(x):
    series = -x * (1.0 + x * (0.5 + x * (1.0 / 6.0 + x * (1.0 / 24.0))))
    return jnp.where(x > -0.05, series, 1.0 - jnp.exp(x))


def _rows(n, d=1):
    return lax.broadcasted_iota(jnp.int32, (n, d), 0)


def _dot(a, b):
    return jnp.dot(a.astype(MXU_DTYPE), b.astype(MXU_DTYPE), preferred_element_type=F32)


def _dot_nt(a, b):
    return lax.dot_general(a.astype(MXU_DTYPE), b.astype(MXU_DTYPE), (((1,), (1,)), ((), ())),
                           preferred_element_type=F32)


def _dot_tn(a, b):
    return lax.dot_general(a.astype(MXU_DTYPE), b.astype(MXU_DTYPE), (((0,), (0,)), ((), ())),
                           preferred_element_type=F32)


def _tile(n, want):
    if n <= want:
        return n
    t = want - want % 8
    while n % t:
        t -= 8
    assert t > 0, (n, want)
    return t


def _my_place():
    x, y, c = lax.axis_index("x"), lax.axis_index("y"), lax.axis_index("c")
    return x, y, c


def _all_gather(name, srcs):
    n = len(srcs)

    def body(*refs):
        src_refs, out_refs = refs[:n], refs[n:2 * n]
        send_sems, recv_sems, local_sems = refs[2 * n:]
        x, y, c = _my_place()
        sibling = (x, y, 1 - c)
        chips = [(1 - x, y), (x, 1 - y), (1 - x, 1 - y)]

        def slot(px, py, pc):
            return 4 * px + 2 * py + pc

        def copy(i, k, block, to, src=None):
            dst = out_refs[i].at[slot(*block)]
            return pltpu.make_async_remote_copy(
                src_ref=dst if src is None else src, dst_ref=dst,
                send_sem=send_sems.at[i, k], recv_sem=recv_sems.at[i, k],
                device_id=to, device_id_type=MESH)

        me = (x, y, c)
        sends, own = [], []
        for i in range(n):
            mine = pltpu.make_async_copy(src_refs[i], out_refs[i].at[slot(*me)], local_sems.at[i])
            mine.start()
            own.append(mine)
            first = [copy(i, 0, me, sibling, src=src_refs[i])]
            first += [copy(i, 1 + j, me, (*chip, c), src=src_refs[i]) for j, chip in enumerate(chips)]
            for cp in first:
                cp.start()
            sends += first
        for i in range(n):
            for j, chip in enumerate(chips):
                copy(i, 1 + j, (*chip, c), me).wait_recv()
                passed = copy(i, 4 + j, (*chip, c), sibling)
                passed.start()
                sends.append(passed)
        for i in range(n):
            copy(i, 0, sibling, me).wait_recv()
            for j, chip in enumerate(chips):
                copy(i, 4 + j, (*chip, 1 - c), me).wait_recv()
        for cp in sends:
            cp.wait_send()
        for cp in own:
            cp.wait()

    outs = pl.pallas_call(
        body, name=name,
        out_shape=[jax.ShapeDtypeStruct((N_DEV,) + s.shape, s.dtype) for s in srcs],
        in_specs=[ANY] * n, out_specs=[ANY] * n,
        scratch_shapes=[pltpu.SemaphoreType.DMA((n, 7)), pltpu.SemaphoreType.DMA((n, 7)),
                        pltpu.SemaphoreType.DMA((n,))],
    )(*srcs)
    return list(outs)


def _pair_exchange(name, srcs):
    n = len(srcs)

    def body(*refs):
        src_refs, out_refs = refs[:n], refs[n:2 * n]
        send_sems, recv_sems = refs[2 * n:]
        x, y, c = _my_place()
        copies = []
        for i in range(n):
            for j in range(4):
                cp = pltpu.make_async_remote_copy(
                    src_ref=src_refs[i].at[2 * j + (1 - c)], dst_ref=out_refs[i].at[j],
                    send_sem=send_sems.at[i, j], recv_sem=recv_sems.at[i, j],
                    device_id=(x, y, 1 - c), device_id_type=MESH)
                cp.start()
                copies.append(cp)
        for cp in copies:
            cp.wait()

    outs = pl.pallas_call(
        body, name=name,
        out_shape=[jax.ShapeDtypeStruct((4,) + s.shape[1:], s.dtype) for s in srcs],
        in_specs=[ANY] * n, out_specs=[ANY] * n,
        scratch_shapes=[pltpu.SemaphoreType.DMA((n, 4)), pltpu.SemaphoreType.DMA((n, 4))],
    )(*srcs)
    return list(outs)


def _chip_exchange(name, srcs):
    n = len(srcs)

    def body(*refs):
        src_refs, out_refs = refs[:n], refs[n:2 * n]
        send_sems, recv_sems, local_sems = refs[2 * n:]
        x, y, c = _my_place()
        chip = 2 * x + y
        copies = []
        for i in range(n):
            mine = pltpu.make_async_copy(src_refs[i].at[chip], out_refs[i].at[3], local_sems.at[i])
            mine.start()
            copies.append(mine)
            for k, (fx, fy) in enumerate([(1, 0), (0, 1), (1, 1)]):
                px = x + fx - 2 * x * fx
                py = y + fy - 2 * y * fy
                cp = pltpu.make_async_remote_copy(
                    src_ref=src_refs[i].at[2 * px + py], dst_ref=out_refs[i].at[k],
                    send_sem=send_sems.at[i, k], recv_sem=recv_sems.at[i, k],
                    device_id=(px, py, c), device_id_type=MESH)
                cp.start()
                copies.append(cp)
        for cp in copies:
            cp.wait()

    outs = pl.pallas_call(
        body, name=name,
        out_shape=[jax.ShapeDtypeStruct(s.shape, s.dtype) for s in srcs],
        in_specs=[ANY] * n, out_specs=[ANY] * n,
        scratch_shapes=[pltpu.SemaphoreType.DMA((n, 3)), pltpu.SemaphoreType.DMA((n, 3)),
                        pltpu.SemaphoreType.DMA((n,))],
    )(*srcs)
    return list(outs)


def _pair_add(name, mine, got):
    assert mine.shape[0] == N_DEV and got.shape[0] == 4
    cdim = mine.shape[-1]
    m4 = mine.reshape(4, 2, -1, cdim)
    g3 = got.reshape(4, -1, cdim)
    rows = m4.shape[2]
    tr = _tile(rows, 512)

    def body(m_ref, g_ref, o_ref, fin_ref):
        x, y, c = _my_place()
        s = (m_ref[c].astype(F32) + g_ref[...].astype(F32)).astype(o_ref.dtype)
        o_ref[...] = s

        @pl.when(pl.program_id(1) == 2 * x + y)
        def _():
            fin_ref[...] = s

    out, fin = pl.pallas_call(
        body, name=name, grid=(rows // tr, 4),
        in_specs=[pl.BlockSpec((None, 2, tr, cdim), lambda i, j: (j, 0, i, 0)),
                  pl.BlockSpec((None, tr, cdim), lambda i, j: (j, i, 0))],
        out_specs=[pl.BlockSpec((None, tr, cdim), lambda i, j: (j, i, 0)),
                   pl.BlockSpec((None, tr, cdim), lambda i, j: (3, i, 0))],
        out_shape=[jax.ShapeDtypeStruct(g3.shape, got.dtype)] * 2,
        compiler_params=_cp("parallel", "arbitrary"),
    )(m4, g3)
    return out.reshape(got.shape), fin.reshape(got.shape)


HBM = pl.BlockSpec(memory_space=pltpu.HBM)
SEM = pl.BlockSpec(memory_space=pltpu.SEMAPHORE)
EFFECT = pltpu.SideEffectType.DATAFLOW_SIDE_EFFECTING
SLOTS = 4


def _hbm(a):
    return pltpu.with_memory_space_constraint(a, pltpu.HBM)


def _remote(src, dst, sems, i, k, to):
    return pltpu.make_async_remote_copy(src_ref=src, dst_ref=dst, send_sem=sems[0].at[i * SLOTS + k],
                                        recv_sem=sems[1].at[i * SLOTS + k], device_id=to, device_id_type=MESH)


def _slot(px, py, pc):
    return 4 * px + 2 * py + pc


def _other_chips(x, y):
    return [(1 - x, y), (x, 1 - y), (1 - x, 1 - y)]


def _plan_gather_own(layers):
    def plan(srcs, lands, sems):
        x, y, c = _my_place()
        out = []
        for i, j in enumerate(layers):
            dst = lands[i].at[_slot(x, y, c)]
            out.append(_remote(srcs[i].at[j], dst, sems, i, 0, (x, y, 1 - c)))
            for k, (px, py) in enumerate(_other_chips(x, y)):
                out.append(_remote(srcs[i].at[j], dst, sems, i, 1 + k, (px, py, c)))
        return out
    return plan


def _plan_gather_pass(n):
    def plan(srcs, lands, sems):
        x, y, c = _my_place()
        out = []
        for i in range(n):
            for k, (px, py) in enumerate(_other_chips(x, y)):
                blk = lands[i].at[_slot(px, py, c)]
                out.append(_remote(blk, blk, sems, i, k, (x, y, 1 - c)))
        return out
    return plan


def _plan_scatter_pair(n):
    def plan(srcs, lands, sems):
        x, y, c = _my_place()
        return [_remote(srcs[i].at[2 * j + (1 - c)], lands[i].at[j], sems, i, j, (x, y, 1 - c))
                for i in range(n) for j in range(4)]
    return plan


def _plan_scatter_chips(n):
    def plan(srcs, lands, sems):
        x, y, c = _my_place()
        return [_remote(srcs[i].at[2 * px + py], lands[i].at[k], sems, i, k, (px, py, c))
                for i in range(n) for k, (px, py) in enumerate(_other_chips(x, y))]
    return plan


def _split_start(name, plan, srcs, lands, deps=()):
    ns, nl, nd = len(srcs), len(lands), len(deps)
    n = max(ns, nl)

    def body(*refs):
        sems = refs[ns + nl + nd:ns + nl + nd + 2]
        for cp in plan(refs[:ns], refs[ns:ns + nl], sems):
            cp.start()
        refs[-1][...] = jnp.zeros_like(refs[-1])

    outs = pl.pallas_call(
        body, name=name,
        out_shape=(pltpu.SemaphoreType.DMA((n * SLOTS,)), pltpu.SemaphoreType.DMA((n * SLOTS,)),
                   *[pltpu.HBM(a.shape, a.dtype) for a in lands], jax.ShapeDtypeStruct((8, 128), F32)),
        in_specs=[HBM] * (ns + nl) + [ANY] * nd,
        out_specs=(SEM, SEM, *[HBM] * nl, pl.BlockSpec(memory_space=pltpu.VMEM)),
        input_output_aliases={ns + i: 2 + i for i in range(nl)},
        compiler_params=pltpu.CompilerParams(has_side_effects=EFFECT),
    )(*[_hbm(a) for a in srcs], *[_hbm(a) for a in lands], *deps)
    return (outs[0], outs[1]), list(outs[2:2 + nl]), outs[-1]


def _split_wait(name, plan, sems, srcs, lands, deps=()):
    ns, nl, nd = len(srcs), len(lands), len(deps)

    def body(*refs):
        for cp in plan(refs[:ns], refs[ns:ns + nl], refs[ns + nl:ns + nl + 2]):
            cp.wait_send()
            cp.wait_recv()

    outs = pl.pallas_call(
        body, name=name,
        out_shape=tuple(pltpu.HBM(a.shape, a.dtype) for a in lands),
        in_specs=[HBM] * (ns + nl) + [SEM, SEM] + [ANY] * nd,
        out_specs=tuple([HBM] * nl),
        input_output_aliases={ns + i: i for i in range(nl)},
        compiler_params=pltpu.CompilerParams(has_side_effects=EFFECT),
    )(*srcs, *lands, *sems, *deps)
    return list(outs)


def _place_own(srcs, layers):
    x, y, c = _my_place()
    zero = jnp.zeros((), jnp.int32)
    return [lax.dynamic_update_slice(lax.empty((N_DEV,) + a.shape[1:], a.dtype), a[j][None],
                                     (_slot(x, y, c),) + (zero,) * (a.ndim - 1))
            for a, j in zip(srcs, layers)]


def _with_deps(body, n_in, deps):
    nd = len(deps)
    if not nd:
        return body

    def wrapped(*refs):
        body(*refs[:n_in], *refs[n_in + nd:])

    return wrapped


def _rmsnorm_fwd(name, x, g, deps=()):
    T, D = x.shape
    tm = _tile(T, 256)

    def body(x_ref, g_ref, o_ref):
        xv = x_ref[...]
        r = lax.rsqrt(jnp.mean(xv * xv, axis=-1, keepdims=True) + EPS)
        o_ref[...] = ((xv * r) * g_ref[...]).astype(o_ref.dtype)

    return pl.pallas_call(
        _with_deps(body, 2, deps), name=name, grid=(T // tm,),
        in_specs=[pl.BlockSpec((tm, D), lambda i: (i, 0)), pl.BlockSpec((1, D), lambda i: (0, 0))]
        + [ANY] * len(deps),
        out_specs=pl.BlockSpec((tm, D), lambda i: (i, 0)),
        out_shape=jax.ShapeDtypeStruct((T, D), MXU_DTYPE), compiler_params=_cp("parallel"),
    )(x, g, *deps)


def _rmsnorm_bwd(name, x, g, dh, dres):
    T, D = x.shape
    tm = _tile(T, 256)

    def body(x_ref, g_ref, dh_ref, dres_ref, dx_ref, dxb_ref, dg_ref):
        xv = x_ref[...]
        r = lax.rsqrt(jnp.mean(xv * xv, axis=-1, keepdims=True) + EPS)
        xh = xv * r
        dhv = dh_ref[...]

        @pl.when(pl.program_id(0) == 0)
        def _():
            dg_ref[...] = jnp.zeros_like(dg_ref)

        dg_ref[...] += jnp.sum(dhv * xh, axis=0, keepdims=True)
        dxh = dhv * g_ref[...]
        dx = dres_ref[...] + r * (dxh - xh * jnp.mean(dxh * xh, axis=-1, keepdims=True))
        dx_ref[...] = dx
        dxb_ref[...] = dx.astype(dxb_ref.dtype)

    return pl.pallas_call(
        body, name=name, grid=(T // tm,),
        in_specs=[pl.BlockSpec((tm, D), lambda i: (i, 0)), pl.BlockSpec((1, D), lambda i: (0, 0)),
                  pl.BlockSpec((tm, D), lambda i: (i, 0)), pl.BlockSpec((tm, D), lambda i: (i, 0))],
        out_specs=[pl.BlockSpec((tm, D), lambda i: (i, 0)), pl.BlockSpec((tm, D), lambda i: (i, 0)),
                   pl.BlockSpec((1, D), lambda i: (0, 0))],
        out_shape=[jax.ShapeDtypeStruct((T, D), F32), jax.ShapeDtypeStruct((T, D), MXU_DTYPE),
                   jax.ShapeDtypeStruct((1, D), F32)],
        compiler_params=_cp("arbitrary"),
    )(x, g, dh, dres)


def _loss_head(name, x, g, tgt):
    T, D = x.shape
    tm = _tile(T, 256)

    def body(x_ref, g_ref, t_ref, loss_ref, dx_ref, dxb_ref, dg_ref):
        xv = x_ref[...]
        r = lax.rsqrt(jnp.mean(xv * xv, axis=-1, keepdims=True) + EPS)
        xh = xv * r
        gv = g_ref[...]
        diff = xh * gv - t_ref[...]

        @pl.when(pl.program_id(0) == 0)
        def _():
            dg_ref[...] = jnp.zeros_like(dg_ref)
            loss_ref[...] = jnp.zeros_like(loss_ref)

        part = 0.5 * jnp.sum(jnp.mean(diff * diff, axis=-1, keepdims=True), axis=0, keepdims=True)
        loss_ref[...] += jnp.broadcast_to(part, loss_ref.shape)
        dy = diff * (1.0 / D)
        dg_ref[...] += jnp.sum(dy * xh, axis=0, keepdims=True)
        dxh = dy * gv
        dx = r * (dxh - xh * jnp.mean(dxh * xh, axis=-1, keepdims=True))
        dx_ref[...] = dx
        dxb_ref[...] = dx.astype(dxb_ref.dtype)

    return pl.pallas_call(
        body, name=name, grid=(T // tm,),
        in_specs=[pl.BlockSpec((tm, D), lambda i: (i, 0)), pl.BlockSpec((1, D), lambda i: (0, 0)),
                  pl.BlockSpec((tm, D), lambda i: (i, 0))],
        out_specs=[pl.BlockSpec((1, 128), lambda i: (0, 0)), pl.BlockSpec((tm, D), lambda i: (i, 0)),
                   pl.BlockSpec((tm, D), lambda i: (i, 0)), pl.BlockSpec((1, D), lambda i: (0, 0))],
        out_shape=[jax.ShapeDtypeStruct((1, 128), F32), jax.ShapeDtypeStruct((T, D), F32),
                   jax.ShapeDtypeStruct((T, D), MXU_DTYPE), jax.ShapeDtypeStruct((1, D), F32)],
        compiler_params=_cp("arbitrary"),
    )(x, g, tgt)


def _adamw(name, w, m, v, parts, off=0, prev=None):
    Rtot, C = w.shape
    P, R = parts.shape[:2]
    tr = _tile(R, 256)
    assert off % tr == 0
    ob = off // tr
    c1 = 1.0 / (1.0 - ADAM_B1 ** ADAM_STEP)
    c2 = 1.0 / (1.0 - ADAM_B2 ** ADAM_STEP)
    chained = R != Rtot
    if chained and prev is None:
        prev = [lax.empty((Rtot, C), F32) for _ in range(4)]
    prev = list(prev) if chained else []

    def body(w_ref, m_ref, v_ref, p_ref, *rest):
        g_ref, d_ref, nm_ref, nv_ref = rest[len(prev):]
        g = p_ref[0].astype(F32)
        for s in range(1, P):
            g = g + p_ref[s].astype(F32)
        nm = ADAM_B1 * m_ref[...] + (1.0 - ADAM_B1) * g
        nv = ADAM_B2 * v_ref[...] + (1.0 - ADAM_B2) * (g * g)
        g_ref[...] = g
        nm_ref[...] = nm
        nv_ref[...] = nv
        d_ref[...] = -ADAM_LR * ((nm * c1) / (jnp.sqrt(nv * c2) + ADAM_EPS) + ADAM_WD * w_ref[...])

    blk = pl.BlockSpec((tr, C), lambda i: (i + ob, 0))
    return pl.pallas_call(
        body, name=name, grid=(R // tr,),
        in_specs=[blk, blk, blk, pl.BlockSpec((P, tr, C), lambda i: (0, i, 0))] + [ANY] * len(prev),
        out_specs=[blk, blk, blk, blk],
        out_shape=[jax.ShapeDtypeStruct((Rtot, C), F32)] * 4,
        input_output_aliases={4 + i: i for i in range(len(prev))},
        compiler_params=_cp("parallel"),
    )(w, m, v, parts, *prev)


def _colsum(name, a):
    T, N = a.shape
    tm = _tile(T, 256)

    def body(a_ref, o_ref):
        @pl.when(pl.program_id(0) == 0)
        def _():
            o_ref[...] = jnp.zeros_like(o_ref)

        o_ref[...] += jnp.sum(a_ref[...].astype(F32), axis=0, keepdims=True)

    return pl.pallas_call(
        body, name=name, grid=(T // tm,),
        in_specs=[pl.BlockSpec((tm, N), lambda i: (i, 0))],
        out_specs=pl.BlockSpec((1, N), lambda i: (0, 0)),
        out_shape=jax.ShapeDtypeStruct((1, N), F32), compiler_params=_cp("arbitrary"),
    )(a)


def _proj_in(name, h, wg, bias, deps=()):
    T, K = h.shape
    n = wg.shape[-1]
    tm = _tile(T, MM_ROWS)

    def body(a_ref, w_ref, b_ref, o_ref):
        o_ref[...] = _dot(a_ref[...], w_ref[...]) + b_ref[...]

    return pl.pallas_call(
        _with_deps(body, 3, deps), name=name, grid=(N_DEV, T // tm),
        in_specs=[pl.BlockSpec((tm, K), lambda s, i: (i, 0)),
                  pl.BlockSpec((None, K, n), lambda s, i: (s, 0, 0)),
                  pl.BlockSpec((1, n), lambda s, i: (0, s))] + [ANY] * len(deps),
        out_specs=pl.BlockSpec((tm, n), lambda s, i: (i, s)),
        out_shape=jax.ShapeDtypeStruct((T, N_DEV * n), F32), compiler_params=_cp("parallel", "parallel"),
    )(h, wg, bias, *deps)


def _ffn_in(name, h, wg_gate, wg_up):
    T, K = h.shape
    n = wg_gate.shape[-1]
    tm = _tile(T, MM_ROWS)

    def body(a_ref, wgt_ref, wup_ref, g_ref, u_ref, hid_ref):
        a = a_ref[...]
        g = _dot(a, wgt_ref[...])
        u = _dot(a, wup_ref[...])
        g_ref[...] = g
        u_ref[...] = u
        hid_ref[...] = (_silu(g) * u).astype(hid_ref.dtype)

    wspec = pl.BlockSpec((None, K, n), lambda s, i: (s, 0, 0))
    ospec = pl.BlockSpec((None, tm, n), lambda s, i: (s, i, 0))
    return pl.pallas_call(
        body, name=name, grid=(N_DEV, T // tm),
        in_specs=[pl.BlockSpec((tm, K), lambda s, i: (i, 0)), wspec, wspec],
        out_specs=[ospec, ospec, ospec],
        out_shape=[jax.ShapeDtypeStruct((N_DEV, T, n), F32), jax.ShapeDtypeStruct((N_DEV, T, n), F32),
                   jax.ShapeDtypeStruct((N_DEV, T, n), MXU_DTYPE)],
        compiler_params=_cp("parallel", "parallel"),
    )(h, wg_gate, wg_up)


def _a_spec(a, tm, k):
    if a.ndim == 2:
        return pl.BlockSpec((tm, k), lambda i, s: (i, s))
    return pl.BlockSpec((None, tm, k), lambda i, s: (s, i, 0))


def _proj_out(name, a, wg, res, deps=(), norm_g=None):
    k, N = wg.shape[-2:]
    T = res.shape[0]
    tm = _tile(T, MM_ROWS // 2)
    extra = [] if norm_g is None else [norm_g]

    def body(a_ref, w_ref, r_ref, *rest):
        o_ref = rest[len(extra)]
        p = _dot(a_ref[...], w_ref[...])

        @pl.when(pl.program_id(1) == 0)
        def _():
            o_ref[...] = r_ref[...] + p

        @pl.when(pl.program_id(1) > 0)
        def _():
            o_ref[...] += p

        if extra:
            @pl.when(pl.program_id(1) == N_DEV - 1)
            def _():
                xv = o_ref[...]
                r = lax.rsqrt(jnp.mean(xv * xv, axis=-1, keepdims=True) + EPS)
                rest[2][...] = ((xv * r) * rest[0][...]).astype(rest[2].dtype)

    row_blk = pl.BlockSpec((tm, N), lambda i, s: (i, 0))
    out = pl.pallas_call(
        _with_deps(body, 3 + len(extra), deps), name=name, grid=(T // tm, N_DEV),
        in_specs=[_a_spec(a, tm, k), pl.BlockSpec((None, k, N), lambda i, s: (s, 0, 0)), row_blk]
        + [pl.BlockSpec((1, N), lambda i, s: (0, 0))] * len(extra) + [ANY] * len(deps),
        out_specs=[row_blk] * (1 + len(extra)),
        out_shape=[jax.ShapeDtypeStruct((T, N), F32)] + [jax.ShapeDtypeStruct((T, N), MXU_DTYPE)] * len(extra),
        compiler_params=_cp("parallel", "arbitrary"),
    )(a, wg, res, *extra, *deps)
    return out[0] if norm_g is None else out


def _bwd_in(name, das, wgs, deps=()):
    K, n = wgs[0].shape[-2:]
    T = das[0].shape[-2]
    tm = _tile(T, MM_ROWS)
    npair = len(das)

    def body(*refs):
        o_ref = refs[-1]
        p = _dot_nt(refs[0][...], refs[npair][...])
        for q in range(1, npair):
            p = p + _dot_nt(refs[q][...], refs[npair + q][...])

        @pl.when(pl.program_id(1) == 0)
        def _():
            o_ref[...] = p

        @pl.when(pl.program_id(1) > 0)
        def _():
            o_ref[...] += p

    return pl.pallas_call(
        _with_deps(body, 2 * npair, deps), name=name, grid=(T // tm, N_DEV),
        in_specs=[_a_spec(a, tm, n) for a in das]
        + [pl.BlockSpec((None, K, n), lambda i, s: (s, 0, 0)) for _ in wgs] + [ANY] * len(deps),
        out_specs=pl.BlockSpec((tm, K), lambda i, s: (i, 0)),
        out_shape=jax.ShapeDtypeStruct((T, K), F32), compiler_params=_cp("parallel", "arbitrary"),
    )(*das, *wgs, *deps)


def _bwd_out(name, dx, wg):
    k, N = wg.shape[-2:]
    T = dx.shape[0]
    tm = _tile(T, MM_ROWS)

    def body(a_ref, w_ref, o_ref):
        o_ref[...] = _dot_nt(a_ref[...], w_ref[...])

    return pl.pallas_call(
        body, name=name, grid=(N_DEV, T // tm),
        in_specs=[pl.BlockSpec((tm, N), lambda s, i: (i, 0)),
                  pl.BlockSpec((None, k, N), lambda s, i: (s, 0, 0))],
        out_specs=pl.BlockSpec((tm, k), lambda s, i: (i, s)),
        out_shape=jax.ShapeDtypeStruct((T, N_DEV * k), F32), compiler_params=_cp("parallel", "parallel"),
    )(dx, wg)


def _ffn_bwd_hidden(name, dx, wg_down, gate, up):
    n, N = wg_down.shape[-2:]
    T = dx.shape[0]
    tm = _tile(T, MM_ROWS)

    def body(a_ref, w_ref, g_ref, u_ref, dg_ref, du_ref):
        dh = _dot_nt(a_ref[...], w_ref[...])
        g = g_ref[...]
        dg_ref[...] = (dh * u_ref[...] * _dsilu(g)).astype(dg_ref.dtype)
        du_ref[...] = (dh * _silu(g)).astype(du_ref.dtype)

    sm = pl.BlockSpec((None, tm, n), lambda s, i: (s, i, 0))
    return pl.pallas_call(
        body, name=name, grid=(N_DEV, T // tm),
        in_specs=[pl.BlockSpec((tm, N), lambda s, i: (i, 0)),
                  pl.BlockSpec((None, n, N), lambda s, i: (s, 0, 0)), sm, sm],
        out_specs=[sm, sm],
        out_shape=[jax.ShapeDtypeStruct((N_DEV, T, n), MXU_DTYPE)] * 2,
        compiler_params=_cp("parallel", "parallel"),
    )(dx, wg_down, gate, up)


def _wgrad(name, a, c, rows, cols, deps=()):
    T = a.shape[-2]
    tk = _tile(T, WGRAD_ROWS)
    nk = T // tk

    def spec(z, w):
        if z.ndim == 3:
            return pl.BlockSpec((None, tk, w), lambda s, k: (s, k, 0))
        if z.shape[1] == w:
            return pl.BlockSpec((tk, w), lambda s, k: (k, 0))
        return pl.BlockSpec((tk, w), lambda s, k: (k, s))

    def body(a_ref, c_ref, o_ref, acc_ref):
        k = pl.program_id(1)
        p = _dot_tn(a_ref[...], c_ref[...])

        @pl.when(k == 0)
        def _():
            acc_ref[...] = p

        @pl.when(k > 0)
        def _():
            acc_ref[...] += p

        @pl.when(k == nk - 1)
        def _():
            o_ref[...] = acc_ref[...].astype(o_ref.dtype)

    return pl.pallas_call(
        _with_deps(body, 2, deps), name=name, grid=(N_DEV, nk),
        in_specs=[spec(a, rows), spec(c, cols)] + [ANY] * len(deps),
        out_specs=pl.BlockSpec((None, rows, cols), lambda s, k: (s, 0, 0)),
        out_shape=jax.ShapeDtypeStruct((N_DEV, rows, cols), WIRE_DTYPE),
        scratch_shapes=[pltpu.VMEM((rows, cols), F32)],
        compiler_params=_cp("parallel", "arbitrary"),
    )(a, c, *deps)


def _shift_down(cur, prev8, sh):
    n = cur.shape[0]
    rolled = pltpu.roll(cur, sh, 0)
    top = jnp.where(_rows(8) < sh, pltpu.roll(prev8, sh, 0), rolled[0:8])
    return jnp.concatenate([top, rolled[8:n]], axis=0)


def _shift_up(cur, next8, sh):
    n = cur.shape[0]
    rolled = pltpu.roll(cur, n - sh, 0)
    bot = jnp.where(_rows(8) >= 8 - sh, pltpu.roll(next8, 8 - sh, 0), rolled[n - 8:n])
    return jnp.concatenate([rolled[0:n - 8], bot], axis=0)


def _lru_gate_terms(r, lam):
    sp = _softplus(-lam)
    la = -LRU_C * r * sp
    a = jnp.exp(la)
    m2 = _one_minus_exp(2.0 * la)
    return sp, la, a, m2


def _lru_fwd(name, proj, conv_w, conv_b, wa, ba, wx, bx, lam):
    T = proj.shape[0]
    H, hd, _ = wa.shape
    W = H * hd
    K = conv_w.shape[0]
    tb = _tile(T, 256)

    def body(xin_ref, gate_ref, cw_ref, cb_ref, wa_ref, ba_ref, wx_ref, bx_ref, lam_ref,
             ya_ref, xc_ref, r_ref, i_ref, hs_ref, tail_ref, hprev_ref):
        blk = pl.program_id(0)

        @pl.when(blk == 0)
        def _():
            tail_ref[...] = jnp.zeros_like(tail_ref)
            hprev_ref[...] = jnp.zeros_like(hprev_ref)

        xin = xin_ref[...]
        prev8 = tail_ref[...]
        xc = cw_ref[K - 1:K, :] * xin
        for sh in range(1, K):
            xc = xc + cw_ref[K - 1 - sh:K - sh, :] * _shift_down(xin, prev8, sh)
        xc = xc + cb_ref[...]
        tail_ref[...] = xin[tb - 8:tb]
        xc_ref[...] = xc
        for h in range(H):
            cs = slice(h * hd, (h + 1) * hd)
            xh = xc[:, cs]
            r_ref[:, cs] = _sigmoid(_dot(xh, wa_ref[h]) + ba_ref[:, cs])
            i_ref[:, cs] = _sigmoid(_dot(xh, wx_ref[h]) + bx_ref[:, cs])
        r = r_ref[...]
        _, _, a, m2 = _lru_gate_terms(r, lam_ref[...])
        row = _rows(tb)
        mult = jnp.where((row == 0) & (blk == 0), 1.0, jnp.sqrt(jnp.maximum(m2, 0.0)))
        u = mult * i_ref[...] * xc
        d = 1
        while d < tb:
            keep = row >= d
            u = a * jnp.where(keep, pltpu.roll(u, d, 0), 0.0) + u
            a = a * jnp.where(keep, pltpu.roll(a, d, 0), 1.0)
            d *= 2
        hs = u + a * hprev_ref[...]
        hprev_ref[...] = hs[tb - 1:tb]
        hs_ref[...] = hs
        ya_ref[...] = (hs * _gelu(gate_ref[...])).astype(ya_ref.dtype)

    full = lambda shape: pl.BlockSpec(shape, lambda i: tuple(0 for _ in shape))
    blk = pl.BlockSpec((tb, W), lambda i: (i, 0))
    return pl.pallas_call(
        body, name=name, grid=(T // tb,),
        in_specs=[pl.BlockSpec((tb, W), lambda i: (i, 0)), pl.BlockSpec((tb, W), lambda i: (i, 1)),
                  full((K, W)), full((1, W)), full((H, hd, hd)), full((1, W)), full((H, hd, hd)),
                  full((1, W)), full((1, W))],
        out_specs=[blk] * 5,
        out_shape=[jax.ShapeDtypeStruct((T, 2 * W), MXU_DTYPE)] + [jax.ShapeDtypeStruct((T, W), F32)] * 4,
        scratch_shapes=[pltpu.VMEM((8, W), F32), pltpu.VMEM((1, W), F32)],
        compiler_params=_cp("arbitrary"),
    )(proj, proj, conv_w, conv_b, wa, ba, wx, bx, lam)


def _lru_bwd(name, proj, dy, xc, r, ig, hs, conv_w, wa, wx, lam, dpbuf):
    T = proj.shape[0]
    H, hd, _ = wa.shape
    W = H * hd
    K = conv_w.shape[0]
    tb = _tile(T, 256)
    nb = T // tb
    t8 = tb // 8

    def body(xin_ref, xprev_ref, gate_ref, dy_ref, xc_ref, r_ref, i_ref, hs_ref, hsprev_ref,
             cw_ref, wa_ref, wx_ref, lam_ref, dpbuf_ref,
             dp_ref, dcw_ref, dcb_ref, dwa_ref, dba_ref, dwx_ref, dbx_ref, dlam_ref,
             cdh_ref, ca_ref, cdxc_ref, dxc_ref):
        del dpbuf_ref
        step = pl.program_id(0)
        blk = nb - 1 - step

        @pl.when(step == 0)
        def _():
            for ref in (dcw_ref, dcb_ref, dwa_ref, dba_ref, dwx_ref, dbx_ref, dlam_ref,
                        cdh_ref, ca_ref, cdxc_ref):
                ref[...] = jnp.zeros_like(ref)

        row = _rows(tb)
        first = blk == 0
        gate = gate_ref[...]
        dy_a = dy_ref[...]
        hsv = hs_ref[...]
        dp_ref[:, W:2 * W] = (dy_a * hsv * _dgelu(gate)).astype(dp_ref.dtype)
        d_hs = dy_a * _gelu(gate)
        lam = lam_ref[...]
        rv = r_ref[...]
        sp, la, a, m2 = _lru_gate_terms(rv, lam)
        an = jnp.where(row == tb - 1, ca_ref[...], pltpu.roll(a, tb - 1, 0))
        u = d_hs
        d = 1
        while d < tb:
            keep = row < tb - d
            u = an * jnp.where(keep, pltpu.roll(u, tb - d, 0), 0.0) + u
            an = an * jnp.where(keep, pltpu.roll(an, tb - d, 0), 1.0)
            d *= 2
        dh = u + an * cdh_ref[...]
        cdh_ref[...] = dh[0:1]
        ca_ref[...] = a[0:1]
        hlast = jnp.where(first, 0.0, hsprev_ref[7:8, :])
        hprev = jnp.where(row == 0, hlast, pltpu.roll(hsv, 1, 0))
        da = dh * hprev
        xcv = xc_ref[...]
        iv = i_ref[...]
        t0 = (row == 0) & first
        mult = jnp.sqrt(jnp.maximum(m2, 0.0))
        mult_eff = jnp.where(t0, 1.0, mult)
        d_mult = dh * iv * xcv
        d_i = dh * mult_eff * xcv
        dxc = dh * mult_eff * iv
        e2 = 1.0 - m2
        d_la = da * a + jnp.where(t0 | (m2 <= 0.0), 0.0, -d_mult * e2 / jnp.where(m2 > 0.0, mult, 1.0))
        d_r = d_la * (-LRU_C * sp)
        dlam_ref[...] += jnp.sum(d_la * (-LRU_C * rv), axis=0, keepdims=True) * (-_sigmoid(-lam))
        d_zr = d_r * rv * (1.0 - rv)
        d_zi = d_i * iv * (1.0 - iv)
        dba_ref[...] += jnp.sum(d_zr, axis=0, keepdims=True)
        dbx_ref[...] += jnp.sum(d_zi, axis=0, keepdims=True)
        for h in range(H):
            cs = slice(h * hd, (h + 1) * hd)
            xh = xcv[:, cs]
            zr, zi = d_zr[:, cs], d_zi[:, cs]
            dwa_ref[h] += _dot_tn(xh, zr)
            dwx_ref[h] += _dot_tn(xh, zi)
            dxc_ref[:, cs] = dxc[:, cs] + _dot_nt(zr, wa_ref[h]) + _dot_nt(zi, wx_ref[h])
        dxc = dxc_ref[...]
        dcb_ref[...] += jnp.sum(dxc, axis=0, keepdims=True)
        xin = xin_ref[...]
        prev8 = jnp.where(first, 0.0, xprev_ref[...])
        next8 = cdxc_ref[...]
        dxin = cw_ref[K - 1:K, :] * dxc
        dcw_ref[K - 1:K, :] += jnp.sum(dxc * xin, axis=0, keepdims=True)
        for sh in range(1, K):
            dxin = dxin + cw_ref[K - 1 - sh:K - sh, :] * _shift_up(dxc, next8, sh)
            dcw_ref[K - 1 - sh:K - sh, :] += jnp.sum(dxc * _shift_down(xin, prev8, sh), axis=0, keepdims=True)
        cdxc_ref[...] = dxc[0:8]
        dp_ref[:, 0:W] = dxin.astype(dp_ref.dtype)

    full = lambda shape: pl.BlockSpec(shape, lambda i: tuple(0 for _ in shape))
    cur = lambda col: pl.BlockSpec((tb, W), lambda i: (nb - 1 - i, col))
    prev = pl.BlockSpec((8, W), lambda i: (jnp.maximum((nb - 1 - i) * t8 - 1, 0), 0))
    return pl.pallas_call(
        body, name=name, grid=(nb,),
        in_specs=[cur(0), prev, cur(1), cur(0), cur(0), cur(0), cur(0), cur(0), prev,
                  full((K, W)), full((H, hd, hd)), full((H, hd, hd)), full((1, W)), ANY],
        out_specs=[pl.BlockSpec((tb, 2 * W), lambda i: (nb - 1 - i, 0)), full((K, W)), full((1, W)),
                   full((H, hd, hd)), full((1, W)), full((H, hd, hd)), full((1, W)), full((1, W))],
        out_shape=[jax.ShapeDtypeStruct(dpbuf.shape, dpbuf.dtype), jax.ShapeDtypeStruct((K, W), F32),
                   jax.ShapeDtypeStruct((1, W), F32), jax.ShapeDtypeStruct((H, hd, hd), F32),
                   jax.ShapeDtypeStruct((1, W), F32), jax.ShapeDtypeStruct((H, hd, hd), F32),
                   jax.ShapeDtypeStruct((1, W), F32), jax.ShapeDtypeStruct((1, W), F32)],
        scratch_shapes=[pltpu.VMEM((1, W), F32), pltpu.VMEM((1, W), F32), pltpu.VMEM((8, W), F32),
                        pltpu.VMEM((tb, W), F32)],
        input_output_aliases={13: 0}, compiler_params=_cp("arbitrary"),
    )(proj, proj, proj, dy, xc, r, ig, hs, hs, conv_w, wa, wx, lam, dpbuf)


def _chunk_cumsum(g, c):
    n = g.shape[0]
    rc = _rows(n) & (c - 1)
    d = 1
    while d < c:
        g = g + jnp.where(rc >= d, pltpu.roll(g, d, 0), 0.0)
        d *= 2
    return g


def _chunk_rcumsum(g, c):
    n = g.shape[0]
    rc = _rows(n) & (c - 1)
    d = 1
    while d < c:
        g = g + jnp.where(rc < c - d, pltpu.roll(g, n - d, 0), 0.0)
        d *= 2
    return g


def _hgrn_pointwise(qr, fr, lb):
    qf = _silu(qr)
    sig = _sigmoid(fr)
    fg = lb + (1.0 - lb) * sig
    gl = jnp.log(jnp.maximum(fg, F_FLOOR))
    kk = (1.0 - lb) * (1.0 - sig)
    return qf, sig, fg, gl, kk


def _hgrn_fwd(name, proj, lb, norm_g, ybuf):
    T = proj.shape[0]
    W = lb.shape[1]
    H = HGRN_HEADS
    dk = W // H
    c = HGRN_SUB
    R = _tile(T, 128)
    nck = R // c

    def body(q_ref, f_ref, v_ref, g_ref, lb_ref, ng_ref, ybuf_ref, yb_ref, o_ref, sall_ref,
             st_ref, qe_ref, ke_ref, acc_ref):
        del ybuf_ref

        @pl.when(pl.program_id(0) == 0)
        def _():
            st_ref[...] = jnp.zeros_like(st_ref)

        qf, _, _, gl, kk = _hgrn_pointwise(q_ref[...], f_ref[...], lb_ref[...])
        b = _chunk_cumsum(gl, c)
        rc = _rows(R) & (c - 1)
        for h in range(H):
            cs = slice(h * dk, (h + 1) * dk)
            qh, kh, bh, vh = qf[:, cs], kk[:, cs], b[:, cs], v_ref[:, cs]
            acc = jnp.sum(qh * kh, axis=1, keepdims=True) * vh
            for d in range(1, c):
                ok = rc >= d
                e = jnp.exp(jnp.where(ok, bh - pltpu.roll(bh, d, 0), 0.0))
                s = jnp.sum(qh * pltpu.roll(kh, d, 0) * e, axis=1, keepdims=True)
                acc = acc + jnp.where(ok, s, 0.0) * pltpu.roll(vh, d, 0)
            acc_ref[:, cs] = acc
        qe_ref[...] = qf * jnp.exp(b)
        for ci in range(nck):
            rs = slice(ci * c, (ci + 1) * c)
            bl = b[ci * c + c - 1:ci * c + c, :]
            ke_ref[rs, :] = kk[rs, :] * jnp.exp(bl - b[rs, :])
            ebl = jnp.exp(bl)
            for h in range(H):
                cs = slice(h * dk, (h + 1) * dk)
                st = st_ref[h]
                sall_ref[ci, h] = st
                o_ref[rs, cs] = acc_ref[rs, cs] + _dot_nt(qe_ref[rs, cs], st)
                st_ref[h] = st * ebl[:, cs] + _dot_tn(v_ref[rs, cs], ke_ref[rs, cs])
        ng = ng_ref[...]
        gg = g_ref[...]
        for h in range(H):
            cs = slice(h * dk, (h + 1) * dk)
            oh = o_ref[:, cs]
            rr = lax.rsqrt(jnp.mean(oh * oh, axis=1, keepdims=True) + EPS)
            yb_ref[:, cs] = ((oh * rr) * ng[:, cs] * _silu(gg[:, cs])).astype(yb_ref.dtype)

    full = lambda shape: pl.BlockSpec(shape, lambda i: tuple(0 for _ in shape))
    col = lambda k: pl.BlockSpec((R, W), lambda i: (i, k))
    blk = pl.BlockSpec((R, W), lambda i: (i, 0))
    return pl.pallas_call(
        body, name=name, grid=(T // R,),
        in_specs=[col(2), col(3), col(4), col(5), full((1, W)), full((1, W)), ANY],
        out_specs=[col(1), blk, pl.BlockSpec((nck, H, dk, dk), lambda i: (i, 0, 0, 0))],
        out_shape=[jax.ShapeDtypeStruct((T, 2 * W), MXU_DTYPE), jax.ShapeDtypeStruct((T, W), F32),
                   jax.ShapeDtypeStruct((T // c, H, dk, dk), F32)],
        scratch_shapes=[pltpu.VMEM((H, dk, dk), F32), pltpu.VMEM((R, W), F32), pltpu.VMEM((R, W), F32),
                        pltpu.VMEM((R, W), F32)],
        input_output_aliases={6: 0}, compiler_params=_cp("arbitrary"),
    )(proj, proj, proj, proj, lb, norm_g, ybuf)


def _hgrn_bwd(name, proj, dy, o, sall, lb, norm_g):
    T = proj.shape[0]
    W = lb.shape[1]
    H = HGRN_HEADS
    dk = W // H
    c = HGRN_SUB
    R = _tile(T, 128)
    nck = R // c
    nb = T // R

    def body(q_ref, f_ref, v_ref, g_ref, dy_ref, o_ref, sall_ref, lb_ref, ng_ref,
             dp_ref, dlb_ref, dng_ref,
             dst_ref, do_ref, dq_ref, dk_ref, dv_ref, ex_ref, qe_ref, ke_ref):
        @pl.when(pl.program_id(0) == 0)
        def _():
            dst_ref[...] = jnp.zeros_like(dst_ref)
            dlb_ref[...] = jnp.zeros_like(dlb_ref)
            dng_ref[...] = jnp.zeros_like(dng_ref)

        lbv = lb_ref[...]
        qr = q_ref[...]
        qf, sig, fg, gl, kk = _hgrn_pointwise(qr, f_ref[...], lbv)
        b = _chunk_cumsum(gl, c)
        rc = _rows(R) & (c - 1)
        ng = ng_ref[...]
        gg = g_ref[...]
        dyv = dy_ref[...]
        sg = _silu(gg)
        for h in range(H):
            cs = slice(h * dk, (h + 1) * dk)
            oh = o_ref[:, cs]
            rr = lax.rsqrt(jnp.mean(oh * oh, axis=1, keepdims=True) + EPS)
            ohat = oh * rr
            dyh = dyv[:, cs]
            dp_ref[:, 5 * W + h * dk:5 * W + (h + 1) * dk] = (
                dyh * ohat * ng[:, cs] * _dsilu(gg[:, cs])).astype(dp_ref.dtype)
            t = dyh * sg[:, cs]
            dng_ref[:, cs] += jnp.sum(t * ohat, axis=0, keepdims=True)
            dohat = t * ng[:, cs]
            do_ref[:, cs] = rr * (dohat - ohat * jnp.mean(dohat * ohat, axis=1, keepdims=True))
        for h in range(H):
            cs = slice(h * dk, (h + 1) * dk)
            qh, kh, bh, vh, doh = qf[:, cs], kk[:, cs], b[:, cs], v_ref[:, cs], do_ref[:, cs]
            da0 = jnp.sum(doh * vh, axis=1, keepdims=True)
            a0 = jnp.sum(qh * kh, axis=1, keepdims=True)
            dq = da0 * kh
            dkk = da0 * qh
            dv = a0 * doh
            for d in range(1, c):
                ok = rc >= d
                e = jnp.exp(jnp.where(ok, bh - pltpu.roll(bh, d, 0), 0.0))
                kr = pltpu.roll(kh, d, 0)
                da = jnp.where(ok, jnp.sum(doh * pltpu.roll(vh, d, 0), axis=1, keepdims=True), 0.0)
                aa = jnp.where(ok, jnp.sum(qh * kr * e, axis=1, keepdims=True), 0.0)
                dq = dq + da * kr * e
                dkk = dkk + pltpu.roll(da * qh * e, R - d, 0)
                dv = dv + pltpu.roll(aa * doh, R - d, 0)
            dq_ref[:, cs] = dq
            dk_ref[:, cs] = dkk
            dv_ref[:, cs] = dv
        eb = jnp.exp(b)
        qe_ref[...] = qf * eb
        ex_ref[...] = jnp.zeros_like(ex_ref)
        for ci in reversed(range(nck)):
            rs = slice(ci * c, (ci + 1) * c)
            bl = b[ci * c + c - 1:ci * c + c, :]
            ebl_rows = jnp.exp(bl - b[rs, :])
            ke_ref[rs, :] = kk[rs, :] * ebl_rows
            ebl = jnp.exp(bl)
            for h in range(H):
                cs = slice(h * dk, (h + 1) * dk)
                st0 = sall_ref[ci, h]
                dst1 = dst_ref[h]
                doc = do_ref[rs, cs]
                vc = v_ref[rs, cs]
                dq_ref[rs, cs] += _dot(doc, st0) * eb[rs, cs]
                dv_ref[rs, cs] += _dot_nt(ke_ref[rs, cs], dst1)
                dki = _dot(vc, dst1) * ebl_rows[:, cs]
                dk_ref[rs, cs] += dki
                ex_ref[ci * c + c - 1:ci * c + c, cs] = (
                    jnp.sum(dki * kk[rs, cs], axis=0, keepdims=True)
                    + ebl[:, cs] * jnp.sum(st0 * dst1, axis=0, keepdims=True))
                dst_ref[h] = dst1 * ebl[:, cs] + _dot_tn(doc, qe_ref[rs, cs])
        dq = dq_ref[...]
        dkk = dk_ref[...]
        db = qf * dq - kk * dkk + ex_ref[...]
        dgl = _chunk_rcumsum(db, c)
        dfg = jnp.where(fg > F_FLOOR, dgl / jnp.maximum(fg, F_FLOOR), 0.0)
        dsig = (dfg - dkk) * (1.0 - lbv)
        dlb_ref[...] += jnp.sum((dfg - dkk) * (1.0 - sig), axis=0, keepdims=True)
        dp_ref[:, 0:2 * W] = jnp.zeros((R, 2 * W), dp_ref.dtype)
        dp_ref[:, 2 * W:3 * W] = (dq * _dsilu(qr)).astype(dp_ref.dtype)
        dp_ref[:, 3 * W:4 * W] = (dsig * sig * (1.0 - sig)).astype(dp_ref.dtype)
        dp_ref[:, 4 * W:5 * W] = dv_ref[...].astype(dp_ref.dtype)

    full = lambda shape: pl.BlockSpec(shape, lambda i: tuple(0 for _ in shape))
    col = lambda k: pl.BlockSpec((R, W), lambda i: (nb - 1 - i, k))
    scr = pltpu.VMEM((R, W), F32)
    return pl.pallas_call(
        body, name=name, grid=(nb,),
        in_specs=[col(2), col(3), col(4), col(5), col(1), col(0),
                  pl.BlockSpec((nck, H, dk, dk), lambda i: (nb - 1 - i, 0, 0, 0)), full((1, W)), full((1, W))],
        out_specs=[pl.BlockSpec((R, 6 * W), lambda i: (nb - 1 - i, 0)), full((1, W)), full((1, W))],
        out_shape=[jax.ShapeDtypeStruct((T, 6 * W), MXU_DTYPE), jax.ShapeDtypeStruct((1, W), F32),
                   jax.ShapeDtypeStruct((1, W), F32)],
        scratch_shapes=[pltpu.VMEM((H, dk, dk), F32), scr, scr, scr, scr, scr, scr, scr],
        compiler_params=_cp("arbitrary"),
    )(proj, proj, proj, proj, dy, o, sall, lb, norm_g)


ODD_HALO = 32


def _row_shifts(ext, up):
    n = ext.shape[0]
    return [ext] + [pltpu.roll(ext, n - b if up else b, 0) for b in range(1, 8)]


def _past(copies, sh, tb):
    a, b = divmod(sh, 8)
    return copies[b][ODD_HALO - 8 * a:ODD_HALO - 8 * a + tb]


def _future(copies, sh, tb):
    a, b = divmod(sh, 8)
    return copies[b][8 * a:8 * a + tb]


def _odd_fwd(name, proj, sc_w, cf_w, cf_b, ln_g, ln_b):
    T = proj.shape[0]
    W = sc_w.shape[1]
    K3, K31 = sc_w.shape[0], cf_w.shape[0]
    tb = _tile(T, 256)
    hb = tb // ODD_HALO
    n = tb + ODD_HALO

    def body(cur_ref, prev_ref, w3_ref, w31_ref, cb_ref, lg_ref, lbeta_ref, y_ref, d_ref):
        keep = (pl.program_id(0) > 0).astype(F32)
        sb = cur_ref[:, 0:W]
        p = cur_ref[:, W:2 * W] * cur_ref[:, 2 * W:3 * W]
        glu = cur_ref[:, 3 * W:4 * W] * _sigmoid(cur_ref[:, 4 * W:5 * W])
        p_prev = prev_ref[:, W:2 * W] * prev_ref[:, 2 * W:3 * W] * keep
        glu_prev = prev_ref[:, 3 * W:4 * W] * _sigmoid(prev_ref[:, 4 * W:5 * W]) * keep
        ext = jnp.concatenate([p_prev, p], axis=0)
        cp = w3_ref[K3 - 1:K3, :] * p
        for sh in range(1, K3):
            cp = cp + w3_ref[K3 - 1 - sh:K3 - sh, :] * pltpu.roll(ext, sh, 0)[ODD_HALO:n]
        y_ref[:, 0:W] = (sb * cp).astype(y_ref.dtype)
        glu_past = _row_shifts(jnp.concatenate([glu_prev, glu], axis=0), up=False)
        d = cb_ref[...] + w31_ref[K31 - 1:K31, :] * glu
        for sh in range(1, K31):
            d = d + w31_ref[K31 - 1 - sh:K31 - sh, :] * _past(glu_past, sh, tb)
        d_ref[...] = d
        mu = jnp.mean(d, axis=1, keepdims=True)
        xc = d - mu
        rstd = lax.rsqrt(jnp.mean(xc * xc, axis=1, keepdims=True) + EPS)
        ln = (xc * rstd) * lg_ref[...] + lbeta_ref[...]
        y_ref[:, W:2 * W] = _silu(ln).astype(y_ref.dtype)

    full = lambda shape: pl.BlockSpec(shape, lambda i: tuple(0 for _ in shape))
    return pl.pallas_call(
        body, name=name, grid=(T // tb,),
        in_specs=[pl.BlockSpec((tb, 5 * W), lambda i: (i, 0)),
                  pl.BlockSpec((ODD_HALO, 5 * W), lambda i: (jnp.maximum(i * hb - 1, 0), 0)),
                  full((K3, W)), full((K31, W)), full((1, W)), full((1, W)), full((1, W))],
        out_specs=[pl.BlockSpec((tb, 2 * W), lambda i: (i, 0)), pl.BlockSpec((tb, W), lambda i: (i, 0))],
        out_shape=[jax.ShapeDtypeStruct((T, 2 * W), MXU_DTYPE), jax.ShapeDtypeStruct((T, W), F32)],
        compiler_params=_cp("parallel"),
    )(proj, proj, sc_w, cf_w, cf_b, ln_g, ln_b)


def _odd_bwd(name, proj, dy, dsave, sc_w, cf_w, ln_g, ln_b):
    T = proj.shape[0]
    W = sc_w.shape[1]
    K3, K31 = sc_w.shape[0], cf_w.shape[0]
    tb = _tile(T, 128)
    nb = T // tb
    hb = tb // ODD_HALO
    nh = T // ODD_HALO
    n = tb + ODD_HALO

    def body(cur_ref, prev_ref, next_ref, dy_ref, dyn_ref, d_ref, dn_ref,
             w3_ref, w31_ref, lg_ref, lbeta_ref,
             dp_ref, dw3_ref, dw31_ref, dcb_ref, dlg_ref, dlb_ref):
        i = pl.program_id(0)

        @pl.when(i == 0)
        def _():
            for ref in (dw3_ref, dw31_ref, dcb_ref, dlg_ref, dlb_ref):
                ref[...] = jnp.zeros_like(ref)

        keep_prev = (i > 0).astype(F32)
        keep_next = (i < nb - 1).astype(F32)
        sb = cur_ref[:, 0:W]
        scv = cur_ref[:, W:2 * W]
        svv = cur_ref[:, 2 * W:3 * W]
        cu = cur_ref[:, 3 * W:4 * W]
        sg = _sigmoid(cur_ref[:, 4 * W:5 * W])
        p = scv * svv
        glu = cu * sg
        p_prev = prev_ref[:, W:2 * W] * prev_ref[:, 2 * W:3 * W] * keep_prev
        glu_prev = prev_ref[:, 3 * W:4 * W] * _sigmoid(prev_ref[:, 4 * W:5 * W]) * keep_prev
        dext = jnp.concatenate([d_ref[...], dn_ref[...]], axis=0)
        dyd = jnp.concatenate([dy_ref[:, W:2 * W], dyn_ref[:, W:2 * W] * keep_next], axis=0)
        mu = jnp.mean(dext, axis=1, keepdims=True)
        xc = dext - mu
        rstd = lax.rsqrt(jnp.mean(xc * xc, axis=1, keepdims=True) + EPS)
        xh = xc * rstd
        lg = lg_ref[...]
        dln = dyd * _dsilu(xh * lg + lbeta_ref[...])
        dxh = dln * lg
        dd = rstd * (dxh - jnp.mean(dxh, axis=1, keepdims=True)
                     - xh * jnp.mean(dxh * xh, axis=1, keepdims=True))
        dlg_ref[...] += jnp.sum((dln * xh)[0:tb], axis=0, keepdims=True)
        dlb_ref[...] += jnp.sum(dln[0:tb], axis=0, keepdims=True)
        ddc = dd[0:tb]
        dcb_ref[...] += jnp.sum(ddc, axis=0, keepdims=True)
        dglu = w31_ref[K31 - 1:K31, :] * ddc
        glu_past = _row_shifts(jnp.concatenate([glu_prev, glu], axis=0), up=False)
        dd_future = _row_shifts(dd, up=True)
        dw31_ref[K31 - 1:K31, :] += jnp.sum(ddc * glu, axis=0, keepdims=True)
        for sh in range(1, K31):
            dglu = dglu + w31_ref[K31 - 1 - sh:K31 - sh, :] * _future(dd_future, sh, tb)
            dw31_ref[K31 - 1 - sh:K31 - sh, :] += jnp.sum(ddc * _past(glu_past, sh, tb), axis=0, keepdims=True)
        dp_ref[:, 3 * W:4 * W] = (dglu * sg).astype(dp_ref.dtype)
        dp_ref[:, 4 * W:5 * W] = (dglu * cu * sg * (1.0 - sg)).astype(dp_ref.dtype)
        dyc = dy_ref[:, 0:W]
        dcp = jnp.concatenate([dyc * sb, dyn_ref[:, 0:W] * next_ref[:, 0:W] * keep_next], axis=0)
        dcpc = dcp[0:tb]
        ext = jnp.concatenate([p_prev, p], axis=0)
        cp = w3_ref[K3 - 1:K3, :] * p
        dpp = w3_ref[K3 - 1:K3, :] * dcpc
        dw3_ref[K3 - 1:K3, :] += jnp.sum(dcpc * p, axis=0, keepdims=True)
        up = dcp
        for sh in range(1, K3):
            up = pltpu.roll(up, n - 1, 0)
            ext = pltpu.roll(ext, 1, 0)
            shifted = ext[ODD_HALO:n]
            cp = cp + w3_ref[K3 - 1 - sh:K3 - sh, :] * shifted
            dpp = dpp + w3_ref[K3 - 1 - sh:K3 - sh, :] * up[0:tb]
            dw3_ref[K3 - 1 - sh:K3 - sh, :] += jnp.sum(dcpc * shifted, axis=0, keepdims=True)
        dp_ref[:, 0:W] = (dyc * cp).astype(dp_ref.dtype)
        dp_ref[:, W:2 * W] = (dpp * svv).astype(dp_ref.dtype)
        dp_ref[:, 2 * W:3 * W] = (dpp * scv).astype(dp_ref.dtype)

    full = lambda shape: pl.BlockSpec(shape, lambda i: tuple(0 for _ in shape))
    prev_map = lambda i: (jnp.maximum(i * hb - 1, 0), 0)
    next_map = lambda i: (jnp.minimum((i + 1) * hb, nh - 1), 0)
    return pl.pallas_call(
        body, name=name, grid=(nb,),
        in_specs=[pl.BlockSpec((tb, 5 * W), lambda i: (i, 0)),
                  pl.BlockSpec((ODD_HALO, 5 * W), prev_map), pl.BlockSpec((ODD_HALO, 5 * W), next_map),
                  pl.BlockSpec((tb, 2 * W), lambda i: (i, 0)), pl.BlockSpec((ODD_HALO, 2 * W), next_map),
                  pl.BlockSpec((tb, W), lambda i: (i, 0)), pl.BlockSpec((ODD_HALO, W), next_map),
                  full((K3, W)), full((K31, W)), full((1, W)), full((1, W))],
        out_specs=[pl.BlockSpec((tb, 5 * W), lambda i: (i, 0)), full((K3, W)), full((K31, W)),
                   full((1, W)), full((1, W)), full((1, W))],
        out_shape=[jax.ShapeDtypeStruct((T, 5 * W), MXU_DTYPE), jax.ShapeDtypeStruct((K3, W), F32),
                   jax.ShapeDtypeStruct((K31, W), F32)] + [jax.ShapeDtypeStruct((1, W), F32)] * 3,
        compiler_params=_cp("arbitrary"),
    )(proj, proj, proj, dy, dy, dsave, dsave, sc_w, cf_w, ln_g, ln_b)


PACK_WIDTH = 1024


def _lower_bounds(logits):
    sm = jax.nn.softmax(logits.astype(F32), axis=0)
    return jnp.cumsum(sm, axis=0) - sm[0]


def _pack_rows(arrays):
    flat = jnp.concatenate([a.reshape(-1) for a in arrays])
    pad = (-flat.shape[0]) % (8 * PACK_WIDTH)
    return jnp.pad(flat, (0, pad)).reshape(-1, PACK_WIDTH)


def _unpack_rows(packed, shapes):
    flat = packed.reshape(-1)
    out, off = [], 0
    for s in shapes:
        sz = math.prod(s)
        out.append(flat[off:off + sz].reshape(s))
        off += sz
    return out


def _shards_last(a):
    n = a.shape[-1] // N_DEV
    return jnp.moveaxis(a.reshape(a.shape[:-1] + (N_DEV, n)), -2, 0)


def _unshard_last(a):
    a = jnp.moveaxis(a, 0, -2)
    return a.reshape(a.shape[:-2] + (a.shape[-2] * a.shape[-1],))


BIG = ("ev_w_in", "ev_w_out", "od_w_in", "od_w_out", "ffn_w_gate", "ffn_w_up", "ffn_w_down")
SMALL_SHARDED = ("lru_conv_w", "od_b_in", "sc_conv_w", "cf_conv_w", "cf_conv_b", "cf_ln_g", "cf_ln_b")
SMALL_REPL = ("ln_mix_g", "ln_ffn_g", "ln_final_g", "ev_b_in", "lru_conv_b", "lru_wa", "lru_ba", "lru_wx",
              "lru_bx", "lru_lambda", "hgrn_lb_logits", "hgrn_norm_g")
WEIGHTS = ("ln_mix_g", "ln_ffn_g", "ln_final_g", "ev_w_in", "ev_b_in", "lru_conv_w", "lru_conv_b", "lru_wa",
           "lru_ba", "lru_wx", "lru_bx", "lru_lambda", "hgrn_lb_logits", "hgrn_norm_g", "ev_w_out", "od_w_in",
           "od_b_in", "sc_conv_w", "cf_conv_w", "cf_conv_b", "cf_ln_g", "cf_ln_b", "od_w_out", "ffn_w_gate",
           "ffn_w_up", "ffn_w_down")


def _layer_weights(l):
    mix = ("ev_w_in", "ev_w_out") if l % 2 == 0 else ("od_w_in", "od_w_out")
    return [(mix[0], l // 2), (mix[1], l // 2), ("ffn_w_gate", l), ("ffn_w_up", l), ("ffn_w_down", l)]


class _MeshExchange:
    def __init__(self, w_bf, w, m, v, depth):
        self.w_bf, self.w, self.m, self.v, self.depth = w_bf, w, m, v, depth
        self.ready, self.flight, self.rs, self.adam = {}, {}, {}, {}

    @staticmethod
    def _names(l, grp):
        names = _layer_weights(l)
        return names[:2] if grp == "mix" else names[2:]

    def _own_start(self, l, grp, deps):
        names = self._names(l, grp)
        srcs = [self.w_bf[k] for k, _ in names]
        layers = [j for _, j in names]
        plan = _plan_gather_own(layers)
        sems, lands, tok = _split_start(f"ag_own_start_{grp}{l}", plan, srcs, _place_own(srcs, layers), deps)
        self.flight[l, grp] = (plan, sems, srcs, lands)
        return tok

    def _turn(self, l, grp, deps):
        plan, sems, srcs, lands = self.flight[l, grp]
        lands = _split_wait(f"ag_own_wait_{grp}{l}", plan, sems, srcs, lands, deps)
        plan = _plan_gather_pass(len(lands))
        sems, passed, tok = _split_start(f"ag_pass_start_{grp}{l}", plan, [], lands)
        self.flight[l, grp] = (plan, sems, [], passed)
        toks = [tok]
        nl, ng = (l, "ffn") if grp == "mix" else (l + 1, "mix")
        if nl < self.depth:
            toks.append(self._own_start(nl, ng, (tok,)))
        return tuple(toks)

    def _pass_wait(self, l, grp, deps):
        plan, sems, srcs, lands = self.flight.pop((l, grp))
        lands = _split_wait(f"ag_pass_wait_{grp}{l}", plan, sems, srcs, lands, deps)
        self.ready[l, grp] = dict(zip([k for k, _ in self._names(l, grp)], lands))

    def layer_begin(self, l):
        toks = ()
        if l == 0:
            self._own_start(0, "mix", ())
            toks = self._turn(0, "mix", ())
            self._pass_wait(0, "mix", ())
        return self.ready.pop((l, "mix")), toks

    def tick(self, l, t, after):
        if t == 2:
            return self._turn(l, "ffn", (after,))
        if t == 3:
            self._pass_wait(l, "ffn", (after,))
        if t == 4 and l + 1 < self.depth:
            return self._turn(l + 1, "mix", (after,))
        if t == 5 and l + 1 < self.depth:
            self._pass_wait(l + 1, "mix", (after,))
        return ()

    def ffn_weights(self, l):
        return self.ready.pop((l, "ffn"))

    def grads(self, tag, named):
        srcs = [g for _, _, g in named]
        lands = [lax.empty((4,) + g.shape[1:], g.dtype) for g in srcs]
        plan = _plan_scatter_pair(len(srcs))
        sems, lands, tok = _split_start(f"rs_pair_start_{tag}", plan, srcs, lands)
        self.rs[tag] = (named, plan, sems, srcs, lands)
        return (tok,)

    def grads_mid(self, tag, after):
        named, plan, sems, srcs, lands = self.rs[tag]
        got = _split_wait(f"rs_pair_wait_{tag}", plan, sems, srcs, lands, (after,))
        both = [_pair_add(f"pair_add_{tag}_{i}", a, b) for i, (a, b) in enumerate(zip(srcs, got))]
        self.rs[tag] = (named, [p for p, _ in both], [f for _, f in both])

    def grads_send(self, tag):
        named, parts, fins = self.rs[tag]
        plan = _plan_scatter_chips(len(parts))
        sems, fins, tok = _split_start(f"rs_chip_start_{tag}", plan, parts, fins)
        self.rs[tag] = (named, plan, sems, parts, fins)
        return (tok,)

    def grads_end(self, tag, after):
        named, plan, sems, parts, fins = self.rs.pop(tag)
        fins = _split_wait(f"rs_chip_wait_{tag}", plan, sems, parts, fins, (after,))
        for (k, j, _), fin in zip(named, fins):
            cols = fin.shape[-1]
            rows = fin.shape[-2]
            two_d = lambda a: a.reshape(-1, cols)
            self.adam[k] = _adamw(f"adamw_{k}_{j}", two_d(self.w[k]), two_d(self.m[k]), two_d(self.v[k]),
                                  fin, off=j * rows, prev=self.adam.get(k))

    def results(self, k):
        return [r.reshape(self.w[k].shape) for r in self.adam[k]]


def _local_step(x, tgt, p, ex):
    T, D = x.shape
    depth = p["ln_mix_g"].shape[0]
    lbs = _lower_bounds(p["hgrn_lb_logits"])
    row = lambda a: a.reshape(1, -1)
    saved = []
    h = None
    for l in range(depth):
        j = l // 2
        wl, tok = ex.layer_begin(l)
        s = {"x": x}
        if h is None:
            h = _rmsnorm_fwd(f"norm_mix{l}", x, row(p["ln_mix_g"][l]))
        s["h"] = h
        if l % 2 == 0:
            proj = _proj_in(f"ev_in{l}", h, wl["ev_w_in"], row(p["ev_b_in"][j]), tok)
            wa = p["lru_wa"][j].astype(MXU_DTYPE)
            wx = p["lru_wx"][j].astype(MXU_DTYPE)
            ya, xc, r, ig, hs = _lru_fwd(f"lru_fwd{l}", proj, p["lru_conv_w"][j], row(p["lru_conv_b"][j]),
                                         wa, row(p["lru_ba"][j]), wx, row(p["lru_bx"][j]),
                                         row(p["lru_lambda"][j]))
            y, o, sall = _hgrn_fwd(f"hgrn_fwd{l}", proj, row(lbs[j]), row(p["hgrn_norm_g"][j]), ya)
            s.update(proj=proj, xc=xc, r=r, ig=ig, hs=hs, o=o, sall=sall, wa=wa, wx=wx)
            w_out = wl["ev_w_out"]
        else:
            proj = _proj_in(f"od_in{l}", h, wl["od_w_in"], row(p["od_b_in"][j]), tok)
            y, dsave = _odd_fwd(f"odd_fwd{l}", proj, p["sc_conv_w"][j], p["cf_conv_w"][j],
                                row(p["cf_conv_b"][j]), row(p["cf_ln_g"][j]), row(p["cf_ln_b"][j]))
            s.update(proj=proj, dsave=dsave)
            w_out = wl["od_w_out"]
        x, h2 = _proj_out(f"mix_out{l}", y, w_out, x, ex.tick(l, 2, y), row(p["ln_ffn_g"][l]))
        s["y"] = y
        s["xmid"] = x
        ex.tick(l, 3, x)
        wl = {**wl, **ex.ffn_weights(l)}
        gate, up, hid = _ffn_in(f"ffn_in{l}", h2, wl["ffn_w_gate"], wl["ffn_w_up"])
        tok = ex.tick(l, 4, hid)
        if l + 1 < depth:
            x, h = _proj_out(f"ffn_out{l}", hid, wl["ffn_w_down"], x, tok, row(p["ln_mix_g"][l + 1]))
        else:
            x = _proj_out(f"ffn_out{l}", hid, wl["ffn_w_down"], x, tok)
        ex.tick(l, 5, x)
        s.update(h2=h2, gate=gate, up=up, hid=hid, w=wl)
        saved.append(s)

    loss, dx, dxb, dg_final = _loss_head("loss_head", x, row(p["ln_final_g"]), tgt)

    gs = {k: [None] * p[k].shape[0] for k in SMALL_REPL + SMALL_SHARDED if k not in ("ln_final_g", "hgrn_lb_logits")}
    d_lb = [None] * (depth // 2 + depth % 2)
    tok = ()
    pending = None
    for l in reversed(range(depth)):
        j = l // 2
        s = saved[l]
        wl = s["w"]
        ffn_shape = wl["ffn_w_gate"].shape[1:]
        dwd = _wgrad(f"ffn_dwd{l}", s["hid"], dxb, ffn_shape[1], ffn_shape[0], tok)
        dgate, dup = _ffn_bwd_hidden(f"ffn_bwd_hid{l}", dxb, wl["ffn_w_down"], s["gate"], s["up"])
        dwg = _wgrad(f"ffn_dwg{l}", s["h2"], dgate, *ffn_shape)
        dwu = _wgrad(f"ffn_dwu{l}", s["h2"], dup, *ffn_shape)
        tok = ex.grads(f"ffn{l}", [("ffn_w_down", l, dwd), ("ffn_w_gate", l, dwg), ("ffn_w_up", l, dwu)])
        dh2 = _bwd_in(f"ffn_dh{l}", [dgate, dup], [wl["ffn_w_gate"], wl["ffn_w_up"]], tok)
        dx, dxb, dg = _rmsnorm_bwd(f"norm_ffn_bwd{l}", s["xmid"], row(p["ln_ffn_g"][l]), dh2, dx)
        gs["ln_ffn_g"][l] = dg[0]
        ex.grads_mid(f"ffn{l}", dxb)
        if pending is not None:
            ex.grads_end(pending, dxb)
        tok = ex.grads_send(f"ffn{l}")
        w_in, w_out = ("ev_w_in", "ev_w_out") if l % 2 == 0 else ("od_w_in", "od_w_out")
        dwo = _wgrad(f"mix_dwo{l}", s["y"], dxb, *wl[w_out].shape[1:], tok)
        if l % 2 == 0:
            dy = _bwd_out(f"ev_dy{l}", dxb, wl["ev_w_out"])
            dph, dlb, dng = _hgrn_bwd(f"hgrn_bwd{l}", s["proj"], dy, s["o"], s["sall"], row(lbs[j]),
                                      row(p["hgrn_norm_g"][j]))
            dproj, d_cw, d_cb, d_wa, d_ba, d_wx, d_bx, d_lam = _lru_bwd(
                f"lru_bwd{l}", s["proj"], dy, s["xc"], s["r"], s["ig"], s["hs"], p["lru_conv_w"][j],
                s["wa"], s["wx"], row(p["lru_lambda"][j]), dph)
            gs["lru_conv_w"][j], gs["lru_conv_b"][j] = d_cw, d_cb[0]
            gs["lru_wa"][j], gs["lru_ba"][j] = d_wa, d_ba.reshape(p["lru_ba"].shape[1:])
            gs["lru_wx"][j], gs["lru_bx"][j] = d_wx, d_bx.reshape(p["lru_bx"].shape[1:])
            gs["lru_lambda"][j], gs["hgrn_norm_g"][j] = d_lam[0], dng[0]
            d_lb[j] = dlb[0]
            gs["ev_b_in"][j] = _colsum(f"ev_db{l}", dproj)[0]
        else:
            dy = _bwd_out(f"od_dy{l}", dxb, wl["od_w_out"])
            dproj, d_w3, d_w31, d_cfb, d_lg, d_lbeta = _odd_bwd(
                f"odd_bwd{l}", s["proj"], dy, s["dsave"], p["sc_conv_w"][j], p["cf_conv_w"][j],
                row(p["cf_ln_g"][j]), row(p["cf_ln_b"][j]))
            gs["sc_conv_w"][j], gs["cf_conv_w"][j] = d_w3, d_w31
            gs["cf_conv_b"][j], gs["cf_ln_g"][j], gs["cf_ln_b"][j] = d_cfb[0], d_lg[0], d_lbeta[0]
            gs["od_b_in"][j] = _colsum(f"od_db{l}", dproj)[0]
        dwi = _wgrad(f"mix_dwi{l}", s["h"], dproj, *wl[w_in].shape[1:])
        tok = ex.grads(f"mix{l}", [(w_out, j, dwo), (w_in, j, dwi)])
        dh = _bwd_in(f"mix_dh{l}", [dproj], [wl[w_in]], tok)
        dx, dxb, dg = _rmsnorm_bwd(f"norm_mix_bwd{l}", s["x"], row(p["ln_mix_g"][l]), dh, dx)
        gs["ln_mix_g"][l] = dg[0]
        ex.grads_mid(f"mix{l}", dxb)
        ex.grads_end(f"ffn{l}", dxb)
        tok = ex.grads_send(f"mix{l}")
        pending = f"mix{l}"
    ex.grads_end(pending, dxb)

    small = {k: jnp.stack(v) for k, v in gs.items()}
    small["ln_final_g"] = dg_final[0]
    _, lb_vjp = jax.vjp(_lower_bounds, p["hgrn_lb_logits"])
    small["hgrn_lb_logits"] = lb_vjp(jnp.stack(d_lb))[0]
    return loss, dx, small


def kernel(x, ln_mix_g, ln_ffn_g, ln_final_g, ev_w_in, ev_b_in, lru_conv_w, lru_conv_b, lru_wa, lru_ba, lru_wx, lru_bx, lru_lambda, hgrn_lb_logits, hgrn_norm_g, ev_w_out, od_w_in, od_b_in, sc_conv_w, cf_conv_w, cf_conv_b, cf_ln_g, cf_ln_b, od_w_out, ffn_w_gate, ffn_w_up, ffn_w_down, loss_target, m_ln_mix_g, m_ln_ffn_g, m_ln_final_g, m_ev_w_in, m_ev_b_in, m_lru_conv_w, m_lru_conv_b, m_lru_wa, m_lru_ba, m_lru_wx, m_lru_bx, m_lru_lambda, m_hgrn_lb_logits, m_hgrn_norm_g, m_ev_w_out, m_od_w_in, m_od_b_in, m_sc_conv_w, m_cf_conv_w, m_cf_conv_b, m_cf_ln_g, m_cf_ln_b, m_od_w_out, m_ffn_w_gate, m_ffn_w_up, m_ffn_w_down, v_ln_mix_g, v_ln_ffn_g, v_ln_final_g, v_ev_w_in, v_ev_b_in, v_lru_conv_w, v_lru_conv_b, v_lru_wa, v_lru_ba, v_lru_wx, v_lru_bx, v_lru_lambda, v_hgrn_lb_logits, v_hgrn_norm_g, v_ev_w_out, v_od_w_in, v_od_b_in, v_sc_conv_w, v_cf_conv_w, v_cf_conv_b, v_cf_ln_g, v_cf_ln_b, v_od_w_out, v_ffn_w_gate, v_ffn_w_up, v_ffn_w_down):
    args = locals()
    w = {k: args[k] for k in WEIGHTS}
    m = {k: args["m_" + k] for k in WEIGHTS}
    v = {k: args["v_" + k] for k in WEIGHTS}
    assert x.shape[0] == 1
    T, D = x.shape[1:]

    local_shapes = [w[k].shape for k in SMALL_SHARDED]
    gathered = _all_gather("gather_small_params", [_pack_rows([w[k] for k in SMALL_SHARDED])])[0]
    p = {k: w[k] for k in SMALL_REPL}
    per_dev = [_unpack_rows(gathered[s], local_shapes) for s in range(N_DEV)]
    for i, k in enumerate(SMALL_SHARDED):
        p[k] = _unshard_last(jnp.stack([per_dev[s][i] for s in range(N_DEV)]))

    ex = _MeshExchange({k: w[k].astype(MXU_DTYPE) for k in BIG}, w, m, v, ln_mix_g.shape[0])
    loss_part, dx, small = _local_step(x[0], loss_target[0], p, ex)
    loss = lax.psum(loss_part[0, 0], ("x", "y", "c"))

    small_sh = jnp.stack([_pack_rows([_shards_last(small[k])[s] for k in SMALL_SHARDED]) for s in range(N_DEV)])
    got = _pair_exchange("small_grads_to_sibling", [small_sh])[0]
    final_small = _chip_exchange("small_grads_to_chips", [_pair_add("pair_add_small", small_sh, got)[0]])[0]
    repl_parts = _all_gather("gather_small_grads", [_pack_rows([small[k] for k in SMALL_REPL])])[0]

    out_g, out_d, out_m, out_v = {}, {}, {}, {}
    for k in BIG:
        out_g[k], out_d[k], out_m[k], out_v[k] = ex.results(k)
    res = _adamw("adamw_small_sharded", *[_pack_rows([t[k] for k in SMALL_SHARDED]) for t in (w, m, v)],
                 final_small)
    for o, r in zip((out_g, out_d, out_m, out_v), res):
        o.update(zip(SMALL_SHARDED, _unpack_rows(r, local_shapes)))
    res = _adamw("adamw_small_repl", *[_pack_rows([t[k] for k in SMALL_REPL]) for t in (w, m, v)], repl_parts)
    for o, r in zip((out_g, out_d, out_m, out_v), res):
        o.update(zip(SMALL_REPL, _unpack_rows(r, [w[k].shape for k in SMALL_REPL])))

    return (loss, dx[None], *[out_g[k] for k in WEIGHTS], *[out_d[k] for k in WEIGHTS],
            *[out_m[k] for k in WEIGHTS], *[out_v[k] for k in WEIGHTS])
```

```python
import functools
import math

import jax
import jax.numpy as jnp
from jax import lax
from jax.experimental import pallas as pl
from jax.experimental.pallas import tpu as pltpu

F32 = jnp.float32
MXU_DTYPE = jnp.bfloat16
WIRE_DTYPE = jnp.bfloat16
N_DEV = 8
EPS = 1e-6
F_FLOOR = 1e-30
LRU_C = 8.0
HGRN_HEADS = 8
HGRN_SUB = 16
ADAM_LR, ADAM_B1, ADAM_B2, ADAM_EPS, ADAM_WD, ADAM_STEP = 0.001, 0.9, 0.999, 1e-08, 0.01, 10
V7X_VMEM_LIMIT = 48 * 1024 * 1024
MM_ROWS = 1024
WGRAD_ROWS = 2048
MESH = pl.DeviceIdType.MESH
ANY = pl.BlockSpec(memory_space=pl.ANY)


def _cp(*sem):
    return pltpu.CompilerParams(dimension_semantics=sem or None, vmem_limit_bytes=V7X_VMEM_LIMIT)


def _sigmoid(x):
    return 1.0 / (1.0 + jnp.exp(-x))


def _silu(x):
    return x * _sigmoid(x)


def _dsilu(x):
    s = _sigmoid(x)
    return s * (1.0 + x * (1.0 - s))


_GELU_C = math.sqrt(2.0 / math.pi)
LOG2E = 1.0 / math.log(2.0)


def _gelu(x):
    return 0.5 * x * (1.0 + jnp.tanh(_GELU_C * (x + 0.044715 * x * x * x)))


def _dgelu(x):
    t = jnp.tanh(_GELU_C * (x + 0.044715 * x * x * x))
    return 0.5 * (1.0 + t) + 0.5 * x * (1.0 - t * t) * _GELU_C * (1.0 + 3.0 * 0.044715 * x * x)


def _log1p(e):
    return jnp.where(e < 1e-2, e * (1.0 - e * (0.5 - e * (1.0 / 3.0))), jnp.log(1.0 + e))


def _softplus(x):
    return jnp.maximum(x, 0.0) + _log1p(jnp.exp(-jnp.abs(x)))


def _one_minus_exp(x):
    series = -x * (1.0 + x * (0.5 + x * (1.0 / 6.0 + x * (1.0 / 24.0))))
    return jnp.where(x > -0.05, series, 1.0 - jnp.exp(x))


def _rows(n, d=1):
    return lax.broadcasted_iota(jnp.int32, (n, d), 0)


def _dot(a, b):
    return jnp.dot(a.astype(MXU_DTYPE), b.astype(MXU_DTYPE), preferred_element_type=F32)


def _dot_nt(a, b):
    return lax.dot_general(a.astype(MXU_DTYPE), b.astype(MXU_DTYPE), (((1,), (1,)), ((), ())),
                           preferred_element_type=F32)


def _dot_tn(a, b):
    return lax.dot_general(a.astype(MXU_DTYPE), b.astype(MXU_DTYPE), (((0,), (0,)), ((), ())),
                           preferred_element_type=F32)


def _tile(n, want):
    if n <= want:
        return n
    t = want - want % 8
    while n % t:
        t -= 8
    assert t > 0, (n, want)
    return t


def _my_place():
    x, y, c = lax.axis_index("x"), lax.axis_index("y"), lax.axis_index("c")
    return x, y, c


def _all_gather(name, srcs):
    n = len(srcs)

    def body(*refs):
        src_refs, out_refs = refs[:n], refs[n:2 * n]
        send_sems, recv_sems, local_sems = refs[2 * n:]
        x, y, c = _my_place()
        sibling = (x, y, 1 - c)
        chips = [(1 - x, y), (x, 1 - y), (1 - x, 1 - y)]

        def slot(px, py, pc):
            return 4 * px + 2 * py + pc

        def copy(i, k, block, to, src=None):
            dst = out_refs[i].at[slot(*block)]
            return pltpu.make_async_remote_copy(
                src_ref=dst if src is None else src, dst_ref=dst,
                send_sem=send_sems.at[i, k], recv_sem=recv_sems.at[i, k],
                device_id=to, device_id_type=MESH)

        me = (x, y, c)
        sends, own = [], []
        for i in range(n):
            mine = pltpu.make_async_copy(src_refs[i], out_refs[i].at[slot(*me)], local_sems.at[i])
            mine.start()
            own.append(mine)
            first = [copy(i, 0, me, sibling, src=src_refs[i])]
            first += [copy(i, 1 + j, me, (*chip, c), src=src_refs[i]) for j, chip in enumerate(chips)]
            for cp in first:
                cp.start()
            sends += first
        for i in range(n):
            for j, chip in enumerate(chips):
                copy(i, 1 + j, (*chip, c), me).wait_recv()
                passed = copy(i, 4 + j, (*chip, c), sibling)
                passed.start()
                sends.append(passed)
        for i in range(n):
            copy(i, 0, sibling, me).wait_recv()
            for j, chip in enumerate(chips):
                copy(i, 4 + j, (*chip, 1 - c), me).wait_recv()
        for cp in sends:
            cp.wait_send()
        for cp in own:
            cp.wait()

    outs = pl.pallas_call(
        body, name=name,
        out_shape=[jax.ShapeDtypeStruct((N_DEV,) + s.shape, s.dtype) for s in srcs],
        in_specs=[ANY] * n, out_specs=[ANY] * n,
        scratch_shapes=[pltpu.SemaphoreType.DMA((n, 7)), pltpu.SemaphoreType.DMA((n, 7)),
                        pltpu.SemaphoreType.DMA((n,))],
    )(*srcs)
    return list(outs)


def _pair_exchange(name, srcs):
    n = len(srcs)

    def body(*refs):
        src_refs, out_refs = refs[:n], refs[n:2 * n]
        send_sems, recv_sems = refs[2 * n:]
        x, y, c = _my_place()
        copies = []
        for i in range(n):
            for j in range(4):
                cp = pltpu.make_async_remote_copy(
                    src_ref=src_refs[i].at[2 * j + (1 - c)], dst_ref=out_refs[i].at[j],
                    send_sem=send_sems.at[i, j], recv_sem=recv_sems.at[i, j],
                    device_id=(x, y, 1 - c), device_id_type=MESH)
                cp.start()
                copies.append(cp)
        for cp in copies:
            cp.wait()

    outs = pl.pallas_call(
        body, name=name,
        out_shape=[jax.ShapeDtypeStruct((4,) + s.shape[1:], s.dtype) for s in srcs],
        in_specs=[ANY] * n, out_specs=[ANY] * n,
        scratch_shapes=[pltpu.SemaphoreType.DMA((n, 4)), pltpu.SemaphoreType.DMA((n, 4))],
    )(*srcs)
    return list(outs)


def _chip_exchange(name, srcs):
    n = len(srcs)

    def body(*refs):
        src_refs, out_refs = refs[:n], refs[n:2 * n]
        send_sems, recv_sems, local_sems = refs[2 * n:]
        x, y, c = _my_place()
        chip = 2 * x + y
        copies = []
        for i in range(n):
            mine = pltpu.make_async_copy(src_refs[i].at[chip], out_refs[i].at[3], local_sems.at[i])
            mine.start()
            copies.append(mine)
            for k, (fx, fy) in enumerate([(1, 0), (0, 1), (1, 1)]):
                px = x + fx - 2 * x * fx
                py = y + fy - 2 * y * fy
                cp = pltpu.make_async_remote_copy(
                    src_ref=src_refs[i].at[2 * px + py], dst_ref=out_refs[i].at[k],
                    send_sem=send_sems.at[i, k], recv_sem=recv_sems.at[i, k],
                    device_id=(px, py, c), device_id_type=MESH)
                cp.start()
                copies.append(cp)
        for cp in copies:
            cp.wait()

    outs = pl.pallas_call(
        body, name=name,
        out_shape=[jax.ShapeDtypeStruct(s.shape, s.dtype) for s in srcs],
        in_specs=[ANY] * n, out_specs=[ANY] * n,
        scratch_shapes=[pltpu.SemaphoreType.DMA((n, 3)), pltpu.SemaphoreType.DMA((n, 3)),
                        pltpu.SemaphoreType.DMA((n,))],
    )(*srcs)
    return list(outs)


def _pair_add(name, mine, got):
    assert mine.shape[0] == N_DEV and got.shape[0] == 4
    cdim = mine.shape[-1]
    m4 = mine.reshape(4, 2, -1, cdim)
    g3 = got.reshape(4, -1, cdim)
    rows = m4.shape[2]
    tr = _tile(rows, 512)

    def body(m_ref, g_ref, o_ref, fin_ref):
        x, y, c = _my_place()
        s = (m_ref[c].astype(F32) + g_ref[...].astype(F32)).astype(o_ref.dtype)
        o_ref[...] = s

        @pl.when(pl.program_id(1) == 2 * x + y)
        def _():
            fin_ref[...] = s

    out, fin = pl.pallas_call(
        body, name=name, grid=(rows // tr, 4),
        in_specs=[pl.BlockSpec((None, 2, tr, cdim), lambda i, j: (j, 0, i, 0)),
                  pl.BlockSpec((None, tr, cdim), lambda i, j: (j, i, 0))],
        out_specs=[pl.BlockSpec((None, tr, cdim), lambda i, j: (j, i, 0)),
                   pl.BlockSpec((None, tr, cdim), lambda i, j: (3, i, 0))],
        out_shape=[jax.ShapeDtypeStruct(g3.shape, got.dtype)] * 2,
        compiler_params=_cp("parallel", "arbitrary"),
    )(m4, g3)
    return out.reshape(got.shape), fin.reshape(got.shape)


HBM = pl.BlockSpec(memory_space=pltpu.HBM)
SEM = pl.BlockSpec(memory_space=pltpu.SEMAPHORE)
EFFECT = pltpu.SideEffectType.DATAFLOW_SIDE_EFFECTING
SLOTS = 4


def _hbm(a):
    return pltpu.with_memory_space_constraint(a, pltpu.HBM)


def _remote(src, dst, sems, i, k, to):
    return pltpu.make_async_remote_copy(src_ref=src, dst_ref=dst, send_sem=sems[0].at[i * SLOTS + k],
                                        recv_sem=sems[1].at[i * SLOTS + k], device_id=to, device_id_type=MESH)


def _slot(px, py, pc):
    return 4 * px + 2 * py + pc


def _other_chips(x, y):
    return [(1 - x, y), (x, 1 - y), (1 - x, 1 - y)]


def _plan_gather_own(layers):
    def plan(srcs, lands, sems):
        x, y, c = _my_place()
        out = []
        for i, j in enumerate(layers):
            dst = lands[i].at[_slot(x, y, c)]
            out.append(_remote(srcs[i].at[j], dst, sems, i, 0, (x, y, 1 - c)))
            for k, (px, py) in enumerate(_other_chips(x, y)):
                out.append(_remote(srcs[i].at[j], dst, sems, i, 1 + k, (px, py, c)))
        return out
    return plan


def _plan_gather_pass(n):
    def plan(srcs, lands, sems):
        x, y, c = _my_place()
        out = []
        for i in range(n):
            for k, (px, py) in enumerate(_other_chips(x, y)):
                blk = lands[i].at[_slot(px, py, c)]
                out.append(_remote(blk, blk, sems, i, k, (x, y, 1 - c)))
        return out
    return plan


def _plan_scatter_pair(n):
    def plan(srcs, lands, sems):
        x, y, c = _my_place()
        return [_remote(srcs[i].at[2 * j + (1 - c)], lands[i].at[j], sems, i, j, (x, y, 1 - c))
                for i in range(n) for j in range(4)]
    return plan


def _plan_scatter_chips(n):
    def plan(srcs, lands, sems):
        x, y, c = _my_place()
        return [_remote(srcs[i].at[2 * px + py], lands[i].at[k], sems, i, k, (px, py, c))
                for i in range(n) for k, (px, py) in enumerate(_other_chips(x, y))]
    return plan


def _split_start(name, plan, srcs, lands, deps=()):
    ns, nl, nd = len(srcs), len(lands), len(deps)
    n = max(ns, nl)

    def body(*refs):
        sems = refs[ns + nl + nd:ns + nl + nd + 2]
        for cp in plan(refs[:ns], refs[ns:ns + nl], sems):
            cp.start()
        refs[-1][...] = jnp.zeros_like(refs[-1])

    outs = pl.pallas_call(
        body, name=name,
        out_shape=(pltpu.SemaphoreType.DMA((n * SLOTS,)), pltpu.SemaphoreType.DMA((n * SLOTS,)),
                   *[pltpu.HBM(a.shape, a.dtype) for a in lands], jax.ShapeDtypeStruct((8, 128), F32)),
        in_specs=[HBM] * (ns + nl) + [ANY] * nd,
        out_specs=(SEM, SEM, *[HBM] * nl, pl.BlockSpec(memory_space=pltpu.VMEM)),
        input_output_aliases={ns + i: 2 + i for i in range(nl)},
        compiler_params=pltpu.CompilerParams(has_side_effects=EFFECT),
    )(*[_hbm(a) for a in srcs], *[_hbm(a) for a in lands], *deps)
    return (outs[0], outs[1]), list(outs[2:2 + nl]), outs[-1]


def _split_wait(name, plan, sems, srcs, lands, deps=()):
    ns, nl, nd = len(srcs), len(lands), len(deps)

    def body(*refs):
        for cp in plan(refs[:ns], refs[ns:ns + nl], refs[ns + nl:ns + nl + 2]):
            cp.wait_send()
            cp.wait_recv()

    outs = pl.pallas_call(
        body, name=name,
        out_shape=tuple(pltpu.HBM(a.shape, a.dtype) for a in lands),
        in_specs=[HBM] * (ns + nl) + [SEM, SEM] + [ANY] * nd,
        out_specs=tuple([HBM] * nl),
        input_output_aliases={ns + i: i for i in range(nl)},
        compiler_params=pltpu.CompilerParams(has_side_effects=EFFECT),
    )(*srcs, *lands, *sems, *deps)
    return list(outs)


def _place_own(srcs, layers):
    x, y, c = _my_place()
    zero = jnp.zeros((), jnp.int32)
    return [lax.dynamic_update_slice(lax.empty((N_DEV,) + a.shape[1:], a.dtype), a[j][None],
                                     (_slot(x, y, c),) + (zero,) * (a.ndim - 1))
            for a, j in zip(srcs, layers)]


def _put(dp_ref, db_ref, lo, hi, val):
    dp_ref[:, lo:hi] = val.astype(dp_ref.dtype)
    db_ref[:, lo:hi] += jnp.sum(val, axis=0, keepdims=True)


def _with_deps(body, n_in, deps):
    nd = len(deps)
    if not nd:
        return body

    def wrapped(*refs):
        body(*refs[:n_in], *refs[n_in + nd:])

    return wrapped


def _rmsnorm_fwd(name, x, g, deps=()):
    T, D = x.shape
    tm = _tile(T, 256)

    def body(x_ref, g_ref, o_ref):
        xv = x_ref[...]
        r = lax.rsqrt(jnp.mean(xv * xv, axis=-1, keepdims=True) + EPS)
        o_ref[...] = ((xv * r) * g_ref[...]).astype(o_ref.dtype)

    return pl.pallas_call(
        _with_deps(body, 2, deps), name=name, grid=(T // tm,),
        in_specs=[pl.BlockSpec((tm, D), lambda i: (i, 0)), pl.BlockSpec((1, D), lambda i: (0, 0))]
        + [ANY] * len(deps),
        out_specs=pl.BlockSpec((tm, D), lambda i: (i, 0)),
        out_shape=jax.ShapeDtypeStruct((T, D), MXU_DTYPE), compiler_params=_cp("parallel"),
    )(x, g, *deps)


def _rmsnorm_bwd(name, x, g, dh, dres):
    T, D = x.shape
    tm = _tile(T, 256)

    def body(x_ref, g_ref, dh_ref, dres_ref, dx_ref, dxb_ref, dg_ref):
        xv = x_ref[...]
        r = lax.rsqrt(jnp.mean(xv * xv, axis=-1, keepdims=True) + EPS)
        xh = xv * r
        dhv = dh_ref[...]

        @pl.when(pl.program_id(0) == 0)
        def _():
            dg_ref[...] = jnp.zeros_like(dg_ref)

        dg_ref[...] += jnp.sum(dhv * xh, axis=0, keepdims=True)
        dxh = dhv * g_ref[...]
        dx = dres_ref[...] + r * (dxh - xh * jnp.mean(dxh * xh, axis=-1, keepdims=True))
        dx_ref[...] = dx
        dxb_ref[...] = dx.astype(dxb_ref.dtype)

    return pl.pallas_call(
        body, name=name, grid=(T // tm,),
        in_specs=[pl.BlockSpec((tm, D), lambda i: (i, 0)), pl.BlockSpec((1, D), lambda i: (0, 0)),
                  pl.BlockSpec((tm, D), lambda i: (i, 0)), pl.BlockSpec((tm, D), lambda i: (i, 0))],
        out_specs=[pl.BlockSpec((tm, D), lambda i: (i, 0)), pl.BlockSpec((tm, D), lambda i: (i, 0)),
                   pl.BlockSpec((1, D), lambda i: (0, 0))],
        out_shape=[jax.ShapeDtypeStruct((T, D), F32), jax.ShapeDtypeStruct((T, D), MXU_DTYPE),
                   jax.ShapeDtypeStruct((1, D), F32)],
        compiler_params=_cp("arbitrary"),
    )(x, g, dh, dres)


def _loss_head(name, x, g, tgt):
    T, D = x.shape
    tm = _tile(T, 256)

    def body(x_ref, g_ref, t_ref, loss_ref, dx_ref, dxb_ref, dg_ref):
        xv = x_ref[...]
        r = lax.rsqrt(jnp.mean(xv * xv, axis=-1, keepdims=True) + EPS)
        xh = xv * r
        gv = g_ref[...]
        diff = xh * gv - t_ref[...]

        @pl.when(pl.program_id(0) == 0)
        def _():
            dg_ref[...] = jnp.zeros_like(dg_ref)
            loss_ref[...] = jnp.zeros_like(loss_ref)

        part = 0.5 * jnp.sum(jnp.mean(diff * diff, axis=-1, keepdims=True), axis=0, keepdims=True)
        loss_ref[...] += jnp.broadcast_to(part, loss_ref.shape)
        dy = diff * (1.0 / D)
        dg_ref[...] += jnp.sum(dy * xh, axis=0, keepdims=True)
        dxh = dy * gv
        dx = r * (dxh - xh * jnp.mean(dxh * xh, axis=-1, keepdims=True))
        dx_ref[...] = dx
        dxb_ref[...] = dx.astype(dxb_ref.dtype)

    return pl.pallas_call(
        body, name=name, grid=(T // tm,),
        in_specs=[pl.BlockSpec((tm, D), lambda i: (i, 0)), pl.BlockSpec((1, D), lambda i: (0, 0)),
                  pl.BlockSpec((tm, D), lambda i: (i, 0))],
        out_specs=[pl.BlockSpec((1, 128), lambda i: (0, 0)), pl.BlockSpec((tm, D), lambda i: (i, 0)),
                   pl.BlockSpec((tm, D), lambda i: (i, 0)), pl.BlockSpec((1, D), lambda i: (0, 0))],
        out_shape=[jax.ShapeDtypeStruct((1, 128), F32), jax.ShapeDtypeStruct((T, D), F32),
                   jax.ShapeDtypeStruct((T, D), MXU_DTYPE), jax.ShapeDtypeStruct((1, D), F32)],
        compiler_params=_cp("arbitrary"),
    )(x, g, tgt)


def _adamw(name, w, m, v, parts, j=0, prev=None):
    L, R, C = w.shape
    P = parts.shape[0]
    tr = _tile(R, 256)
    c1 = 1.0 / (1.0 - ADAM_B1 ** ADAM_STEP)
    c2 = 1.0 / (1.0 - ADAM_B2 ** ADAM_STEP)
    chained = L > 1
    if chained and prev is None:
        prev = [lax.empty(w.shape, F32) for _ in range(4)]
    prev = list(prev) if chained else []

    def body(w_ref, m_ref, v_ref, p_ref, *rest):
        g_ref, d_ref, nm_ref, nv_ref = rest[len(prev):]
        g = p_ref[0].astype(F32)
        for s in range(1, P):
            g = g + p_ref[s].astype(F32)
        nm = ADAM_B1 * m_ref[...] + (1.0 - ADAM_B1) * g
        nv = ADAM_B2 * v_ref[...] + (1.0 - ADAM_B2) * (g * g)
        g_ref[...] = g
        nm_ref[...] = nm
        nv_ref[...] = nv
        d_ref[...] = -ADAM_LR * ((nm * c1) / (jnp.sqrt(nv * c2) + ADAM_EPS) + ADAM_WD * w_ref[...])

    blk = pl.BlockSpec((None, tr, C), lambda i: (j, i, 0))
    return pl.pallas_call(
        body, name=name, grid=(R // tr,),
        in_specs=[blk, blk, blk, pl.BlockSpec((P, tr, C), lambda i: (0, i, 0))] + [ANY] * len(prev),
        out_specs=[blk, blk, blk, blk],
        out_shape=[jax.ShapeDtypeStruct(w.shape, F32)] * 4,
        input_output_aliases={4 + i: i for i in range(len(prev))},
        compiler_params=_cp("parallel"),
    )(w, m, v, parts, *prev)


def _proj_in(name, h, wg, bias, deps=()):
    T, K = h.shape
    n = wg.shape[-1]
    tm = _tile(T, MM_ROWS)

    def body(a_ref, w_ref, b_ref, o_ref):
        o_ref[...] = _dot(a_ref[...], w_ref[...]) + b_ref[...]

    return pl.pallas_call(
        _with_deps(body, 3, deps), name=name, grid=(N_DEV, T // tm),
        in_specs=[pl.BlockSpec((tm, K), lambda s, i: (i, 0)),
                  pl.BlockSpec((None, K, n), lambda s, i: (s, 0, 0)),
                  pl.BlockSpec((1, n), lambda s, i: (0, s))] + [ANY] * len(deps),
        out_specs=pl.BlockSpec((tm, n), lambda s, i: (i, s)),
        out_shape=jax.ShapeDtypeStruct((T, N_DEV * n), F32), compiler_params=_cp("parallel", "parallel"),
    )(h, wg, bias, *deps)


def _ffn_in(name, h, wg_gate, wg_up):
    T, K = h.shape
    n = wg_gate.shape[-1]
    tm = _tile(T, MM_ROWS)

    def body(a_ref, wgt_ref, wup_ref, g_ref, u_ref, hid_ref):
        a = a_ref[...]
        g = _dot(a, wgt_ref[...])
        u = _dot(a, wup_ref[...])
        g_ref[...] = g
        u_ref[...] = u
        hid_ref[...] = (_silu(g) * u).astype(hid_ref.dtype)

    wspec = pl.BlockSpec((None, K, n), lambda s, i: (s, 0, 0))
    ospec = pl.BlockSpec((None, tm, n), lambda s, i: (s, i, 0))
    return pl.pallas_call(
        body, name=name, grid=(N_DEV, T // tm),
        in_specs=[pl.BlockSpec((tm, K), lambda s, i: (i, 0)), wspec, wspec],
        out_specs=[ospec, ospec, ospec],
        out_shape=[jax.ShapeDtypeStruct((N_DEV, T, n), F32), jax.ShapeDtypeStruct((N_DEV, T, n), F32),
                   jax.ShapeDtypeStruct((N_DEV, T, n), MXU_DTYPE)],
        compiler_params=_cp("parallel", "parallel"),
    )(h, wg_gate, wg_up)


def _a_spec(a, tm, k):
    if a.ndim == 2:
        return pl.BlockSpec((tm, k), lambda i, s: (i, s))
    return pl.BlockSpec((None, tm, k), lambda i, s: (s, i, 0))


def _proj_out(name, a, wg, res, deps=(), norm_g=None):
    k, N = wg.shape[-2:]
    T = res.shape[0]
    tm = _tile(T, MM_ROWS // 2)
    extra = [] if norm_g is None else [norm_g]

    def body(a_ref, w_ref, r_ref, *rest):
        o_ref = rest[len(extra)]
        p = _dot(a_ref[...], w_ref[...])

        @pl.when(pl.program_id(1) == 0)
        def _():
            o_ref[...] = r_ref[...] + p

        @pl.when(pl.program_id(1) > 0)
        def _():
            o_ref[...] += p

        if extra:
            @pl.when(pl.program_id(1) == N_DEV - 1)
            def _():
                xv = o_ref[...]
                r = lax.rsqrt(jnp.mean(xv * xv, axis=-1, keepdims=True) + EPS)
                rest[2][...] = ((xv * r) * rest[0][...]).astype(rest[2].dtype)

    row_blk = pl.BlockSpec((tm, N), lambda i, s: (i, 0))
    out = pl.pallas_call(
        _with_deps(body, 3 + len(extra), deps), name=name, grid=(T // tm, N_DEV),
        in_specs=[_a_spec(a, tm, k), pl.BlockSpec((None, k, N), lambda i, s: (s, 0, 0)), row_blk]
        + [pl.BlockSpec((1, N), lambda i, s: (0, 0))] * len(extra) + [ANY] * len(deps),
        out_specs=[row_blk] * (1 + len(extra)),
        out_shape=[jax.ShapeDtypeStruct((T, N), F32)] + [jax.ShapeDtypeStruct((T, N), MXU_DTYPE)] * len(extra),
        compiler_params=_cp("parallel", "arbitrary"),
    )(a, wg, res, *extra, *deps)
    return out[0] if norm_g is None else out


def _mix_out(name, y, wg, res, norm_g, deps=()):
    N = wg.shape[-1]
    w2 = wg.reshape(-1, N)
    K = w2.shape[0]
    T = res.shape[0]
    tm = _tile(T, MM_ROWS // 2)

    def body(a_ref, w_ref, r_ref, g_ref, o_ref, h_ref):
        xv = r_ref[...] + _dot(a_ref[...], w_ref[...])
        o_ref[...] = xv
        r = lax.rsqrt(jnp.mean(xv * xv, axis=-1, keepdims=True) + EPS)
        h_ref[...] = ((xv * r) * g_ref[...]).astype(h_ref.dtype)

    row_blk = pl.BlockSpec((tm, N), lambda i: (i, 0))
    return pl.pallas_call(
        _with_deps(body, 4, deps), name=name, grid=(T // tm,),
        in_specs=[pl.BlockSpec((tm, K), lambda i: (i, 0)),
                  pl.BlockSpec((K, N), lambda i: (0, 0), pipeline_mode=pl.Buffered(1)),
                  row_blk, pl.BlockSpec((1, N), lambda i: (0, 0))] + [ANY] * len(deps),
        out_specs=[row_blk, row_blk],
        out_shape=[jax.ShapeDtypeStruct((T, N), F32), jax.ShapeDtypeStruct((T, N), MXU_DTYPE)],
        compiler_params=_cp("parallel"),
    )(y, w2, res, norm_g, *deps)


def _bwd_in(name, das, wgs, deps=()):
    K, n = wgs[0].shape[-2:]
    T = das[0].shape[-2]
    tm = _tile(T, MM_ROWS)
    npair = len(das)

    def body(*refs):
        o_ref = refs[-1]
        p = _dot_nt(refs[0][...], refs[npair][...])
        for q in range(1, npair):
            p = p + _dot_nt(refs[q][...], refs[npair + q][...])

        @pl.when(pl.program_id(1) == 0)
        def _():
            o_ref[...] = p

        @pl.when(pl.program_id(1) > 0)
        def _():
            o_ref[...] += p

    return pl.pallas_call(
        _with_deps(body, 2 * npair, deps), name=name, grid=(T // tm, N_DEV),
        in_specs=[_a_spec(a, tm, n) for a in das]
        + [pl.BlockSpec((None, K, n), lambda i, s: (s, 0, 0)) for _ in wgs] + [ANY] * len(deps),
        out_specs=pl.BlockSpec((tm, K), lambda i, s: (i, 0)),
        out_shape=jax.ShapeDtypeStruct((T, K), F32), compiler_params=_cp("parallel", "arbitrary"),
    )(*das, *wgs, *deps)


def _bwd_out(name, dx, wg):
    k, N = wg.shape[-2:]
    T = dx.shape[0]
    tm = _tile(T, MM_ROWS)

    def body(a_ref, w_ref, o_ref):
        o_ref[...] = _dot_nt(a_ref[...], w_ref[...])

    return pl.pallas_call(
        body, name=name, grid=(N_DEV, T // tm),
        in_specs=[pl.BlockSpec((tm, N), lambda s, i: (i, 0)),
                  pl.BlockSpec((None, k, N), lambda s, i: (s, 0, 0))],
        out_specs=pl.BlockSpec((tm, k), lambda s, i: (i, s)),
        out_shape=jax.ShapeDtypeStruct((T, N_DEV * k), F32), compiler_params=_cp("parallel", "parallel"),
    )(dx, wg)


def _ffn_bwd_hidden(name, dx, wg_down, gate, up):
    n, N = wg_down.shape[-2:]
    T = dx.shape[0]
    tm = _tile(T, MM_ROWS)

    def body(a_ref, w_ref, g_ref, u_ref, dg_ref, du_ref):
        dh = _dot_nt(a_ref[...], w_ref[...])
        g = g_ref[...]
        dg_ref[...] = (dh * u_ref[...] * _dsilu(g)).astype(dg_ref.dtype)
        du_ref[...] = (dh * _silu(g)).astype(du_ref.dtype)

    sm = pl.BlockSpec((None, tm, n), lambda s, i: (s, i, 0))
    return pl.pallas_call(
        body, name=name, grid=(N_DEV, T // tm),
        in_specs=[pl.BlockSpec((tm, N), lambda s, i: (i, 0)),
                  pl.BlockSpec((None, n, N), lambda s, i: (s, 0, 0)), sm, sm],
        out_specs=[sm, sm],
        out_shape=[jax.ShapeDtypeStruct((N_DEV, T, n), MXU_DTYPE)] * 2,
        compiler_params=_cp("parallel", "parallel"),
    )(dx, wg_down, gate, up)


def _wgrad(name, a, c, rows, cols, deps=()):
    T = a.shape[-2]
    tk = _tile(T, WGRAD_ROWS)
    nk = T // tk

    def spec(z, w):
        if z.ndim == 3:
            return pl.BlockSpec((None, tk, w), lambda s, k: (s, k, 0))
        if z.shape[1] == w:
            return pl.BlockSpec((tk, w), lambda s, k: (k, 0))
        return pl.BlockSpec((tk, w), lambda s, k: (k, s))

    def body(a_ref, c_ref, o_ref, acc_ref):
        k = pl.program_id(1)
        p = _dot_tn(a_ref[...], c_ref[...])

        @pl.when(k == 0)
        def _():
            acc_ref[...] = p

        @pl.when(k > 0)
        def _():
            acc_ref[...] += p

        @pl.when(k == nk - 1)
        def _():
            o_ref[...] = acc_ref[...].astype(o_ref.dtype)

    return pl.pallas_call(
        _with_deps(body, 2, deps), name=name, grid=(N_DEV, nk),
        in_specs=[spec(a, rows), spec(c, cols)] + [ANY] * len(deps),
        out_specs=pl.BlockSpec((None, rows, cols), lambda s, k: (s, 0, 0)),
        out_shape=jax.ShapeDtypeStruct((N_DEV, rows, cols), WIRE_DTYPE),
        scratch_shapes=[pltpu.VMEM((rows, cols), F32)],
        compiler_params=_cp("parallel", "arbitrary"),
    )(a, c, *deps)


def _shift_down(cur, prev8, sh):
    n = cur.shape[0]
    rolled = pltpu.roll(cur, sh, 0)
    top = jnp.where(_rows(8) < sh, pltpu.roll(prev8, sh, 0), rolled[0:8])
    return jnp.concatenate([top, rolled[8:n]], axis=0)


def _shift_up(cur, next8, sh):
    n = cur.shape[0]
    rolled = pltpu.roll(cur, n - sh, 0)
    bot = jnp.where(_rows(8) >= 8 - sh, pltpu.roll(next8, 8 - sh, 0), rolled[n - 8:n])
    return jnp.concatenate([rolled[0:n - 8], bot], axis=0)


def _lru_gate_terms(r, lam):
    sp = _softplus(-lam)
    la = -LRU_C * r * sp
    a = jnp.exp(la)
    m2 = _one_minus_exp(2.0 * la)
    return sp, la, a, m2


def _lru_fwd(name, proj, conv_w, conv_b, wa, ba, wx, bx, lam):
    T = proj.shape[0]
    H, hd, _ = wa.shape
    W = H * hd
    K = conv_w.shape[0]
    tb = _tile(T, 256)

    def body(xin_ref, gate_ref, cw_ref, cb_ref, wa_ref, ba_ref, wx_ref, bx_ref, lam_ref,
             ya_ref, xc_ref, r_ref, i_ref, hs_ref, tail_ref, hprev_ref):
        blk = pl.program_id(0)

        @pl.when(blk == 0)
        def _():
            tail_ref[...] = jnp.zeros_like(tail_ref)
            hprev_ref[...] = jnp.zeros_like(hprev_ref)

        xin = xin_ref[...]
        prev8 = tail_ref[...]
        xc = cw_ref[K - 1:K, :] * xin
        for sh in range(1, K):
            xc = xc + cw_ref[K - 1 - sh:K - sh, :] * _shift_down(xin, prev8, sh)
        xc = xc + cb_ref[...]
        tail_ref[...] = xin[tb - 8:tb]
        xc_ref[...] = xc
        for h in range(H):
            cs = slice(h * hd, (h + 1) * hd)
            xh = xc[:, cs]
            r_ref[:, cs] = _sigmoid(_dot(xh, wa_ref[h]) + ba_ref[:, cs])
            i_ref[:, cs] = _sigmoid(_dot(xh, wx_ref[h]) + bx_ref[:, cs])
        r = r_ref[...]
        _, _, a, m2 = _lru_gate_terms(r, lam_ref[...])
        row = _rows(tb)
        mult = jnp.where((row == 0) & (blk == 0), 1.0, jnp.sqrt(jnp.maximum(m2, 0.0)))
        u = mult * i_ref[...] * xc
        d = 1
        while d < tb:
            keep = row >= d
            u = a * jnp.where(keep, pltpu.roll(u, d, 0), 0.0) + u
            a = a * jnp.where(keep, pltpu.roll(a, d, 0), 1.0)
            d *= 2
        hs = u + a * hprev_ref[...]
        hprev_ref[...] = hs[tb - 1:tb]
        hs_ref[...] = hs
        ya_ref[...] = (hs * _gelu(gate_ref[...])).astype(ya_ref.dtype)

    full = lambda shape: pl.BlockSpec(shape, lambda i: tuple(0 for _ in shape))
    blk = pl.BlockSpec((tb, W), lambda i: (i, 0))
    return pl.pallas_call(
        body, name=name, grid=(T // tb,),
        in_specs=[pl.BlockSpec((tb, W), lambda i: (i, 0)), pl.BlockSpec((tb, W), lambda i: (i, 1)),
                  full((K, W)), full((1, W)), full((H, hd, hd)), full((1, W)), full((H, hd, hd)),
                  full((1, W)), full((1, W))],
        out_specs=[blk] * 5,
        out_shape=[jax.ShapeDtypeStruct((T, 2 * W), MXU_DTYPE)] + [jax.ShapeDtypeStruct((T, W), F32)] * 4,
        scratch_shapes=[pltpu.VMEM((8, W), F32), pltpu.VMEM((1, W), F32)],
        compiler_params=_cp("arbitrary"),
    )(proj, proj, conv_w, conv_b, wa, ba, wx, bx, lam)


def _lru_bwd(name, proj, dy, xc, r, ig, hs, conv_w, wa, wx, lam, dpbuf):
    T = proj.shape[0]
    H, hd, _ = wa.shape
    W = H * hd
    K = conv_w.shape[0]
    tb = _tile(T, 256)
    nb = T // tb
    t8 = tb // 8

    def body(xin_ref, xprev_ref, gate_ref, dy_ref, xc_ref, r_ref, i_ref, hs_ref, hsprev_ref,
             cw_ref, wa_ref, wx_ref, lam_ref, dpbuf_ref,
             dp_ref, db_ref, dcw_ref, dcb_ref, dwa_ref, dba_ref, dwx_ref, dbx_ref, dlam_ref,
             cdh_ref, ca_ref, cdxc_ref, dxc_ref):
        del dpbuf_ref
        step = pl.program_id(0)
        blk = nb - 1 - step

        @pl.when(step == 0)
        def _():
            for ref in (db_ref, dcw_ref, dcb_ref, dwa_ref, dba_ref, dwx_ref, dbx_ref, dlam_ref,
                        cdh_ref, ca_ref, cdxc_ref):
                ref[...] = jnp.zeros_like(ref)

        row = _rows(tb)
        first = blk == 0
        gate = gate_ref[...]
        dy_a = dy_ref[...]
        hsv = hs_ref[...]
        _put(dp_ref, db_ref, W, 2 * W, dy_a * hsv * _dgelu(gate))
        d_hs = dy_a * _gelu(gate)
        lam = lam_ref[...]
        rv = r_ref[...]
        sp, la, a, m2 = _lru_gate_terms(rv, lam)
        an = jnp.where(row == tb - 1, ca_ref[...], pltpu.roll(a, tb - 1, 0))
        u = d_hs
        d = 1
        while d < tb:
            keep = row < tb - d
            u = an * jnp.where(keep, pltpu.roll(u, tb - d, 0), 0.0) + u
            an = an * jnp.where(keep, pltpu.roll(an, tb - d, 0), 1.0)
            d *= 2
        dh = u + an * cdh_ref[...]
        cdh_ref[...] = dh[0:1]
        ca_ref[...] = a[0:1]
        hlast = jnp.where(first, 0.0, hsprev_ref[7:8, :])
        hprev = jnp.where(row == 0, hlast, pltpu.roll(hsv, 1, 0))
        da = dh * hprev
        xcv = xc_ref[...]
        iv = i_ref[...]
        t0 = (row == 0) & first
        mult = jnp.sqrt(jnp.maximum(m2, 0.0))
        mult_eff = jnp.where(t0, 1.0, mult)
        d_mult = dh * iv * xcv
        d_i = dh * mult_eff * xcv
        dxc = dh * mult_eff * iv
        e2 = 1.0 - m2
        d_la = da * a + jnp.where(t0 | (m2 <= 0.0), 0.0, -d_mult * e2 / jnp.where(m2 > 0.0, mult, 1.0))
        d_r = d_la * (-LRU_C * sp)
        dlam_ref[...] += jnp.sum(d_la * (-LRU_C * rv), axis=0, keepdims=True) * (-_sigmoid(-lam))
        d_zr = d_r * rv * (1.0 - rv)
        d_zi = d_i * iv * (1.0 - iv)
        dba_ref[...] += jnp.sum(d_zr, axis=0, keepdims=True)
        dbx_ref[...] += jnp.sum(d_zi, axis=0, keepdims=True)
        for h in range(H):
            cs = slice(h * hd, (h + 1) * hd)
            xh = xcv[:, cs]
            zr, zi = d_zr[:, cs], d_zi[:, cs]
            dwa_ref[h] += _dot_tn(xh, zr)
            dwx_ref[h] += _dot_tn(xh, zi)
            dxc_ref[:, cs] = dxc[:, cs] + _dot_nt(zr, wa_ref[h]) + _dot_nt(zi, wx_ref[h])
        dxc = dxc_ref[...]
        dcb_ref[...] += jnp.sum(dxc, axis=0, keepdims=True)
        xin = xin_ref[...]
        prev8 = jnp.where(first, 0.0, xprev_ref[...])
        next8 = cdxc_ref[...]
        dxin = cw_ref[K - 1:K, :] * dxc
        dcw_ref[K - 1:K, :] += jnp.sum(dxc * xin, axis=0, keepdims=True)
        for sh in range(1, K):
            dxin = dxin + cw_ref[K - 1 - sh:K - sh, :] * _shift_up(dxc, next8, sh)
            dcw_ref[K - 1 - sh:K - sh, :] += jnp.sum(dxc * _shift_down(xin, prev8, sh), axis=0, keepdims=True)
        cdxc_ref[...] = dxc[0:8]
        _put(dp_ref, db_ref, 0, W, dxin)

    full = lambda shape: pl.BlockSpec(shape, lambda i: tuple(0 for _ in shape))
    cur = lambda col: pl.BlockSpec((tb, W), lambda i: (nb - 1 - i, col))
    prev = pl.BlockSpec((8, W), lambda i: (jnp.maximum((nb - 1 - i) * t8 - 1, 0), 0))
    return pl.pallas_call(
        body, name=name, grid=(nb,),
        in_specs=[cur(0), prev, cur(1), cur(0), cur(0), cur(0), cur(0), cur(0), prev,
                  full((K, W)), full((H, hd, hd)), full((H, hd, hd)), full((1, W)), ANY],
        out_specs=[pl.BlockSpec((tb, 2 * W), lambda i: (nb - 1 - i, 0)), full((1, 2 * W)), full((K, W)),
                   full((1, W)),
                   full((H, hd, hd)), full((1, W)), full((H, hd, hd)), full((1, W)), full((1, W))],
        out_shape=[jax.ShapeDtypeStruct(dpbuf.shape, dpbuf.dtype), jax.ShapeDtypeStruct((1, 2 * W), F32),
                   jax.ShapeDtypeStruct((K, W), F32),
                   jax.ShapeDtypeStruct((1, W), F32), jax.ShapeDtypeStruct((H, hd, hd), F32),
                   jax.ShapeDtypeStruct((1, W), F32), jax.ShapeDtypeStruct((H, hd, hd), F32),
                   jax.ShapeDtypeStruct((1, W), F32), jax.ShapeDtypeStruct((1, W), F32)],
        scratch_shapes=[pltpu.VMEM((1, W), F32), pltpu.VMEM((1, W), F32), pltpu.VMEM((8, W), F32),
                        pltpu.VMEM((tb, W), F32)],
        input_output_aliases={13: 0}, compiler_params=_cp("arbitrary"),
    )(proj, proj, proj, dy, xc, r, ig, hs, hs, conv_w, wa, wx, lam, dpbuf)


def _chunk_cumsum(g, c):
    n = g.shape[0]
    rc = _rows(n) & (c - 1)
    d = 1
    while d < c:
        g = g + jnp.where(rc >= d, pltpu.roll(g, d, 0), 0.0)
        d *= 2
    return g


def _chunk_rcumsum(g, c):
    n = g.shape[0]
    rc = _rows(n) & (c - 1)
    d = 1
    while d < c:
        g = g + jnp.where(rc < c - d, pltpu.roll(g, n - d, 0), 0.0)
        d *= 2
    return g


def _hgrn_pointwise(qr, fr, lb):
    qf = _silu(qr)
    sig = _sigmoid(fr)
    fg = lb + (1.0 - lb) * sig
    gl = jnp.log(jnp.maximum(fg, F_FLOOR))
    kk = (1.0 - lb) * (1.0 - sig)
    return qf, sig, fg, gl, kk


def _hgrn_fwd(name, proj, lb, norm_g, ybuf):
    T = proj.shape[0]
    W = lb.shape[1]
    H = HGRN_HEADS
    dk = W // H
    c = HGRN_SUB
    R = _tile(T, 128)
    nck = R // c

    def body(q_ref, f_ref, v_ref, g_ref, lb_ref, ng_ref, ybuf_ref, yb_ref, o_ref, sall_ref,
             st_ref, qe_ref, ke_ref, acc_ref):
        del ybuf_ref

        @pl.when(pl.program_id(0) == 0)
        def _():
            st_ref[...] = jnp.zeros_like(st_ref)

        qf, _, _, gl, kk = _hgrn_pointwise(q_ref[...], f_ref[...], lb_ref[...])
        b = _chunk_cumsum(gl * LOG2E, c)
        rc = _rows(R) & (c - 1)
        for h in range(H):
            cs = slice(h * dk, (h + 1) * dk)
            qh, kh, bh, vh = qf[:, cs], kk[:, cs], b[:, cs], v_ref[:, cs]
            acc = jnp.sum(qh * kh, axis=1, keepdims=True) * vh
            for d in range(1, c):
                e = jnp.where(rc >= d, jnp.exp2(bh - pltpu.roll(bh, d, 0)), 0.0)
                s = jnp.sum(qh * pltpu.roll(kh, d, 0) * e, axis=1, keepdims=True)
                acc = acc + s * pltpu.roll(vh, d, 0)
            acc_ref[:, cs] = acc
        qe_ref[...] = qf * jnp.exp2(b)
        for ci in range(nck):
            rs = slice(ci * c, (ci + 1) * c)
            bl = b[ci * c + c - 1:ci * c + c, :]
            ke_ref[rs, :] = kk[rs, :] * jnp.exp2(bl - b[rs, :])
            ebl = jnp.exp2(bl)
            for h in range(H):
                cs = slice(h * dk, (h + 1) * dk)
                st = st_ref[h]
                sall_ref[ci, h] = st
                o_ref[rs, cs] = acc_ref[rs, cs] + _dot_nt(qe_ref[rs, cs], st)
                st_ref[h] = st * ebl[:, cs] + _dot_tn(v_ref[rs, cs], ke_ref[rs, cs])
        ng = ng_ref[...]
        gg = g_ref[...]
        for h in range(H):
            cs = slice(h * dk, (h + 1) * dk)
            oh = o_ref[:, cs]
            rr = lax.rsqrt(jnp.mean(oh * oh, axis=1, keepdims=True) + EPS)
            yb_ref[:, cs] = ((oh * rr) * ng[:, cs] * _silu(gg[:, cs])).astype(yb_ref.dtype)

    full = lambda shape: pl.BlockSpec(shape, lambda i: tuple(0 for _ in shape))
    col = lambda k: pl.BlockSpec((R, W), lambda i: (i, k))
    blk = pl.BlockSpec((R, W), lambda i: (i, 0))
    return pl.pallas_call(
        body, name=name, grid=(T // R,),
        in_specs=[col(2), col(3), col(4), col(5), full((1, W)), full((1, W)), ANY],
        out_specs=[col(1), blk, pl.BlockSpec((nck, H, dk, dk), lambda i: (i, 0, 0, 0))],
        out_shape=[jax.ShapeDtypeStruct((T, 2 * W), MXU_DTYPE), jax.ShapeDtypeStruct((T, W), F32),
                   jax.ShapeDtypeStruct((T // c, H, dk, dk), F32)],
        scratch_shapes=[pltpu.VMEM((H, dk, dk), F32), pltpu.VMEM((R, W), F32), pltpu.VMEM((R, W), F32),
                        pltpu.VMEM((R, W), F32)],
        input_output_aliases={6: 0}, compiler_params=_cp("arbitrary"),
    )(proj, proj, proj, proj, lb, norm_g, ybuf)


def _hgrn_bwd(name, proj, dy, o, sall, lb, norm_g):
    T = proj.shape[0]
    W = lb.shape[1]
    H = HGRN_HEADS
    dk = W // H
    c = HGRN_SUB
    R = _tile(T, 128)
    nck = R // c
    nb = T // R

    def body(q_ref, f_ref, v_ref, g_ref, dy_ref, o_ref, sall_ref, lb_ref, ng_ref,
             dp_ref, db_ref, dlb_ref, dng_ref,
             dst_ref, do_ref, dq_ref, dk_ref, dv_ref, ex_ref, qe_ref, ke_ref):
        @pl.when(pl.program_id(0) == 0)
        def _():
            dst_ref[...] = jnp.zeros_like(dst_ref)
            db_ref[...] = jnp.zeros_like(db_ref)
            dlb_ref[...] = jnp.zeros_like(dlb_ref)
            dng_ref[...] = jnp.zeros_like(dng_ref)

        lbv = lb_ref[...]
        qr = q_ref[...]
        qf, sig, fg, gl, kk = _hgrn_pointwise(qr, f_ref[...], lbv)
        b = _chunk_cumsum(gl * LOG2E, c)
        rc = _rows(R) & (c - 1)
        ng = ng_ref[...]
        gg = g_ref[...]
        dyv = dy_ref[...]
        sg = _silu(gg)
        for h in range(H):
            cs = slice(h * dk, (h + 1) * dk)
            oh = o_ref[:, cs]
            rr = lax.rsqrt(jnp.mean(oh * oh, axis=1, keepdims=True) + EPS)
            ohat = oh * rr
            dyh = dyv[:, cs]
            _put(dp_ref, db_ref, 5 * W + h * dk, 5 * W + (h + 1) * dk,
                 dyh * ohat * ng[:, cs] * _dsilu(gg[:, cs]))
            t = dyh * sg[:, cs]
            dng_ref[:, cs] += jnp.sum(t * ohat, axis=0, keepdims=True)
            dohat = t * ng[:, cs]
            do_ref[:, cs] = rr * (dohat - ohat * jnp.mean(dohat * ohat, axis=1, keepdims=True))
        for h in range(H):
            cs = slice(h * dk, (h + 1) * dk)
            qh, kh, bh, vh, doh = qf[:, cs], kk[:, cs], b[:, cs], v_ref[:, cs], do_ref[:, cs]
            da0 = jnp.sum(doh * vh, axis=1, keepdims=True)
            a0 = jnp.sum(qh * kh, axis=1, keepdims=True)
            dq = da0 * kh
            dkk = da0 * qh
            dv = a0 * doh
            for d in range(1, c):
                e = jnp.where(rc >= d, jnp.exp2(bh - pltpu.roll(bh, d, 0)), 0.0)
                kr = pltpu.roll(kh, d, 0)
                da = jnp.sum(doh * pltpu.roll(vh, d, 0), axis=1, keepdims=True)
                aa = jnp.sum(qh * kr * e, axis=1, keepdims=True)
                dq = dq + da * kr * e
                dkk = dkk + pltpu.roll(da * qh * e, R - d, 0)
                dv = dv + pltpu.roll(aa * doh, R - d, 0)
            dq_ref[:, cs] = dq
            dk_ref[:, cs] = dkk
            dv_ref[:, cs] = dv
        eb = jnp.exp2(b)
        qe_ref[...] = qf * eb
        ex_ref[...] = jnp.zeros_like(ex_ref)
        for ci in reversed(range(nck)):
            rs = slice(ci * c, (ci + 1) * c)
            bl = b[ci * c + c - 1:ci * c + c, :]
            ebl_rows = jnp.exp2(bl - b[rs, :])
            ke_ref[rs, :] = kk[rs, :] * ebl_rows
            ebl = jnp.exp2(bl)
            for h in range(H):
                cs = slice(h * dk, (h + 1) * dk)
                st0 = sall_ref[ci, h]
                dst1 = dst_ref[h]
                doc = do_ref[rs, cs]
                vc = v_ref[rs, cs]
                dq_ref[rs, cs] += _dot(doc, st0) * eb[rs, cs]
                dv_ref[rs, cs] += _dot_nt(ke_ref[rs, cs], dst1)
                dki = _dot(vc, dst1) * ebl_rows[:, cs]
                dk_ref[rs, cs] += dki
                ex_ref[ci * c + c - 1:ci * c + c, cs] = (
                    jnp.sum(dki * kk[rs, cs], axis=0, keepdims=True)
                    + ebl[:, cs] * jnp.sum(st0 * dst1, axis=0, keepdims=True))
                dst_ref[h] = dst1 * ebl[:, cs] + _dot_tn(doc, qe_ref[rs, cs])
        dq = dq_ref[...]
        dkk = dk_ref[...]
        db = qf * dq - kk * dkk + ex_ref[...]
        dgl = _chunk_rcumsum(db, c)
        dfg = jnp.where(fg > F_FLOOR, dgl / jnp.maximum(fg, F_FLOOR), 0.0)
        dsig = (dfg - dkk) * (1.0 - lbv)
        dlb_ref[...] += jnp.sum((dfg - dkk) * (1.0 - sig), axis=0, keepdims=True)
        dp_ref[:, 0:2 * W] = jnp.zeros((R, 2 * W), dp_ref.dtype)
        _put(dp_ref, db_ref, 2 * W, 3 * W, dq * _dsilu(qr))
        _put(dp_ref, db_ref, 3 * W, 4 * W, dsig * sig * (1.0 - sig))
        _put(dp_ref, db_ref, 4 * W, 5 * W, dv_ref[...])

    full = lambda shape: pl.BlockSpec(shape, lambda i: tuple(0 for _ in shape))
    col = lambda k: pl.BlockSpec((R, W), lambda i: (nb - 1 - i, k))
    scr = pltpu.VMEM((R, W), F32)
    return pl.pallas_call(
        body, name=name, grid=(nb,),
        in_specs=[col(2), col(3), col(4), col(5), col(1), col(0),
                  pl.BlockSpec((nck, H, dk, dk), lambda i: (nb - 1 - i, 0, 0, 0)), full((1, W)), full((1, W))],
        out_specs=[pl.BlockSpec((R, 6 * W), lambda i: (nb - 1 - i, 0)), full((1, 6 * W)), full((1, W)),
                   full((1, W))],
        out_shape=[jax.ShapeDtypeStruct((T, 6 * W), MXU_DTYPE), jax.ShapeDtypeStruct((1, 6 * W), F32),
                   jax.ShapeDtypeStruct((1, W), F32), jax.ShapeDtypeStruct((1, W), F32)],
        scratch_shapes=[pltpu.VMEM((H, dk, dk), F32), scr, scr, scr, scr, scr, scr, scr],
        compiler_params=_cp("arbitrary"),
    )(proj, proj, proj, proj, dy, o, sall, lb, norm_g)


ODD_HALO = 32


def _row_shifts(ext, up):
    n = ext.shape[0]
    return [ext] + [pltpu.roll(ext, n - b if up else b, 0) for b in range(1, 8)]


def _past(copies, sh, tb):
    a, b = divmod(sh, 8)
    return copies[b][ODD_HALO - 8 * a:ODD_HALO - 8 * a + tb]


def _future(copies, sh, tb):
    a, b = divmod(sh, 8)
    return copies[b][8 * a:8 * a + tb]


def _odd_fwd(name, proj, sc_w, cf_w, cf_b, ln_g, ln_b):
    T = proj.shape[0]
    W = sc_w.shape[1]
    K3, K31 = sc_w.shape[0], cf_w.shape[0]
    tb = _tile(T, 256)
    hb = tb // ODD_HALO
    n = tb + ODD_HALO

    def body(cur_ref, prev_ref, w3_ref, w31_ref, cb_ref, lg_ref, lbeta_ref, y_ref, d_ref):
        keep = (pl.program_id(0) > 0).astype(F32)
        sb = cur_ref[:, 0:W]
        p = cur_ref[:, W:2 * W] * cur_ref[:, 2 * W:3 * W]
        glu = cur_ref[:, 3 * W:4 * W] * _sigmoid(cur_ref[:, 4 * W:5 * W])
        p_prev = prev_ref[:, W:2 * W] * prev_ref[:, 2 * W:3 * W] * keep
        glu_prev = prev_ref[:, 3 * W:4 * W] * _sigmoid(prev_ref[:, 4 * W:5 * W]) * keep
        ext = jnp.concatenate([p_prev, p], axis=0)
        cp = w3_ref[K3 - 1:K3, :] * p
        for sh in range(1, K3):
            cp = cp + w3_ref[K3 - 1 - sh:K3 - sh, :] * pltpu.roll(ext, sh, 0)[ODD_HALO:n]
        y_ref[:, 0:W] = (sb * cp).astype(y_ref.dtype)
        glu_past = _row_shifts(jnp.concatenate([glu_prev, glu], axis=0), up=False)
        d = cb_ref[...] + w31_ref[K31 - 1:K31, :] * glu
        for sh in range(1, K31):
            d = d + w31_ref[K31 - 1 - sh:K31 - sh, :] * _past(glu_past, sh, tb)
        d_ref[...] = d
        mu = jnp.mean(d, axis=1, keepdims=True)
        xc = d - mu
        rstd = lax.rsqrt(jnp.mean(xc * xc, axis=1, keepdims=True) + EPS)
        ln = (xc * rstd) * lg_ref[...] + lbeta_ref[...]
        y_ref[:, W:2 * W] = _silu(ln).astype(y_ref.dtype)

    full = lambda shape: pl.BlockSpec(shape, lambda i: tuple(0 for _ in shape))
    return pl.pallas_call(
        body, name=name, grid=(T // tb,),
        in_specs=[pl.BlockSpec((tb, 5 * W), lambda i: (i, 0)),
                  pl.BlockSpec((ODD_HALO, 5 * W), lambda i: (jnp.maximum(i * hb - 1, 0), 0)),
                  full((K3, W)), full((K31, W)), full((1, W)), full((1, W)), full((1, W))],
        out_specs=[pl.BlockSpec((tb, 2 * W), lambda i: (i, 0)), pl.BlockSpec((tb, W), lambda i: (i, 0))],
        out_shape=[jax.ShapeDtypeStruct((T, 2 * W), MXU_DTYPE), jax.ShapeDtypeStruct((T, W), F32)],
        compiler_params=_cp("parallel"),
    )(proj, proj, sc_w, cf_w, cf_b, ln_g, ln_b)


def _odd_bwd(name, proj, dy, dsave, sc_w, cf_w, ln_g, ln_b):
    T = proj.shape[0]
    W = sc_w.shape[1]
    K3, K31 = sc_w.shape[0], cf_w.shape[0]
    tb = _tile(T, 128)
    nb = T // tb
    hb = tb // ODD_HALO
    nh = T // ODD_HALO
    n = tb + ODD_HALO

    def body(cur_ref, prev_ref, next_ref, dy_ref, dyn_ref, d_ref, dn_ref,
             w3_ref, w31_ref, lg_ref, lbeta_ref,
             dp_ref, db_ref, dw3_ref, dw31_ref, dcb_ref, dlg_ref, dlb_ref):
        i = pl.program_id(0)

        @pl.when(i == 0)
        def _():
            for ref in (db_ref, dw3_ref, dw31_ref, dcb_ref, dlg_ref, dlb_ref):
                ref[...] = jnp.zeros_like(ref)

        keep_prev = (i > 0).astype(F32)
        keep_next = (i < nb - 1).astype(F32)
        sb = cur_ref[:, 0:W]
        scv = cur_ref[:, W:2 * W]
        svv = cur_ref[:, 2 * W:3 * W]
        cu = cur_ref[:, 3 * W:4 * W]
        sg = _sigmoid(cur_ref[:, 4 * W:5 * W])
        p = scv * svv
        glu = cu * sg
        p_prev = prev_ref[:, W:2 * W] * prev_ref[:, 2 * W:3 * W] * keep_prev
        glu_prev = prev_ref[:, 3 * W:4 * W] * _sigmoid(prev_ref[:, 4 * W:5 * W]) * keep_prev
        dext = jnp.concatenate([d_ref[...], dn_ref[...]], axis=0)
        dyd = jnp.concatenate([dy_ref[:, W:2 * W], dyn_ref[:, W:2 * W] * keep_next], axis=0)
        mu = jnp.mean(dext, axis=1, keepdims=True)
        xc = dext - mu
        rstd = lax.rsqrt(jnp.mean(xc * xc, axis=1, keepdims=True) + EPS)
        xh = xc * rstd
        lg = lg_ref[...]
        dln = dyd * _dsilu(xh * lg + lbeta_ref[...])
        dxh = dln * lg
        dd = rstd * (dxh - jnp.mean(dxh, axis=1, keepdims=True)
                     - xh * jnp.mean(dxh * xh, axis=1, keepdims=True))
        dlg_ref[...] += jnp.sum((dln * xh)[0:tb], axis=0, keepdims=True)
        dlb_ref[...] += jnp.sum(dln[0:tb], axis=0, keepdims=True)
        ddc = dd[0:tb]
        dcb_ref[...] += jnp.sum(ddc, axis=0, keepdims=True)
        dglu = w31_ref[K31 - 1:K31, :] * ddc
        glu_past = _row_shifts(jnp.concatenate([glu_prev, glu], axis=0), up=False)
        dd_future = _row_shifts(dd, up=True)
        dw31_ref[K31 - 1:K31, :] += jnp.sum(ddc * glu, axis=0, keepdims=True)
        for sh in range(1, K31):
            dglu = dglu + w31_ref[K31 - 1 - sh:K31 - sh, :] * _future(dd_future, sh, tb)
            dw31_ref[K31 - 1 - sh:K31 - sh, :] += jnp.sum(ddc * _past(glu_past, sh, tb), axis=0, keepdims=True)
        _put(dp_ref, db_ref, 3 * W, 4 * W, dglu * sg)
        _put(dp_ref, db_ref, 4 * W, 5 * W, dglu * cu * sg * (1.0 - sg))
        dyc = dy_ref[:, 0:W]
        dcp = jnp.concatenate([dyc * sb, dyn_ref[:, 0:W] * next_ref[:, 0:W] * keep_next], axis=0)
        dcpc = dcp[0:tb]
        ext = jnp.concatenate([p_prev, p], axis=0)
        cp = w3_ref[K3 - 1:K3, :] * p
        dpp = w3_ref[K3 - 1:K3, :] * dcpc
        dw3_ref[K3 - 1:K3, :] += jnp.sum(dcpc * p, axis=0, keepdims=True)
        up = dcp
        for sh in range(1, K3):
            up = pltpu.roll(up, n - 1, 0)
            ext = pltpu.roll(ext, 1, 0)
            shifted = ext[ODD_HALO:n]
            cp = cp + w3_ref[K3 - 1 - sh:K3 - sh, :] * shifted
            dpp = dpp + w3_ref[K3 - 1 - sh:K3 - sh, :] * up[0:tb]
            dw3_ref[K3 - 1 - sh:K3 - sh, :] += jnp.sum(dcpc * shifted, axis=0, keepdims=True)
        _put(dp_ref, db_ref, 0, W, dyc * cp)
        _put(dp_ref, db_ref, W, 2 * W, dpp * svv)
        _put(dp_ref, db_ref, 2 * W, 3 * W, dpp * scv)

    full = lambda shape: pl.BlockSpec(shape, lambda i: tuple(0 for _ in shape))
    prev_map = lambda i: (jnp.maximum(i * hb - 1, 0), 0)
    next_map = lambda i: (jnp.minimum((i + 1) * hb, nh - 1), 0)
    return pl.pallas_call(
        body, name=name, grid=(nb,),
        in_specs=[pl.BlockSpec((tb, 5 * W), lambda i: (i, 0)),
                  pl.BlockSpec((ODD_HALO, 5 * W), prev_map), pl.BlockSpec((ODD_HALO, 5 * W), next_map),
                  pl.BlockSpec((tb, 2 * W), lambda i: (i, 0)), pl.BlockSpec((ODD_HALO, 2 * W), next_map),
                  pl.BlockSpec((tb, W), lambda i: (i, 0)), pl.BlockSpec((ODD_HALO, W), next_map),
                  full((K3, W)), full((K31, W)), full((1, W)), full((1, W))],
        out_specs=[pl.BlockSpec((tb, 5 * W), lambda i: (i, 0)), full((1, 5 * W)), full((K3, W)), full((K31, W)),
                   full((1, W)), full((1, W)), full((1, W))],
        out_shape=[jax.ShapeDtypeStruct((T, 5 * W), MXU_DTYPE), jax.ShapeDtypeStruct((1, 5 * W), F32),
                   jax.ShapeDtypeStruct((K3, W), F32),
                   jax.ShapeDtypeStruct((K31, W), F32)] + [jax.ShapeDtypeStruct((1, W), F32)] * 3,
        compiler_params=_cp("arbitrary"),
    )(proj, proj, proj, dy, dy, dsave, dsave, sc_w, cf_w, ln_g, ln_b)


PACK_WIDTH = 1024


def _lower_bounds(logits):
    sm = jax.nn.softmax(logits.astype(F32), axis=0)
    return jnp.cumsum(sm, axis=0) - sm[0]


def _pack_rows(arrays):
    flat = jnp.concatenate([a.reshape(-1) for a in arrays])
    pad = (-flat.shape[0]) % (8 * PACK_WIDTH)
    return jnp.pad(flat, (0, pad)).reshape(-1, PACK_WIDTH)


def _unpack_rows(packed, shapes):
    flat = packed.reshape(-1)
    out, off = [], 0
    for s in shapes:
        sz = math.prod(s)
        out.append(flat[off:off + sz].reshape(s))
        off += sz
    return out


def _shards_last(a):
    n = a.shape[-1] // N_DEV
    return jnp.moveaxis(a.reshape(a.shape[:-1] + (N_DEV, n)), -2, 0)


def _unshard_last(a):
    a = jnp.moveaxis(a, 0, -2)
    return a.reshape(a.shape[:-2] + (a.shape[-2] * a.shape[-1],))


BIG = ("ev_w_in", "ev_w_out", "od_w_in", "od_w_out", "ffn_w_gate", "ffn_w_up", "ffn_w_down")
SMALL_SHARDED = ("lru_conv_w", "od_b_in", "sc_conv_w", "cf_conv_w", "cf_conv_b", "cf_ln_g", "cf_ln_b")
SMALL_REPL = ("ln_mix_g", "ln_ffn_g", "ln_final_g", "ev_b_in", "lru_conv_b", "lru_wa", "lru_ba", "lru_wx",
              "lru_bx", "lru_lambda", "hgrn_lb_logits", "hgrn_norm_g")
WEIGHTS = ("ln_mix_g", "ln_ffn_g", "ln_final_g", "ev_w_in", "ev_b_in", "lru_conv_w", "lru_conv_b", "lru_wa",
           "lru_ba", "lru_wx", "lru_bx", "lru_lambda", "hgrn_lb_logits", "hgrn_norm_g", "ev_w_out", "od_w_in",
           "od_b_in", "sc_conv_w", "cf_conv_w", "cf_conv_b", "cf_ln_g", "cf_ln_b", "od_w_out", "ffn_w_gate",
           "ffn_w_up", "ffn_w_down")


def _layer_weights(l):
    mix = ("ev_w_in", "ev_w_out") if l % 2 == 0 else ("od_w_in", "od_w_out")
    return [(mix[0], l // 2), (mix[1], l // 2), ("ffn_w_gate", l), ("ffn_w_up", l), ("ffn_w_down", l)]


class _MeshExchange:
    def __init__(self, w_bf, w, m, v, depth):
        self.w_bf, self.w, self.m, self.v, self.depth = w_bf, w, m, v, depth
        self.ready, self.flight, self.rs, self.adam = {}, {}, {}, {}

    @staticmethod
    def _names(l, grp):
        names = _layer_weights(l)
        return names[:2] if grp == "mix" else names[2:]

    def _own_start(self, l, grp, deps):
        names = self._names(l, grp)
        srcs = [self.w_bf[k] for k, _ in names]
        layers = [j for _, j in names]
        plan = _plan_gather_own(layers)
        sems, lands, tok = _split_start(f"ag_own_start_{grp}{l}", plan, srcs, _place_own(srcs, layers), deps)
        self.flight[l, grp] = (plan, sems, srcs, lands)
        return tok

    def _turn(self, l, grp, deps):
        plan, sems, srcs, lands = self.flight[l, grp]
        lands = _split_wait(f"ag_own_wait_{grp}{l}", plan, sems, srcs, lands, deps)
        plan = _plan_gather_pass(len(lands))
        sems, passed, tok = _split_start(f"ag_pass_start_{grp}{l}", plan, [], lands)
        self.flight[l, grp] = (plan, sems, [], passed)
        toks = [tok]
        nl, ng = (l, "ffn") if grp == "mix" else (l + 1, "mix")
        if nl < self.depth:
            toks.append(self._own_start(nl, ng, (tok,)))
        return tuple(toks)

    def _pass_wait(self, l, grp, deps):
        plan, sems, srcs, lands = self.flight.pop((l, grp))
        lands = _split_wait(f"ag_pass_wait_{grp}{l}", plan, sems, srcs, lands, deps)
        self.ready[l, grp] = dict(zip([k for k, _ in self._names(l, grp)], lands))

    def layer_begin(self, l):
        toks = ()
        if l == 0:
            self._own_start(0, "mix", ())
            toks = self._turn(0, "mix", ())
            self._pass_wait(0, "mix", ())
        return self.ready.pop((l, "mix")), toks

    def tick(self, l, t, after):
        if t == 2:
            return self._turn(l, "ffn", (after,))
        if t == 3:
            self._pass_wait(l, "ffn", (after,))
        if t == 4 and l + 1 < self.depth:
            return self._turn(l + 1, "mix", (after,))
        if t == 5 and l + 1 < self.depth:
            self._pass_wait(l + 1, "mix", (after,))
        return ()

    def ffn_weights(self, l):
        return self.ready.pop((l, "ffn"))

    def grads(self, tag, named):
        srcs = [g for _, _, g in named]
        lands = [lax.empty((4,) + g.shape[1:], g.dtype) for g in srcs]
        plan = _plan_scatter_pair(len(srcs))
        sems, lands, tok = _split_start(f"rs_pair_start_{tag}", plan, srcs, lands)
        self.rs[tag] = (named, plan, sems, srcs, lands)
        return (tok,)

    def grads_mid(self, tag, after):
        named, plan, sems, srcs, lands = self.rs[tag]
        got = _split_wait(f"rs_pair_wait_{tag}", plan, sems, srcs, lands, (after,))
        both = [_pair_add(f"pair_add_{tag}_{i}", a, b) for i, (a, b) in enumerate(zip(srcs, got))]
        self.rs[tag] = (named, [p for p, _ in both], [f for _, f in both])

    def grads_send(self, tag):
        named, parts, fins = self.rs[tag]
        plan = _plan_scatter_chips(len(parts))
        sems, fins, tok = _split_start(f"rs_chip_start_{tag}", plan, parts, fins)
        self.rs[tag] = (named, plan, sems, parts, fins)
        return (tok,)

    def grads_end(self, tag, after):
        named, plan, sems, parts, fins = self.rs.pop(tag)
        fins = _split_wait(f"rs_chip_wait_{tag}", plan, sems, parts, fins, (after,))
        for (k, j, _), fin in zip(named, fins):
            self.adam[k] = _adamw(f"adamw_{k}_{j}", self.w[k], self.m[k], self.v[k], fin, j, self.adam.get(k))

    def results(self, k):
        return self.adam[k]


def _local_step(x, tgt, p, ex):
    T, D = x.shape
    depth = p["ln_mix_g"].shape[0]
    lbs = _lower_bounds(p["hgrn_lb_logits"])
    row = lambda a: a.reshape(1, -1)
    saved = []
    h = None
    for l in range(depth):
        j = l // 2
        wl, tok = ex.layer_begin(l)
        s = {"x": x}
        if h is None:
            h = _rmsnorm_fwd(f"norm_mix{l}", x, row(p["ln_mix_g"][l]))
        s["h"] = h
        if l % 2 == 0:
            proj = _proj_in(f"ev_in{l}", h, wl["ev_w_in"], row(p["ev_b_in"][j]), tok)
            wa = p["lru_wa"][j].astype(MXU_DTYPE)
            wx = p["lru_wx"][j].astype(MXU_DTYPE)
            ya, xc, r, ig, hs = _lru_fwd(f"lru_fwd{l}", proj, p["lru_conv_w"][j], row(p["lru_conv_b"][j]),
                                         wa, row(p["lru_ba"][j]), wx, row(p["lru_bx"][j]),
                                         row(p["lru_lambda"][j]))
            y, o, sall = _hgrn_fwd(f"hgrn_fwd{l}", proj, row(lbs[j]), row(p["hgrn_norm_g"][j]), ya)
            s.update(proj=proj, xc=xc, r=r, ig=ig, hs=hs, o=o, sall=sall, wa=wa, wx=wx)
            w_out = wl["ev_w_out"]
        else:
            proj = _proj_in(f"od_in{l}", h, wl["od_w_in"], row(p["od_b_in"][j]), tok)
            y, dsave = _odd_fwd(f"odd_fwd{l}", proj, p["sc_conv_w"][j], p["cf_conv_w"][j],
                                row(p["cf_conv_b"][j]), row(p["cf_ln_g"][j]), row(p["cf_ln_b"][j]))
            s.update(proj=proj, dsave=dsave)
            w_out = wl["od_w_out"]
        x, h2 = _mix_out(f"mix_out{l}", y, w_out, x, row(p["ln_ffn_g"][l]), ex.tick(l, 2, y))
        s["y"] = y
        s["xmid"] = x
        ex.tick(l, 3, x)
        wl = {**wl, **ex.ffn_weights(l)}
        gate, up, hid = _ffn_in(f"ffn_in{l}", h2, wl["ffn_w_gate"], wl["ffn_w_up"])
        tok = ex.tick(l, 4, hid)
        if l + 1 < depth:
            x, h = _proj_out(f"ffn_out{l}", hid, wl["ffn_w_down"], x, tok, row(p["ln_mix_g"][l + 1]))
        else:
            x = _proj_out(f"ffn_out{l}", hid, wl["ffn_w_down"], x, tok)
        ex.tick(l, 5, x)
        s.update(h2=h2, gate=gate, up=up, hid=hid, w=wl)
        saved.append(s)

    loss, dx, dxb, dg_final = _loss_head("loss_head", x, row(p["ln_final_g"]), tgt)

    gs = {k: [None] * p[k].shape[0] for k in SMALL_REPL + SMALL_SHARDED if k not in ("ln_final_g", "hgrn_lb_logits")}
    d_lb = [None] * (depth // 2 + depth % 2)
    tok = ()
    pending = None
    for l in reversed(range(depth)):
        j = l // 2
        s = saved[l]
        wl = s["w"]
        ffn_shape = wl["ffn_w_gate"].shape[1:]
        dwd = _wgrad(f"ffn_dwd{l}", s["hid"], dxb, ffn_shape[1], ffn_shape[0], tok)
        dgate, dup = _ffn_bwd_hidden(f"ffn_bwd_hid{l}", dxb, wl["ffn_w_down"], s["gate"], s["up"])
        dwg = _wgrad(f"ffn_dwg{l}", s["h2"], dgate, *ffn_shape)
        dwu = _wgrad(f"ffn_dwu{l}", s["h2"], dup, *ffn_shape)
        tok = ex.grads(f"ffn{l}", [("ffn_w_down", l, dwd), ("ffn_w_gate", l, dwg), ("ffn_w_up", l, dwu)])
        dh2 = _bwd_in(f"ffn_dh{l}", [dgate, dup], [wl["ffn_w_gate"], wl["ffn_w_up"]], tok)
        dx, dxb, dg = _rmsnorm_bwd(f"norm_ffn_bwd{l}", s["xmid"], row(p["ln_ffn_g"][l]), dh2, dx)
        gs["ln_ffn_g"][l] = dg[0]
        ex.grads_mid(f"ffn{l}", dxb)
        if pending is not None:
            ex.grads_end(pending, dxb)
        tok = ex.grads_send(f"ffn{l}")
        w_in, w_out = ("ev_w_in", "ev_w_out") if l % 2 == 0 else ("od_w_in", "od_w_out")
        dwo = _wgrad(f"mix_dwo{l}", s["y"], dxb, *wl[w_out].shape[1:], tok)
        if l % 2 == 0:
            dy = _bwd_out(f"ev_dy{l}", dxb, wl["ev_w_out"])
            dph, db_h, dlb, dng = _hgrn_bwd(f"hgrn_bwd{l}", s["proj"], dy, s["o"], s["sall"], row(lbs[j]),
                                            row(p["hgrn_norm_g"][j]))
            dproj, db_a, d_cw, d_cb, d_wa, d_ba, d_wx, d_bx, d_lam = _lru_bwd(
                f"lru_bwd{l}", s["proj"], dy, s["xc"], s["r"], s["ig"], s["hs"], p["lru_conv_w"][j],
                s["wa"], s["wx"], row(p["lru_lambda"][j]), dph)
            gs["lru_conv_w"][j], gs["lru_conv_b"][j] = d_cw, d_cb[0]
            gs["lru_wa"][j], gs["lru_ba"][j] = d_wa, d_ba.reshape(p["lru_ba"].shape[1:])
            gs["lru_wx"][j], gs["lru_bx"][j] = d_wx, d_bx.reshape(p["lru_bx"].shape[1:])
            gs["lru_lambda"][j], gs["hgrn_norm_g"][j] = d_lam[0], dng[0]
            d_lb[j] = dlb[0]
            gs["ev_b_in"][j] = jnp.concatenate([db_a[0], db_h[0, db_a.shape[1]:]])
        else:
            dy = _bwd_out(f"od_dy{l}", dxb, wl["od_w_out"])
            dproj, db_in, d_w3, d_w31, d_cfb, d_lg, d_lbeta = _odd_bwd(
                f"odd_bwd{l}", s["proj"], dy, s["dsave"], p["sc_conv_w"][j], p["cf_conv_w"][j],
                row(p["cf_ln_g"][j]), row(p["cf_ln_b"][j]))
            gs["sc_conv_w"][j], gs["cf_conv_w"][j] = d_w3, d_w31
            gs["cf_conv_b"][j], gs["cf_ln_g"][j], gs["cf_ln_b"][j] = d_cfb[0], d_lg[0], d_lbeta[0]
            gs["od_b_in"][j] = db_in[0]
        dwi = _wgrad(f"mix_dwi{l}", s["h"], dproj, *wl[w_in].shape[1:])
        tok = ex.grads(f"mix{l}", [(w_out, j, dwo), (w_in, j, dwi)])
        dh = _bwd_in(f"mix_dh{l}", [dproj], [wl[w_in]], tok)
        dx, dxb, dg = _rmsnorm_bwd(f"norm_mix_bwd{l}", s["x"], row(p["ln_mix_g"][l]), dh, dx)
        gs["ln_mix_g"][l] = dg[0]
        ex.grads_mid(f"mix{l}", dxb)
        ex.grads_end(f"ffn{l}", dxb)
        tok = ex.grads_send(f"mix{l}")
        pending = f"mix{l}"
    ex.grads_end(pending, dxb)

    small = {k: jnp.stack(v) for k, v in gs.items()}
    small["ln_final_g"] = dg_final[0]
    _, lb_vjp = jax.vjp(_lower_bounds, p["hgrn_lb_logits"])
    small["hgrn_lb_logits"] = lb_vjp(jnp.stack(d_lb))[0]
    return loss, dx, small


def kernel(x, ln_mix_g, ln_ffn_g, ln_final_g, ev_w_in, ev_b_in, lru_conv_w, lru_conv_b, lru_wa, lru_ba, lru_wx, lru_bx, lru_lambda, hgrn_lb_logits, hgrn_norm_g, ev_w_out, od_w_in, od_b_in, sc_conv_w, cf_conv_w, cf_conv_b, cf_ln_g, cf_ln_b, od_w_out, ffn_w_gate, ffn_w_up, ffn_w_down, loss_target, m_ln_mix_g, m_ln_ffn_g, m_ln_final_g, m_ev_w_in, m_ev_b_in, m_lru_conv_w, m_lru_conv_b, m_lru_wa, m_lru_ba, m_lru_wx, m_lru_bx, m_lru_lambda, m_hgrn_lb_logits, m_hgrn_norm_g, m_ev_w_out, m_od_w_in, m_od_b_in, m_sc_conv_w, m_cf_conv_w, m_cf_conv_b, m_cf_ln_g, m_cf_ln_b, m_od_w_out, m_ffn_w_gate, m_ffn_w_up, m_ffn_w_down, v_ln_mix_g, v_ln_ffn_g, v_ln_final_g, v_ev_w_in, v_ev_b_in, v_lru_conv_w, v_lru_conv_b, v_lru_wa, v_lru_ba, v_lru_wx, v_lru_bx, v_lru_lambda, v_hgrn_lb_logits, v_hgrn_norm_g, v_ev_w_out, v_od_w_in, v_od_b_in, v_sc_conv_w, v_cf_conv_w, v_cf_conv_b, v_cf_ln_g, v_cf_ln_b, v_od_w_out, v_ffn_w_gate, v_ffn_w_up, v_ffn_w_down):
    args = locals()
    w = {k: args[k] for k in WEIGHTS}
    m = {k: args["m_" + k] for k in WEIGHTS}
    v = {k: args["v_" + k] for k in WEIGHTS}
    assert x.shape[0] == 1
    T, D = x.shape[1:]

    local_shapes = [w[k].shape for k in SMALL_SHARDED]
    gathered = _all_gather("gather_small_params", [_pack_rows([w[k] for k in SMALL_SHARDED])])[0]
    p = {k: w[k] for k in SMALL_REPL}
    per_dev = [_unpack_rows(gathered[s], local_shapes) for s in range(N_DEV)]
    for i, k in enumerate(SMALL_SHARDED):
        p[k] = _unshard_last(jnp.stack([per_dev[s][i] for s in range(N_DEV)]))

    ex = _MeshExchange({k: w[k].astype(MXU_DTYPE) for k in BIG}, w, m, v, ln_mix_g.shape[0])
    loss_part, dx, small = _local_step(x[0], loss_target[0], p, ex)
    loss = lax.psum(loss_part[0, 0], ("x", "y", "c"))

    small_sh = jnp.stack([_pack_rows([_shards_last(small[k])[s] for k in SMALL_SHARDED]) for s in range(N_DEV)])
    got = _pair_exchange("small_grads_to_sibling", [small_sh])[0]
    final_small = _chip_exchange("small_grads_to_chips", [_pair_add("pair_add_small", small_sh, got)[0]])[0]
    repl_parts = _all_gather("gather_small_grads", [_pack_rows([small[k] for k in SMALL_REPL])])[0]

    out_g, out_d, out_m, out_v = {}, {}, {}, {}
    for k in BIG:
        out_g[k], out_d[k], out_m[k], out_v[k] = ex.results(k)
    res = _adamw("adamw_small_sharded", *[_pack_rows([t[k] for k in SMALL_SHARDED])[None] for t in (w, m, v)],
                 final_small)
    for o, r in zip((out_g, out_d, out_m, out_v), res):
        o.update(zip(SMALL_SHARDED, _unpack_rows(r, local_shapes)))
    res = _adamw("adamw_small_repl", *[_pack_rows([t[k] for k in SMALL_REPL])[None] for t in (w, m, v)], repl_parts)
    for o, r in zip((out_g, out_d, out_m, out_v), res):
        o.update(zip(SMALL_REPL, _unpack_rows(r, [w[k].shape for k in SMALL_REPL])))

    return (loss, dx[None], *[out_g[k] for k in WEIGHTS], *[out_d[k] for k in WEIGHTS],
            *[out_m[k] for k in WEIGHTS], *[out_v[k] for k in WEIGHTS])
```

```python
import functools
import math

import jax
import jax.numpy as jnp
from jax import lax
from jax.experimental import pallas as pl
from jax.experimental.pallas import tpu as pltpu

F32 = jnp.float32
MXU_DTYPE = jnp.bfloat16
WIRE_DTYPE = jnp.bfloat16
N_DEV = 8
EPS = 1e-6
F_FLOOR = 1e-30
LRU_C = 8.0
HGRN_HEADS = 8
HGRN_SUB = 16
ADAM_LR, ADAM_B1, ADAM_B2, ADAM_EPS, ADAM_WD, ADAM_STEP = 0.001, 0.9, 0.999, 1e-08, 0.01, 10
V7X_VMEM_LIMIT = 48 * 1024 * 1024
MM_ROWS = 1024
WGRAD_ROWS = 2048
MESH = pl.DeviceIdType.MESH
ANY = pl.BlockSpec(memory_space=pl.ANY)


def _cp(*sem):
    return pltpu.CompilerParams(dimension_semantics=sem or None, vmem_limit_bytes=V7X_VMEM_LIMIT)


def _sigmoid(x):
    return 1.0 / (1.0 + jnp.exp(-x))


def _silu(x):
    return x * _sigmoid(x)


def _dsilu(x):
    s = _sigmoid(x)
    return s * (1.0 + x * (1.0 - s))


_GELU_C = math.sqrt(2.0 / math.pi)
LOG2E = 1.0 / math.log(2.0)


def _gelu(x):
    return 0.5 * x * (1.0 + jnp.tanh(_GELU_C * (x + 0.044715 * x * x * x)))


def _dgelu(x):
    t = jnp.tanh(_GELU_C * (x + 0.044715 * x * x * x))
    return 0.5 * (1.0 + t) + 0.5 * x * (1.0 - t * t) * _GELU_C * (1.0 + 3.0 * 0.044715 * x * x)


def _log1p(e):
    return jnp.where(e < 1e-2, e * (1.0 - e * (0.5 - e * (1.0 / 3.0))), jnp.log(1.0 + e))


def _softplus(x):
    return jnp.maximum(x, 0.0) + _log1p(jnp.exp(-jnp.abs(x)))


def _one_minus_exp(x):
    series = -x * (1.0 + x * (0.5 + x * (1.0 / 6.0 + x * (1.0 / 24.0))))
    return jnp.where(x > -0.05, series, 1.0 - jnp.exp(x))


def _rows(n, d=1):
    return lax.broadcasted_iota(jnp.int32, (n, d), 0)


def _dot(a, b):
    return jnp.dot(a.astype(MXU_DTYPE), b.astype(MXU_DTYPE), preferred_element_type=F32)


def _dot_nt(a, b):
    return lax.dot_general(a.astype(MXU_DTYPE), b.astype(MXU_DTYPE), (((1,), (1,)), ((), ())),
                           preferred_element_type=F32)


def _dot_tn(a, b):
    return lax.dot_general(a.astype(MXU_DTYPE), b.astype(MXU_DTYPE), (((0,), (0,)), ((), ())),
                           preferred_element_type=F32)


def _tile(n, want):
    if n <= want:
        return n
    t = want - want % 8
    while n % t:
        t -= 8
    assert t > 0, (n, want)
    return t


def _my_place():
    x, y, c = lax.axis_index("x"), lax.axis_index("y"), lax.axis_index("c")
    return x, y, c


def _all_gather(name, srcs):
    n = len(srcs)

    def body(*refs):
        src_refs, out_refs = refs[:n], refs[n:2 * n]
        send_sems, recv_sems, local_sems = refs[2 * n:]
        x, y, c = _my_place()
        sibling = (x, y, 1 - c)
        chips = [(1 - x, y), (x, 1 - y), (1 - x, 1 - y)]

        def slot(px, py, pc):
            return 4 * px + 2 * py + pc

        def copy(i, k, block, to, src=None):
            dst = out_refs[i].at[slot(*block)]
            return pltpu.make_async_remote_copy(
                src_ref=dst if src is None else src, dst_ref=dst,
                send_sem=send_sems.at[i, k], recv_sem=recv_sems.at[i, k],
                device_id=to, device_id_type=MESH)

        me = (x, y, c)
        sends, own = [], []
        for i in range(n):
            mine = pltpu.make_async_copy(src_refs[i], out_refs[i].at[slot(*me)], local_sems.at[i])
            mine.start()
            own.append(mine)
            first = [copy(i, 0, me, sibling, src=src_refs[i])]
            first += [copy(i, 1 + j, me, (*chip, c), src=src_refs[i]) for j, chip in enumerate(chips)]
            for cp in first:
                cp.start()
            sends += first
        for i in range(n):
            for j, chip in enumerate(chips):
                copy(i, 1 + j, (*chip, c), me).wait_recv()
                passed = copy(i, 4 + j, (*chip, c), sibling)
                passed.start()
                sends.append(passed)
        for i in range(n):
            copy(i, 0, sibling, me).wait_recv()
            for j, chip in enumerate(chips):
                copy(i, 4 + j, (*chip, 1 - c), me).wait_recv()
        for cp in sends:
            cp.wait_send()
        for cp in own:
            cp.wait()

    outs = pl.pallas_call(
        body, name=name,
        out_shape=[jax.ShapeDtypeStruct((N_DEV,) + s.shape, s.dtype) for s in srcs],
        in_specs=[ANY] * n, out_specs=[ANY] * n,
        scratch_shapes=[pltpu.SemaphoreType.DMA((n, 7)), pltpu.SemaphoreType.DMA((n, 7)),
                        pltpu.SemaphoreType.DMA((n,))],
    )(*srcs)
    return list(outs)


def _pair_exchange(name, srcs):
    n = len(srcs)

    def body(*refs):
        src_refs, out_refs = refs[:n], refs[n:2 * n]
        send_sems, recv_sems = refs[2 * n:]
        x, y, c = _my_place()
        copies = []
        for i in range(n):
            for j in range(4):
                cp = pltpu.make_async_remote_copy(
                    src_ref=src_refs[i].at[2 * j + (1 - c)], dst_ref=out_refs[i].at[j],
                    send_sem=send_sems.at[i, j], recv_sem=recv_sems.at[i, j],
                    device_id=(x, y, 1 - c), device_id_type=MESH)
                cp.start()
                copies.append(cp)
        for cp in copies:
            cp.wait()

    outs = pl.pallas_call(
        body, name=name,
        out_shape=[jax.ShapeDtypeStruct((4,) + s.shape[1:], s.dtype) for s in srcs],
        in_specs=[ANY] * n, out_specs=[ANY] * n,
        scratch_shapes=[pltpu.SemaphoreType.DMA((n, 4)), pltpu.SemaphoreType.DMA((n, 4))],
    )(*srcs)
    return list(outs)


def _chip_exchange(name, srcs):
    n = len(srcs)

    def body(*refs):
        src_refs, out_refs = refs[:n], refs[n:2 * n]
        send_sems, recv_sems, local_sems = refs[2 * n:]
        x, y, c = _my_place()
        chip = 2 * x + y
        copies = []
        for i in range(n):
            mine = pltpu.make_async_copy(src_refs[i].at[chip], out_refs[i].at[3], local_sems.at[i])
            mine.start()
            copies.append(mine)
            for k, (fx, fy) in enumerate([(1, 0), (0, 1), (1, 1)]):
                px = x + fx - 2 * x * fx
                py = y + fy - 2 * y * fy
                cp = pltpu.make_async_remote_copy(
                    src_ref=src_refs[i].at[2 * px + py], dst_ref=out_refs[i].at[k],
                    send_sem=send_sems.at[i, k], recv_sem=recv_sems.at[i, k],
                    device_id=(px, py, c), device_id_type=MESH)
                cp.start()
                copies.append(cp)
        for cp in copies:
            cp.wait()

    outs = pl.pallas_call(
        body, name=name,
        out_shape=[jax.ShapeDtypeStruct(s.shape, s.dtype) for s in srcs],
        in_specs=[ANY] * n, out_specs=[ANY] * n,
        scratch_shapes=[pltpu.SemaphoreType.DMA((n, 3)), pltpu.SemaphoreType.DMA((n, 3)),
                        pltpu.SemaphoreType.DMA((n,))],
    )(*srcs)
    return list(outs)


def _pair_add(name, mine, got):
    assert mine.shape[0] == N_DEV and got.shape[0] == 4
    cdim = mine.shape[-1]
    m4 = mine.reshape(4, 2, -1, cdim)
    g3 = got.reshape(4, -1, cdim)
    rows = m4.shape[2]
    tr = _tile(rows, 512)

    def body(m_ref, g_ref, o_ref, fin_ref):
        x, y, c = _my_place()
        s = (m_ref[c].astype(F32) + g_ref[...].astype(F32)).astype(o_ref.dtype)
        o_ref[...] = s

        @pl.when(pl.program_id(1) == 2 * x + y)
        def _():
            fin_ref[...] = s

    out, fin = pl.pallas_call(
        body, name=name, grid=(rows // tr, 4),
        in_specs=[pl.BlockSpec((None, 2, tr, cdim), lambda i, j: (j, 0, i, 0)),
                  pl.BlockSpec((None, tr, cdim), lambda i, j: (j, i, 0))],
        out_specs=[pl.BlockSpec((None, tr, cdim), lambda i, j: (j, i, 0)),
                   pl.BlockSpec((None, tr, cdim), lambda i, j: (3, i, 0))],
        out_shape=[jax.ShapeDtypeStruct(g3.shape, got.dtype)] * 2,
        compiler_params=_cp("parallel", "arbitrary"),
    )(m4, g3)
    return out.reshape(got.shape), fin.reshape(got.shape)


HBM = pl.BlockSpec(memory_space=pltpu.HBM)
SEM = pl.BlockSpec(memory_space=pltpu.SEMAPHORE)
EFFECT = pltpu.SideEffectType.DATAFLOW_SIDE_EFFECTING
SLOTS = 4


def _hbm(a):
    return pltpu.with_memory_space_constraint(a, pltpu.HBM)


def _remote(src, dst, sems, i, k, to):
    return pltpu.make_async_remote_copy(src_ref=src, dst_ref=dst, send_sem=sems[0].at[i * SLOTS + k],
                                        recv_sem=sems[1].at[i * SLOTS + k], device_id=to, device_id_type=MESH)


def _slot(px, py, pc):
    return 4 * px + 2 * py + pc


def _other_chips(x, y):
    return [(1 - x, y), (x, 1 - y), (1 - x, 1 - y)]


def _plan_gather_own(layers):
    def plan(srcs, lands, sems):
        x, y, c = _my_place()
        out = []
        for i, j in enumerate(layers):
            dst = lands[i].at[_slot(x, y, c)]
            out.append(_remote(srcs[i].at[j], dst, sems, i, 0, (x, y, 1 - c)))
            for k, (px, py) in enumerate(_other_chips(x, y)):
                out.append(_remote(srcs[i].at[j], dst, sems, i, 1 + k, (px, py, c)))
        return out
    return plan


def _plan_gather_pass(n):
    def plan(srcs, lands, sems):
        x, y, c = _my_place()
        out = []
        for i in range(n):
            for k, (px, py) in enumerate(_other_chips(x, y)):
                blk = lands[i].at[_slot(px, py, c)]
                out.append(_remote(blk, blk, sems, i, k, (x, y, 1 - c)))
        return out
    return plan


def _plan_scatter_pair(n):
    def plan(srcs, lands, sems):
        x, y, c = _my_place()
        return [_remote(srcs[i].at[2 * j + (1 - c)], lands[i].at[j], sems, i, j, (x, y, 1 - c))
                for i in range(n) for j in range(4)]
    return plan


def _plan_scatter_chips(n):
    def plan(srcs, lands, sems):
        x, y, c = _my_place()
        return [_remote(srcs[i].at[2 * px + py], lands[i].at[k], sems, i, k, (px, py, c))
                for i in range(n) for k, (px, py) in enumerate(_other_chips(x, y))]
    return plan


def _split_start(name, plan, srcs, lands, deps=()):
    ns, nl, nd = len(srcs), len(lands), len(deps)
    n = max(ns, nl)

    def body(*refs):
        sems = refs[ns + nl + nd:ns + nl + nd + 2]
        for cp in plan(refs[:ns], refs[ns:ns + nl], sems):
            cp.start()
        refs[-1][...] = jnp.zeros_like(refs[-1])

    outs = pl.pallas_call(
        body, name=name,
        out_shape=(pltpu.SemaphoreType.DMA((n * SLOTS,)), pltpu.SemaphoreType.DMA((n * SLOTS,)),
                   *[pltpu.HBM(a.shape, a.dtype) for a in lands], jax.ShapeDtypeStruct((8, 128), F32)),
        in_specs=[HBM] * (ns + nl) + [ANY] * nd,
        out_specs=(SEM, SEM, *[HBM] * nl, pl.BlockSpec(memory_space=pltpu.VMEM)),
        input_output_aliases={ns + i: 2 + i for i in range(nl)},
        compiler_params=pltpu.CompilerParams(has_side_effects=EFFECT),
    )(*[_hbm(a) for a in srcs], *[_hbm(a) for a in lands], *deps)
    return (outs[0], outs[1]), list(outs[2:2 + nl]), outs[-1]


def _split_wait(name, plan, sems, srcs, lands, deps=()):
    ns, nl, nd = len(srcs), len(lands), len(deps)

    def body(*refs):
        for cp in plan(refs[:ns], refs[ns:ns + nl], refs[ns + nl:ns + nl + 2]):
            cp.wait_send()
            cp.wait_recv()

    outs = pl.pallas_call(
        body, name=name,
        out_shape=tuple(pltpu.HBM(a.shape, a.dtype) for a in lands),
        in_specs=[HBM] * (ns + nl) + [SEM, SEM] + [ANY] * nd,
        out_specs=tuple([HBM] * nl),
        input_output_aliases={ns + i: i for i in range(nl)},
        compiler_params=pltpu.CompilerParams(has_side_effects=EFFECT),
    )(*srcs, *lands, *sems, *deps)
    return list(outs)


def _place_own(srcs, layers):
    x, y, c = _my_place()
    zero = jnp.zeros((), jnp.int32)
    return [lax.dynamic_update_slice(lax.empty((N_DEV,) + a.shape[1:], a.dtype), a[j][None],
                                     (_slot(x, y, c),) + (zero,) * (a.ndim - 1))
            for a, j in zip(srcs, layers)]


def _put(dp_ref, db_ref, lo, hi, val):
    dp_ref[:, lo:hi] = val.astype(dp_ref.dtype)
    db_ref[:, lo:hi] += jnp.sum(val, axis=0, keepdims=True)


def _with_deps(body, n_in, deps):
    nd = len(deps)
    if not nd:
        return body

    def wrapped(*refs):
        body(*refs[:n_in], *refs[n_in + nd:])

    return wrapped


def _rmsnorm_fwd(name, x, g, deps=()):
    T, D = x.shape
    tm = _tile(T, 256)

    def body(x_ref, g_ref, o_ref):
        xv = x_ref[...]
        r = lax.rsqrt(jnp.mean(xv * xv, axis=-1, keepdims=True) + EPS)
        o_ref[...] = ((xv * r) * g_ref[...]).astype(o_ref.dtype)

    return pl.pallas_call(
        _with_deps(body, 2, deps), name=name, grid=(T // tm,),
        in_specs=[pl.BlockSpec((tm, D), lambda i: (i, 0)), pl.BlockSpec((1, D), lambda i: (0, 0))]
        + [ANY] * len(deps),
        out_specs=pl.BlockSpec((tm, D), lambda i: (i, 0)),
        out_shape=jax.ShapeDtypeStruct((T, D), MXU_DTYPE), compiler_params=_cp("parallel"),
    )(x, g, *deps)


def _rmsnorm_bwd(name, x, g, dh, dres):
    T, D = x.shape
    tm = _tile(T, 256)

    def body(x_ref, g_ref, dh_ref, dres_ref, dx_ref, dxb_ref, dg_ref):
        xv = x_ref[...]
        r = lax.rsqrt(jnp.mean(xv * xv, axis=-1, keepdims=True) + EPS)
        xh = xv * r
        dhv = dh_ref[...]

        @pl.when(pl.program_id(0) == 0)
        def _():
            dg_ref[...] = jnp.zeros_like(dg_ref)

        dg_ref[...] += jnp.sum(dhv * xh, axis=0, keepdims=True)
        dxh = dhv * g_ref[...]
        dx = dres_ref[...] + r * (dxh - xh * jnp.mean(dxh * xh, axis=-1, keepdims=True))
        dx_ref[...] = dx
        dxb_ref[...] = dx.astype(dxb_ref.dtype)

    return pl.pallas_call(
        body, name=name, grid=(T // tm,),
        in_specs=[pl.BlockSpec((tm, D), lambda i: (i, 0)), pl.BlockSpec((1, D), lambda i: (0, 0)),
                  pl.BlockSpec((tm, D), lambda i: (i, 0)), pl.BlockSpec((tm, D), lambda i: (i, 0))],
        out_specs=[pl.BlockSpec((tm, D), lambda i: (i, 0)), pl.BlockSpec((tm, D), lambda i: (i, 0)),
                   pl.BlockSpec((1, D), lambda i: (0, 0))],
        out_shape=[jax.ShapeDtypeStruct((T, D), F32), jax.ShapeDtypeStruct((T, D), MXU_DTYPE),
                   jax.ShapeDtypeStruct((1, D), F32)],
        compiler_params=_cp("arbitrary"),
    )(x, g, dh, dres)


def _loss_head(name, x, g, tgt):
    T, D = x.shape
    tm = _tile(T, 256)

    def body(x_ref, g_ref, t_ref, loss_ref, dx_ref, dxb_ref, dg_ref):
        xv = x_ref[...]
        r = lax.rsqrt(jnp.mean(xv * xv, axis=-1, keepdims=True) + EPS)
        xh = xv * r
        gv = g_ref[...]
        diff = xh * gv - t_ref[...]

        @pl.when(pl.program_id(0) == 0)
        def _():
            dg_ref[...] = jnp.zeros_like(dg_ref)
            loss_ref[...] = jnp.zeros_like(loss_ref)

        part = 0.5 * jnp.sum(jnp.mean(diff * diff, axis=-1, keepdims=True), axis=0, keepdims=True)
        loss_ref[...] += jnp.broadcast_to(part, loss_ref.shape)
        dy = diff * (1.0 / D)
        dg_ref[...] += jnp.sum(dy * xh, axis=0, keepdims=True)
        dxh = dy * gv
        dx = r * (dxh - xh * jnp.mean(dxh * xh, axis=-1, keepdims=True))
        dx_ref[...] = dx
        dxb_ref[...] = dx.astype(dxb_ref.dtype)

    return pl.pallas_call(
        body, name=name, grid=(T // tm,),
        in_specs=[pl.BlockSpec((tm, D), lambda i: (i, 0)), pl.BlockSpec((1, D), lambda i: (0, 0)),
                  pl.BlockSpec((tm, D), lambda i: (i, 0))],
        out_specs=[pl.BlockSpec((1, 128), lambda i: (0, 0)), pl.BlockSpec((tm, D), lambda i: (i, 0)),
                   pl.BlockSpec((tm, D), lambda i: (i, 0)), pl.BlockSpec((1, D), lambda i: (0, 0))],
        out_shape=[jax.ShapeDtypeStruct((1, 128), F32), jax.ShapeDtypeStruct((T, D), F32),
                   jax.ShapeDtypeStruct((T, D), MXU_DTYPE), jax.ShapeDtypeStruct((1, D), F32)],
        compiler_params=_cp("arbitrary"),
    )(x, g, tgt)


def _adamw(name, w, m, v, parts, j=0, prev=None):
    L, R, C = w.shape
    P = parts.shape[0]
    tr = _tile(R, 256)
    ob = j * (R // tr)
    w, m, v = (a.reshape(L * R, C) for a in (w, m, v))
    c1 = 1.0 / (1.0 - ADAM_B1 ** ADAM_STEP)
    c2 = 1.0 / (1.0 - ADAM_B2 ** ADAM_STEP)
    chained = L > 1
    if chained and prev is None:
        prev = [lax.empty(w.shape, F32) for _ in range(4)]
    prev = list(prev) if chained else []

    def body(w_ref, m_ref, v_ref, p_ref, *rest):
        g_ref, d_ref, nm_ref, nv_ref = rest[len(prev):]
        g = p_ref[0].astype(F32)
        for s in range(1, P):
            g = g + p_ref[s].astype(F32)
        nm = ADAM_B1 * m_ref[...] + (1.0 - ADAM_B1) * g
        nv = ADAM_B2 * v_ref[...] + (1.0 - ADAM_B2) * (g * g)
        g_ref[...] = g
        nm_ref[...] = nm
        nv_ref[...] = nv
        d_ref[...] = -ADAM_LR * ((nm * c1) / (jnp.sqrt(nv * c2) + ADAM_EPS) + ADAM_WD * w_ref[...])

    blk = pl.BlockSpec((tr, C), lambda i: (i + ob, 0))
    return pl.pallas_call(
        body, name=name, grid=(R // tr,),
        in_specs=[blk, blk, blk, pl.BlockSpec((P, tr, C), lambda i: (0, i, 0))] + [ANY] * len(prev),
        out_specs=[blk, blk, blk, blk],
        out_shape=[jax.ShapeDtypeStruct(w.shape, F32)] * 4,
        input_output_aliases={4 + i: i for i in range(len(prev))},
        compiler_params=_cp("parallel"),
    )(w, m, v, parts, *prev)


def _proj_in(name, h, wg, bias, deps=()):
    T, K = h.shape
    n = wg.shape[-1]
    tm = _tile(T, MM_ROWS)

    def body(a_ref, w_ref, b_ref, o_ref):
        o_ref[...] = _dot(a_ref[...], w_ref[...]) + b_ref[...]

    return pl.pallas_call(
        _with_deps(body, 3, deps), name=name, grid=(N_DEV, T // tm),
        in_specs=[pl.BlockSpec((tm, K), lambda s, i: (i, 0)),
                  pl.BlockSpec((None, K, n), lambda s, i: (s, 0, 0)),
                  pl.BlockSpec((1, n), lambda s, i: (0, s))] + [ANY] * len(deps),
        out_specs=pl.BlockSpec((tm, n), lambda s, i: (i, s)),
        out_shape=jax.ShapeDtypeStruct((T, N_DEV * n), F32), compiler_params=_cp("parallel", "parallel"),
    )(h, wg, bias, *deps)


def _ffn_in(name, h, wg_gate, wg_up):
    T, K = h.shape
    n = wg_gate.shape[-1]
    tm = _tile(T, MM_ROWS)

    def body(a_ref, wgt_ref, wup_ref, g_ref, u_ref, hid_ref):
        a = a_ref[...]
        g = _dot(a, wgt_ref[...])
        u = _dot(a, wup_ref[...])
        g_ref[...] = g
        u_ref[...] = u
        hid_ref[...] = (_silu(g) * u).astype(hid_ref.dtype)

    wspec = pl.BlockSpec((None, K, n), lambda s, i: (s, 0, 0))
    ospec = pl.BlockSpec((None, tm, n), lambda s, i: (s, i, 0))
    return pl.pallas_call(
        body, name=name, grid=(N_DEV, T // tm),
        in_specs=[pl.BlockSpec((tm, K), lambda s, i: (i, 0)), wspec, wspec],
        out_specs=[ospec, ospec, ospec],
        out_shape=[jax.ShapeDtypeStruct((N_DEV, T, n), F32), jax.ShapeDtypeStruct((N_DEV, T, n), F32),
                   jax.ShapeDtypeStruct((N_DEV, T, n), MXU_DTYPE)],
        compiler_params=_cp("parallel", "parallel"),
    )(h, wg_gate, wg_up)


def _a_spec(a, tm, k):
    if a.ndim == 2:
        return pl.BlockSpec((tm, k), lambda i, s: (i, s))
    return pl.BlockSpec((None, tm, k), lambda i, s: (s, i, 0))


def _proj_out(name, a, wg, res, deps=(), norm_g=None):
    k, N = wg.shape[-2:]
    T = res.shape[0]
    tm = _tile(T, MM_ROWS // 2)
    extra = [] if norm_g is None else [norm_g]

    def body(a_ref, w_ref, r_ref, *rest):
        o_ref = rest[len(extra)]
        p = _dot(a_ref[...], w_ref[...])

        @pl.when(pl.program_id(1) == 0)
        def _():
            o_ref[...] = r_ref[...] + p

        @pl.when(pl.program_id(1) > 0)
        def _():
            o_ref[...] += p

        if extra:
            @pl.when(pl.program_id(1) == N_DEV - 1)
            def _():
                xv = o_ref[...]
                r = lax.rsqrt(jnp.mean(xv * xv, axis=-1, keepdims=True) + EPS)
                rest[2][...] = ((xv * r) * rest[0][...]).astype(rest[2].dtype)

    row_blk = pl.BlockSpec((tm, N), lambda i, s: (i, 0))
    out = pl.pallas_call(
        _with_deps(body, 3 + len(extra), deps), name=name, grid=(T // tm, N_DEV),
        in_specs=[_a_spec(a, tm, k), pl.BlockSpec((None, k, N), lambda i, s: (s, 0, 0)), row_blk]
        + [pl.BlockSpec((1, N), lambda i, s: (0, 0))] * len(extra) + [ANY] * len(deps),
        out_specs=[row_blk] * (1 + len(extra)),
        out_shape=[jax.ShapeDtypeStruct((T, N), F32)] + [jax.ShapeDtypeStruct((T, N), MXU_DTYPE)] * len(extra),
        compiler_params=_cp("parallel", "arbitrary"),
    )(a, wg, res, *extra, *deps)
    return out[0] if norm_g is None else out


def _mix_out(name, y, wg, res, norm_g, deps=()):
    N = wg.shape[-1]
    w2 = wg.reshape(-1, N)
    K = w2.shape[0]
    T = res.shape[0]
    tm = _tile(T, MM_ROWS // 2)

    def body(a_ref, w_ref, r_ref, g_ref, o_ref, h_ref):
        xv = r_ref[...] + _dot(a_ref[...], w_ref[...])
        o_ref[...] = xv
        r = lax.rsqrt(jnp.mean(xv * xv, axis=-1, keepdims=True) + EPS)
        h_ref[...] = ((xv * r) * g_ref[...]).astype(h_ref.dtype)

    row_blk = pl.BlockSpec((tm, N), lambda i: (i, 0))
    return pl.pallas_call(
        _with_deps(body, 4, deps), name=name, grid=(T // tm,),
        in_specs=[pl.BlockSpec((tm, K), lambda i: (i, 0)),
                  pl.BlockSpec((K, N), lambda i: (0, 0), pipeline_mode=pl.Buffered(1)),
                  row_blk, pl.BlockSpec((1, N), lambda i: (0, 0))] + [ANY] * len(deps),
        out_specs=[row_blk, row_blk],
        out_shape=[jax.ShapeDtypeStruct((T, N), F32), jax.ShapeDtypeStruct((T, N), MXU_DTYPE)],
        compiler_params=_cp("parallel"),
    )(y, w2, res, norm_g, *deps)


def _bwd_in(name, das, wgs, deps=()):
    K, n = wgs[0].shape[-2:]
    T = das[0].shape[-2]
    tm = _tile(T, MM_ROWS)
    npair = len(das)

    def body(*refs):
        o_ref = refs[-1]
        p = _dot_nt(refs[0][...], refs[npair][...])
        for q in range(1, npair):
            p = p + _dot_nt(refs[q][...], refs[npair + q][...])

        @pl.when(pl.program_id(1) == 0)
        def _():
            o_ref[...] = p

        @pl.when(pl.program_id(1) > 0)
        def _():
            o_ref[...] += p

    return pl.pallas_call(
        _with_deps(body, 2 * npair, deps), name=name, grid=(T // tm, N_DEV),
        in_specs=[_a_spec(a, tm, n) for a in das]
        + [pl.BlockSpec((None, K, n), lambda i, s: (s, 0, 0)) for _ in wgs] + [ANY] * len(deps),
        out_specs=pl.BlockSpec((tm, K), lambda i, s: (i, 0)),
        out_shape=jax.ShapeDtypeStruct((T, K), F32), compiler_params=_cp("parallel", "arbitrary"),
    )(*das, *wgs, *deps)


def _bwd_out(name, dx, wg, deps=()):
    k, N = wg.shape[-2:]
    T = dx.shape[0]
    tm = _tile(T, MM_ROWS)

    def body(a_ref, w_ref, o_ref):
        o_ref[...] = _dot_nt(a_ref[...], w_ref[...])

    return pl.pallas_call(
        _with_deps(body, 2, deps), name=name, grid=(N_DEV, T // tm),
        in_specs=[pl.BlockSpec((tm, N), lambda s, i: (i, 0)),
                  pl.BlockSpec((None, k, N), lambda s, i: (s, 0, 0))] + [ANY] * len(deps),
        out_specs=pl.BlockSpec((tm, k), lambda s, i: (i, s)),
        out_shape=jax.ShapeDtypeStruct((T, N_DEV * k), F32), compiler_params=_cp("parallel", "parallel"),
    )(dx, wg, *deps)


def _ffn_bwd_hidden(name, dx, wg_down, gate, up, deps=()):
    n, N = wg_down.shape[-2:]
    T = dx.shape[0]
    tm = _tile(T, MM_ROWS)

    def body(a_ref, w_ref, g_ref, u_ref, dg_ref, du_ref):
        dh = _dot_nt(a_ref[...], w_ref[...])
        g = g_ref[...]
        dg_ref[...] = (dh * u_ref[...] * _dsilu(g)).astype(dg_ref.dtype)
        du_ref[...] = (dh * _silu(g)).astype(du_ref.dtype)

    sm = pl.BlockSpec((None, tm, n), lambda s, i: (s, i, 0))
    return pl.pallas_call(
        _with_deps(body, 4, deps), name=name, grid=(N_DEV, T // tm),
        in_specs=[pl.BlockSpec((tm, N), lambda s, i: (i, 0)),
                  pl.BlockSpec((None, n, N), lambda s, i: (s, 0, 0)), sm, sm] + [ANY] * len(deps),
        out_specs=[sm, sm],
        out_shape=[jax.ShapeDtypeStruct((N_DEV, T, n), MXU_DTYPE)] * 2,
        compiler_params=_cp("parallel", "parallel"),
    )(dx, wg_down, gate, up, *deps)


def _wgrad(name, a, c, rows, cols, deps=()):
    T = a.shape[-2]
    tk = _tile(T, WGRAD_ROWS)
    nk = T // tk

    def spec(z, w):
        if z.ndim == 3:
            return pl.BlockSpec((None, tk, w), lambda s, k: (s, k, 0))
        if z.shape[1] == w:
            return pl.BlockSpec((tk, w), lambda s, k: (k, 0))
        return pl.BlockSpec((tk, w), lambda s, k: (k, s))

    def body(a_ref, c_ref, o_ref, acc_ref):
        k = pl.program_id(1)
        p = _dot_tn(a_ref[...], c_ref[...])

        @pl.when(k == 0)
        def _():
            acc_ref[...] = p

        @pl.when(k > 0)
        def _():
            acc_ref[...] += p

        @pl.when(k == nk - 1)
        def _():
            o_ref[...] = acc_ref[...].astype(o_ref.dtype)

    return pl.pallas_call(
        _with_deps(body, 2, deps), name=name, grid=(N_DEV, nk),
        in_specs=[spec(a, rows), spec(c, cols)] + [ANY] * len(deps),
        out_specs=pl.BlockSpec((None, rows, cols), lambda s, k: (s, 0, 0)),
        out_shape=jax.ShapeDtypeStruct((N_DEV, rows, cols), WIRE_DTYPE),
        scratch_shapes=[pltpu.VMEM((rows, cols), F32)],
        compiler_params=_cp("parallel", "arbitrary"),
    )(a, c, *deps)


def _shift_down(cur, prev8, sh):
    n = cur.shape[0]
    rolled = pltpu.roll(cur, sh, 0)
    top = jnp.where(_rows(8) < sh, pltpu.roll(prev8, sh, 0), rolled[0:8])
    return jnp.concatenate([top, rolled[8:n]], axis=0)


def _shift_up(cur, next8, sh):
    n = cur.shape[0]
    rolled = pltpu.roll(cur, n - sh, 0)
    bot = jnp.where(_rows(8) >= 8 - sh, pltpu.roll(next8, 8 - sh, 0), rolled[n - 8:n])
    return jnp.concatenate([rolled[0:n - 8], bot], axis=0)


def _lru_gate_terms(r, lam):
    sp = _softplus(-lam)
    la = -LRU_C * r * sp
    a = jnp.exp(la)
    m2 = _one_minus_exp(2.0 * la)
    return sp, la, a, m2


def _lru_fwd(name, proj, conv_w, conv_b, wa, ba, wx, bx, lam):
    T = proj.shape[0]
    H, hd, _ = wa.shape
    W = H * hd
    K = conv_w.shape[0]
    tb = _tile(T, 256)

    def body(xin_ref, gate_ref, cw_ref, cb_ref, wa_ref, ba_ref, wx_ref, bx_ref, lam_ref,
             ya_ref, xc_ref, r_ref, i_ref, hs_ref, tail_ref, hprev_ref):
        blk = pl.program_id(0)

        @pl.when(blk == 0)
        def _():
            tail_ref[...] = jnp.zeros_like(tail_ref)
            hprev_ref[...] = jnp.zeros_like(hprev_ref)

        xin = xin_ref[...]
        prev8 = tail_ref[...]
        xc = cw_ref[K - 1:K, :] * xin
        for sh in range(1, K):
            xc = xc + cw_ref[K - 1 - sh:K - sh, :] * _shift_down(xin, prev8, sh)
        xc = xc + cb_ref[...]
        tail_ref[...] = xin[tb - 8:tb]
        xc_ref[...] = xc
        for h in range(H):
            cs = slice(h * hd, (h + 1) * hd)
            xh = xc[:, cs]
            r_ref[:, cs] = _sigmoid(_dot(xh, wa_ref[h]) + ba_ref[:, cs])
            i_ref[:, cs] = _sigmoid(_dot(xh, wx_ref[h]) + bx_ref[:, cs])
        r = r_ref[...]
        _, _, a, m2 = _lru_gate_terms(r, lam_ref[...])
        row = _rows(tb)
        mult = jnp.where((row == 0) & (blk == 0), 1.0, jnp.sqrt(jnp.maximum(m2, 0.0)))
        u = mult * i_ref[...] * xc
        d = 1
        while d < tb:
            keep = row >= d
            u = a * jnp.where(keep, pltpu.roll(u, d, 0), 0.0) + u
            a = a * jnp.where(keep, pltpu.roll(a, d, 0), 1.0)
            d *= 2
        hs = u + a * hprev_ref[...]
        hprev_ref[...] = hs[tb - 1:tb]
        hs_ref[...] = hs
        ya_ref[...] = (hs * _gelu(gate_ref[...])).astype(ya_ref.dtype)

    full = lambda shape: pl.BlockSpec(shape, lambda i: tuple(0 for _ in shape))
    blk = pl.BlockSpec((tb, W), lambda i: (i, 0))
    return pl.pallas_call(
        body, name=name, grid=(T // tb,),
        in_specs=[pl.BlockSpec((tb, W), lambda i: (i, 0)), pl.BlockSpec((tb, W), lambda i: (i, 1)),
                  full((K, W)), full((1, W)), full((H, hd, hd)), full((1, W)), full((H, hd, hd)),
                  full((1, W)), full((1, W))],
        out_specs=[blk] * 5,
        out_shape=[jax.ShapeDtypeStruct((T, 2 * W), MXU_DTYPE)] + [jax.ShapeDtypeStruct((T, W), F32)] * 4,
        scratch_shapes=[pltpu.VMEM((8, W), F32), pltpu.VMEM((1, W), F32)],
        compiler_params=_cp("arbitrary"),
    )(proj, proj, conv_w, conv_b, wa, ba, wx, bx, lam)


def _lru_bwd(name, proj, dy, xc, r, ig, hs, conv_w, wa, wx, lam, dpbuf):
    T = proj.shape[0]
    H, hd, _ = wa.shape
    W = H * hd
    K = conv_w.shape[0]
    tb = _tile(T, 256)
    nb = T // tb
    t8 = tb // 8

    def body(xin_ref, xprev_ref, gate_ref, dy_ref, xc_ref, r_ref, i_ref, hs_ref, hsprev_ref,
             cw_ref, wa_ref, wx_ref, lam_ref, dpbuf_ref,
             dp_ref, db_ref, dcw_ref, dcb_ref, dwa_ref, dba_ref, dwx_ref, dbx_ref, dlam_ref,
             cdh_ref, ca_ref, cdxc_ref, dxc_ref):
        del dpbuf_ref
        step = pl.program_id(0)
        blk = nb - 1 - step

        @pl.when(step == 0)
        def _():
            for ref in (db_ref, dcw_ref, dcb_ref, dwa_ref, dba_ref, dwx_ref, dbx_ref, dlam_ref,
                        cdh_ref, ca_ref, cdxc_ref):
                ref[...] = jnp.zeros_like(ref)

        row = _rows(tb)
        first = blk == 0
        gate = gate_ref[...]
        dy_a = dy_ref[...]
        hsv = hs_ref[...]
        _put(dp_ref, db_ref, W, 2 * W, dy_a * hsv * _dgelu(gate))
        d_hs = dy_a * _gelu(gate)
        lam = lam_ref[...]
        rv = r_ref[...]
        sp, la, a, m2 = _lru_gate_terms(rv, lam)
        an = jnp.where(row == tb - 1, ca_ref[...], pltpu.roll(a, tb - 1, 0))
        u = d_hs
        d = 1
        while d < tb:
            keep = row < tb - d
            u = an * jnp.where(keep, pltpu.roll(u, tb - d, 0), 0.0) + u
            an = an * jnp.where(keep, pltpu.roll(an, tb - d, 0), 1.0)
            d *= 2
        dh = u + an * cdh_ref[...]
        cdh_ref[...] = dh[0:1]
        ca_ref[...] = a[0:1]
        hlast = jnp.where(first, 0.0, hsprev_ref[7:8, :])
        hprev = jnp.where(row == 0, hlast, pltpu.roll(hsv, 1, 0))
        da = dh * hprev
        xcv = xc_ref[...]
        iv = i_ref[...]
        t0 = (row == 0) & first
        mult = jnp.sqrt(jnp.maximum(m2, 0.0))
        mult_eff = jnp.where(t0, 1.0, mult)
        d_mult = dh * iv * xcv
        d_i = dh * mult_eff * xcv
        dxc = dh * mult_eff * iv
        e2 = 1.0 - m2
        d_la = da * a + jnp.where(t0 | (m2 <= 0.0), 0.0, -d_mult * e2 / jnp.where(m2 > 0.0, mult, 1.0))
        d_r = d_la * (-LRU_C * sp)
        dlam_ref[...] += jnp.sum(d_la * (-LRU_C * rv), axis=0, keepdims=True) * (-_sigmoid(-lam))
        d_zr = d_r * rv * (1.0 - rv)
        d_zi = d_i * iv * (1.0 - iv)
        dba_ref[...] += jnp.sum(d_zr, axis=0, keepdims=True)
        dbx_ref[...] += jnp.sum(d_zi, axis=0, keepdims=True)
        for h in range(H):
            cs = slice(h * hd, (h + 1) * hd)
            xh = xcv[:, cs]
            zr, zi = d_zr[:, cs], d_zi[:, cs]
            dwa_ref[h] += _dot_tn(xh, zr)
            dwx_ref[h] += _dot_tn(xh, zi)
            dxc_ref[:, cs] = dxc[:, cs] + _dot_nt(zr, wa_ref[h]) + _dot_nt(zi, wx_ref[h])
        dxc = dxc_ref[...]
        dcb_ref[...] += jnp.sum(dxc, axis=0, keepdims=True)
        xin = xin_ref[...]
        prev8 = jnp.where(first, 0.0, xprev_ref[...])
        next8 = cdxc_ref[...]
        dxin = cw_ref[K - 1:K, :] * dxc
        dcw_ref[K - 1:K, :] += jnp.sum(dxc * xin, axis=0, keepdims=True)
        for sh in range(1, K):
            dxin = dxin + cw_ref[K - 1 - sh:K - sh, :] * _shift_up(dxc, next8, sh)
            dcw_ref[K - 1 - sh:K - sh, :] += jnp.sum(dxc * _shift_down(xin, prev8, sh), axis=0, keepdims=True)
        cdxc_ref[...] = dxc[0:8]
        _put(dp_ref, db_ref, 0, W, dxin)

    full = lambda shape: pl.BlockSpec(shape, lambda i: tuple(0 for _ in shape))
    cur = lambda col: pl.BlockSpec((tb, W), lambda i: (nb - 1 - i, col))
    prev = pl.BlockSpec((8, W), lambda i: (jnp.maximum((nb - 1 - i) * t8 - 1, 0), 0))
    return pl.pallas_call(
        body, name=name, grid=(nb,),
        in_specs=[cur(0), prev, cur(1), cur(0), cur(0), cur(0), cur(0), cur(0), prev,
                  full((K, W)), full((H, hd, hd)), full((H, hd, hd)), full((1, W)), ANY],
        out_specs=[pl.BlockSpec((tb, 2 * W), lambda i: (nb - 1 - i, 0)), full((1, 2 * W)), full((K, W)),
                   full((1, W)),
                   full((H, hd, hd)), full((1, W)), full((H, hd, hd)), full((1, W)), full((1, W))],
        out_shape=[jax.ShapeDtypeStruct(dpbuf.shape, dpbuf.dtype), jax.ShapeDtypeStruct((1, 2 * W), F32),
                   jax.ShapeDtypeStruct((K, W), F32),
                   jax.ShapeDtypeStruct((1, W), F32), jax.ShapeDtypeStruct((H, hd, hd), F32),
                   jax.ShapeDtypeStruct((1, W), F32), jax.ShapeDtypeStruct((H, hd, hd), F32),
                   jax.ShapeDtypeStruct((1, W), F32), jax.ShapeDtypeStruct((1, W), F32)],
        scratch_shapes=[pltpu.VMEM((1, W), F32), pltpu.VMEM((1, W), F32), pltpu.VMEM((8, W), F32),
                        pltpu.VMEM((tb, W), F32)],
        input_output_aliases={13: 0}, compiler_params=_cp("arbitrary"),
    )(proj, proj, proj, dy, xc, r, ig, hs, hs, conv_w, wa, wx, lam, dpbuf)


def _chunk_cumsum(g, c):
    n = g.shape[0]
    rc = _rows(n) & (c - 1)
    d = 1
    while d < c:
        g = g + jnp.where(rc >= d, pltpu.roll(g, d, 0), 0.0)
        d *= 2
    return g


def _chunk_rcumsum(g, c):
    n = g.shape[0]
    rc = _rows(n) & (c - 1)
    d = 1
    while d < c:
        g = g + jnp.where(rc < c - d, pltpu.roll(g, n - d, 0), 0.0)
        d *= 2
    return g


def _hgrn_pointwise(qr, fr, lb):
    qf = _silu(qr)
    sig = _sigmoid(fr)
    fg = lb + (1.0 - lb) * sig
    gl = jnp.log(jnp.maximum(fg, F_FLOOR))
    kk = (1.0 - lb) * (1.0 - sig)
    return qf, sig, fg, gl, kk


def _hgrn_fwd(name, proj, lb, norm_g, ybuf):
    T = proj.shape[0]
    W = lb.shape[1]
    H = HGRN_HEADS
    dk = W // H
    c = HGRN_SUB
    R = _tile(T, 128)
    nck = R // c

    def body(q_ref, f_ref, v_ref, g_ref, lb_ref, ng_ref, ybuf_ref, yb_ref, o_ref, sall_ref,
             st_ref, qe_ref, ke_ref, acc_ref):
        del ybuf_ref

        @pl.when(pl.program_id(0) == 0)
        def _():
            st_ref[...] = jnp.zeros_like(st_ref)

        qf, _, _, gl, kk = _hgrn_pointwise(q_ref[...], f_ref[...], lb_ref[...])
        b = _chunk_cumsum(gl * LOG2E, c)
        rc = _rows(R) & (c - 1)
        for h in range(H):
            cs = slice(h * dk, (h + 1) * dk)
            qh, kh, bh, vh = qf[:, cs], kk[:, cs], b[:, cs], v_ref[:, cs]
            acc = jnp.sum(qh * kh, axis=1, keepdims=True) * vh
            for d in range(1, c):
                e = jnp.where(rc >= d, jnp.exp2(bh - pltpu.roll(bh, d, 0)), 0.0)
                s = jnp.sum(qh * pltpu.roll(kh, d, 0) * e, axis=1, keepdims=True)
                acc = acc + s * pltpu.roll(vh, d, 0)
            acc_ref[:, cs] = acc
        qe_ref[...] = qf * jnp.exp2(b)
        for ci in range(nck):
            rs = slice(ci * c, (ci + 1) * c)
            bl = b[ci * c + c - 1:ci * c + c, :]
            ke_ref[rs, :] = kk[rs, :] * jnp.exp2(bl - b[rs, :])
            ebl = jnp.exp2(bl)
            for h in range(H):
                cs = slice(h * dk, (h + 1) * dk)
                st = st_ref[h]
                sall_ref[ci, h] = st
                o_ref[rs, cs] = acc_ref[rs, cs] + _dot_nt(qe_ref[rs, cs], st)
                st_ref[h] = st * ebl[:, cs] + _dot_tn(v_ref[rs, cs], ke_ref[rs, cs])
        ng = ng_ref[...]
        gg = g_ref[...]
        for h in range(H):
            cs = slice(h * dk, (h + 1) * dk)
            oh = o_ref[:, cs]
            rr = lax.rsqrt(jnp.mean(oh * oh, axis=1, keepdims=True) + EPS)
            yb_ref[:, cs] = ((oh * rr) * ng[:, cs] * _silu(gg[:, cs])).astype(yb_ref.dtype)

    full = lambda shape: pl.BlockSpec(shape, lambda i: tuple(0 for _ in shape))
    col = lambda k: pl.BlockSpec((R, W), lambda i: (i, k))
    blk = pl.BlockSpec((R, W), lambda i: (i, 0))
    return pl.pallas_call(
        body, name=name, grid=(T // R,),
        in_specs=[col(2), col(3), col(4), col(5), full((1, W)), full((1, W)), ANY],
        out_specs=[col(1), blk, pl.BlockSpec((nck, H, dk, dk), lambda i: (i, 0, 0, 0))],
        out_shape=[jax.ShapeDtypeStruct((T, 2 * W), MXU_DTYPE), jax.ShapeDtypeStruct((T, W), F32),
                   jax.ShapeDtypeStruct((T // c, H, dk, dk), F32)],
        scratch_shapes=[pltpu.VMEM((H, dk, dk), F32), pltpu.VMEM((R, W), F32), pltpu.VMEM((R, W), F32),
                        pltpu.VMEM((R, W), F32)],
        input_output_aliases={6: 0}, compiler_params=_cp("arbitrary"),
    )(proj, proj, proj, proj, lb, norm_g, ybuf)


def _hgrn_bwd(name, proj, dy, o, sall, lb, norm_g):
    T = proj.shape[0]
    W = lb.shape[1]
    H = HGRN_HEADS
    dk = W // H
    c = HGRN_SUB
    R = _tile(T, 128)
    nck = R // c
    nb = T // R

    def body(q_ref, f_ref, v_ref, g_ref, dy_ref, o_ref, sall_ref, lb_ref, ng_ref,
             dp_ref, db_ref, dlb_ref, dng_ref,
             dst_ref, do_ref, dq_ref, dk_ref, dv_ref, ex_ref, qe_ref, ke_ref):
        @pl.when(pl.program_id(0) == 0)
        def _():
            dst_ref[...] = jnp.zeros_like(dst_ref)
            db_ref[...] = jnp.zeros_like(db_ref)
            dlb_ref[...] = jnp.zeros_like(dlb_ref)
            dng_ref[...] = jnp.zeros_like(dng_ref)

        lbv = lb_ref[...]
        qr = q_ref[...]
        qf, sig, fg, gl, kk = _hgrn_pointwise(qr, f_ref[...], lbv)
        b = _chunk_cumsum(gl * LOG2E, c)
        rc = _rows(R) & (c - 1)
        ng = ng_ref[...]
        gg = g_ref[...]
        dyv = dy_ref[...]
        sg = _silu(gg)
        for h in range(H):
            cs = slice(h * dk, (h + 1) * dk)
            oh = o_ref[:, cs]
            rr = lax.rsqrt(jnp.mean(oh * oh, axis=1, keepdims=True) + EPS)
            ohat = oh * rr
            dyh = dyv[:, cs]
            _put(dp_ref, db_ref, 5 * W + h * dk, 5 * W + (h + 1) * dk,
                 dyh * ohat * ng[:, cs] * _dsilu(gg[:, cs]))
            t = dyh * sg[:, cs]
            dng_ref[:, cs] += jnp.sum(t * ohat, axis=0, keepdims=True)
            dohat = t * ng[:, cs]
            do_ref[:, cs] = rr * (dohat - ohat * jnp.mean(dohat * ohat, axis=1, keepdims=True))
        for h in range(H):
            cs = slice(h * dk, (h + 1) * dk)
            qh, kh, bh, vh, doh = qf[:, cs], kk[:, cs], b[:, cs], v_ref[:, cs], do_ref[:, cs]
            da0 = jnp.sum(doh * vh, axis=1, keepdims=True)
            a0 = jnp.sum(qh * kh, axis=1, keepdims=True)
            dq = da0 * kh
            dkk = da0 * qh
            dv = a0 * doh
            for d in range(1, c):
                e = jnp.where(rc >= d, jnp.exp2(bh - pltpu.roll(bh, d, 0)), 0.0)
                kr = pltpu.roll(kh, d, 0)
                da = jnp.sum(doh * pltpu.roll(vh, d, 0), axis=1, keepdims=True)
                aa = jnp.sum(qh * kr * e, axis=1, keepdims=True)
                dq = dq + da * kr * e
                dkk = dkk + pltpu.roll(da * qh * e, R - d, 0)
                dv = dv + pltpu.roll(aa * doh, R - d, 0)
            dq_ref[:, cs] = dq
            dk_ref[:, cs] = dkk
            dv_ref[:, cs] = dv
        eb = jnp.exp2(b)
        qe_ref[...] = qf * eb
        ex_ref[...] = jnp.zeros_like(ex_ref)
        for ci in reversed(range(nck)):
            rs = slice(ci * c, (ci + 1) * c)
            bl = b[ci * c + c - 1:ci * c + c, :]
            ebl_rows = jnp.exp2(bl - b[rs, :])
            ke_ref[rs, :] = kk[rs, :] * ebl_rows
            ebl = jnp.exp2(bl)
            for h in range(H):
                cs = slice(h * dk, (h + 1) * dk)
                st0 = sall_ref[ci, h]
                dst1 = dst_ref[h]
                doc = do_ref[rs, cs]
                vc = v_ref[rs, cs]
                dq_ref[rs, cs] += _dot(doc, st0) * eb[rs, cs]
                dv_ref[rs, cs] += _dot_nt(ke_ref[rs, cs], dst1)
                dki = _dot(vc, dst1) * ebl_rows[:, cs]
                dk_ref[rs, cs] += dki
                ex_ref[ci * c + c - 1:ci * c + c, cs] = (
                    jnp.sum(dki * kk[rs, cs], axis=0, keepdims=True)
                    + ebl[:, cs] * jnp.sum(st0 * dst1, axis=0, keepdims=True))
                dst_ref[h] = dst1 * ebl[:, cs] + _dot_tn(doc, qe_ref[rs, cs])
        dq = dq_ref[...]
        dkk = dk_ref[...]
        db = qf * dq - kk * dkk + ex_ref[...]
        dgl = _chunk_rcumsum(db, c)
        dfg = jnp.where(fg > F_FLOOR, dgl / jnp.maximum(fg, F_FLOOR), 0.0)
        dsig = (dfg - dkk) * (1.0 - lbv)
        dlb_ref[...] += jnp.sum((dfg - dkk) * (1.0 - sig), axis=0, keepdims=True)
        dp_ref[:, 0:2 * W] = jnp.zeros((R, 2 * W), dp_ref.dtype)
        _put(dp_ref, db_ref, 2 * W, 3 * W, dq * _dsilu(qr))
        _put(dp_ref, db_ref, 3 * W, 4 * W, dsig * sig * (1.0 - sig))
        _put(dp_ref, db_ref, 4 * W, 5 * W, dv_ref[...])

    full = lambda shape: pl.BlockSpec(shape, lambda i: tuple(0 for _ in shape))
    col = lambda k: pl.BlockSpec((R, W), lambda i: (nb - 1 - i, k))
    scr = pltpu.VMEM((R, W), F32)
    return pl.pallas_call(
        body, name=name, grid=(nb,),
        in_specs=[col(2), col(3), col(4), col(5), col(1), col(0),
                  pl.BlockSpec((nck, H, dk, dk), lambda i: (nb - 1 - i, 0, 0, 0)), full((1, W)), full((1, W))],
        out_specs=[pl.BlockSpec((R, 6 * W), lambda i: (nb - 1 - i, 0)), full((1, 6 * W)), full((1, W)),
                   full((1, W))],
        out_shape=[jax.ShapeDtypeStruct((T, 6 * W), MXU_DTYPE), jax.ShapeDtypeStruct((1, 6 * W), F32),
                   jax.ShapeDtypeStruct((1, W), F32), jax.ShapeDtypeStruct((1, W), F32)],
        scratch_shapes=[pltpu.VMEM((H, dk, dk), F32), scr, scr, scr, scr, scr, scr, scr],
        compiler_params=_cp("arbitrary"),
    )(proj, proj, proj, proj, dy, o, sall, lb, norm_g)


ODD_HALO = 32


def _row_shifts(ext, up):
    n = ext.shape[0]
    return [ext] + [pltpu.roll(ext, n - b if up else b, 0) for b in range(1, 8)]


def _past(copies, sh, tb):
    a, b = divmod(sh, 8)
    return copies[b][ODD_HALO - 8 * a:ODD_HALO - 8 * a + tb]


def _future(copies, sh, tb):
    a, b = divmod(sh, 8)
    return copies[b][8 * a:8 * a + tb]


def _odd_fwd(name, proj, sc_w, cf_w, cf_b, ln_g, ln_b):
    T = proj.shape[0]
    W = sc_w.shape[1]
    K3, K31 = sc_w.shape[0], cf_w.shape[0]
    tb = _tile(T, 256)
    hb = tb // ODD_HALO
    n = tb + ODD_HALO

    def body(cur_ref, prev_ref, w3_ref, w31_ref, cb_ref, lg_ref, lbeta_ref, y_ref, d_ref):
        keep = (pl.program_id(0) > 0).astype(F32)
        sb = cur_ref[:, 0:W]
        p = cur_ref[:, W:2 * W] * cur_ref[:, 2 * W:3 * W]
        glu = cur_ref[:, 3 * W:4 * W] * _sigmoid(cur_ref[:, 4 * W:5 * W])
        p_prev = prev_ref[:, W:2 * W] * prev_ref[:, 2 * W:3 * W] * keep
        glu_prev = prev_ref[:, 3 * W:4 * W] * _sigmoid(prev_ref[:, 4 * W:5 * W]) * keep
        ext = jnp.concatenate([p_prev, p], axis=0)
        cp = w3_ref[K3 - 1:K3, :] * p
        for sh in range(1, K3):
            cp = cp + w3_ref[K3 - 1 - sh:K3 - sh, :] * pltpu.roll(ext, sh, 0)[ODD_HALO:n]
        y_ref[:, 0:W] = (sb * cp).astype(y_ref.dtype)
        glu_past = _row_shifts(jnp.concatenate([glu_prev, glu], axis=0), up=False)
        d = cb_ref[...] + w31_ref[K31 - 1:K31, :] * glu
        for sh in range(1, K31):
            d = d + w31_ref[K31 - 1 - sh:K31 - sh, :] * _past(glu_past, sh, tb)
        d_ref[...] = d
        mu = jnp.mean(d, axis=1, keepdims=True)
        xc = d - mu
        rstd = lax.rsqrt(jnp.mean(xc * xc, axis=1, keepdims=True) + EPS)
        ln = (xc * rstd) * lg_ref[...] + lbeta_ref[...]
        y_ref[:, W:2 * W] = _silu(ln).astype(y_ref.dtype)

    full = lambda shape: pl.BlockSpec(shape, lambda i: tuple(0 for _ in shape))
    return pl.pallas_call(
        body, name=name, grid=(T // tb,),
        in_specs=[pl.BlockSpec((tb, 5 * W), lambda i: (i, 0)),
                  pl.BlockSpec((ODD_HALO, 5 * W), lambda i: (jnp.maximum(i * hb - 1, 0), 0)),
                  full((K3, W)), full((K31, W)), full((1, W)), full((1, W)), full((1, W))],
        out_specs=[pl.BlockSpec((tb, 2 * W), lambda i: (i, 0)), pl.BlockSpec((tb, W), lambda i: (i, 0))],
        out_shape=[jax.ShapeDtypeStruct((T, 2 * W), MXU_DTYPE), jax.ShapeDtypeStruct((T, W), F32)],
        compiler_params=_cp("parallel"),
    )(proj, proj, sc_w, cf_w, cf_b, ln_g, ln_b)


def _odd_bwd(name, proj, dy, dsave, sc_w, cf_w, ln_g, ln_b):
    T = proj.shape[0]
    W = sc_w.shape[1]
    K3, K31 = sc_w.shape[0], cf_w.shape[0]
    tb = _tile(T, 128)
    nb = T // tb
    hb = tb // ODD_HALO
    nh = T // ODD_HALO
    n = tb + ODD_HALO

    def body(cur_ref, prev_ref, next_ref, dy_ref, dyn_ref, d_ref, dn_ref,
             w3_ref, w31_ref, lg_ref, lbeta_ref,
             dp_ref, db_ref, dw3_ref, dw31_ref, dcb_ref, dlg_ref, dlb_ref):
        i = pl.program_id(0)

        @pl.when(i == 0)
        def _():
            for ref in (db_ref, dw3_ref, dw31_ref, dcb_ref, dlg_ref, dlb_ref):
                ref[...] = jnp.zeros_like(ref)

        keep_prev = (i > 0).astype(F32)
        keep_next = (i < nb - 1).astype(F32)
        sb = cur_ref[:, 0:W]
        scv = cur_ref[:, W:2 * W]
        svv = cur_ref[:, 2 * W:3 * W]
        cu = cur_ref[:, 3 * W:4 * W]
        sg = _sigmoid(cur_ref[:, 4 * W:5 * W])
        p = scv * svv
        glu = cu * sg
        p_prev = prev_ref[:, W:2 * W] * prev_ref[:, 2 * W:3 * W] * keep_prev
        glu_prev = prev_ref[:, 3 * W:4 * W] * _sigmoid(prev_ref[:, 4 * W:5 * W]) * keep_prev
        dext = jnp.concatenate([d_ref[...], dn_ref[...]], axis=0)
        dyd = jnp.concatenate([dy_ref[:, W:2 * W], dyn_ref[:, W:2 * W] * keep_next], axis=0)
        mu = jnp.mean(dext, axis=1, keepdims=True)
        xc = dext - mu
        rstd = lax.rsqrt(jnp.mean(xc * xc, axis=1, keepdims=True) + EPS)
        xh = xc * rstd
        lg = lg_ref[...]
        dln = dyd * _dsilu(xh * lg + lbeta_ref[...])
        dxh = dln * lg
        dd = rstd * (dxh - jnp.mean(dxh, axis=1, keepdims=True)
                     - xh * jnp.mean(dxh * xh, axis=1, keepdims=True))
        dlg_ref[...] += jnp.sum((dln * xh)[0:tb], axis=0, keepdims=True)
        dlb_ref[...] += jnp.sum(dln[0:tb], axis=0, keepdims=True)
        ddc = dd[0:tb]
        dcb_ref[...] += jnp.sum(ddc, axis=0, keepdims=True)
        dglu = w31_ref[K31 - 1:K31, :] * ddc
        glu_past = _row_shifts(jnp.concatenate([glu_prev, glu], axis=0), up=False)
        dd_future = _row_shifts(dd, up=True)
        dw31_ref[K31 - 1:K31, :] += jnp.sum(ddc * glu, axis=0, keepdims=True)
        for sh in range(1, K31):
            dglu = dglu + w31_ref[K31 - 1 - sh:K31 - sh, :] * _future(dd_future, sh, tb)
            dw31_ref[K31 - 1 - sh:K31 - sh, :] += jnp.sum(ddc * _past(glu_past, sh, tb), axis=0, keepdims=True)
        _put(dp_ref, db_ref, 3 * W, 4 * W, dglu * sg)
        _put(dp_ref, db_ref, 4 * W, 5 * W, dglu * cu * sg * (1.0 - sg))
        dyc = dy_ref[:, 0:W]
        dcp = jnp.concatenate([dyc * sb, dyn_ref[:, 0:W] * next_ref[:, 0:W] * keep_next], axis=0)
        dcpc = dcp[0:tb]
        ext = jnp.concatenate([p_prev, p], axis=0)
        cp = w3_ref[K3 - 1:K3, :] * p
        dpp = w3_ref[K3 - 1:K3, :] * dcpc
        dw3_ref[K3 - 1:K3, :] += jnp.sum(dcpc * p, axis=0, keepdims=True)
        up = dcp
        for sh in range(1, K3):
            up = pltpu.roll(up, n - 1, 0)
            ext = pltpu.roll(ext, 1, 0)
            shifted = ext[ODD_HALO:n]
            cp = cp + w3_ref[K3 - 1 - sh:K3 - sh, :] * shifted
            dpp = dpp + w3_ref[K3 - 1 - sh:K3 - sh, :] * up[0:tb]
            dw3_ref[K3 - 1 - sh:K3 - sh, :] += jnp.sum(dcpc * shifted, axis=0, keepdims=True)
        _put(dp_ref, db_ref, 0, W, dyc * cp)
        _put(dp_ref, db_ref, W, 2 * W, dpp * svv)
        _put(dp_ref, db_ref, 2 * W, 3 * W, dpp * scv)

    full = lambda shape: pl.BlockSpec(shape, lambda i: tuple(0 for _ in shape))
    prev_map = lambda i: (jnp.maximum(i * hb - 1, 0), 0)
    next_map = lambda i: (jnp.minimum((i + 1) * hb, nh - 1), 0)
    return pl.pallas_call(
        body, name=name, grid=(nb,),
        in_specs=[pl.BlockSpec((tb, 5 * W), lambda i: (i, 0)),
                  pl.BlockSpec((ODD_HALO, 5 * W), prev_map), pl.BlockSpec((ODD_HALO, 5 * W), next_map),
                  pl.BlockSpec((tb, 2 * W), lambda i: (i, 0)), pl.BlockSpec((ODD_HALO, 2 * W), next_map),
                  pl.BlockSpec((tb, W), lambda i: (i, 0)), pl.BlockSpec((ODD_HALO, W), next_map),
                  full((K3, W)), full((K31, W)), full((1, W)), full((1, W))],
        out_specs=[pl.BlockSpec((tb, 5 * W), lambda i: (i, 0)), full((1, 5 * W)), full((K3, W)), full((K31, W)),
                   full((1, W)), full((1, W)), full((1, W))],
        out_shape=[jax.ShapeDtypeStruct((T, 5 * W), MXU_DTYPE), jax.ShapeDtypeStruct((1, 5 * W), F32),
                   jax.ShapeDtypeStruct((K3, W), F32),
                   jax.ShapeDtypeStruct((K31, W), F32)] + [jax.ShapeDtypeStruct((1, W), F32)] * 3,
        compiler_params=_cp("arbitrary"),
    )(proj, proj, proj, dy, dy, dsave, dsave, sc_w, cf_w, ln_g, ln_b)


PACK_WIDTH = 1024


def _lower_bounds(logits):
    sm = jax.nn.softmax(logits.astype(F32), axis=0)
    return jnp.cumsum(sm, axis=0) - sm[0]


def _pack_rows(arrays):
    flat = jnp.concatenate([a.reshape(-1) for a in arrays])
    pad = (-flat.shape[0]) % (8 * PACK_WIDTH)
    return jnp.pad(flat, (0, pad)).reshape(-1, PACK_WIDTH)


def _unpack_rows(packed, shapes):
    flat = packed.reshape(-1)
    out, off = [], 0
    for s in shapes:
        sz = math.prod(s)
        out.append(flat[off:off + sz].reshape(s))
        off += sz
    return out


def _shards_last(a):
    n = a.shape[-1] // N_DEV
    return jnp.moveaxis(a.reshape(a.shape[:-1] + (N_DEV, n)), -2, 0)


def _unshard_last(a):
    a = jnp.moveaxis(a, 0, -2)
    return a.reshape(a.shape[:-2] + (a.shape[-2] * a.shape[-1],))


BIG = ("ev_w_in", "ev_w_out", "od_w_in", "od_w_out", "ffn_w_gate", "ffn_w_up", "ffn_w_down")
SMALL_SHARDED = ("lru_conv_w", "od_b_in", "sc_conv_w", "cf_conv_w", "cf_conv_b", "cf_ln_g", "cf_ln_b")
SMALL_REPL = ("ln_mix_g", "ln_ffn_g", "ln_final_g", "ev_b_in", "lru_conv_b", "lru_wa", "lru_ba", "lru_wx",
              "lru_bx", "lru_lambda", "hgrn_lb_logits", "hgrn_norm_g")
WEIGHTS = ("ln_mix_g", "ln_ffn_g", "ln_final_g", "ev_w_in", "ev_b_in", "lru_conv_w", "lru_conv_b", "lru_wa",
           "lru_ba", "lru_wx", "lru_bx", "lru_lambda", "hgrn_lb_logits", "hgrn_norm_g", "ev_w_out", "od_w_in",
           "od_b_in", "sc_conv_w", "cf_conv_w", "cf_conv_b", "cf_ln_g", "cf_ln_b", "od_w_out", "ffn_w_gate",
           "ffn_w_up", "ffn_w_down")


def _layer_weights(l):
    mix = ("ev_w_in", "ev_w_out") if l % 2 == 0 else ("od_w_in", "od_w_out")
    return [(mix[0], l // 2), (mix[1], l // 2), ("ffn_w_gate", l), ("ffn_w_up", l), ("ffn_w_down", l)]


class _MeshExchange:
    def __init__(self, w_bf, w, m, v, depth, first):
        self.w_bf, self.w, self.m, self.v, self.depth, self.first = w_bf, w, m, v, depth, tuple(first)
        self.ready, self.flight, self.rs, self.adam = {}, {}, {}, {}

    @staticmethod
    def _names(l, grp):
        names = _layer_weights(l)
        return names[:2] if grp == "mix" else names[2:]

    def _own_start(self, l, grp, deps):
        names = self._names(l, grp)
        srcs = [self.w_bf[k] for k, _ in names]
        layers = [j for _, j in names]
        plan = _plan_gather_own(layers)
        sems, lands, tok = _split_start(f"ag_own_start_{grp}{l}", plan, srcs, _place_own(srcs, layers), deps)
        self.flight[l, grp] = (plan, sems, srcs, lands)
        return tok

    def _turn(self, l, grp, deps):
        plan, sems, srcs, lands = self.flight[l, grp]
        lands = _split_wait(f"ag_own_wait_{grp}{l}", plan, sems, srcs, lands, deps)
        plan = _plan_gather_pass(len(lands))
        sems, passed, tok = _split_start(f"ag_pass_start_{grp}{l}", plan, [], lands)
        self.flight[l, grp] = (plan, sems, [], passed)
        toks = [tok]
        nl, ng = (l, "ffn") if grp == "mix" else (l + 1, "mix")
        if nl < self.depth:
            toks.append(self._own_start(nl, ng, (tok,)))
        return tuple(toks)

    def _pass_wait(self, l, grp, deps):
        plan, sems, srcs, lands = self.flight.pop((l, grp))
        lands = _split_wait(f"ag_pass_wait_{grp}{l}", plan, sems, srcs, lands, deps)
        self.ready[l, grp] = dict(zip([k for k, _ in self._names(l, grp)], lands))

    def layer_begin(self, l):
        toks = ()
        if l == 0:
            self._own_start(0, "mix", self.first)
            toks = self._turn(0, "mix", tuple(self.w_bf.values()))
            self._pass_wait(0, "mix", ())
        return self.ready.pop((l, "mix")), toks

    def tick(self, l, t, after):
        if t == 2:
            return self._turn(l, "ffn", (after,))
        if t == 3:
            self._pass_wait(l, "ffn", (after,))
        if t == 4 and l + 1 < self.depth:
            return self._turn(l + 1, "mix", (after,))
        if t == 5 and l + 1 < self.depth:
            self._pass_wait(l + 1, "mix", (after,))
        return ()

    def ffn_weights(self, l):
        return self.ready.pop((l, "ffn"))

    def grads(self, tag, named):
        srcs = [g for _, _, g in named]
        lands = [lax.empty((4,) + g.shape[1:], g.dtype) for g in srcs]
        plan = _plan_scatter_pair(len(srcs))
        sems, lands, tok = _split_start(f"rs_pair_start_{tag}", plan, srcs, lands)
        self.rs[tag] = (named, plan, sems, srcs, lands)
        return (tok,)

    def grads_mid(self, tag, after):
        named, plan, sems, srcs, lands = self.rs[tag]
        got = _split_wait(f"rs_pair_wait_{tag}", plan, sems, srcs, lands, (after,))
        both = [_pair_add(f"pair_add_{tag}_{i}", a, b) for i, (a, b) in enumerate(zip(srcs, got))]
        self.rs[tag] = (named, [p for p, _ in both], [f for _, f in both])

    def grads_send(self, tag):
        named, parts, fins = self.rs[tag]
        plan = _plan_scatter_chips(len(parts))
        sems, fins, tok = _split_start(f"rs_chip_start_{tag}", plan, parts, fins)
        self.rs[tag] = (named, plan, sems, parts, fins)
        return (tok,)

    def grads_end(self, tag, after):
        named, plan, sems, parts, fins = self.rs.pop(tag)
        fins = _split_wait(f"rs_chip_wait_{tag}", plan, sems, parts, fins, (after,))
        for (k, j, _), fin in zip(named, fins):
            self.adam[k] = _adamw(f"adamw_{k}_{j}", self.w[k], self.m[k], self.v[k], fin, j, self.adam.get(k))

    def finish(self, after):
        for tag in list(self.rs):
            self.grads_end(tag, after)

    def results(self, k):
        return [r.reshape(self.w[k].shape) for r in self.adam[k]]


def _local_step(x, tgt, p, ex):
    T, D = x.shape
    depth = p["ln_mix_g"].shape[0]
    lbs = _lower_bounds(p["hgrn_lb_logits"])
    row = lambda a: a.reshape(1, -1)
    saved = []
    h = None
    for l in range(depth):
        j = l // 2
        wl, tok = ex.layer_begin(l)
        s = {"x": x}
        if h is None:
            h = _rmsnorm_fwd(f"norm_mix{l}", x, row(p["ln_mix_g"][l]))
        s["h"] = h
        if l % 2 == 0:
            proj = _proj_in(f"ev_in{l}", h, wl["ev_w_in"], row(p["ev_b_in"][j]), tok)
            wa = p["lru_wa"][j].astype(MXU_DTYPE)
            wx = p["lru_wx"][j].astype(MXU_DTYPE)
            ya, xc, r, ig, hs = _lru_fwd(f"lru_fwd{l}", proj, p["lru_conv_w"][j], row(p["lru_conv_b"][j]),
                                         wa, row(p["lru_ba"][j]), wx, row(p["lru_bx"][j]),
                                         row(p["lru_lambda"][j]))
            y, o, sall = _hgrn_fwd(f"hgrn_fwd{l}", proj, row(lbs[j]), row(p["hgrn_norm_g"][j]), ya)
            s.update(proj=proj, xc=xc, r=r, ig=ig, hs=hs, o=o, sall=sall, wa=wa, wx=wx)
            w_out = wl["ev_w_out"]
        else:
            proj = _proj_in(f"od_in{l}", h, wl["od_w_in"], row(p["od_b_in"][j]), tok)
            y, dsave = _odd_fwd(f"odd_fwd{l}", proj, p["sc_conv_w"][j], p["cf_conv_w"][j],
                                row(p["cf_conv_b"][j]), row(p["cf_ln_g"][j]), row(p["cf_ln_b"][j]))
            s.update(proj=proj, dsave=dsave)
            w_out = wl["od_w_out"]
        x, h2 = _mix_out(f"mix_out{l}", y, w_out, x, row(p["ln_ffn_g"][l]), ex.tick(l, 2, y))
        s["y"] = y
        s["xmid"] = x
        ex.tick(l, 3, x)
        wl = {**wl, **ex.ffn_weights(l)}
        gate, up, hid = _ffn_in(f"ffn_in{l}", h2, wl["ffn_w_gate"], wl["ffn_w_up"])
        tok = ex.tick(l, 4, hid)
        if l + 1 < depth:
            x, h = _proj_out(f"ffn_out{l}", hid, wl["ffn_w_down"], x, tok, row(p["ln_mix_g"][l + 1]))
        else:
            x = _proj_out(f"ffn_out{l}", hid, wl["ffn_w_down"], x, tok)
        ex.tick(l, 5, x)
        s.update(h2=h2, gate=gate, up=up, hid=hid, w=wl)
        saved.append(s)

    loss, dx, dxb, dg_final = _loss_head("loss_head", x, row(p["ln_final_g"]), tgt)

    gs = {k: [None] * p[k].shape[0] for k in SMALL_REPL + SMALL_SHARDED if k not in ("ln_final_g", "hgrn_lb_logits")}
    d_lb = [None] * (depth // 2 + depth % 2)
    tok = ()
    pending = None
    for l in reversed(range(depth)):
        j = l // 2
        s = saved[l]
        wl = s["w"]
        ffn_shape = wl["ffn_w_gate"].shape[1:]
        dwd = _wgrad(f"ffn_dwd{l}", s["hid"], dxb, ffn_shape[1], ffn_shape[0])
        dgate, dup = _ffn_bwd_hidden(f"ffn_bwd_hid{l}", dxb, wl["ffn_w_down"], s["gate"], s["up"], tok)
        dwg = _wgrad(f"ffn_dwg{l}", s["h2"], dgate, *ffn_shape)
        dwu = _wgrad(f"ffn_dwu{l}", s["h2"], dup, *ffn_shape)
        tok = ex.grads(f"ffn{l}", [("ffn_w_down", l, dwd), ("ffn_w_gate", l, dwg), ("ffn_w_up", l, dwu)])
        dh2 = _bwd_in(f"ffn_dh{l}", [dgate, dup], [wl["ffn_w_gate"], wl["ffn_w_up"]], tok)
        dx, dxb, dg = _rmsnorm_bwd(f"norm_ffn_bwd{l}", s["xmid"], row(p["ln_ffn_g"][l]), dh2, dx)
        gs["ln_ffn_g"][l] = dg[0]
        ex.grads_mid(f"ffn{l}", dxb)
        if pending is not None:
            ex.grads_end(pending, dxb)
        tok = ex.grads_send(f"ffn{l}")
        w_in, w_out = ("ev_w_in", "ev_w_out") if l % 2 == 0 else ("od_w_in", "od_w_out")
        dwo = _wgrad(f"mix_dwo{l}", s["y"], dxb, *wl[w_out].shape[1:])
        if l % 2 == 0:
            dy = _bwd_out(f"ev_dy{l}", dxb, wl["ev_w_out"], tok)
            dph, db_h, dlb, dng = _hgrn_bwd(f"hgrn_bwd{l}", s["proj"], dy, s["o"], s["sall"], row(lbs[j]),
                                            row(p["hgrn_norm_g"][j]))
            dproj, db_a, d_cw, d_cb, d_wa, d_ba, d_wx, d_bx, d_lam = _lru_bwd(
                f"lru_bwd{l}", s["proj"], dy, s["xc"], s["r"], s["ig"], s["hs"], p["lru_conv_w"][j],
                s["wa"], s["wx"], row(p["lru_lambda"][j]), dph)
            gs["lru_conv_w"][j], gs["lru_conv_b"][j] = d_cw, d_cb[0]
            gs["lru_wa"][j], gs["lru_ba"][j] = d_wa, d_ba.reshape(p["lru_ba"].shape[1:])
            gs["lru_wx"][j], gs["lru_bx"][j] = d_wx, d_bx.reshape(p["lru_bx"].shape[1:])
            gs["lru_lambda"][j], gs["hgrn_norm_g"][j] = d_lam[0], dng[0]
            d_lb[j] = dlb[0]
            gs["ev_b_in"][j] = jnp.concatenate([db_a[0], db_h[0, db_a.shape[1]:]])
        else:
            dy = _bwd_out(f"od_dy{l}", dxb, wl["od_w_out"], tok)
            dproj, db_in, d_w3, d_w31, d_cfb, d_lg, d_lbeta = _odd_bwd(
                f"odd_bwd{l}", s["proj"], dy, s["dsave"], p["sc_conv_w"][j], p["cf_conv_w"][j],
                row(p["cf_ln_g"][j]), row(p["cf_ln_b"][j]))
            gs["sc_conv_w"][j], gs["cf_conv_w"][j] = d_w3, d_w31
            gs["cf_conv_b"][j], gs["cf_ln_g"][j], gs["cf_ln_b"][j] = d_cfb[0], d_lg[0], d_lbeta[0]
            gs["od_b_in"][j] = db_in[0]
        dwi = _wgrad(f"mix_dwi{l}", s["h"], dproj, *wl[w_in].shape[1:])
        tok = ex.grads(f"mix{l}", [(w_out, j, dwo), (w_in, j, dwi)])
        dh = _bwd_in(f"mix_dh{l}", [dproj], [wl[w_in]], tok)
        dx, dxb, dg = _rmsnorm_bwd(f"norm_mix_bwd{l}", s["x"], row(p["ln_mix_g"][l]), dh, dx)
        gs["ln_mix_g"][l] = dg[0]
        ex.grads_mid(f"mix{l}", dxb)
        if l > 0:
            ex.grads_end(f"ffn{l}", dxb)
            tok = ex.grads_send(f"mix{l}")
        else:
            tok = ex.grads_send(f"mix{l}")
            ex.grads_end(f"ffn{l}", dxb)
        pending = f"mix{l}"

    small = {k: jnp.stack(v) for k, v in gs.items()}
    small["ln_final_g"] = dg_final[0]
    _, lb_vjp = jax.vjp(_lower_bounds, p["hgrn_lb_logits"])
    small["hgrn_lb_logits"] = lb_vjp(jnp.stack(d_lb))[0]
    return loss, dx, small


def kernel(x, ln_mix_g, ln_ffn_g, ln_final_g, ev_w_in, ev_b_in, lru_conv_w, lru_conv_b, lru_wa, lru_ba, lru_wx, lru_bx, lru_lambda, hgrn_lb_logits, hgrn_norm_g, ev_w_out, od_w_in, od_b_in, sc_conv_w, cf_conv_w, cf_conv_b, cf_ln_g, cf_ln_b, od_w_out, ffn_w_gate, ffn_w_up, ffn_w_down, loss_target, m_ln_mix_g, m_ln_ffn_g, m_ln_final_g, m_ev_w_in, m_ev_b_in, m_lru_conv_w, m_lru_conv_b, m_lru_wa, m_lru_ba, m_lru_wx, m_lru_bx, m_lru_lambda, m_hgrn_lb_logits, m_hgrn_norm_g, m_ev_w_out, m_od_w_in, m_od_b_in, m_sc_conv_w, m_cf_conv_w, m_cf_conv_b, m_cf_ln_g, m_cf_ln_b, m_od_w_out, m_ffn_w_gate, m_ffn_w_up, m_ffn_w_down, v_ln_mix_g, v_ln_ffn_g, v_ln_final_g, v_ev_w_in, v_ev_b_in, v_lru_conv_w, v_lru_conv_b, v_lru_wa, v_lru_ba, v_lru_wx, v_lru_bx, v_lru_lambda, v_hgrn_lb_logits, v_hgrn_norm_g, v_ev_w_out, v_od_w_in, v_od_b_in, v_sc_conv_w, v_cf_conv_w, v_cf_conv_b, v_cf_ln_g, v_cf_ln_b, v_od_w_out, v_ffn_w_gate, v_ffn_w_up, v_ffn_w_down):
    args = locals()
    w = {k: args[k] for k in WEIGHTS}
    m = {k: args["m_" + k] for k in WEIGHTS}
    v = {k: args["v_" + k] for k in WEIGHTS}
    assert x.shape[0] == 1
    T, D = x.shape[1:]

    local_shapes = [w[k].shape for k in SMALL_SHARDED]
    gathered = _all_gather("gather_small_params", [_pack_rows([w[k] for k in SMALL_SHARDED])])[0]
    p = {k: w[k] for k in SMALL_REPL}
    per_dev = [_unpack_rows(gathered[s], local_shapes) for s in range(N_DEV)]
    for i, k in enumerate(SMALL_SHARDED):
        p[k] = _unshard_last(jnp.stack([per_dev[s][i] for s in range(N_DEV)]))

    ex = _MeshExchange({k: w[k].astype(MXU_DTYPE) for k in BIG}, w, m, v, ln_mix_g.shape[0], [gathered])
    loss_part, dx, small = _local_step(x[0], loss_target[0], p, ex)
    loss = lax.psum(loss_part[0, 0], ("x", "y", "c"))

    small_sh = jnp.stack([_pack_rows([_shards_last(small[k])[s] for k in SMALL_SHARDED]) for s in range(N_DEV)])
    got = _pair_exchange("small_grads_to_sibling", [small_sh])[0]
    final_small = _chip_exchange("small_grads_to_chips", [_pair_add("pair_add_small", small_sh, got)[0]])[0]
    repl_parts = _all_gather("gather_small_grads", [_pack_rows([small[k] for k in SMALL_REPL])])[0]

    out_g, out_d, out_m, out_v = {}, {}, {}, {}
    res = _adamw("adamw_small_sharded", *[_pack_rows([t[k] for k in SMALL_SHARDED])[None] for t in (w, m, v)],
                 final_small)
    for o, r in zip((out_g, out_d, out_m, out_v), res):
        o.update(zip(SMALL_SHARDED, _unpack_rows(r, local_shapes)))
    res = _adamw("adamw_small_repl", *[_pack_rows([t[k] for k in SMALL_REPL])[None] for t in (w, m, v)], repl_parts)
    for o, r in zip((out_g, out_d, out_m, out_v), res):
        o.update(zip(SMALL_REPL, _unpack_rows(r, [w[k].shape for k in SMALL_REPL])))
    ex.finish(res[0])
    for k in BIG:
        out_g[k], out_d[k], out_m[k], out_v[k] = ex.results(k)

    return (loss, dx[None], *[out_g[k] for k in WEIGHTS], *[out_d[k] for k in WEIGHTS],
            *[out_m[k] for k in WEIGHTS], *[out_v[k] for k in WEIGHTS])
```

```python
import functools
import math

import jax
import jax.numpy as jnp
from jax import lax
from jax.experimental import pallas as pl
from jax.experimental.pallas import tpu as pltpu

F32 = jnp.float32
MXU_DTYPE = jnp.bfloat16
WIRE_DTYPE = jnp.bfloat16
N_DEV = 8
EPS = 1e-6
F_FLOOR = 1e-30
LRU_C = 8.0
HGRN_HEADS = 8
HGRN_SUB = 16
ADAM_LR, ADAM_B1, ADAM_B2, ADAM_EPS, ADAM_WD, ADAM_STEP = 0.001, 0.9, 0.999, 1e-08, 0.01, 10
V7X_VMEM_LIMIT = 48 * 1024 * 1024
MM_ROWS = 1024
WGRAD_ROWS = 2048
MESH = pl.DeviceIdType.MESH
ANY = pl.BlockSpec(memory_space=pl.ANY)


def _cp(*sem):
    return pltpu.CompilerParams(dimension_semantics=sem or None, vmem_limit_bytes=V7X_VMEM_LIMIT)


def _sigmoid(x):
    return 1.0 / (1.0 + jnp.exp(-x))


def _silu(x):
    return x * _sigmoid(x)


def _dsilu(x):
    s = _sigmoid(x)
    return s * (1.0 + x * (1.0 - s))


_GELU_C = math.sqrt(2.0 / math.pi)
LOG2E = 1.0 / math.log(2.0)


def _gelu(x):
    return 0.5 * x * (1.0 + jnp.tanh(_GELU_C * (x + 0.044715 * x * x * x)))


def _dgelu(x):
    t = jnp.tanh(_GELU_C * (x + 0.044715 * x * x * x))
    return 0.5 * (1.0 + t) + 0.5 * x * (1.0 - t * t) * _GELU_C * (1.0 + 3.0 * 0.044715 * x * x)


def _log1p(e):
    return jnp.where(e < 1e-2, e * (1.0 - e * (0.5 - e * (1.0 / 3.0))), jnp.log(1.0 + e))


def _softplus(x):
    return jnp.maximum(x, 0.0) + _log1p(jnp.exp(-jnp.abs(x)))


def _one_minus_exp(x):
    series = -x * (1.0 + x * (0.5 + x * (1.0 / 6.0 + x * (1.0 / 24.0))))
    return jnp.where(x > -0.05, series, 1.0 - jnp.exp(x))


def _rows(n, d=1):
    return lax.broadcasted_iota(jnp.int32, (n, d), 0)


def _dot(a, b):
    return jnp.dot(a.astype(MXU_DTYPE), b.astype(MXU_DTYPE), preferred_element_type=F32)


def _dot_nt(a, b):
    return lax.dot_general(a.astype(MXU_DTYPE), b.astype(MXU_DTYPE), (((1,), (1,)), ((), ())),
                           preferred_element_type=F32)


def _dot_tn(a, b):
    return lax.dot_general(a.astype(MXU_DTYPE), b.astype(MXU_DTYPE), (((0,), (0,)), ((), ())),
                           preferred_element_type=F32)


def _tile(n, want):
    if n <= want:
        return n
    t = want - want % 8
    while n % t:
        t -= 8
    assert t > 0, (n, want)
    return t


def _my_place():
    x, y, c = lax.axis_index("x"), lax.axis_index("y"), lax.axis_index("c")
    return x, y, c


def _all_gather(name, srcs):
    n = len(srcs)

    def body(*refs):
        src_refs, out_refs = refs[:n], refs[n:2 * n]
        send_sems, recv_sems, local_sems = refs[2 * n:]
        x, y, c = _my_place()
        sibling = (x, y, 1 - c)
        chips = [(1 - x, y), (x, 1 - y), (1 - x, 1 - y)]

        def slot(px, py, pc):
            return 4 * px + 2 * py + pc

        def copy(i, k, block, to, src=None):
            dst = out_refs[i].at[slot(*block)]
            return pltpu.make_async_remote_copy(
                src_ref=dst if src is None else src, dst_ref=dst,
                send_sem=send_sems.at[i, k], recv_sem=recv_sems.at[i, k],
                device_id=to, device_id_type=MESH)

        me = (x, y, c)
        sends, own = [], []
        for i in range(n):
            mine = pltpu.make_async_copy(src_refs[i], out_refs[i].at[slot(*me)], local_sems.at[i])
            mine.start()
            own.append(mine)
            first = [copy(i, 0, me, sibling, src=src_refs[i])]
            first += [copy(i, 1 + j, me, (*chip, c), src=src_refs[i]) for j, chip in enumerate(chips)]
            for cp in first:
                cp.start()
            sends += first
        for i in range(n):
            for j, chip in enumerate(chips):
                copy(i, 1 + j, (*chip, c), me).wait_recv()
                passed = copy(i, 4 + j, (*chip, c), sibling)
                passed.start()
                sends.append(passed)
        for i in range(n):
            copy(i, 0, sibling, me).wait_recv()
            for j, chip in enumerate(chips):
                copy(i, 4 + j, (*chip, 1 - c), me).wait_recv()
        for cp in sends:
            cp.wait_send()
        for cp in own:
            cp.wait()

    outs = pl.pallas_call(
        body, name=name,
        out_shape=[jax.ShapeDtypeStruct((N_DEV,) + s.shape, s.dtype) for s in srcs],
        in_specs=[ANY] * n, out_specs=[ANY] * n,
        scratch_shapes=[pltpu.SemaphoreType.DMA((n, 7)), pltpu.SemaphoreType.DMA((n, 7)),
                        pltpu.SemaphoreType.DMA((n,))],
    )(*srcs)
    return list(outs)


def _pair_exchange(name, srcs):
    n = len(srcs)

    def body(*refs):
        src_refs, out_refs = refs[:n], refs[n:2 * n]
        send_sems, recv_sems = refs[2 * n:]
        x, y, c = _my_place()
        copies = []
        for i in range(n):
            for j in range(4):
                cp = pltpu.make_async_remote_copy(
                    src_ref=src_refs[i].at[2 * j + (1 - c)], dst_ref=out_refs[i].at[j],
                    send_sem=send_sems.at[i, j], recv_sem=recv_sems.at[i, j],
                    device_id=(x, y, 1 - c), device_id_type=MESH)
                cp.start()
                copies.append(cp)
        for cp in copies:
            cp.wait()

    outs = pl.pallas_call(
        body, name=name,
        out_shape=[jax.ShapeDtypeStruct((4,) + s.shape[1:], s.dtype) for s in srcs],
        in_specs=[ANY] * n, out_specs=[ANY] * n,
        scratch_shapes=[pltpu.SemaphoreType.DMA((n, 4)), pltpu.SemaphoreType.DMA((n, 4))],
    )(*srcs)
    return list(outs)


def _chip_exchange(name, srcs):
    n = len(srcs)

    def body(*refs):
        src_refs, out_refs = refs[:n], refs[n:2 * n]
        send_sems, recv_sems, local_sems = refs[2 * n:]
        x, y, c = _my_place()
        chip = 2 * x + y
        copies = []
        for i in range(n):
            mine = pltpu.make_async_copy(src_refs[i].at[chip], out_refs[i].at[3], local_sems.at[i])
            mine.start()
            copies.append(mine)
            for k, (fx, fy) in enumerate([(1, 0), (0, 1), (1, 1)]):
                px = x + fx - 2 * x * fx
                py = y + fy - 2 * y * fy
                cp = pltpu.make_async_remote_copy(
                    src_ref=src_refs[i].at[2 * px + py], dst_ref=out_refs[i].at[k],
                    send_sem=send_sems.at[i, k], recv_sem=recv_sems.at[i, k],
                    device_id=(px, py, c), device_id_type=MESH)
                cp.start()
                copies.append(cp)
        for cp in copies:
            cp.wait()

    outs = pl.pallas_call(
        body, name=name,
        out_shape=[jax.ShapeDtypeStruct(s.shape, s.dtype) for s in srcs],
        in_specs=[ANY] * n, out_specs=[ANY] * n,
        scratch_shapes=[pltpu.SemaphoreType.DMA((n, 3)), pltpu.SemaphoreType.DMA((n, 3)),
                        pltpu.SemaphoreType.DMA((n,))],
    )(*srcs)
    return list(outs)


def _pair_add(name, mine, got):
    assert mine.shape[0] == N_DEV and got.shape[0] == 4
    cdim = mine.shape[-1]
    m4 = mine.reshape(4, 2, -1, cdim)
    g3 = got.reshape(4, -1, cdim)
    rows = m4.shape[2]
    tr = _tile(rows, 512)

    def body(m_ref, g_ref, o_ref, fin_ref):
        x, y, c = _my_place()
        s = (m_ref[c].astype(F32) + g_ref[...].astype(F32)).astype(o_ref.dtype)
        o_ref[...] = s

        @pl.when(pl.program_id(1) == 2 * x + y)
        def _():
            fin_ref[...] = s

    out, fin = pl.pallas_call(
        body, name=name, grid=(rows // tr, 4),
        in_specs=[pl.BlockSpec((None, 2, tr, cdim), lambda i, j: (j, 0, i, 0)),
                  pl.BlockSpec((None, tr, cdim), lambda i, j: (j, i, 0))],
        out_specs=[pl.BlockSpec((None, tr, cdim), lambda i, j: (j, i, 0)),
                   pl.BlockSpec((None, tr, cdim), lambda i, j: (3, i, 0))],
        out_shape=[jax.ShapeDtypeStruct(g3.shape, got.dtype)] * 2,
        compiler_params=_cp("parallel", "arbitrary"),
    )(m4, g3)
    return out.reshape(got.shape), fin.reshape(got.shape)


HBM = pl.BlockSpec(memory_space=pltpu.HBM)
SEM = pl.BlockSpec(memory_space=pltpu.SEMAPHORE)
EFFECT = pltpu.SideEffectType.DATAFLOW_SIDE_EFFECTING
SLOTS = 4


def _hbm(a):
    return pltpu.with_memory_space_constraint(a, pltpu.HBM)


def _remote(src, dst, sems, i, k, to):
    return pltpu.make_async_remote_copy(src_ref=src, dst_ref=dst, send_sem=sems[0].at[i * SLOTS + k],
                                        recv_sem=sems[1].at[i * SLOTS + k], device_id=to, device_id_type=MESH)


def _slot(px, py, pc):
    return 4 * px + 2 * py + pc


def _other_chips(x, y):
    return [(1 - x, y), (x, 1 - y), (1 - x, 1 - y)]


def _plan_gather_own(layers):
    def plan(srcs, lands, sems):
        x, y, c = _my_place()
        out = []
        for i, j in enumerate(layers):
            dst = lands[i].at[_slot(x, y, c)]
            out.append(_remote(srcs[i].at[j], dst, sems, i, 0, (x, y, 1 - c)))
            for k, (px, py) in enumerate(_other_chips(x, y)):
                out.append(_remote(srcs[i].at[j], dst, sems, i, 1 + k, (px, py, c)))
        return out
    return plan


def _plan_gather_pass(n):
    def plan(srcs, lands, sems):
        x, y, c = _my_place()
        out = []
        for i in range(n):
            for k, (px, py) in enumerate(_other_chips(x, y)):
                blk = lands[i].at[_slot(px, py, c)]
                out.append(_remote(blk, blk, sems, i, k, (x, y, 1 - c)))
        return out
    return plan


def _plan_scatter_pair(n):
    def plan(srcs, lands, sems):
        x, y, c = _my_place()
        return [_remote(srcs[i].at[2 * j + (1 - c)], lands[i].at[j], sems, i, j, (x, y, 1 - c))
                for i in range(n) for j in range(4)]
    return plan


def _plan_scatter_chips(n):
    def plan(srcs, lands, sems):
        x, y, c = _my_place()
        return [_remote(srcs[i].at[2 * px + py], lands[i].at[k], sems, i, k, (px, py, c))
                for i in range(n) for k, (px, py) in enumerate(_other_chips(x, y))]
    return plan


def _split_start(name, plan, srcs, lands, deps=()):
    ns, nl, nd = len(srcs), len(lands), len(deps)
    n = max(ns, nl)

    def body(*refs):
        sems = refs[ns + nl + nd:ns + nl + nd + 2]
        for cp in plan(refs[:ns], refs[ns:ns + nl], sems):
            cp.start()
        refs[-1][...] = jnp.zeros_like(refs[-1])

    outs = pl.pallas_call(
        body, name=name,
        out_shape=(pltpu.SemaphoreType.DMA((n * SLOTS,)), pltpu.SemaphoreType.DMA((n * SLOTS,)),
                   *[pltpu.HBM(a.shape, a.dtype) for a in lands], jax.ShapeDtypeStruct((8, 128), F32)),
        in_specs=[HBM] * (ns + nl) + [ANY] * nd,
        out_specs=(SEM, SEM, *[HBM] * nl, pl.BlockSpec(memory_space=pltpu.VMEM)),
        input_output_aliases={ns + i: 2 + i for i in range(nl)},
        compiler_params=pltpu.CompilerParams(has_side_effects=EFFECT),
    )(*[_hbm(a) for a in srcs], *[_hbm(a) for a in lands], *deps)
    return (outs[0], outs[1]), list(outs[2:2 + nl]), outs[-1]


def _split_wait(name, plan, sems, srcs, lands, deps=()):
    ns, nl, nd = len(srcs), len(lands), len(deps)

    def body(*refs):
        for cp in plan(refs[:ns], refs[ns:ns + nl], refs[ns + nl:ns + nl + 2]):
            cp.wait_send()
            cp.wait_recv()

    outs = pl.pallas_call(
        body, name=name,
        out_shape=tuple(pltpu.HBM(a.shape, a.dtype) for a in lands),
        in_specs=[HBM] * (ns + nl) + [SEM, SEM] + [ANY] * nd,
        out_specs=tuple([HBM] * nl),
        input_output_aliases={ns + i: i for i in range(nl)},
        compiler_params=pltpu.CompilerParams(has_side_effects=EFFECT),
    )(*srcs, *lands, *sems, *deps)
    return list(outs)


def _place_own(srcs, layers):
    x, y, c = _my_place()
    zero = jnp.zeros((), jnp.int32)
    return [lax.dynamic_update_slice(lax.empty((N_DEV,) + a.shape[1:], a.dtype), a[j][None],
                                     (_slot(x, y, c),) + (zero,) * (a.ndim - 1))
            for a, j in zip(srcs, layers)]


def _put(dp_ref, db_ref, lo, hi, val):
    dp_ref[:, lo:hi] = val.astype(dp_ref.dtype)
    db_ref[:, lo:hi] += jnp.sum(val, axis=0, keepdims=True)


def _with_deps(body, n_in, deps):
    nd = len(deps)
    if not nd:
        return body

    def wrapped(*refs):
        body(*refs[:n_in], *refs[n_in + nd:])

    return wrapped


def _rmsnorm_fwd(name, x, g, deps=()):
    T, D = x.shape
    tm = _tile(T, 256)

    def body(x_ref, g_ref, o_ref):
        xv = x_ref[...]
        r = lax.rsqrt(jnp.mean(xv * xv, axis=-1, keepdims=True) + EPS)
        o_ref[...] = ((xv * r) * g_ref[...]).astype(o_ref.dtype)

    return pl.pallas_call(
        _with_deps(body, 2, deps), name=name, grid=(T // tm,),
        in_specs=[pl.BlockSpec((tm, D), lambda i: (i, 0)), pl.BlockSpec((1, D), lambda i: (0, 0))]
        + [ANY] * len(deps),
        out_specs=pl.BlockSpec((tm, D), lambda i: (i, 0)),
        out_shape=jax.ShapeDtypeStruct((T, D), MXU_DTYPE), compiler_params=_cp("parallel"),
    )(x, g, *deps)


def _rmsnorm_bwd(name, x, g, dh, dres):
    T, D = x.shape
    tm = _tile(T, 256)

    def body(x_ref, g_ref, dh_ref, dres_ref, dx_ref, dxb_ref, dg_ref):
        xv = x_ref[...]
        r = lax.rsqrt(jnp.mean(xv * xv, axis=-1, keepdims=True) + EPS)
        xh = xv * r
        dhv = dh_ref[...]

        @pl.when(pl.program_id(0) == 0)
        def _():
            dg_ref[...] = jnp.zeros_like(dg_ref)

        dg_ref[...] += jnp.sum(dhv * xh, axis=0, keepdims=True)
        dxh = dhv * g_ref[...]
        dx = dres_ref[...] + r * (dxh - xh * jnp.mean(dxh * xh, axis=-1, keepdims=True))
        dx_ref[...] = dx
        dxb_ref[...] = dx.astype(dxb_ref.dtype)

    return pl.pallas_call(
        body, name=name, grid=(T // tm,),
        in_specs=[pl.BlockSpec((tm, D), lambda i: (i, 0)), pl.BlockSpec((1, D), lambda i: (0, 0)),
                  pl.BlockSpec((tm, D), lambda i: (i, 0)), pl.BlockSpec((tm, D), lambda i: (i, 0))],
        out_specs=[pl.BlockSpec((tm, D), lambda i: (i, 0)), pl.BlockSpec((tm, D), lambda i: (i, 0)),
                   pl.BlockSpec((1, D), lambda i: (0, 0))],
        out_shape=[jax.ShapeDtypeStruct((T, D), F32), jax.ShapeDtypeStruct((T, D), MXU_DTYPE),
                   jax.ShapeDtypeStruct((1, D), F32)],
        compiler_params=_cp("arbitrary"),
    )(x, g, dh, dres)


def _loss_head(name, x, g, tgt):
    T, D = x.shape
    tm = _tile(T, 256)

    def body(x_ref, g_ref, t_ref, loss_ref, dx_ref, dxb_ref, dg_ref):
        xv = x_ref[...]
        r = lax.rsqrt(jnp.mean(xv * xv, axis=-1, keepdims=True) + EPS)
        xh = xv * r
        gv = g_ref[...]
        diff = xh * gv - t_ref[...]

        @pl.when(pl.program_id(0) == 0)
        def _():
            dg_ref[...] = jnp.zeros_like(dg_ref)
            loss_ref[...] = jnp.zeros_like(loss_ref)

        part = 0.5 * jnp.sum(jnp.mean(diff * diff, axis=-1, keepdims=True), axis=0, keepdims=True)
        loss_ref[...] += jnp.broadcast_to(part, loss_ref.shape)
        dy = diff * (1.0 / D)
        dg_ref[...] += jnp.sum(dy * xh, axis=0, keepdims=True)
        dxh = dy * gv
        dx = r * (dxh - xh * jnp.mean(dxh * xh, axis=-1, keepdims=True))
        dx_ref[...] = dx
        dxb_ref[...] = dx.astype(dxb_ref.dtype)

    return pl.pallas_call(
        body, name=name, grid=(T // tm,),
        in_specs=[pl.BlockSpec((tm, D), lambda i: (i, 0)), pl.BlockSpec((1, D), lambda i: (0, 0)),
                  pl.BlockSpec((tm, D), lambda i: (i, 0))],
        out_specs=[pl.BlockSpec((1, 128), lambda i: (0, 0)), pl.BlockSpec((tm, D), lambda i: (i, 0)),
                   pl.BlockSpec((tm, D), lambda i: (i, 0)), pl.BlockSpec((1, D), lambda i: (0, 0))],
        out_shape=[jax.ShapeDtypeStruct((1, 128), F32), jax.ShapeDtypeStruct((T, D), F32),
                   jax.ShapeDtypeStruct((T, D), MXU_DTYPE), jax.ShapeDtypeStruct((1, D), F32)],
        compiler_params=_cp("arbitrary"),
    )(x, g, tgt)


def _adamw(name, w, m, v, parts, j=0, prev=None):
    L, R, C = w.shape
    P = parts.shape[0]
    tr = _tile(R, 256)
    ob = j * (R // tr)
    w, m, v = (a.reshape(L * R, C) for a in (w, m, v))
    c1 = 1.0 / (1.0 - ADAM_B1 ** ADAM_STEP)
    c2 = 1.0 / (1.0 - ADAM_B2 ** ADAM_STEP)
    chained = L > 1
    if chained and prev is None:
        prev = [lax.empty(w.shape, F32) for _ in range(4)]
    prev = list(prev) if chained else []

    def body(w_ref, m_ref, v_ref, p_ref, *rest):
        g_ref, d_ref, nm_ref, nv_ref = rest[len(prev):]
        g = p_ref[0].astype(F32)
        for s in range(1, P):
            g = g + p_ref[s].astype(F32)
        nm = ADAM_B1 * m_ref[...] + (1.0 - ADAM_B1) * g
        nv = ADAM_B2 * v_ref[...] + (1.0 - ADAM_B2) * (g * g)
        g_ref[...] = g
        nm_ref[...] = nm
        nv_ref[...] = nv
        d_ref[...] = -ADAM_LR * ((nm * c1) / (jnp.sqrt(nv * c2) + ADAM_EPS) + ADAM_WD * w_ref[...])

    blk = pl.BlockSpec((tr, C), lambda i: (i + ob, 0))
    return pl.pallas_call(
        body, name=name, grid=(R // tr,),
        in_specs=[blk, blk, blk, pl.BlockSpec((P, tr, C), lambda i: (0, i, 0))] + [ANY] * len(prev),
        out_specs=[blk, blk, blk, blk],
        out_shape=[jax.ShapeDtypeStruct(w.shape, F32)] * 4,
        input_output_aliases={4 + i: i for i in range(len(prev))},
        compiler_params=_cp("parallel"),
    )(w, m, v, parts, *prev)


def _proj_in(name, h, wg, bias, deps=()):
    T, K = h.shape
    n = wg.shape[-1]
    tm = _tile(T, MM_ROWS)

    def body(a_ref, w_ref, b_ref, o_ref):
        o_ref[...] = _dot(a_ref[...], w_ref[...]) + b_ref[...]

    return pl.pallas_call(
        _with_deps(body, 3, deps), name=name, grid=(N_DEV, T // tm),
        in_specs=[pl.BlockSpec((tm, K), lambda s, i: (i, 0)),
                  pl.BlockSpec((None, K, n), lambda s, i: (s, 0, 0)),
                  pl.BlockSpec((1, n), lambda s, i: (0, s))] + [ANY] * len(deps),
        out_specs=pl.BlockSpec((tm, n), lambda s, i: (i, s)),
        out_shape=jax.ShapeDtypeStruct((T, N_DEV * n), F32), compiler_params=_cp("parallel", "parallel"),
    )(h, wg, bias, *deps)


def _ffn_in(name, h, wg_gate, wg_up):
    T, K = h.shape
    n = wg_gate.shape[-1]
    tm = _tile(T, MM_ROWS)

    def body(a_ref, wgt_ref, wup_ref, g_ref, u_ref, hid_ref):
        a = a_ref[...]
        g = _dot(a, wgt_ref[...])
        u = _dot(a, wup_ref[...])
        g_ref[...] = g
        u_ref[...] = u
        hid_ref[...] = (_silu(g) * u).astype(hid_ref.dtype)

    wspec = pl.BlockSpec((None, K, n), lambda s, i: (s, 0, 0))
    ospec = pl.BlockSpec((None, tm, n), lambda s, i: (s, i, 0))
    return pl.pallas_call(
        body, name=name, grid=(N_DEV, T // tm),
        in_specs=[pl.BlockSpec((tm, K), lambda s, i: (i, 0)), wspec, wspec],
        out_specs=[ospec, ospec, ospec],
        out_shape=[jax.ShapeDtypeStruct((N_DEV, T, n), F32), jax.ShapeDtypeStruct((N_DEV, T, n), F32),
                   jax.ShapeDtypeStruct((N_DEV, T, n), MXU_DTYPE)],
        compiler_params=_cp("parallel", "parallel"),
    )(h, wg_gate, wg_up)


def _a_spec(a, tm, k):
    if a.ndim == 2:
        return pl.BlockSpec((tm, k), lambda i, s: (i, s))
    return pl.BlockSpec((None, tm, k), lambda i, s: (s, i, 0))


def _proj_out(name, a, wg, res, deps=(), norm_g=None):
    k, N = wg.shape[-2:]
    T = res.shape[0]
    tm = _tile(T, MM_ROWS // 2)
    extra = [] if norm_g is None else [norm_g]

    def body(a_ref, w_ref, r_ref, *rest):
        o_ref = rest[len(extra)]
        p = _dot(a_ref[...], w_ref[...])

        @pl.when(pl.program_id(1) == 0)
        def _():
            o_ref[...] = r_ref[...] + p

        @pl.when(pl.program_id(1) > 0)
        def _():
            o_ref[...] += p

        if extra:
            @pl.when(pl.program_id(1) == N_DEV - 1)
            def _():
                xv = o_ref[...]
                r = lax.rsqrt(jnp.mean(xv * xv, axis=-1, keepdims=True) + EPS)
                rest[2][...] = ((xv * r) * rest[0][...]).astype(rest[2].dtype)

    row_blk = pl.BlockSpec((tm, N), lambda i, s: (i, 0))
    out = pl.pallas_call(
        _with_deps(body, 3 + len(extra), deps), name=name, grid=(T // tm, N_DEV),
        in_specs=[_a_spec(a, tm, k), pl.BlockSpec((None, k, N), lambda i, s: (s, 0, 0)), row_blk]
        + [pl.BlockSpec((1, N), lambda i, s: (0, 0))] * len(extra) + [ANY] * len(deps),
        out_specs=[row_blk] * (1 + len(extra)),
        out_shape=[jax.ShapeDtypeStruct((T, N), F32)] + [jax.ShapeDtypeStruct((T, N), MXU_DTYPE)] * len(extra),
        compiler_params=_cp("parallel", "arbitrary"),
    )(a, wg, res, *extra, *deps)
    return out[0] if norm_g is None else out


def _mix_out(name, y, wg, res, norm_g, deps=()):
    N = wg.shape[-1]
    w2 = wg.reshape(-1, N)
    K = w2.shape[0]
    T = res.shape[0]
    tm = _tile(T, MM_ROWS // 2)

    def body(a_ref, w_ref, r_ref, g_ref, o_ref, h_ref):
        xv = r_ref[...] + _dot(a_ref[...], w_ref[...])
        o_ref[...] = xv
        r = lax.rsqrt(jnp.mean(xv * xv, axis=-1, keepdims=True) + EPS)
        h_ref[...] = ((xv * r) * g_ref[...]).astype(h_ref.dtype)

    row_blk = pl.BlockSpec((tm, N), lambda i: (i, 0))
    return pl.pallas_call(
        _with_deps(body, 4, deps), name=name, grid=(T // tm,),
        in_specs=[pl.BlockSpec((tm, K), lambda i: (i, 0)),
                  pl.BlockSpec((K, N), lambda i: (0, 0), pipeline_mode=pl.Buffered(1)),
                  row_blk, pl.BlockSpec((1, N), lambda i: (0, 0))] + [ANY] * len(deps),
        out_specs=[row_blk, row_blk],
        out_shape=[jax.ShapeDtypeStruct((T, N), F32), jax.ShapeDtypeStruct((T, N), MXU_DTYPE)],
        compiler_params=_cp("parallel"),
    )(y, w2, res, norm_g, *deps)


def _bwd_in(name, das, wgs, deps=()):
    K, n = wgs[0].shape[-2:]
    T = das[0].shape[-2]
    tm = _tile(T, MM_ROWS)
    npair = len(das)

    def body(*refs):
        o_ref = refs[-1]
        p = _dot_nt(refs[0][...], refs[npair][...])
        for q in range(1, npair):
            p = p + _dot_nt(refs[q][...], refs[npair + q][...])

        @pl.when(pl.program_id(1) == 0)
        def _():
            o_ref[...] = p

        @pl.when(pl.program_id(1) > 0)
        def _():
            o_ref[...] += p

    return pl.pallas_call(
        _with_deps(body, 2 * npair, deps), name=name, grid=(T // tm, N_DEV),
        in_specs=[_a_spec(a, tm, n) for a in das]
        + [pl.BlockSpec((None, K, n), lambda i, s: (s, 0, 0)) for _ in wgs] + [ANY] * len(deps),
        out_specs=pl.BlockSpec((tm, K), lambda i, s: (i, 0)),
        out_shape=jax.ShapeDtypeStruct((T, K), F32), compiler_params=_cp("parallel", "arbitrary"),
    )(*das, *wgs, *deps)


def _bwd_out(name, dx, wg, deps=()):
    k, N = wg.shape[-2:]
    T = dx.shape[0]
    tm = _tile(T, MM_ROWS)

    def body(a_ref, w_ref, o_ref):
        o_ref[...] = _dot_nt(a_ref[...], w_ref[...])

    return pl.pallas_call(
        _with_deps(body, 2, deps), name=name, grid=(N_DEV, T // tm),
        in_specs=[pl.BlockSpec((tm, N), lambda s, i: (i, 0)),
                  pl.BlockSpec((None, k, N), lambda s, i: (s, 0, 0))] + [ANY] * len(deps),
        out_specs=pl.BlockSpec((tm, k), lambda s, i: (i, s)),
        out_shape=jax.ShapeDtypeStruct((T, N_DEV * k), F32), compiler_params=_cp("parallel", "parallel"),
    )(dx, wg, *deps)


def _ffn_bwd_hidden(name, dx, wg_down, gate, up, deps=()):
    n, N = wg_down.shape[-2:]
    T = dx.shape[0]
    tm = _tile(T, MM_ROWS)

    def body(a_ref, w_ref, g_ref, u_ref, dg_ref, du_ref):
        dh = _dot_nt(a_ref[...], w_ref[...])
        g = g_ref[...]
        dg_ref[...] = (dh * u_ref[...] * _dsilu(g)).astype(dg_ref.dtype)
        du_ref[...] = (dh * _silu(g)).astype(du_ref.dtype)

    sm = pl.BlockSpec((None, tm, n), lambda s, i: (s, i, 0))
    return pl.pallas_call(
        _with_deps(body, 4, deps), name=name, grid=(N_DEV, T // tm),
        in_specs=[pl.BlockSpec((tm, N), lambda s, i: (i, 0)),
                  pl.BlockSpec((None, n, N), lambda s, i: (s, 0, 0)), sm, sm] + [ANY] * len(deps),
        out_specs=[sm, sm],
        out_shape=[jax.ShapeDtypeStruct((N_DEV, T, n), MXU_DTYPE)] * 2,
        compiler_params=_cp("parallel", "parallel"),
    )(dx, wg_down, gate, up, *deps)


def _wgrad(name, a, c, rows, cols, deps=()):
    T = a.shape[-2]
    tk = _tile(T, WGRAD_ROWS)
    nk = T // tk

    def spec(z, w):
        if z.ndim == 3:
            return pl.BlockSpec((None, tk, w), lambda s, k: (s, k, 0))
        if z.shape[1] == w:
            return pl.BlockSpec((tk, w), lambda s, k: (k, 0))
        return pl.BlockSpec((tk, w), lambda s, k: (k, s))

    def body(a_ref, c_ref, o_ref, acc_ref):
        k = pl.program_id(1)
        p = _dot_tn(a_ref[...], c_ref[...])

        @pl.when(k == 0)
        def _():
            acc_ref[...] = p

        @pl.when(k > 0)
        def _():
            acc_ref[...] += p

        @pl.when(k == nk - 1)
        def _():
            o_ref[...] = acc_ref[...].astype(o_ref.dtype)

    return pl.pallas_call(
        _with_deps(body, 2, deps), name=name, grid=(N_DEV, nk),
        in_specs=[spec(a, rows), spec(c, cols)] + [ANY] * len(deps),
        out_specs=pl.BlockSpec((None, rows, cols), lambda s, k: (s, 0, 0)),
        out_shape=jax.ShapeDtypeStruct((N_DEV, rows, cols), WIRE_DTYPE),
        scratch_shapes=[pltpu.VMEM((rows, cols), F32)],
        compiler_params=_cp("parallel", "arbitrary"),
    )(a, c, *deps)


def _shift_down(cur, prev8, sh):
    n = cur.shape[0]
    rolled = pltpu.roll(cur, sh, 0)
    top = jnp.where(_rows(8) < sh, pltpu.roll(prev8, sh, 0), rolled[0:8])
    return jnp.concatenate([top, rolled[8:n]], axis=0)


def _shift_up(cur, next8, sh):
    n = cur.shape[0]
    rolled = pltpu.roll(cur, n - sh, 0)
    bot = jnp.where(_rows(8) >= 8 - sh, pltpu.roll(next8, 8 - sh, 0), rolled[n - 8:n])
    return jnp.concatenate([rolled[0:n - 8], bot], axis=0)


def _lru_gate_terms(r, lam):
    sp = _softplus(-lam)
    la = -LRU_C * r * sp
    a = jnp.exp(la)
    m2 = _one_minus_exp(2.0 * la)
    return sp, la, a, m2


def _lru_fwd(name, proj, conv_w, conv_b, wa, ba, wx, bx, lam):
    T = proj.shape[0]
    H, hd, _ = wa.shape
    W = H * hd
    K = conv_w.shape[0]
    tb = _tile(T, 256)

    def body(xin_ref, gate_ref, cw_ref, cb_ref, wa_ref, ba_ref, wx_ref, bx_ref, lam_ref,
             ya_ref, xc_ref, r_ref, i_ref, hs_ref, tail_ref, hprev_ref):
        blk = pl.program_id(0)

        @pl.when(blk == 0)
        def _():
            tail_ref[...] = jnp.zeros_like(tail_ref)
            hprev_ref[...] = jnp.zeros_like(hprev_ref)

        xin = xin_ref[...]
        prev8 = tail_ref[...]
        xc = cw_ref[K - 1:K, :] * xin
        for sh in range(1, K):
            xc = xc + cw_ref[K - 1 - sh:K - sh, :] * _shift_down(xin, prev8, sh)
        xc = xc + cb_ref[...]
        tail_ref[...] = xin[tb - 8:tb]
        xc_ref[...] = xc
        for h in range(H):
            cs = slice(h * hd, (h + 1) * hd)
            xh = xc[:, cs]
            r_ref[:, cs] = _sigmoid(_dot(xh, wa_ref[h]) + ba_ref[:, cs])
            i_ref[:, cs] = _sigmoid(_dot(xh, wx_ref[h]) + bx_ref[:, cs])
        r = r_ref[...]
        _, _, a, m2 = _lru_gate_terms(r, lam_ref[...])
        row = _rows(tb)
        mult = jnp.where((row == 0) & (blk == 0), 1.0, jnp.sqrt(jnp.maximum(m2, 0.0)))
        u = mult * i_ref[...] * xc
        d = 1
        while d < tb:
            keep = row >= d
            u = a * jnp.where(keep, pltpu.roll(u, d, 0), 0.0) + u
            a = a * jnp.where(keep, pltpu.roll(a, d, 0), 1.0)
            d *= 2
        hs = u + a * hprev_ref[...]
        hprev_ref[...] = hs[tb - 1:tb]
        hs_ref[...] = hs
        ya_ref[...] = (hs * _gelu(gate_ref[...])).astype(ya_ref.dtype)

    full = lambda shape: pl.BlockSpec(shape, lambda i: tuple(0 for _ in shape))
    blk = pl.BlockSpec((tb, W), lambda i: (i, 0))
    return pl.pallas_call(
        body, name=name, grid=(T // tb,),
        in_specs=[pl.BlockSpec((tb, W), lambda i: (i, 0)), pl.BlockSpec((tb, W), lambda i: (i, 1)),
                  full((K, W)), full((1, W)), full((H, hd, hd)), full((1, W)), full((H, hd, hd)),
                  full((1, W)), full((1, W))],
        out_specs=[blk] * 5,
        out_shape=[jax.ShapeDtypeStruct((T, 2 * W), MXU_DTYPE)] + [jax.ShapeDtypeStruct((T, W), F32)] * 4,
        scratch_shapes=[pltpu.VMEM((8, W), F32), pltpu.VMEM((1, W), F32)],
        compiler_params=_cp("arbitrary"),
    )(proj, proj, conv_w, conv_b, wa, ba, wx, bx, lam)


def _lru_bwd(name, proj, dy, xc, r, ig, hs, conv_w, wa, wx, lam, dpbuf):
    T = proj.shape[0]
    H, hd, _ = wa.shape
    W = H * hd
    K = conv_w.shape[0]
    tb = _tile(T, 256)
    nb = T // tb
    t8 = tb // 8

    def body(xin_ref, xprev_ref, gate_ref, dy_ref, xc_ref, r_ref, i_ref, hs_ref, hsprev_ref,
             cw_ref, wa_ref, wx_ref, lam_ref, dpbuf_ref,
             dp_ref, db_ref, dcw_ref, dcb_ref, dwa_ref, dba_ref, dwx_ref, dbx_ref, dlam_ref,
             cdh_ref, ca_ref, cdxc_ref, dxc_ref):
        del dpbuf_ref
        step = pl.program_id(0)
        blk = nb - 1 - step

        @pl.when(step == 0)
        def _():
            for ref in (db_ref, dcw_ref, dcb_ref, dwa_ref, dba_ref, dwx_ref, dbx_ref, dlam_ref,
                        cdh_ref, ca_ref, cdxc_ref):
                ref[...] = jnp.zeros_like(ref)

        row = _rows(tb)
        first = blk == 0
        gate = gate_ref[...]
        dy_a = dy_ref[...]
        hsv = hs_ref[...]
        _put(dp_ref, db_ref, W, 2 * W, dy_a * hsv * _dgelu(gate))
        d_hs = dy_a * _gelu(gate)
        lam = lam_ref[...]
        rv = r_ref[...]
        sp, la, a, m2 = _lru_gate_terms(rv, lam)
        an = jnp.where(row == tb - 1, ca_ref[...], pltpu.roll(a, tb - 1, 0))
        u = d_hs
        d = 1
        while d < tb:
            keep = row < tb - d
            u = an * jnp.where(keep, pltpu.roll(u, tb - d, 0), 0.0) + u
            an = an * jnp.where(keep, pltpu.roll(an, tb - d, 0), 1.0)
            d *= 2
        dh = u + an * cdh_ref[...]
        cdh_ref[...] = dh[0:1]
        ca_ref[...] = a[0:1]
        hlast = jnp.where(first, 0.0, hsprev_ref[7:8, :])
        hprev = jnp.where(row == 0, hlast, pltpu.roll(hsv, 1, 0))
        da = dh * hprev
        xcv = xc_ref[...]
        iv = i_ref[...]
        t0 = (row == 0) & first
        mult = jnp.sqrt(jnp.maximum(m2, 0.0))
        mult_eff = jnp.where(t0, 1.0, mult)
        d_mult = dh * iv * xcv
        d_i = dh * mult_eff * xcv
        dxc = dh * mult_eff * iv
        e2 = 1.0 - m2
        d_la = da * a + jnp.where(t0 | (m2 <= 0.0), 0.0, -d_mult * e2 / jnp.where(m2 > 0.0, mult, 1.0))
        d_r = d_la * (-LRU_C * sp)
        dlam_ref[...] += jnp.sum(d_la * (-LRU_C * rv), axis=0, keepdims=True) * (-_sigmoid(-lam))
        d_zr = d_r * rv * (1.0 - rv)
        d_zi = d_i * iv * (1.0 - iv)
        dba_ref[...] += jnp.sum(d_zr, axis=0, keepdims=True)
        dbx_ref[...] += jnp.sum(d_zi, axis=0, keepdims=True)
        for h in range(H):
            cs = slice(h * hd, (h + 1) * hd)
            xh = xcv[:, cs]
            zr, zi = d_zr[:, cs], d_zi[:, cs]
            dwa_ref[h] += _dot_tn(xh, zr)
            dwx_ref[h] += _dot_tn(xh, zi)
            dxc_ref[:, cs] = dxc[:, cs] + _dot_nt(zr, wa_ref[h]) + _dot_nt(zi, wx_ref[h])
        dxc = dxc_ref[...]
        dcb_ref[...] += jnp.sum(dxc, axis=0, keepdims=True)
        xin = xin_ref[...]
        prev8 = jnp.where(first, 0.0, xprev_ref[...])
        next8 = cdxc_ref[...]
        dxin = cw_ref[K - 1:K, :] * dxc
        dcw_ref[K - 1:K, :] += jnp.sum(dxc * xin, axis=0, keepdims=True)
        for sh in range(1, K):
            dxin = dxin + cw_ref[K - 1 - sh:K - sh, :] * _shift_up(dxc, next8, sh)
            dcw_ref[K - 1 - sh:K - sh, :] += jnp.sum(dxc * _shift_down(xin, prev8, sh), axis=0, keepdims=True)
        cdxc_ref[...] = dxc[0:8]
        _put(dp_ref, db_ref, 0, W, dxin)

    full = lambda shape: pl.BlockSpec(shape, lambda i: tuple(0 for _ in shape))
    cur = lambda col: pl.BlockSpec((tb, W), lambda i: (nb - 1 - i, col))
    prev = pl.BlockSpec((8, W), lambda i: (jnp.maximum((nb - 1 - i) * t8 - 1, 0), 0))
    return pl.pallas_call(
        body, name=name, grid=(nb,),
        in_specs=[cur(0), prev, cur(1), cur(0), cur(0), cur(0), cur(0), cur(0), prev,
                  full((K, W)), full((H, hd, hd)), full((H, hd, hd)), full((1, W)), ANY],
        out_specs=[pl.BlockSpec((tb, 2 * W), lambda i: (nb - 1 - i, 0)), full((1, 2 * W)), full((K, W)),
                   full((1, W)),
                   full((H, hd, hd)), full((1, W)), full((H, hd, hd)), full((1, W)), full((1, W))],
        out_shape=[jax.ShapeDtypeStruct(dpbuf.shape, dpbuf.dtype), jax.ShapeDtypeStruct((1, 2 * W), F32),
                   jax.ShapeDtypeStruct((K, W), F32),
                   jax.ShapeDtypeStruct((1, W), F32), jax.ShapeDtypeStruct((H, hd, hd), F32),
                   jax.ShapeDtypeStruct((1, W), F32), jax.ShapeDtypeStruct((H, hd, hd), F32),
                   jax.ShapeDtypeStruct((1, W), F32), jax.ShapeDtypeStruct((1, W), F32)],
        scratch_shapes=[pltpu.VMEM((1, W), F32), pltpu.VMEM((1, W), F32), pltpu.VMEM((8, W), F32),
                        pltpu.VMEM((tb, W), F32)],
        input_output_aliases={13: 0}, compiler_params=_cp("arbitrary"),
    )(proj, proj, proj, dy, xc, r, ig, hs, hs, conv_w, wa, wx, lam, dpbuf)


def _chunk_cumsum(g, c):
    n = g.shape[0]
    rc = _rows(n) & (c - 1)
    d = 1
    while d < c:
        g = g + jnp.where(rc >= d, pltpu.roll(g, d, 0), 0.0)
        d *= 2
    return g


def _chunk_rcumsum(g, c):
    n = g.shape[0]
    rc = _rows(n) & (c - 1)
    d = 1
    while d < c:
        g = g + jnp.where(rc < c - d, pltpu.roll(g, n - d, 0), 0.0)
        d *= 2
    return g


def _hgrn_pointwise(qr, fr, lb):
    qf = _silu(qr)
    sig = _sigmoid(fr)
    fg = lb + (1.0 - lb) * sig
    gl = jnp.log(jnp.maximum(fg, F_FLOOR))
    kk = (1.0 - lb) * (1.0 - sig)
    return qf, sig, fg, gl, kk


def _hgrn_fwd(name, proj, lb, norm_g, ybuf):
    T = proj.shape[0]
    W = lb.shape[1]
    H = HGRN_HEADS
    dk = W // H
    c = HGRN_SUB
    R = _tile(T, 128)
    nck = R // c

    def body(q_ref, f_ref, v_ref, g_ref, lb_ref, ng_ref, ybuf_ref, yb_ref, o_ref, sall_ref,
             st_ref, qe_ref, ke_ref, acc_ref):
        del ybuf_ref

        @pl.when(pl.program_id(0) == 0)
        def _():
            st_ref[...] = jnp.zeros_like(st_ref)

        qf, _, _, gl, kk = _hgrn_pointwise(q_ref[...], f_ref[...], lb_ref[...])
        b = _chunk_cumsum(gl * LOG2E, c)
        rc = _rows(R) & (c - 1)
        for h in range(H):
            cs = slice(h * dk, (h + 1) * dk)
            qh, kh, bh, vh = qf[:, cs], kk[:, cs], b[:, cs], v_ref[:, cs]
            acc = jnp.sum(qh * kh, axis=1, keepdims=True) * vh
            for d in range(1, c):
                e = jnp.where(rc >= d, jnp.exp2(bh - pltpu.roll(bh, d, 0)), 0.0)
                s = jnp.sum(qh * pltpu.roll(kh, d, 0) * e, axis=1, keepdims=True)
                acc = acc + s * pltpu.roll(vh, d, 0)
            acc_ref[:, cs] = acc
        qe_ref[...] = qf * jnp.exp2(b)
        for ci in range(nck):
            rs = slice(ci * c, (ci + 1) * c)
            bl = b[ci * c + c - 1:ci * c + c, :]
            ke_ref[rs, :] = kk[rs, :] * jnp.exp2(bl - b[rs, :])
            ebl = jnp.exp2(bl)
            for h in range(H):
                cs = slice(h * dk, (h + 1) * dk)
                st = st_ref[h]
                sall_ref[ci, h] = st
                o_ref[rs, cs] = acc_ref[rs, cs] + _dot_nt(qe_ref[rs, cs], st)
                st_ref[h] = st * ebl[:, cs] + _dot_tn(v_ref[rs, cs], ke_ref[rs, cs])
        ng = ng_ref[...]
        gg = g_ref[...]
        for h in range(H):
            cs = slice(h * dk, (h + 1) * dk)
            oh = o_ref[:, cs]
            rr = lax.rsqrt(jnp.mean(oh * oh, axis=1, keepdims=True) + EPS)
            yb_ref[:, cs] = ((oh * rr) * ng[:, cs] * _silu(gg[:, cs])).astype(yb_ref.dtype)

    full = lambda shape: pl.BlockSpec(shape, lambda i: tuple(0 for _ in shape))
    col = lambda k: pl.BlockSpec((R, W), lambda i: (i, k))
    blk = pl.BlockSpec((R, W), lambda i: (i, 0))
    return pl.pallas_call(
        body, name=name, grid=(T // R,),
        in_specs=[col(2), col(3), col(4), col(5), full((1, W)), full((1, W)), ANY],
        out_specs=[col(1), blk, pl.BlockSpec((nck, H, dk, dk), lambda i: (i, 0, 0, 0))],
        out_shape=[jax.ShapeDtypeStruct((T, 2 * W), MXU_DTYPE), jax.ShapeDtypeStruct((T, W), F32),
                   jax.ShapeDtypeStruct((T // c, H, dk, dk), F32)],
        scratch_shapes=[pltpu.VMEM((H, dk, dk), F32), pltpu.VMEM((R, W), F32), pltpu.VMEM((R, W), F32),
                        pltpu.VMEM((R, W), F32)],
        input_output_aliases={6: 0}, compiler_params=_cp("arbitrary"),
    )(proj, proj, proj, proj, lb, norm_g, ybuf)


def _hgrn_bwd(name, proj, dy, o, sall, lb, norm_g):
    T = proj.shape[0]
    W = lb.shape[1]
    H = HGRN_HEADS
    dk = W // H
    c = HGRN_SUB
    R = _tile(T, 128)
    nck = R // c
    nb = T // R

    def body(q_ref, f_ref, v_ref, g_ref, dy_ref, o_ref, sall_ref, lb_ref, ng_ref,
             dp_ref, db_ref, dlb_ref, dng_ref,
             dst_ref, do_ref, dq_ref, dk_ref, dv_ref, ex_ref, qe_ref, ke_ref):
        @pl.when(pl.program_id(0) == 0)
        def _():
            dst_ref[...] = jnp.zeros_like(dst_ref)
            db_ref[...] = jnp.zeros_like(db_ref)
            dlb_ref[...] = jnp.zeros_like(dlb_ref)
            dng_ref[...] = jnp.zeros_like(dng_ref)

        lbv = lb_ref[...]
        qr = q_ref[...]
        qf, sig, fg, gl, kk = _hgrn_pointwise(qr, f_ref[...], lbv)
        b = _chunk_cumsum(gl * LOG2E, c)
        rc = _rows(R) & (c - 1)
        ng = ng_ref[...]
        gg = g_ref[...]
        dyv = dy_ref[...]
        sg = _silu(gg)
        for h in range(H):
            cs = slice(h * dk, (h + 1) * dk)
            oh = o_ref[:, cs]
            rr = lax.rsqrt(jnp.mean(oh * oh, axis=1, keepdims=True) + EPS)
            ohat = oh * rr
            dyh = dyv[:, cs]
            _put(dp_ref, db_ref, 5 * W + h * dk, 5 * W + (h + 1) * dk,
                 dyh * ohat * ng[:, cs] * _dsilu(gg[:, cs]))
            t = dyh * sg[:, cs]
            dng_ref[:, cs] += jnp.sum(t * ohat, axis=0, keepdims=True)
            dohat = t * ng[:, cs]
            do_ref[:, cs] = rr * (dohat - ohat * jnp.mean(dohat * ohat, axis=1, keepdims=True))
        for h in range(H):
            cs = slice(h * dk, (h + 1) * dk)
            qh, kh, bh, vh, doh = qf[:, cs], kk[:, cs], b[:, cs], v_ref[:, cs], do_ref[:, cs]
            da0 = jnp.sum(doh * vh, axis=1, keepdims=True)
            a0 = jnp.sum(qh * kh, axis=1, keepdims=True)
            dq = da0 * kh
            dkk = da0 * qh
            dv = a0 * doh
            for d in range(1, c):
                e = jnp.where(rc >= d, jnp.exp2(bh - pltpu.roll(bh, d, 0)), 0.0)
                kr = pltpu.roll(kh, d, 0)
                da = jnp.sum(doh * pltpu.roll(vh, d, 0), axis=1, keepdims=True)
                aa = jnp.sum(qh * kr * e, axis=1, keepdims=True)
                dq = dq + da * kr * e
                dkk = dkk + pltpu.roll(da * qh * e, R - d, 0)
                dv = dv + pltpu.roll(aa * doh, R - d, 0)
            dq_ref[:, cs] = dq
            dk_ref[:, cs] = dkk
            dv_ref[:, cs] = dv
        eb = jnp.exp2(b)
        qe_ref[...] = qf * eb
        ex_ref[...] = jnp.zeros_like(ex_ref)
        for ci in reversed(range(nck)):
            rs = slice(ci * c, (ci + 1) * c)
            bl = b[ci * c + c - 1:ci * c + c, :]
            ebl_rows = jnp.exp2(bl - b[rs, :])
            ke_ref[rs, :] = kk[rs, :] * ebl_rows
            ebl = jnp.exp2(bl)
            for h in range(H):
                cs = slice(h * dk, (h + 1) * dk)
                st0 = sall_ref[ci, h]
                dst1 = dst_ref[h]
                doc = do_ref[rs, cs]
                vc = v_ref[rs, cs]
                dq_ref[rs, cs] += _dot(doc, st0) * eb[rs, cs]
                dv_ref[rs, cs] += _dot_nt(ke_ref[rs, cs], dst1)
                dki = _dot(vc, dst1) * ebl_rows[:, cs]
                dk_ref[rs, cs] += dki
                ex_ref[ci * c + c - 1:ci * c + c, cs] = (
                    jnp.sum(dki * kk[rs, cs], axis=0, keepdims=True)
                    + ebl[:, cs] * jnp.sum(st0 * dst1, axis=0, keepdims=True))
                dst_ref[h] = dst1 * ebl[:, cs] + _dot_tn(doc, qe_ref[rs, cs])
        dq = dq_ref[...]
        dkk = dk_ref[...]
        db = qf * dq - kk * dkk + ex_ref[...]
        dgl = _chunk_rcumsum(db, c)
        dfg = jnp.where(fg > F_FLOOR, dgl / jnp.maximum(fg, F_FLOOR), 0.0)
        dsig = (dfg - dkk) * (1.0 - lbv)
        dlb_ref[...] += jnp.sum((dfg - dkk) * (1.0 - sig), axis=0, keepdims=True)
        dp_ref[:, 0:2 * W] = jnp.zeros((R, 2 * W), dp_ref.dtype)
        _put(dp_ref, db_ref, 2 * W, 3 * W, dq * _dsilu(qr))
        _put(dp_ref, db_ref, 3 * W, 4 * W, dsig * sig * (1.0 - sig))
        _put(dp_ref, db_ref, 4 * W, 5 * W, dv_ref[...])

    full = lambda shape: pl.BlockSpec(shape, lambda i: tuple(0 for _ in shape))
    col = lambda k: pl.BlockSpec((R, W), lambda i: (nb - 1 - i, k))
    scr = pltpu.VMEM((R, W), F32)
    return pl.pallas_call(
        body, name=name, grid=(nb,),
        in_specs=[col(2), col(3), col(4), col(5), col(1), col(0),
                  pl.BlockSpec((nck, H, dk, dk), lambda i: (nb - 1 - i, 0, 0, 0)), full((1, W)), full((1, W))],
        out_specs=[pl.BlockSpec((R, 6 * W), lambda i: (nb - 1 - i, 0)), full((1, 6 * W)), full((1, W)),
                   full((1, W))],
        out_shape=[jax.ShapeDtypeStruct((T, 6 * W), MXU_DTYPE), jax.ShapeDtypeStruct((1, 6 * W), F32),
                   jax.ShapeDtypeStruct((1, W), F32), jax.ShapeDtypeStruct((1, W), F32)],
        scratch_shapes=[pltpu.VMEM((H, dk, dk), F32), scr, scr, scr, scr, scr, scr, scr],
        compiler_params=_cp("arbitrary"),
    )(proj, proj, proj, proj, dy, o, sall, lb, norm_g)


ODD_HALO = 32


def _row_shifts(ext, up):
    n = ext.shape[0]
    return [ext] + [pltpu.roll(ext, n - b if up else b, 0) for b in range(1, 8)]


def _past(copies, sh, tb):
    a, b = divmod(sh, 8)
    return copies[b][ODD_HALO - 8 * a:ODD_HALO - 8 * a + tb]


def _future(copies, sh, tb):
    a, b = divmod(sh, 8)
    return copies[b][8 * a:8 * a + tb]


def _odd_fwd(name, proj, sc_w, cf_w, cf_b, ln_g, ln_b):
    T = proj.shape[0]
    W = sc_w.shape[1]
    K3, K31 = sc_w.shape[0], cf_w.shape[0]
    tb = _tile(T, 256)
    hb = tb // ODD_HALO
    n = tb + ODD_HALO

    def body(cur_ref, prev_ref, w3_ref, w31_ref, cb_ref, lg_ref, lbeta_ref, y_ref, d_ref):
        keep = (pl.program_id(0) > 0).astype(F32)
        sb = cur_ref[:, 0:W]
        p = cur_ref[:, W:2 * W] * cur_ref[:, 2 * W:3 * W]
        glu = cur_ref[:, 3 * W:4 * W] * _sigmoid(cur_ref[:, 4 * W:5 * W])
        p_prev = prev_ref[:, W:2 * W] * prev_ref[:, 2 * W:3 * W] * keep
        glu_prev = prev_ref[:, 3 * W:4 * W] * _sigmoid(prev_ref[:, 4 * W:5 * W]) * keep
        ext = jnp.concatenate([p_prev, p], axis=0)
        cp = w3_ref[K3 - 1:K3, :] * p
        for sh in range(1, K3):
            cp = cp + w3_ref[K3 - 1 - sh:K3 - sh, :] * pltpu.roll(ext, sh, 0)[ODD_HALO:n]
        y_ref[:, 0:W] = (sb * cp).astype(y_ref.dtype)
        glu_past = _row_shifts(jnp.concatenate([glu_prev, glu], axis=0), up=False)
        d = cb_ref[...] + w31_ref[K31 - 1:K31, :] * glu
        for sh in range(1, K31):
            d = d + w31_ref[K31 - 1 - sh:K31 - sh, :] * _past(glu_past, sh, tb)
        d_ref[...] = d
        mu = jnp.mean(d, axis=1, keepdims=True)
        xc = d - mu
        rstd = lax.rsqrt(jnp.mean(xc * xc, axis=1, keepdims=True) + EPS)
        ln = (xc * rstd) * lg_ref[...] + lbeta_ref[...]
        y_ref[:, W:2 * W] = _silu(ln).astype(y_ref.dtype)

    full = lambda shape: pl.BlockSpec(shape, lambda i: tuple(0 for _ in shape))
    return pl.pallas_call(
        body, name=name, grid=(T // tb,),
        in_specs=[pl.BlockSpec((tb, 5 * W), lambda i: (i, 0)),
                  pl.BlockSpec((ODD_HALO, 5 * W), lambda i: (jnp.maximum(i * hb - 1, 0), 0)),
                  full((K3, W)), full((K31, W)), full((1, W)), full((1, W)), full((1, W))],
        out_specs=[pl.BlockSpec((tb, 2 * W), lambda i: (i, 0)), pl.BlockSpec((tb, W), lambda i: (i, 0))],
        out_shape=[jax.ShapeDtypeStruct((T, 2 * W), MXU_DTYPE), jax.ShapeDtypeStruct((T, W), F32)],
        compiler_params=_cp("parallel"),
    )(proj, proj, sc_w, cf_w, cf_b, ln_g, ln_b)


def _odd_bwd(name, proj, dy, dsave, sc_w, cf_w, ln_g, ln_b):
    T = proj.shape[0]
    W = sc_w.shape[1]
    K3, K31 = sc_w.shape[0], cf_w.shape[0]
    tb = _tile(T, 128)
    nb = T // tb
    hb = tb // ODD_HALO
    nh = T // ODD_HALO
    n = tb + ODD_HALO

    def body(cur_ref, prev_ref, next_ref, dy_ref, dyn_ref, d_ref, dn_ref,
             w3_ref, w31_ref, lg_ref, lbeta_ref,
             dp_ref, db_ref, dw3_ref, dw31_ref, dcb_ref, dlg_ref, dlb_ref):
        i = pl.program_id(0)

        @pl.when(i == 0)
        def _():
            for ref in (db_ref, dw3_ref, dw31_ref, dcb_ref, dlg_ref, dlb_ref):
                ref[...] = jnp.zeros_like(ref)

        keep_prev = (i > 0).astype(F32)
        keep_next = (i < nb - 1).astype(F32)
        sb = cur_ref[:, 0:W]
        scv = cur_ref[:, W:2 * W]
        svv = cur_ref[:, 2 * W:3 * W]
        cu = cur_ref[:, 3 * W:4 * W]
        sg = _sigmoid(cur_ref[:, 4 * W:5 * W])
        p = scv * svv
        glu = cu * sg
        p_prev = prev_ref[:, W:2 * W] * prev_ref[:, 2 * W:3 * W] * keep_prev
        glu_prev = prev_ref[:, 3 * W:4 * W] * _sigmoid(prev_ref[:, 4 * W:5 * W]) * keep_prev
        dext = jnp.concatenate([d_ref[...], dn_ref[...]], axis=0)
        dyd = jnp.concatenate([dy_ref[:, W:2 * W], dyn_ref[:, W:2 * W] * keep_next], axis=0)
        mu = jnp.mean(dext, axis=1, keepdims=True)
        xc = dext - mu
        rstd = lax.rsqrt(jnp.mean(xc * xc, axis=1, keepdims=True) + EPS)
        xh = xc * rstd
        lg = lg_ref[...]
        dln = dyd * _dsilu(xh * lg + lbeta_ref[...])
        dxh = dln * lg
        dd = rstd * (dxh - jnp.mean(dxh, axis=1, keepdims=True)
                     - xh * jnp.mean(dxh * xh, axis=1, keepdims=True))
        dlg_ref[...] += jnp.sum((dln * xh)[0:tb], axis=0, keepdims=True)
        dlb_ref[...] += jnp.sum(dln[0:tb], axis=0, keepdims=True)
        ddc = dd[0:tb]
        dcb_ref[...] += jnp.sum(ddc, axis=0, keepdims=True)
        dglu = w31_ref[K31 - 1:K31, :] * ddc
        glu_past = _row_shifts(jnp.concatenate([glu_prev, glu], axis=0), up=False)
        dd_future = _row_shifts(dd, up=True)
        dw31_ref[K31 - 1:K31, :] += jnp.sum(ddc * glu, axis=0, keepdims=True)
        for sh in range(1, K31):
            dglu = dglu + w31_ref[K31 - 1 - sh:K31 - sh, :] * _future(dd_future, sh, tb)
            dw31_ref[K31 - 1 - sh:K31 - sh, :] += jnp.sum(ddc * _past(glu_past, sh, tb), axis=0, keepdims=True)
        _put(dp_ref, db_ref, 3 * W, 4 * W, dglu * sg)
        _put(dp_ref, db_ref, 4 * W, 5 * W, dglu * cu * sg * (1.0 - sg))
        dyc = dy_ref[:, 0:W]
        dcp = jnp.concatenate([dyc * sb, dyn_ref[:, 0:W] * next_ref[:, 0:W] * keep_next], axis=0)
        dcpc = dcp[0:tb]
        ext = jnp.concatenate([p_prev, p], axis=0)
        cp = w3_ref[K3 - 1:K3, :] * p
        dpp = w3_ref[K3 - 1:K3, :] * dcpc
        dw3_ref[K3 - 1:K3, :] += jnp.sum(dcpc * p, axis=0, keepdims=True)
        up = dcp
        for sh in range(1, K3):
            up = pltpu.roll(up, n - 1, 0)
            ext = pltpu.roll(ext, 1, 0)
            shifted = ext[ODD_HALO:n]
            cp = cp + w3_ref[K3 - 1 - sh:K3 - sh, :] * shifted
            dpp = dpp + w3_ref[K3 - 1 - sh:K3 - sh, :] * up[0:tb]
            dw3_ref[K3 - 1 - sh:K3 - sh, :] += jnp.sum(dcpc * shifted, axis=0, keepdims=True)
        _put(dp_ref, db_ref, 0, W, dyc * cp)
        _put(dp_ref, db_ref, W, 2 * W, dpp * svv)
        _put(dp_ref, db_ref, 2 * W, 3 * W, dpp * scv)

    full = lambda shape: pl.BlockSpec(shape, lambda i: tuple(0 for _ in shape))
    prev_map = lambda i: (jnp.maximum(i * hb - 1, 0), 0)
    next_map = lambda i: (jnp.minimum((i + 1) * hb, nh - 1), 0)
    return pl.pallas_call(
        body, name=name, grid=(nb,),
        in_specs=[pl.BlockSpec((tb, 5 * W), lambda i: (i, 0)),
                  pl.BlockSpec((ODD_HALO, 5 * W), prev_map), pl.BlockSpec((ODD_HALO, 5 * W), next_map),
                  pl.BlockSpec((tb, 2 * W), lambda i: (i, 0)), pl.BlockSpec((ODD_HALO, 2 * W), next_map),
                  pl.BlockSpec((tb, W), lambda i: (i, 0)), pl.BlockSpec((ODD_HALO, W), next_map),
                  full((K3, W)), full((K31, W)), full((1, W)), full((1, W))],
        out_specs=[pl.BlockSpec((tb, 5 * W), lambda i: (i, 0)), full((1, 5 * W)), full((K3, W)), full((K31, W)),
                   full((1, W)), full((1, W)), full((1, W))],
        out_shape=[jax.ShapeDtypeStruct((T, 5 * W), MXU_DTYPE), jax.ShapeDtypeStruct((1, 5 * W), F32),
                   jax.ShapeDtypeStruct((K3, W), F32),
                   jax.ShapeDtypeStruct((K31, W), F32)] + [jax.ShapeDtypeStruct((1, W), F32)] * 3,
        compiler_params=_cp("arbitrary"),
    )(proj, proj, proj, dy, dy, dsave, dsave, sc_w, cf_w, ln_g, ln_b)


PACK_WIDTH = 1024


def _lower_bounds(logits):
    sm = jax.nn.softmax(logits.astype(F32), axis=0)
    return jnp.cumsum(sm, axis=0) - sm[0]


def _pack_rows(arrays):
    flat = jnp.concatenate([a.reshape(-1) for a in arrays])
    pad = (-flat.shape[0]) % (8 * PACK_WIDTH)
    return jnp.pad(flat, (0, pad)).reshape(-1, PACK_WIDTH)


def _unpack_rows(packed, shapes):
    flat = packed.reshape(-1)
    out, off = [], 0
    for s in shapes:
        sz = math.prod(s)
        out.append(flat[off:off + sz].reshape(s))
        off += sz
    return out


def _shards_last(a):
    n = a.shape[-1] // N_DEV
    return jnp.moveaxis(a.reshape(a.shape[:-1] + (N_DEV, n)), -2, 0)


def _unshard_last(a):
    a = jnp.moveaxis(a, 0, -2)
    return a.reshape(a.shape[:-2] + (a.shape[-2] * a.shape[-1],))


BIG = ("ev_w_in", "ev_w_out", "od_w_in", "od_w_out", "ffn_w_gate", "ffn_w_up", "ffn_w_down")
COLUMN_MAJOR = ("ffn_w_gate", "ffn_w_up")
SMALL_SHARDED = ("lru_conv_w", "od_b_in", "sc_conv_w", "cf_conv_w", "cf_conv_b", "cf_ln_g", "cf_ln_b")
SMALL_REPL = ("ln_mix_g", "ln_ffn_g", "ln_final_g", "ev_b_in", "lru_conv_b", "lru_wa", "lru_ba", "lru_wx",
              "lru_bx", "lru_lambda", "hgrn_lb_logits", "hgrn_norm_g")
WEIGHTS = ("ln_mix_g", "ln_ffn_g", "ln_final_g", "ev_w_in", "ev_b_in", "lru_conv_w", "lru_conv_b", "lru_wa",
           "lru_ba", "lru_wx", "lru_bx", "lru_lambda", "hgrn_lb_logits", "hgrn_norm_g", "ev_w_out", "od_w_in",
           "od_b_in", "sc_conv_w", "cf_conv_w", "cf_conv_b", "cf_ln_g", "cf_ln_b", "od_w_out", "ffn_w_gate",
           "ffn_w_up", "ffn_w_down")


def _layer_weights(l):
    mix = ("ev_w_in", "ev_w_out") if l % 2 == 0 else ("od_w_in", "od_w_out")
    return [(mix[0], l // 2), (mix[1], l // 2), ("ffn_w_gate", l), ("ffn_w_up", l), ("ffn_w_down", l)]


class _MeshExchange:
    def __init__(self, w, m, v, depth, first):
        self.w, self.m, self.v, self.depth, self.first = w, m, v, depth, tuple(first)
        self.w_bf = {k: w[k].astype(MXU_DTYPE) for k, _ in self._names(0, "mix")}
        self.ready, self.flight, self.rs, self.adam = {}, {}, {}, {}

    @staticmethod
    def _names(l, grp):
        names = _layer_weights(l)
        return names[:2] if grp == "mix" else names[2:]

    def _own_start(self, l, grp, deps):
        names = self._names(l, grp)
        srcs = [self.w_bf[k] for k, _ in names]
        layers = [j for _, j in names]
        plan = _plan_gather_own(layers)
        sems, lands, tok = _split_start(f"ag_own_start_{grp}{l}", plan, srcs, _place_own(srcs, layers), deps)
        self.flight[l, grp] = (plan, sems, srcs, lands)
        return tok

    def _turn(self, l, grp, deps):
        plan, sems, srcs, lands = self.flight[l, grp]
        lands = _split_wait(f"ag_own_wait_{grp}{l}", plan, sems, srcs, lands, deps)
        plan = _plan_gather_pass(len(lands))
        sems, passed, tok = _split_start(f"ag_pass_start_{grp}{l}", plan, [], lands)
        self.flight[l, grp] = (plan, sems, [], passed)
        toks = [tok]
        nl, ng = (l, "ffn") if grp == "mix" else (l + 1, "mix")
        if nl < self.depth:
            toks.append(self._own_start(nl, ng, (tok,)))
        return tuple(toks)

    def _pass_wait(self, l, grp, deps):
        plan, sems, srcs, lands = self.flight.pop((l, grp))
        lands = _split_wait(f"ag_pass_wait_{grp}{l}", plan, sems, srcs, lands, deps)
        self.ready[l, grp] = dict(zip([k for k, _ in self._names(l, grp)], lands))

    def layer_begin(self, l):
        toks = ()
        if l == 0:
            started = self._own_start(0, "mix", self.first)
            for k in BIG:
                if k not in self.w_bf:
                    self.w_bf[k] = lax.optimization_barrier((self.w[k], started))[0].astype(MXU_DTYPE)
            toks = self._turn(0, "mix", tuple(self.w_bf.values()))
            self._pass_wait(0, "mix", ())
        return self.ready.pop((l, "mix")), toks

    def tick(self, l, t, after):
        if t == 2:
            return self._turn(l, "ffn", (after,))
        if t == 3:
            self._pass_wait(l, "ffn", (after,))
        if t == 4 and l + 1 < self.depth:
            return self._turn(l + 1, "mix", (after,))
        if t == 5 and l + 1 < self.depth:
            self._pass_wait(l + 1, "mix", (after,))
        return ()

    def ffn_weights(self, l):
        return self.ready.pop((l, "ffn"))

    def grads(self, tag, named):
        srcs = [g for _, _, g in named]
        lands = [lax.empty((4,) + g.shape[1:], g.dtype) for g in srcs]
        plan = _plan_scatter_pair(len(srcs))
        sems, lands, tok = _split_start(f"rs_pair_start_{tag}", plan, srcs, lands)
        self.rs[tag] = (named, plan, sems, srcs, lands)
        return (tok,)

    def grads_mid(self, tag, after):
        named, plan, sems, srcs, lands = self.rs[tag]
        got = _split_wait(f"rs_pair_wait_{tag}", plan, sems, srcs, lands, (after,))
        both = [_pair_add(f"pair_add_{tag}_{i}", a, b) for i, (a, b) in enumerate(zip(srcs, got))]
        self.rs[tag] = (named, [p for p, _ in both], [f for _, f in both])

    def grads_send(self, tag):
        named, parts, fins = self.rs[tag]
        plan = _plan_scatter_chips(len(parts))
        sems, fins, tok = _split_start(f"rs_chip_start_{tag}", plan, parts, fins)
        self.rs[tag] = (named, plan, sems, parts, fins)
        return (tok,)

    def grads_end(self, tag, after):
        named, plan, sems, parts, fins = self.rs.pop(tag)
        fins = _split_wait(f"rs_chip_wait_{tag}", plan, sems, parts, fins, (after,))
        for (k, j, _), fin in zip(named, fins):
            view = (lambda a: a.transpose(0, 2, 1)) if k in COLUMN_MAJOR else (lambda a: a)
            self.adam[k] = _adamw(f"adamw_{k}_{j}", view(self.w[k]), view(self.m[k]), view(self.v[k]), fin, j,
                                  self.adam.get(k))

    def finish(self, after):
        for tag in list(self.rs):
            self.grads_end(tag, after)

    def results(self, k):
        if k in COLUMN_MAJOR:
            L, r, c = self.w[k].shape
            return [a.reshape(L, c, r).transpose(0, 2, 1) for a in self.adam[k]]
        return [a.reshape(self.w[k].shape) for a in self.adam[k]]


def _local_step(x, tgt, p, ex):
    T, D = x.shape
    depth = p["ln_mix_g"].shape[0]
    lbs = _lower_bounds(p["hgrn_lb_logits"])
    row = lambda a: a.reshape(1, -1)
    saved = []
    h = None
    for l in range(depth):
        j = l // 2
        wl, tok = ex.layer_begin(l)
        s = {"x": x}
        if h is None:
            h = _rmsnorm_fwd(f"norm_mix{l}", x, row(p["ln_mix_g"][l]))
        s["h"] = h
        if l % 2 == 0:
            proj = _proj_in(f"ev_in{l}", h, wl["ev_w_in"], row(p["ev_b_in"][j]), tok)
            wa = p["lru_wa"][j].astype(MXU_DTYPE)
            wx = p["lru_wx"][j].astype(MXU_DTYPE)
            ya, xc, r, ig, hs = _lru_fwd(f"lru_fwd{l}", proj, p["lru_conv_w"][j], row(p["lru_conv_b"][j]),
                                         wa, row(p["lru_ba"][j]), wx, row(p["lru_bx"][j]),
                                         row(p["lru_lambda"][j]))
            y, o, sall = _hgrn_fwd(f"hgrn_fwd{l}", proj, row(lbs[j]), row(p["hgrn_norm_g"][j]), ya)
            s.update(proj=proj, xc=xc, r=r, ig=ig, hs=hs, o=o, sall=sall, wa=wa, wx=wx)
            w_out = wl["ev_w_out"]
        else:
            proj = _proj_in(f"od_in{l}", h, wl["od_w_in"], row(p["od_b_in"][j]), tok)
            y, dsave = _odd_fwd(f"odd_fwd{l}", proj, p["sc_conv_w"][j], p["cf_conv_w"][j],
                                row(p["cf_conv_b"][j]), row(p["cf_ln_g"][j]), row(p["cf_ln_b"][j]))
            s.update(proj=proj, dsave=dsave)
            w_out = wl["od_w_out"]
        x, h2 = _mix_out(f"mix_out{l}", y, w_out, x, row(p["ln_ffn_g"][l]), ex.tick(l, 2, y))
        s["y"] = y
        s["xmid"] = x
        ex.tick(l, 3, x)
        wl = {**wl, **ex.ffn_weights(l)}
        gate, up, hid = _ffn_in(f"ffn_in{l}", h2, wl["ffn_w_gate"], wl["ffn_w_up"])
        tok = ex.tick(l, 4, hid)
        if l + 1 < depth:
            x, h = _proj_out(f"ffn_out{l}", hid, wl["ffn_w_down"], x, tok, row(p["ln_mix_g"][l + 1]))
        else:
            x = _proj_out(f"ffn_out{l}", hid, wl["ffn_w_down"], x, tok)
        ex.tick(l, 5, x)
        s.update(h2=h2, gate=gate, up=up, hid=hid, w=wl)
        saved.append(s)

    loss, dx, dxb, dg_final = _loss_head("loss_head", x, row(p["ln_final_g"]), tgt)

    gs = {k: [None] * p[k].shape[0] for k in SMALL_REPL + SMALL_SHARDED if k not in ("ln_final_g", "hgrn_lb_logits")}
    d_lb = [None] * (depth // 2 + depth % 2)
    tok = ()
    pending = None
    for l in reversed(range(depth)):
        j = l // 2
        s = saved[l]
        wl = s["w"]
        ffn_shape = wl["ffn_w_gate"].shape[1:]
        dwd = _wgrad(f"ffn_dwd{l}", s["hid"], dxb, ffn_shape[1], ffn_shape[0])
        dgate, dup = _ffn_bwd_hidden(f"ffn_bwd_hid{l}", dxb, wl["ffn_w_down"], s["gate"], s["up"], tok)
        dwg = _wgrad(f"ffn_dwg{l}", dgate, s["h2"], ffn_shape[1], ffn_shape[0])
        dwu = _wgrad(f"ffn_dwu{l}", dup, s["h2"], ffn_shape[1], ffn_shape[0])
        tok = ex.grads(f"ffn{l}", [("ffn_w_down", l, dwd), ("ffn_w_gate", l, dwg), ("ffn_w_up", l, dwu)])
        dh2 = _bwd_in(f"ffn_dh{l}", [dgate, dup], [wl["ffn_w_gate"], wl["ffn_w_up"]], tok)
        dx, dxb, dg = _rmsnorm_bwd(f"norm_ffn_bwd{l}", s["xmid"], row(p["ln_ffn_g"][l]), dh2, dx)
        gs["ln_ffn_g"][l] = dg[0]
        ex.grads_mid(f"ffn{l}", dxb)
        if pending is not None:
            ex.grads_end(pending, dxb)
        tok = ex.grads_send(f"ffn{l}")
        w_in, w_out = ("ev_w_in", "ev_w_out") if l % 2 == 0 else ("od_w_in", "od_w_out")
        dwo = _wgrad(f"mix_dwo{l}", s["y"], dxb, *wl[w_out].shape[1:])
        if l % 2 == 0:
            dy = _bwd_out(f"ev_dy{l}", dxb, wl["ev_w_out"], tok)
            dph, db_h, dlb, dng = _hgrn_bwd(f"hgrn_bwd{l}", s["proj"], dy, s["o"], s["sall"], row(lbs[j]),
                                            row(p["hgrn_norm_g"][j]))
            dproj, db_a, d_cw, d_cb, d_wa, d_ba, d_wx, d_bx, d_lam = _lru_bwd(
                f"lru_bwd{l}", s["proj"], dy, s["xc"], s["r"], s["ig"], s["hs"], p["lru_conv_w"][j],
                s["wa"], s["wx"], row(p["lru_lambda"][j]), dph)
            gs["lru_conv_w"][j], gs["lru_conv_b"][j] = d_cw, d_cb[0]
            gs["lru_wa"][j], gs["lru_ba"][j] = d_wa, d_ba.reshape(p["lru_ba"].shape[1:])
            gs["lru_wx"][j], gs["lru_bx"][j] = d_wx, d_bx.reshape(p["lru_bx"].shape[1:])
            gs["lru_lambda"][j], gs["hgrn_norm_g"][j] = d_lam[0], dng[0]
            d_lb[j] = dlb[0]
            gs["ev_b_in"][j] = jnp.concatenate([db_a[0], db_h[0, db_a.shape[1]:]])
        else:
            dy = _bwd_out(f"od_dy{l}", dxb, wl["od_w_out"], tok)
            dproj, db_in, d_w3, d_w31, d_cfb, d_lg, d_lbeta = _odd_bwd(
                f"odd_bwd{l}", s["proj"], dy, s["dsave"], p["sc_conv_w"][j], p["cf_conv_w"][j],
                row(p["cf_ln_g"][j]), row(p["cf_ln_b"][j]))
            gs["sc_conv_w"][j], gs["cf_conv_w"][j] = d_w3, d_w31
            gs["cf_conv_b"][j], gs["cf_ln_g"][j], gs["cf_ln_b"][j] = d_cfb[0], d_lg[0], d_lbeta[0]
            gs["od_b_in"][j] = db_in[0]
        dwi = _wgrad(f"mix_dwi{l}", s["h"], dproj, *wl[w_in].shape[1:])
        tok = ex.grads(f"mix{l}", [(w_out, j, dwo), (w_in, j, dwi)])
        dh = _bwd_in(f"mix_dh{l}", [dproj], [wl[w_in]], tok)
        dx, dxb, dg = _rmsnorm_bwd(f"norm_mix_bwd{l}", s["x"], row(p["ln_mix_g"][l]), dh, dx)
        gs["ln_mix_g"][l] = dg[0]
        ex.grads_mid(f"mix{l}", dxb)
        if l > 0:
            ex.grads_end(f"ffn{l}", dxb)
            tok = ex.grads_send(f"mix{l}")
        else:
            tok = ex.grads_send(f"mix{l}")
            ex.grads_end(f"ffn{l}", dxb)
        pending = f"mix{l}"

    small = {k: jnp.stack(v) for k, v in gs.items()}
    small["ln_final_g"] = dg_final[0]
    _, lb_vjp = jax.vjp(_lower_bounds, p["hgrn_lb_logits"])
    small["hgrn_lb_logits"] = lb_vjp(jnp.stack(d_lb))[0]
    if tok:
        small = lax.optimization_barrier((small, tok))[0]
    return loss, dx, small


def kernel(x, ln_mix_g, ln_ffn_g, ln_final_g, ev_w_in, ev_b_in, lru_conv_w, lru_conv_b, lru_wa, lru_ba, lru_wx, lru_bx, lru_lambda, hgrn_lb_logits, hgrn_norm_g, ev_w_out, od_w_in, od_b_in, sc_conv_w, cf_conv_w, cf_conv_b, cf_ln_g, cf_ln_b, od_w_out, ffn_w_gate, ffn_w_up, ffn_w_down, loss_target, m_ln_mix_g, m_ln_ffn_g, m_ln_final_g, m_ev_w_in, m_ev_b_in, m_lru_conv_w, m_lru_conv_b, m_lru_wa, m_lru_ba, m_lru_wx, m_lru_bx, m_lru_lambda, m_hgrn_lb_logits, m_hgrn_norm_g, m_ev_w_out, m_od_w_in, m_od_b_in, m_sc_conv_w, m_cf_conv_w, m_cf_conv_b, m_cf_ln_g, m_cf_ln_b, m_od_w_out, m_ffn_w_gate, m_ffn_w_up, m_ffn_w_down, v_ln_mix_g, v_ln_ffn_g, v_ln_final_g, v_ev_w_in, v_ev_b_in, v_lru_conv_w, v_lru_conv_b, v_lru_wa, v_lru_ba, v_lru_wx, v_lru_bx, v_lru_lambda, v_hgrn_lb_logits, v_hgrn_norm_g, v_ev_w_out, v_od_w_in, v_od_b_in, v_sc_conv_w, v_cf_conv_w, v_cf_conv_b, v_cf_ln_g, v_cf_ln_b, v_od_w_out, v_ffn_w_gate, v_ffn_w_up, v_ffn_w_down):
    args = locals()
    w = {k: args[k] for k in WEIGHTS}
    m = {k: args["m_" + k] for k in WEIGHTS}
    v = {k: args["v_" + k] for k in WEIGHTS}
    assert x.shape[0] == 1
    T, D = x.shape[1:]

    local_shapes = [w[k].shape for k in SMALL_SHARDED]
    gathered = _all_gather("gather_small_params", [_pack_rows([w[k] for k in SMALL_SHARDED])])[0]
    p = {k: w[k] for k in SMALL_REPL}
    per_dev = [_unpack_rows(gathered[s], local_shapes) for s in range(N_DEV)]
    for i, k in enumerate(SMALL_SHARDED):
        p[k] = _unshard_last(jnp.stack([per_dev[s][i] for s in range(N_DEV)]))

    ex = _MeshExchange(w, m, v, ln_mix_g.shape[0], [gathered])
    loss_part, dx, small = _local_step(x[0], loss_target[0], p, ex)
    loss = lax.psum(loss_part[0, 0], ("x", "y", "c"))

    small_sh = jnp.stack([_pack_rows([_shards_last(small[k])[s] for k in SMALL_SHARDED]) for s in range(N_DEV)])
    got = _pair_exchange("small_grads_to_sibling", [small_sh])[0]
    final_small = _chip_exchange("small_grads_to_chips", [_pair_add("pair_add_small", small_sh, got)[0]])[0]
    repl_parts = _all_gather("gather_small_grads", [_pack_rows([small[k] for k in SMALL_REPL])])[0]

    out_g, out_d, out_m, out_v = {}, {}, {}, {}
    res = _adamw("adamw_small_sharded", *[_pack_rows([t[k] for k in SMALL_SHARDED])[None] for t in (w, m, v)],
                 final_small)
    for o, r in zip((out_g, out_d, out_m, out_v), res):
        o.update(zip(SMALL_SHARDED, _unpack_rows(r, local_shapes)))
    res = _adamw("adamw_small_repl", *[_pack_rows([t[k] for k in SMALL_REPL])[None] for t in (w, m, v)], repl_parts)
    for o, r in zip((out_g, out_d, out_m, out_v), res):
        o.update(zip(SMALL_REPL, _unpack_rows(r, [w[k].shape for k in SMALL_REPL])))
    ex.finish(res[0])
    for k in BIG:
        out_g[k], out_d[k], out_m[k], out_v[k] = ex.results(k)

    return (loss, dx[None], *[out_g[k] for k in WEIGHTS], *[out_d[k] for k in WEIGHTS],
            *[out_m[k] for k in WEIGHTS], *[out_v[k] for k in WEIGHTS])
```

```python
import functools
import math

import jax
import jax.numpy as jnp
from jax import lax
from jax.experimental import pallas as pl
from jax.experimental.pallas import tpu as pltpu

F32 = jnp.float32
MXU_DTYPE = jnp.bfloat16
WIRE_DTYPE = jnp.bfloat16
N_DEV = 8
EPS = 1e-6
F_FLOOR = 1e-30
LRU_C = 8.0
HGRN_HEADS = 8
HGRN_SUB = 16
ADAM_LR, ADAM_B1, ADAM_B2, ADAM_EPS, ADAM_WD, ADAM_STEP = 0.001, 0.9, 0.999, 1e-08, 0.01, 10
V7X_VMEM_LIMIT = 48 * 1024 * 1024
MM_ROWS = 1024
WGRAD_ROWS = 2048
MESH = pl.DeviceIdType.MESH
ANY = pl.BlockSpec(memory_space=pl.ANY)


def _cp(*sem):
    return pltpu.CompilerParams(dimension_semantics=sem or None, vmem_limit_bytes=V7X_VMEM_LIMIT)


def _sigmoid(x):
    return 1.0 / (1.0 + jnp.exp(-x))


def _silu(x):
    return x * _sigmoid(x)


def _dsilu(x):
    s = _sigmoid(x)
    return s * (1.0 + x * (1.0 - s))


_GELU_C = math.sqrt(2.0 / math.pi)
LOG2E = 1.0 / math.log(2.0)


def _gelu(x):
    return 0.5 * x * (1.0 + jnp.tanh(_GELU_C * (x + 0.044715 * x * x * x)))


def _dgelu(x):
    t = jnp.tanh(_GELU_C * (x + 0.044715 * x * x * x))
    return 0.5 * (1.0 + t) + 0.5 * x * (1.0 - t * t) * _GELU_C * (1.0 + 3.0 * 0.044715 * x * x)


def _log1p(e):
    return jnp.where(e < 1e-2, e * (1.0 - e * (0.5 - e * (1.0 / 3.0))), jnp.log(1.0 + e))


def _softplus(x):
    return jnp.maximum(x, 0.0) + _log1p(jnp.exp(-jnp.abs(x)))


def _one_minus_exp(x):
    series = -x * (1.0 + x * (0.5 + x * (1.0 / 6.0 + x * (1.0 / 24.0))))
    return jnp.where(x > -0.05, series, 1.0 - jnp.exp(x))


def _rows(n, d=1):
    return lax.broadcasted_iota(jnp.int32, (n, d), 0)


def _dot(a, b):
    return jnp.dot(a.astype(MXU_DTYPE), b.astype(MXU_DTYPE), preferred_element_type=F32)


def _dot_nt(a, b):
    return lax.dot_general(a.astype(MXU_DTYPE), b.astype(MXU_DTYPE), (((1,), (1,)), ((), ())),
                           preferred_element_type=F32)


def _dot_tn(a, b):
    return lax.dot_general(a.astype(MXU_DTYPE), b.astype(MXU_DTYPE), (((0,), (0,)), ((), ())),
                           preferred_element_type=F32)


def _tile(n, want):
    if n <= want:
        return n
    t = want - want % 8
    while n % t:
        t -= 8
    assert t > 0, (n, want)
    return t


def _my_place():
    x, y, c = lax.axis_index("x"), lax.axis_index("y"), lax.axis_index("c")
    return x, y, c


def _all_gather(name, srcs):
    n = len(srcs)

    def body(*refs):
        src_refs, out_refs = refs[:n], refs[n:2 * n]
        send_sems, recv_sems, local_sems = refs[2 * n:]
        x, y, c = _my_place()
        sibling = (x, y, 1 - c)
        chips = [(1 - x, y), (x, 1 - y), (1 - x, 1 - y)]

        def slot(px, py, pc):
            return 4 * px + 2 * py + pc

        def copy(i, k, block, to, src=None):
            dst = out_refs[i].at[slot(*block)]
            return pltpu.make_async_remote_copy(
                src_ref=dst if src is None else src, dst_ref=dst,
                send_sem=send_sems.at[i, k], recv_sem=recv_sems.at[i, k],
                device_id=to, device_id_type=MESH)

        me = (x, y, c)
        sends, own = [], []
        for i in range(n):
            mine = pltpu.make_async_copy(src_refs[i], out_refs[i].at[slot(*me)], local_sems.at[i])
            mine.start()
            own.append(mine)
            first = [copy(i, 0, me, sibling, src=src_refs[i])]
            first += [copy(i, 1 + j, me, (*chip, c), src=src_refs[i]) for j, chip in enumerate(chips)]
            for cp in first:
                cp.start()
            sends += first
        for i in range(n):
            for j, chip in enumerate(chips):
                copy(i, 1 + j, (*chip, c), me).wait_recv()
                passed = copy(i, 4 + j, (*chip, c), sibling)
                passed.start()
                sends.append(passed)
        for i in range(n):
            copy(i, 0, sibling, me).wait_recv()
            for j, chip in enumerate(chips):
                copy(i, 4 + j, (*chip, 1 - c), me).wait_recv()
        for cp in sends:
            cp.wait_send()
        for cp in own:
            cp.wait()

    outs = pl.pallas_call(
        body, name=name,
        out_shape=[jax.ShapeDtypeStruct((N_DEV,) + s.shape, s.dtype) for s in srcs],
        in_specs=[ANY] * n, out_specs=[ANY] * n,
        scratch_shapes=[pltpu.SemaphoreType.DMA((n, 7)), pltpu.SemaphoreType.DMA((n, 7)),
                        pltpu.SemaphoreType.DMA((n,))],
    )(*srcs)
    return list(outs)


def _pair_exchange(name, srcs):
    n = len(srcs)

    def body(*refs):
        src_refs, out_refs = refs[:n], refs[n:2 * n]
        send_sems, recv_sems = refs[2 * n:]
        x, y, c = _my_place()
        copies = []
        for i in range(n):
            for j in range(4):
                cp = pltpu.make_async_remote_copy(
                    src_ref=src_refs[i].at[2 * j + (1 - c)], dst_ref=out_refs[i].at[j],
                    send_sem=send_sems.at[i, j], recv_sem=recv_sems.at[i, j],
                    device_id=(x, y, 1 - c), device_id_type=MESH)
                cp.start()
                copies.append(cp)
        for cp in copies:
            cp.wait()

    outs = pl.pallas_call(
        body, name=name,
        out_shape=[jax.ShapeDtypeStruct((4,) + s.shape[1:], s.dtype) for s in srcs],
        in_specs=[ANY] * n, out_specs=[ANY] * n,
        scratch_shapes=[pltpu.SemaphoreType.DMA((n, 4)), pltpu.SemaphoreType.DMA((n, 4))],
    )(*srcs)
    return list(outs)


def _chip_exchange(name, srcs):
    n = len(srcs)

    def body(*refs):
        src_refs, out_refs = refs[:n], refs[n:2 * n]
        send_sems, recv_sems, local_sems = refs[2 * n:]
        x, y, c = _my_place()
        chip = 2 * x + y
        copies = []
        for i in range(n):
            mine = pltpu.make_async_copy(src_refs[i].at[chip], out_refs[i].at[3], local_sems.at[i])
            mine.start()
            copies.append(mine)
            for k, (fx, fy) in enumerate([(1, 0), (0, 1), (1, 1)]):
                px = x + fx - 2 * x * fx
                py = y + fy - 2 * y * fy
                cp = pltpu.make_async_remote_copy(
                    src_ref=src_refs[i].at[2 * px + py], dst_ref=out_refs[i].at[k],
                    send_sem=send_sems.at[i, k], recv_sem=recv_sems.at[i, k],
                    device_id=(px, py, c), device_id_type=MESH)
                cp.start()
                copies.append(cp)
        for cp in copies:
            cp.wait()

    outs = pl.pallas_call(
        body, name=name,
        out_shape=[jax.ShapeDtypeStruct(s.shape, s.dtype) for s in srcs],
        in_specs=[ANY] * n, out_specs=[ANY] * n,
        scratch_shapes=[pltpu.SemaphoreType.DMA((n, 3)), pltpu.SemaphoreType.DMA((n, 3)),
                        pltpu.SemaphoreType.DMA((n,))],
    )(*srcs)
    return list(outs)


def _pair_add(name, mine, got):
    assert mine.shape[0] == N_DEV and got.shape[0] == 4
    cdim = mine.shape[-1]
    m4 = mine.reshape(4, 2, -1, cdim)
    g3 = got.reshape(4, -1, cdim)
    rows = m4.shape[2]
    tr = _tile(rows, 512)

    def body(m_ref, g_ref, o_ref, fin_ref):
        x, y, c = _my_place()
        s = (m_ref[c].astype(F32) + g_ref[...].astype(F32)).astype(o_ref.dtype)
        o_ref[...] = s

        @pl.when(pl.program_id(1) == 2 * x + y)
        def _():
            fin_ref[...] = s

    out, fin = pl.pallas_call(
        body, name=name, grid=(rows // tr, 4),
        in_specs=[pl.BlockSpec((None, 2, tr, cdim), lambda i, j: (j, 0, i, 0)),
                  pl.BlockSpec((None, tr, cdim), lambda i, j: (j, i, 0))],
        out_specs=[pl.BlockSpec((None, tr, cdim), lambda i, j: (j, i, 0)),
                   pl.BlockSpec((None, tr, cdim), lambda i, j: (3, i, 0))],
        out_shape=[jax.ShapeDtypeStruct(g3.shape, got.dtype)] * 2,
        compiler_params=_cp("parallel", "arbitrary"),
    )(m4, g3)
    return out.reshape(got.shape), fin.reshape(got.shape)


HBM = pl.BlockSpec(memory_space=pltpu.HBM)
SEM = pl.BlockSpec(memory_space=pltpu.SEMAPHORE)
EFFECT = pltpu.SideEffectType.DATAFLOW_SIDE_EFFECTING
SLOTS = 4


def _hbm(a):
    return pltpu.with_memory_space_constraint(a, pltpu.HBM)


def _remote(src, dst, sems, i, k, to):
    return pltpu.make_async_remote_copy(src_ref=src, dst_ref=dst, send_sem=sems[0].at[i * SLOTS + k],
                                        recv_sem=sems[1].at[i * SLOTS + k], device_id=to, device_id_type=MESH)


def _slot(px, py, pc):
    return 4 * px + 2 * py + pc


def _other_chips(x, y):
    return [(1 - x, y), (x, 1 - y), (1 - x, 1 - y)]


def _plan_gather_own(layers):
    def plan(srcs, lands, sems):
        x, y, c = _my_place()
        out = []
        for i, j in enumerate(layers):
            dst = lands[i].at[_slot(x, y, c)]
            out.append(_remote(srcs[i].at[j], dst, sems, i, 0, (x, y, 1 - c)))
            for k, (px, py) in enumerate(_other_chips(x, y)):
                out.append(_remote(srcs[i].at[j], dst, sems, i, 1 + k, (px, py, c)))
        return out
    return plan


def _plan_gather_pass(n):
    def plan(srcs, lands, sems):
        x, y, c = _my_place()
        out = []
        for i in range(n):
            for k, (px, py) in enumerate(_other_chips(x, y)):
                blk = lands[i].at[_slot(px, py, c)]
                out.append(_remote(blk, blk, sems, i, k, (x, y, 1 - c)))
        return out
    return plan


def _plan_scatter_pair(n):
    def plan(srcs, lands, sems):
        x, y, c = _my_place()
        return [_remote(srcs[i].at[2 * j + (1 - c)], lands[i].at[j], sems, i, j, (x, y, 1 - c))
                for i in range(n) for j in range(4)]
    return plan


def _plan_scatter_chips(n):
    def plan(srcs, lands, sems):
        x, y, c = _my_place()
        return [_remote(srcs[i].at[2 * px + py], lands[i].at[k], sems, i, k, (px, py, c))
                for i in range(n) for k, (px, py) in enumerate(_other_chips(x, y))]
    return plan


def _split_start(name, plan, srcs, lands, deps=()):
    ns, nl, nd = len(srcs), len(lands), len(deps)
    n = max(ns, nl)

    def body(*refs):
        sems = refs[ns + nl + nd:ns + nl + nd + 2]
        for cp in plan(refs[:ns], refs[ns:ns + nl], sems):
            cp.start()
        refs[-1][...] = jnp.zeros_like(refs[-1])

    outs = pl.pallas_call(
        body, name=name,
        out_shape=(pltpu.SemaphoreType.DMA((n * SLOTS,)), pltpu.SemaphoreType.DMA((n * SLOTS,)),
                   *[pltpu.HBM(a.shape, a.dtype) for a in lands], jax.ShapeDtypeStruct((8, 128), F32)),
        in_specs=[HBM] * (ns + nl) + [ANY] * nd,
        out_specs=(SEM, SEM, *[HBM] * nl, pl.BlockSpec(memory_space=pltpu.VMEM)),
        input_output_aliases={ns + i: 2 + i for i in range(nl)},
        compiler_params=pltpu.CompilerParams(has_side_effects=EFFECT),
    )(*[_hbm(a) for a in srcs], *[_hbm(a) for a in lands], *deps)
    return (outs[0], outs[1]), list(outs[2:2 + nl]), outs[-1]


def _split_wait(name, plan, sems, srcs, lands, deps=()):
    ns, nl, nd = len(srcs), len(lands), len(deps)

    def body(*refs):
        for cp in plan(refs[:ns], refs[ns:ns + nl], refs[ns + nl:ns + nl + 2]):
            cp.wait_send()
            cp.wait_recv()

    outs = pl.pallas_call(
        body, name=name,
        out_shape=tuple(pltpu.HBM(a.shape, a.dtype) for a in lands),
        in_specs=[HBM] * (ns + nl) + [SEM, SEM] + [ANY] * nd,
        out_specs=tuple([HBM] * nl),
        input_output_aliases={ns + i: i for i in range(nl)},
        compiler_params=pltpu.CompilerParams(has_side_effects=EFFECT),
    )(*srcs, *lands, *sems, *deps)
    return list(outs)


def _place_own(srcs, layers):
    x, y, c = _my_place()
    zero = jnp.zeros((), jnp.int32)
    return [lax.dynamic_update_slice(lax.empty((N_DEV,) + a.shape[1:], a.dtype), a[j][None],
                                     (_slot(x, y, c),) + (zero,) * (a.ndim - 1))
            for a, j in zip(srcs, layers)]


def _put(dp_ref, db_ref, lo, hi, val):
    dp_ref[:, lo:hi] = val.astype(dp_ref.dtype)
    db_ref[:, lo:hi] += jnp.sum(val, axis=0, keepdims=True)


def _with_deps(body, n_in, deps):
    nd = len(deps)
    if not nd:
        return body

    def wrapped(*refs):
        body(*refs[:n_in], *refs[n_in + nd:])

    return wrapped


def _rmsnorm_fwd(name, x, g, deps=()):
    T, D = x.shape
    tm = _tile(T, 256)

    def body(x_ref, g_ref, o_ref):
        xv = x_ref[...]
        r = lax.rsqrt(jnp.mean(xv * xv, axis=-1, keepdims=True) + EPS)
        o_ref[...] = ((xv * r) * g_ref[...]).astype(o_ref.dtype)

    return pl.pallas_call(
        _with_deps(body, 2, deps), name=name, grid=(T // tm,),
        in_specs=[pl.BlockSpec((tm, D), lambda i: (i, 0)), pl.BlockSpec((1, D), lambda i: (0, 0))]
        + [ANY] * len(deps),
        out_specs=pl.BlockSpec((tm, D), lambda i: (i, 0)),
        out_shape=jax.ShapeDtypeStruct((T, D), MXU_DTYPE), compiler_params=_cp("parallel"),
    )(x, g, *deps)


def _rmsnorm_bwd(name, x, g, dh, dres):
    T, D = x.shape
    tm = _tile(T, 256)

    def body(x_ref, g_ref, dh_ref, dres_ref, dx_ref, dxb_ref, dg_ref):
        xv = x_ref[...]
        r = lax.rsqrt(jnp.mean(xv * xv, axis=-1, keepdims=True) + EPS)
        xh = xv * r
        dhv = dh_ref[...]

        @pl.when(pl.program_id(0) == 0)
        def _():
            dg_ref[...] = jnp.zeros_like(dg_ref)

        dg_ref[...] += jnp.sum(dhv * xh, axis=0, keepdims=True)
        dxh = dhv * g_ref[...]
        dx = dres_ref[...] + r * (dxh - xh * jnp.mean(dxh * xh, axis=-1, keepdims=True))
        dx_ref[...] = dx
        dxb_ref[...] = dx.astype(dxb_ref.dtype)

    return pl.pallas_call(
        body, name=name, grid=(T // tm,),
        in_specs=[pl.BlockSpec((tm, D), lambda i: (i, 0)), pl.BlockSpec((1, D), lambda i: (0, 0)),
                  pl.BlockSpec((tm, D), lambda i: (i, 0)), pl.BlockSpec((tm, D), lambda i: (i, 0))],
        out_specs=[pl.BlockSpec((tm, D), lambda i: (i, 0)), pl.BlockSpec((tm, D), lambda i: (i, 0)),
                   pl.BlockSpec((1, D), lambda i: (0, 0))],
        out_shape=[jax.ShapeDtypeStruct((T, D), F32), jax.ShapeDtypeStruct((T, D), MXU_DTYPE),
                   jax.ShapeDtypeStruct((1, D), F32)],
        compiler_params=_cp("arbitrary"),
    )(x, g, dh, dres)


def _loss_head(name, x, g, tgt):
    T, D = x.shape
    tm = _tile(T, 256)

    def body(x_ref, g_ref, t_ref, loss_ref, dx_ref, dxb_ref, dg_ref):
        xv = x_ref[...]
        r = lax.rsqrt(jnp.mean(xv * xv, axis=-1, keepdims=True) + EPS)
        xh = xv * r
        gv = g_ref[...]
        diff = xh * gv - t_ref[...]

        @pl.when(pl.program_id(0) == 0)
        def _():
            dg_ref[...] = jnp.zeros_like(dg_ref)
            loss_ref[...] = jnp.zeros_like(loss_ref)

        part = 0.5 * jnp.sum(jnp.mean(diff * diff, axis=-1, keepdims=True), axis=0, keepdims=True)
        loss_ref[...] += jnp.broadcast_to(part, loss_ref.shape)
        dy = diff * (1.0 / D)
        dg_ref[...] += jnp.sum(dy * xh, axis=0, keepdims=True)
        dxh = dy * gv
        dx = r * (dxh - xh * jnp.mean(dxh * xh, axis=-1, keepdims=True))
        dx_ref[...] = dx
        dxb_ref[...] = dx.astype(dxb_ref.dtype)

    return pl.pallas_call(
        body, name=name, grid=(T // tm,),
        in_specs=[pl.BlockSpec((tm, D), lambda i: (i, 0)), pl.BlockSpec((1, D), lambda i: (0, 0)),
                  pl.BlockSpec((tm, D), lambda i: (i, 0))],
        out_specs=[pl.BlockSpec((1, 128), lambda i: (0, 0)), pl.BlockSpec((tm, D), lambda i: (i, 0)),
                   pl.BlockSpec((tm, D), lambda i: (i, 0)), pl.BlockSpec((1, D), lambda i: (0, 0))],
        out_shape=[jax.ShapeDtypeStruct((1, 128), F32), jax.ShapeDtypeStruct((T, D), F32),
                   jax.ShapeDtypeStruct((T, D), MXU_DTYPE), jax.ShapeDtypeStruct((1, D), F32)],
        compiler_params=_cp("arbitrary"),
    )(x, g, tgt)


def _adamw(name, w, m, v, parts, j=0, prev=None):
    L, R, C = w.shape
    P = parts.shape[0]
    tr = _tile(R, 256)
    ob = j * (R // tr)
    w, m, v = (a.reshape(L * R, C) for a in (w, m, v))
    c1 = 1.0 / (1.0 - ADAM_B1 ** ADAM_STEP)
    c2 = 1.0 / (1.0 - ADAM_B2 ** ADAM_STEP)
    chained = L > 1
    if chained and prev is None:
        prev = [lax.empty(w.shape, F32) for _ in range(4)]
    prev = list(prev) if chained else []

    def body(w_ref, m_ref, v_ref, p_ref, *rest):
        g_ref, d_ref, nm_ref, nv_ref, done_ref = rest[len(prev):]
        done_ref[...] = jnp.zeros_like(done_ref)
        g = p_ref[0].astype(F32)
        for s in range(1, P):
            g = g + p_ref[s].astype(F32)
        nm = ADAM_B1 * m_ref[...] + (1.0 - ADAM_B1) * g
        nv = ADAM_B2 * v_ref[...] + (1.0 - ADAM_B2) * (g * g)
        g_ref[...] = g
        nm_ref[...] = nm
        nv_ref[...] = nv
        d_ref[...] = -ADAM_LR * ((nm * c1) / (jnp.sqrt(nv * c2) + ADAM_EPS) + ADAM_WD * w_ref[...])

    blk = pl.BlockSpec((tr, C), lambda i: (i + ob, 0))
    return pl.pallas_call(
        body, name=name, grid=(R // tr,),
        in_specs=[blk, blk, blk, pl.BlockSpec((P, tr, C), lambda i: (0, i, 0))] + [ANY] * len(prev),
        out_specs=[blk, blk, blk, blk, pl.BlockSpec((8, 128), lambda i: (0, 0))],
        out_shape=[jax.ShapeDtypeStruct(w.shape, F32)] * 4 + [jax.ShapeDtypeStruct((8, 128), F32)],
        input_output_aliases={4 + i: i for i in range(len(prev))},
        compiler_params=_cp("arbitrary"),
    )(w, m, v, parts, *prev)


def _proj_in(name, h, wg, bias, deps=()):
    T, K = h.shape
    n = wg.shape[-1]
    tm = _tile(T, MM_ROWS)

    def body(a_ref, w_ref, b_ref, o_ref):
        o_ref[...] = _dot(a_ref[...], w_ref[...]) + b_ref[...]

    return pl.pallas_call(
        _with_deps(body, 3, deps), name=name, grid=(N_DEV, T // tm),
        in_specs=[pl.BlockSpec((tm, K), lambda s, i: (i, 0)),
                  pl.BlockSpec((None, K, n), lambda s, i: (s, 0, 0)),
                  pl.BlockSpec((1, n), lambda s, i: (0, s))] + [ANY] * len(deps),
        out_specs=pl.BlockSpec((tm, n), lambda s, i: (i, s)),
        out_shape=jax.ShapeDtypeStruct((T, N_DEV * n), F32), compiler_params=_cp("parallel", "parallel"),
    )(h, wg, bias, *deps)


def _ffn_in(name, h, wg_gate, wg_up):
    T, K = h.shape
    n = wg_gate.shape[-1]
    tm = _tile(T, MM_ROWS)

    def body(a_ref, wgt_ref, wup_ref, g_ref, u_ref, hid_ref):
        a = a_ref[...]
        g = _dot(a, wgt_ref[...])
        u = _dot(a, wup_ref[...])
        g_ref[...] = g
        u_ref[...] = u
        hid_ref[...] = (_silu(g) * u).astype(hid_ref.dtype)

    wspec = pl.BlockSpec((None, K, n), lambda s, i: (s, 0, 0))
    ospec = pl.BlockSpec((None, tm, n), lambda s, i: (s, i, 0))
    return pl.pallas_call(
        body, name=name, grid=(N_DEV, T // tm),
        in_specs=[pl.BlockSpec((tm, K), lambda s, i: (i, 0)), wspec, wspec],
        out_specs=[ospec, ospec, ospec],
        out_shape=[jax.ShapeDtypeStruct((N_DEV, T, n), F32), jax.ShapeDtypeStruct((N_DEV, T, n), F32),
                   jax.ShapeDtypeStruct((N_DEV, T, n), MXU_DTYPE)],
        compiler_params=_cp("parallel", "parallel"),
    )(h, wg_gate, wg_up)


def _a_spec(a, tm, k):
    if a.ndim == 2:
        return pl.BlockSpec((tm, k), lambda i, s: (i, s))
    return pl.BlockSpec((None, tm, k), lambda i, s: (s, i, 0))


def _proj_out(name, a, wg, res, deps=(), norm_g=None):
    k, N = wg.shape[-2:]
    T = res.shape[0]
    tm = _tile(T, MM_ROWS // 2)
    extra = [] if norm_g is None else [norm_g]

    def body(a_ref, w_ref, r_ref, *rest):
        o_ref = rest[len(extra)]
        p = _dot(a_ref[...], w_ref[...])

        @pl.when(pl.program_id(1) == 0)
        def _():
            o_ref[...] = r_ref[...] + p

        @pl.when(pl.program_id(1) > 0)
        def _():
            o_ref[...] += p

        if extra:
            @pl.when(pl.program_id(1) == N_DEV - 1)
            def _():
                xv = o_ref[...]
                r = lax.rsqrt(jnp.mean(xv * xv, axis=-1, keepdims=True) + EPS)
                rest[2][...] = ((xv * r) * rest[0][...]).astype(rest[2].dtype)

    row_blk = pl.BlockSpec((tm, N), lambda i, s: (i, 0))
    out = pl.pallas_call(
        _with_deps(body, 3 + len(extra), deps), name=name, grid=(T // tm, N_DEV),
        in_specs=[_a_spec(a, tm, k), pl.BlockSpec((None, k, N), lambda i, s: (s, 0, 0)), row_blk]
        + [pl.BlockSpec((1, N), lambda i, s: (0, 0))] * len(extra) + [ANY] * len(deps),
        out_specs=[row_blk] * (1 + len(extra)),
        out_shape=[jax.ShapeDtypeStruct((T, N), F32)] + [jax.ShapeDtypeStruct((T, N), MXU_DTYPE)] * len(extra),
        compiler_params=_cp("parallel", "arbitrary"),
    )(a, wg, res, *extra, *deps)
    return out[0] if norm_g is None else out


def _mix_out(name, y, wg, res, norm_g, deps=()):
    N = wg.shape[-1]
    w2 = wg.reshape(-1, N)
    K = w2.shape[0]
    T = res.shape[0]
    tm = _tile(T, MM_ROWS // 2)

    def body(a_ref, w_ref, r_ref, g_ref, o_ref, h_ref):
        xv = r_ref[...] + _dot(a_ref[...], w_ref[...])
        o_ref[...] = xv
        r = lax.rsqrt(jnp.mean(xv * xv, axis=-1, keepdims=True) + EPS)
        h_ref[...] = ((xv * r) * g_ref[...]).astype(h_ref.dtype)

    row_blk = pl.BlockSpec((tm, N), lambda i: (i, 0))
    return pl.pallas_call(
        _with_deps(body, 4, deps), name=name, grid=(T // tm,),
        in_specs=[pl.BlockSpec((tm, K), lambda i: (i, 0)),
                  pl.BlockSpec((K, N), lambda i: (0, 0), pipeline_mode=pl.Buffered(1)),
                  row_blk, pl.BlockSpec((1, N), lambda i: (0, 0))] + [ANY] * len(deps),
        out_specs=[row_blk, row_blk],
        out_shape=[jax.ShapeDtypeStruct((T, N), F32), jax.ShapeDtypeStruct((T, N), MXU_DTYPE)],
        compiler_params=_cp("parallel"),
    )(y, w2, res, norm_g, *deps)


def _bwd_in(name, das, wgs, deps=()):
    K, n = wgs[0].shape[-2:]
    T = das[0].shape[-2]
    tm = _tile(T, MM_ROWS)
    npair = len(das)

    def body(*refs):
        o_ref = refs[-1]
        p = _dot_nt(refs[0][...], refs[npair][...])
        for q in range(1, npair):
            p = p + _dot_nt(refs[q][...], refs[npair + q][...])

        @pl.when(pl.program_id(1) == 0)
        def _():
            o_ref[...] = p

        @pl.when(pl.program_id(1) > 0)
        def _():
            o_ref[...] += p

    return pl.pallas_call(
        _with_deps(body, 2 * npair, deps), name=name, grid=(T // tm, N_DEV),
        in_specs=[_a_spec(a, tm, n) for a in das]
        + [pl.BlockSpec((None, K, n), lambda i, s: (s, 0, 0)) for _ in wgs] + [ANY] * len(deps),
        out_specs=pl.BlockSpec((tm, K), lambda i, s: (i, 0)),
        out_shape=jax.ShapeDtypeStruct((T, K), F32), compiler_params=_cp("parallel", "arbitrary"),
    )(*das, *wgs, *deps)


def _bwd_out(name, dx, wg, deps=()):
    k, N = wg.shape[-2:]
    T = dx.shape[0]
    tm = _tile(T, MM_ROWS)

    def body(a_ref, w_ref, o_ref):
        o_ref[...] = _dot_nt(a_ref[...], w_ref[...])

    return pl.pallas_call(
        _with_deps(body, 2, deps), name=name, grid=(N_DEV, T // tm),
        in_specs=[pl.BlockSpec((tm, N), lambda s, i: (i, 0)),
                  pl.BlockSpec((None, k, N), lambda s, i: (s, 0, 0))] + [ANY] * len(deps),
        out_specs=pl.BlockSpec((tm, k), lambda s, i: (i, s)),
        out_shape=jax.ShapeDtypeStruct((T, N_DEV * k), F32), compiler_params=_cp("parallel", "parallel"),
    )(dx, wg, *deps)


def _ffn_bwd_hidden(name, dx, wg_down, gate, up, deps=()):
    n, N = wg_down.shape[-2:]
    T = dx.shape[0]
    tm = _tile(T, MM_ROWS)

    def body(a_ref, w_ref, g_ref, u_ref, dg_ref, du_ref):
        dh = _dot_nt(a_ref[...], w_ref[...])
        g = g_ref[...]
        dg_ref[...] = (dh * u_ref[...] * _dsilu(g)).astype(dg_ref.dtype)
        du_ref[...] = (dh * _silu(g)).astype(du_ref.dtype)

    sm = pl.BlockSpec((None, tm, n), lambda s, i: (s, i, 0))
    return pl.pallas_call(
        _with_deps(body, 4, deps), name=name, grid=(N_DEV, T // tm),
        in_specs=[pl.BlockSpec((tm, N), lambda s, i: (i, 0)),
                  pl.BlockSpec((None, n, N), lambda s, i: (s, 0, 0)), sm, sm] + [ANY] * len(deps),
        out_specs=[sm, sm],
        out_shape=[jax.ShapeDtypeStruct((N_DEV, T, n), MXU_DTYPE)] * 2,
        compiler_params=_cp("parallel", "parallel"),
    )(dx, wg_down, gate, up, *deps)


def _wgrad(name, a, c, rows, cols, deps=()):
    T = a.shape[-2]
    tk = _tile(T, WGRAD_ROWS)
    nk = T // tk

    def spec(z, w):
        if z.ndim == 3:
            return pl.BlockSpec((None, tk, w), lambda s, k: (s, k, 0))
        if z.shape[1] == w:
            return pl.BlockSpec((tk, w), lambda s, k: (k, 0))
        return pl.BlockSpec((tk, w), lambda s, k: (k, s))

    def body(a_ref, c_ref, o_ref, acc_ref):
        k = pl.program_id(1)
        p = _dot_tn(a_ref[...], c_ref[...])

        @pl.when(k == 0)
        def _():
            acc_ref[...] = p

        @pl.when(k > 0)
        def _():
            acc_ref[...] += p

        @pl.when(k == nk - 1)
        def _():
            o_ref[...] = acc_ref[...].astype(o_ref.dtype)

    return pl.pallas_call(
        _with_deps(body, 2, deps), name=name, grid=(N_DEV, nk),
        in_specs=[spec(a, rows), spec(c, cols)] + [ANY] * len(deps),
        out_specs=pl.BlockSpec((None, rows, cols), lambda s, k: (s, 0, 0)),
        out_shape=jax.ShapeDtypeStruct((N_DEV, rows, cols), WIRE_DTYPE),
        scratch_shapes=[pltpu.VMEM((rows, cols), F32)],
        compiler_params=_cp("parallel", "arbitrary"),
    )(a, c, *deps)


def _shift_down(cur, prev8, sh):
    n = cur.shape[0]
    rolled = pltpu.roll(cur, sh, 0)
    top = jnp.where(_rows(8) < sh, pltpu.roll(prev8, sh, 0), rolled[0:8])
    return jnp.concatenate([top, rolled[8:n]], axis=0)


def _shift_up(cur, next8, sh):
    n = cur.shape[0]
    rolled = pltpu.roll(cur, n - sh, 0)
    bot = jnp.where(_rows(8) >= 8 - sh, pltpu.roll(next8, 8 - sh, 0), rolled[n - 8:n])
    return jnp.concatenate([rolled[0:n - 8], bot], axis=0)


def _lru_gate_terms(r, lam):
    sp = _softplus(-lam)
    la = -LRU_C * r * sp
    a = jnp.exp(la)
    m2 = _one_minus_exp(2.0 * la)
    return sp, la, a, m2


def _lru_fwd(name, proj, conv_w, conv_b, wa, ba, wx, bx, lam):
    T = proj.shape[0]
    H, hd, _ = wa.shape
    W = H * hd
    K = conv_w.shape[0]
    tb = _tile(T, 256)

    def body(xin_ref, gate_ref, cw_ref, cb_ref, wa_ref, ba_ref, wx_ref, bx_ref, lam_ref,
             ya_ref, xc_ref, r_ref, i_ref, hs_ref, tail_ref, hprev_ref):
        blk = pl.program_id(0)

        @pl.when(blk == 0)
        def _():
            tail_ref[...] = jnp.zeros_like(tail_ref)
            hprev_ref[...] = jnp.zeros_like(hprev_ref)

        xin = xin_ref[...]
        prev8 = tail_ref[...]
        xc = cw_ref[K - 1:K, :] * xin
        for sh in range(1, K):
            xc = xc + cw_ref[K - 1 - sh:K - sh, :] * _shift_down(xin, prev8, sh)
        xc = xc + cb_ref[...]
        tail_ref[...] = xin[tb - 8:tb]
        xc_ref[...] = xc
        for h in range(H):
            cs = slice(h * hd, (h + 1) * hd)
            xh = xc[:, cs]
            r_ref[:, cs] = _sigmoid(_dot(xh, wa_ref[h]) + ba_ref[:, cs])
            i_ref[:, cs] = _sigmoid(_dot(xh, wx_ref[h]) + bx_ref[:, cs])
        r = r_ref[...]
        _, _, a, m2 = _lru_gate_terms(r, lam_ref[...])
        row = _rows(tb)
        mult = jnp.where((row == 0) & (blk == 0), 1.0, jnp.sqrt(jnp.maximum(m2, 0.0)))
        u = mult * i_ref[...] * xc
        d = 1
        while d < tb:
            keep = row >= d
            u = a * jnp.where(keep, pltpu.roll(u, d, 0), 0.0) + u
            a = a * jnp.where(keep, pltpu.roll(a, d, 0), 1.0)
            d *= 2
        hs = u + a * hprev_ref[...]
        hprev_ref[...] = hs[tb - 1:tb]
        hs_ref[...] = hs
        ya_ref[...] = (hs * _gelu(gate_ref[...])).astype(ya_ref.dtype)

    full = lambda shape: pl.BlockSpec(shape, lambda i: tuple(0 for _ in shape))
    blk = pl.BlockSpec((tb, W), lambda i: (i, 0))
    return pl.pallas_call(
        body, name=name, grid=(T // tb,),
        in_specs=[pl.BlockSpec((tb, W), lambda i: (i, 0)), pl.BlockSpec((tb, W), lambda i: (i, 1)),
                  full((K, W)), full((1, W)), full((H, hd, hd)), full((1, W)), full((H, hd, hd)),
                  full((1, W)), full((1, W))],
        out_specs=[blk] * 5,
        out_shape=[jax.ShapeDtypeStruct((T, 2 * W), MXU_DTYPE)] + [jax.ShapeDtypeStruct((T, W), F32)] * 4,
        scratch_shapes=[pltpu.VMEM((8, W), F32), pltpu.VMEM((1, W), F32)],
        compiler_params=_cp("arbitrary"),
    )(proj, proj, conv_w, conv_b, wa, ba, wx, bx, lam)


def _lru_bwd(name, proj, dy, xc, r, ig, hs, conv_w, wa, wx, lam, dpbuf):
    T = proj.shape[0]
    H, hd, _ = wa.shape
    W = H * hd
    K = conv_w.shape[0]
    tb = _tile(T, 256)
    nb = T // tb
    t8 = tb // 8

    def body(xin_ref, xprev_ref, gate_ref, dy_ref, xc_ref, r_ref, i_ref, hs_ref, hsprev_ref,
             cw_ref, wa_ref, wx_ref, lam_ref, dpbuf_ref,
             dp_ref, db_ref, dcw_ref, dcb_ref, dwa_ref, dba_ref, dwx_ref, dbx_ref, dlam_ref,
             cdh_ref, ca_ref, cdxc_ref, dxc_ref):
        del dpbuf_ref
        step = pl.program_id(0)
        blk = nb - 1 - step

        @pl.when(step == 0)
        def _():
            for ref in (db_ref, dcw_ref, dcb_ref, dwa_ref, dba_ref, dwx_ref, dbx_ref, dlam_ref,
                        cdh_ref, ca_ref, cdxc_ref):
                ref[...] = jnp.zeros_like(ref)

        row = _rows(tb)
        first = blk == 0
        gate = gate_ref[...]
        dy_a = dy_ref[...]
        hsv = hs_ref[...]
        _put(dp_ref, db_ref, W, 2 * W, dy_a * hsv * _dgelu(gate))
        d_hs = dy_a * _gelu(gate)
        lam = lam_ref[...]
        rv = r_ref[...]
        sp, la, a, m2 = _lru_gate_terms(rv, lam)
        an = jnp.where(row == tb - 1, ca_ref[...], pltpu.roll(a, tb - 1, 0))
        u = d_hs
        d = 1
        while d < tb:
            keep = row < tb - d
            u = an * jnp.where(keep, pltpu.roll(u, tb - d, 0), 0.0) + u
            an = an * jnp.where(keep, pltpu.roll(an, tb - d, 0), 1.0)
            d *= 2
        dh = u + an * cdh_ref[...]
        cdh_ref[...] = dh[0:1]
        ca_ref[...] = a[0:1]
        hlast = jnp.where(first, 0.0, hsprev_ref[7:8, :])
        hprev = jnp.where(row == 0, hlast, pltpu.roll(hsv, 1, 0))
        da = dh * hprev
        xcv = xc_ref[...]
        iv = i_ref[...]
        t0 = (row == 0) & first
        mult = jnp.sqrt(jnp.maximum(m2, 0.0))
        mult_eff = jnp.where(t0, 1.0, mult)
        d_mult = dh * iv * xcv
        d_i = dh * mult_eff * xcv
        dxc = dh * mult_eff * iv
        e2 = 1.0 - m2
        d_la = da * a + jnp.where(t0 | (m2 <= 0.0), 0.0, -d_mult * e2 / jnp.where(m2 > 0.0, mult, 1.0))
        d_r = d_la * (-LRU_C * sp)
        dlam_ref[...] += jnp.sum(d_la * (-LRU_C * rv), axis=0, keepdims=True) * (-_sigmoid(-lam))
        d_zr = d_r * rv * (1.0 - rv)
        d_zi = d_i * iv * (1.0 - iv)
        dba_ref[...] += jnp.sum(d_zr, axis=0, keepdims=True)
        dbx_ref[...] += jnp.sum(d_zi, axis=0, keepdims=True)
        for h in range(H):
            cs = slice(h * hd, (h + 1) * hd)
            xh = xcv[:, cs]
            zr, zi = d_zr[:, cs], d_zi[:, cs]
            dwa_ref[h] += _dot_tn(xh, zr)
            dwx_ref[h] += _dot_tn(xh, zi)
            dxc_ref[:, cs] = dxc[:, cs] + _dot_nt(zr, wa_ref[h]) + _dot_nt(zi, wx_ref[h])
        dxc = dxc_ref[...]
        dcb_ref[...] += jnp.sum(dxc, axis=0, keepdims=True)
        xin = xin_ref[...]
        prev8 = jnp.where(first, 0.0, xprev_ref[...])
        next8 = cdxc_ref[...]
        dxin = cw_ref[K - 1:K, :] * dxc
        dcw_ref[K - 1:K, :] += jnp.sum(dxc * xin, axis=0, keepdims=True)
        for sh in range(1, K):
            dxin = dxin + cw_ref[K - 1 - sh:K - sh, :] * _shift_up(dxc, next8, sh)
            dcw_ref[K - 1 - sh:K - sh, :] += jnp.sum(dxc * _shift_down(xin, prev8, sh), axis=0, keepdims=True)
        cdxc_ref[...] = dxc[0:8]
        _put(dp_ref, db_ref, 0, W, dxin)

    full = lambda shape: pl.BlockSpec(shape, lambda i: tuple(0 for _ in shape))
    cur = lambda col: pl.BlockSpec((tb, W), lambda i: (nb - 1 - i, col))
    prev = pl.BlockSpec((8, W), lambda i: (jnp.maximum((nb - 1 - i) * t8 - 1, 0), 0))
    return pl.pallas_call(
        body, name=name, grid=(nb,),
        in_specs=[cur(0), prev, cur(1), cur(0), cur(0), cur(0), cur(0), cur(0), prev,
                  full((K, W)), full((H, hd, hd)), full((H, hd, hd)), full((1, W)), ANY],
        out_specs=[pl.BlockSpec((tb, 2 * W), lambda i: (nb - 1 - i, 0)), full((1, 2 * W)), full((K, W)),
                   full((1, W)),
                   full((H, hd, hd)), full((1, W)), full((H, hd, hd)), full((1, W)), full((1, W))],
        out_shape=[jax.ShapeDtypeStruct(dpbuf.shape, dpbuf.dtype), jax.ShapeDtypeStruct((1, 2 * W), F32),
                   jax.ShapeDtypeStruct((K, W), F32),
                   jax.ShapeDtypeStruct((1, W), F32), jax.ShapeDtypeStruct((H, hd, hd), F32),
                   jax.ShapeDtypeStruct((1, W), F32), jax.ShapeDtypeStruct((H, hd, hd), F32),
                   jax.ShapeDtypeStruct((1, W), F32), jax.ShapeDtypeStruct((1, W), F32)],
        scratch_shapes=[pltpu.VMEM((1, W), F32), pltpu.VMEM((1, W), F32), pltpu.VMEM((8, W), F32),
                        pltpu.VMEM((tb, W), F32)],
        input_output_aliases={13: 0}, compiler_params=_cp("arbitrary"),
    )(proj, proj, proj, dy, xc, r, ig, hs, hs, conv_w, wa, wx, lam, dpbuf)


def _chunk_cumsum(g, c):
    n = g.shape[0]
    rc = _rows(n) & (c - 1)
    d = 1
    while d < c:
        g = g + jnp.where(rc >= d, pltpu.roll(g, d, 0), 0.0)
        d *= 2
    return g


def _chunk_rcumsum(g, c):
    n = g.shape[0]
    rc = _rows(n) & (c - 1)
    d = 1
    while d < c:
        g = g + jnp.where(rc < c - d, pltpu.roll(g, n - d, 0), 0.0)
        d *= 2
    return g


def _hgrn_pointwise(qr, fr, lb):
    qf = _silu(qr)
    sig = _sigmoid(fr)
    fg = lb + (1.0 - lb) * sig
    gl = jnp.log(jnp.maximum(fg, F_FLOOR))
    kk = (1.0 - lb) * (1.0 - sig)
    return qf, sig, fg, gl, kk


def _hgrn_fwd(name, proj, lb, norm_g, ybuf):
    T = proj.shape[0]
    W = lb.shape[1]
    H = HGRN_HEADS
    dk = W // H
    c = HGRN_SUB
    R = _tile(T, 128)
    nck = R // c

    def body(q_ref, f_ref, v_ref, g_ref, lb_ref, ng_ref, ybuf_ref, yb_ref, o_ref, sall_ref,
             st_ref, qe_ref, ke_ref, acc_ref):
        del ybuf_ref

        @pl.when(pl.program_id(0) == 0)
        def _():
            st_ref[...] = jnp.zeros_like(st_ref)

        qf, _, _, gl, kk = _hgrn_pointwise(q_ref[...], f_ref[...], lb_ref[...])
        b = _chunk_cumsum(gl * LOG2E, c)
        rc = _rows(R) & (c - 1)
        for h in range(H):
            cs = slice(h * dk, (h + 1) * dk)
            qh, kh, bh, vh = qf[:, cs], kk[:, cs], b[:, cs], v_ref[:, cs]
            acc = jnp.sum(qh * kh, axis=1, keepdims=True) * vh
            for d in range(1, c):
                e = jnp.where(rc >= d, jnp.exp2(bh - pltpu.roll(bh, d, 0)), 0.0)
                s = jnp.sum(qh * pltpu.roll(kh, d, 0) * e, axis=1, keepdims=True)
                acc = acc + s * pltpu.roll(vh, d, 0)
            acc_ref[:, cs] = acc
        qe_ref[...] = qf * jnp.exp2(b)
        for ci in range(nck):
            rs = slice(ci * c, (ci + 1) * c)
            bl = b[ci * c + c - 1:ci * c + c, :]
            ke_ref[rs, :] = kk[rs, :] * jnp.exp2(bl - b[rs, :])
            ebl = jnp.exp2(bl)
            for h in range(H):
                cs = slice(h * dk, (h + 1) * dk)
                st = st_ref[h]
                sall_ref[ci, h] = st
                o_ref[rs, cs] = acc_ref[rs, cs] + _dot_nt(qe_ref[rs, cs], st)
                st_ref[h] = st * ebl[:, cs] + _dot_tn(v_ref[rs, cs], ke_ref[rs, cs])
        ng = ng_ref[...]
        gg = g_ref[...]
        for h in range(H):
            cs = slice(h * dk, (h + 1) * dk)
            oh = o_ref[:, cs]
            rr = lax.rsqrt(jnp.mean(oh * oh, axis=1, keepdims=True) + EPS)
            yb_ref[:, cs] = ((oh * rr) * ng[:, cs] * _silu(gg[:, cs])).astype(yb_ref.dtype)

    full = lambda shape: pl.BlockSpec(shape, lambda i: tuple(0 for _ in shape))
    col = lambda k: pl.BlockSpec((R, W), lambda i: (i, k))
    blk = pl.BlockSpec((R, W), lambda i: (i, 0))
    return pl.pallas_call(
        body, name=name, grid=(T // R,),
        in_specs=[col(2), col(3), col(4), col(5), full((1, W)), full((1, W)), ANY],
        out_specs=[col(1), blk, pl.BlockSpec((nck, H, dk, dk), lambda i: (i, 0, 0, 0))],
        out_shape=[jax.ShapeDtypeStruct((T, 2 * W), MXU_DTYPE), jax.ShapeDtypeStruct((T, W), F32),
                   jax.ShapeDtypeStruct((T // c, H, dk, dk), F32)],
        scratch_shapes=[pltpu.VMEM((H, dk, dk), F32), pltpu.VMEM((R, W), F32), pltpu.VMEM((R, W), F32),
                        pltpu.VMEM((R, W), F32)],
        input_output_aliases={6: 0}, compiler_params=_cp("arbitrary"),
    )(proj, proj, proj, proj, lb, norm_g, ybuf)


def _hgrn_bwd(name, proj, dy, o, sall, lb, norm_g):
    T = proj.shape[0]
    W = lb.shape[1]
    H = HGRN_HEADS
    dk = W // H
    c = HGRN_SUB
    R = _tile(T, 128)
    nck = R // c
    nb = T // R

    def body(q_ref, f_ref, v_ref, g_ref, dy_ref, o_ref, sall_ref, lb_ref, ng_ref,
             dp_ref, db_ref, dlb_ref, dng_ref,
             dst_ref, do_ref, dq_ref, dk_ref, dv_ref, ex_ref, qe_ref, ke_ref):
        @pl.when(pl.program_id(0) == 0)
        def _():
            dst_ref[...] = jnp.zeros_like(dst_ref)
            db_ref[...] = jnp.zeros_like(db_ref)
            dlb_ref[...] = jnp.zeros_like(dlb_ref)
            dng_ref[...] = jnp.zeros_like(dng_ref)

        lbv = lb_ref[...]
        qr = q_ref[...]
        qf, sig, fg, gl, kk = _hgrn_pointwise(qr, f_ref[...], lbv)
        b = _chunk_cumsum(gl * LOG2E, c)
        rc = _rows(R) & (c - 1)
        ng = ng_ref[...]
        gg = g_ref[...]
        dyv = dy_ref[...]
        sg = _silu(gg)
        for h in range(H):
            cs = slice(h * dk, (h + 1) * dk)
            oh = o_ref[:, cs]
            rr = lax.rsqrt(jnp.mean(oh * oh, axis=1, keepdims=True) + EPS)
            ohat = oh * rr
            dyh = dyv[:, cs]
            _put(dp_ref, db_ref, 5 * W + h * dk, 5 * W + (h + 1) * dk,
                 dyh * ohat * ng[:, cs] * _dsilu(gg[:, cs]))
            t = dyh * sg[:, cs]
            dng_ref[:, cs] += jnp.sum(t * ohat, axis=0, keepdims=True)
            dohat = t * ng[:, cs]
            do_ref[:, cs] = rr * (dohat - ohat * jnp.mean(dohat * ohat, axis=1, keepdims=True))
        for h in range(H):
            cs = slice(h * dk, (h + 1) * dk)
            qh, kh, bh, vh, doh = qf[:, cs], kk[:, cs], b[:, cs], v_ref[:, cs], do_ref[:, cs]
            da0 = jnp.sum(doh * vh, axis=1, keepdims=True)
            a0 = jnp.sum(qh * kh, axis=1, keepdims=True)
            dq = da0 * kh
            dkk = da0 * qh
            dv = a0 * doh
            for d in range(1, c):
                e = jnp.where(rc >= d, jnp.exp2(bh - pltpu.roll(bh, d, 0)), 0.0)
                kr = pltpu.roll(kh, d, 0)
                da = jnp.sum(doh * pltpu.roll(vh, d, 0), axis=1, keepdims=True)
                aa = jnp.sum(qh * kr * e, axis=1, keepdims=True)
                dq = dq + da * kr * e
                dkk = dkk + pltpu.roll(da * qh * e, R - d, 0)
                dv = dv + pltpu.roll(aa * doh, R - d, 0)
            dq_ref[:, cs] = dq
            dk_ref[:, cs] = dkk
            dv_ref[:, cs] = dv
        eb = jnp.exp2(b)
        qe_ref[...] = qf * eb
        ex_ref[...] = jnp.zeros_like(ex_ref)
        for ci in reversed(range(nck)):
            rs = slice(ci * c, (ci + 1) * c)
            bl = b[ci * c + c - 1:ci * c + c, :]
            ebl_rows = jnp.exp2(bl - b[rs, :])
            ke_ref[rs, :] = kk[rs, :] * ebl_rows
            ebl = jnp.exp2(bl)
            for h in range(H):
                cs = slice(h * dk, (h + 1) * dk)
                st0 = sall_ref[ci, h]
                dst1 = dst_ref[h]
                doc = do_ref[rs, cs]
                vc = v_ref[rs, cs]
                dq_ref[rs, cs] += _dot(doc, st0) * eb[rs, cs]
                dv_ref[rs, cs] += _dot_nt(ke_ref[rs, cs], dst1)
                dki = _dot(vc, dst1) * ebl_rows[:, cs]
                dk_ref[rs, cs] += dki
                ex_ref[ci * c + c - 1:ci * c + c, cs] = (
                    jnp.sum(dki * kk[rs, cs], axis=0, keepdims=True)
                    + ebl[:, cs] * jnp.sum(st0 * dst1, axis=0, keepdims=True))
                dst_ref[h] = dst1 * ebl[:, cs] + _dot_tn(doc, qe_ref[rs, cs])
        dq = dq_ref[...]
        dkk = dk_ref[...]
        db = qf * dq - kk * dkk + ex_ref[...]
        dgl = _chunk_rcumsum(db, c)
        dfg = jnp.where(fg > F_FLOOR, dgl / jnp.maximum(fg, F_FLOOR), 0.0)
        dsig = (dfg - dkk) * (1.0 - lbv)
        dlb_ref[...] += jnp.sum((dfg - dkk) * (1.0 - sig), axis=0, keepdims=True)
        dp_ref[:, 0:2 * W] = jnp.zeros((R, 2 * W), dp_ref.dtype)
        _put(dp_ref, db_ref, 2 * W, 3 * W, dq * _dsilu(qr))
        _put(dp_ref, db_ref, 3 * W, 4 * W, dsig * sig * (1.0 - sig))
        _put(dp_ref, db_ref, 4 * W, 5 * W, dv_ref[...])

    full = lambda shape: pl.BlockSpec(shape, lambda i: tuple(0 for _ in shape))
    col = lambda k: pl.BlockSpec((R, W), lambda i: (nb - 1 - i, k))
    scr = pltpu.VMEM((R, W), F32)
    return pl.pallas_call(
        body, name=name, grid=(nb,),
        in_specs=[col(2), col(3), col(4), col(5), col(1), col(0),
                  pl.BlockSpec((nck, H, dk, dk), lambda i: (nb - 1 - i, 0, 0, 0)), full((1, W)), full((1, W))],
        out_specs=[pl.BlockSpec((R, 6 * W), lambda i: (nb - 1 - i, 0)), full((1, 6 * W)), full((1, W)),
                   full((1, W))],
        out_shape=[jax.ShapeDtypeStruct((T, 6 * W), MXU_DTYPE), jax.ShapeDtypeStruct((1, 6 * W), F32),
                   jax.ShapeDtypeStruct((1, W), F32), jax.ShapeDtypeStruct((1, W), F32)],
        scratch_shapes=[pltpu.VMEM((H, dk, dk), F32), scr, scr, scr, scr, scr, scr, scr],
        compiler_params=_cp("arbitrary"),
    )(proj, proj, proj, proj, dy, o, sall, lb, norm_g)


ODD_HALO = 32


def _row_shifts(ext, up):
    n = ext.shape[0]
    return [ext] + [pltpu.roll(ext, n - b if up else b, 0) for b in range(1, 8)]


def _past(copies, sh, tb):
    a, b = divmod(sh, 8)
    return copies[b][ODD_HALO - 8 * a:ODD_HALO - 8 * a + tb]


def _future(copies, sh, tb):
    a, b = divmod(sh, 8)
    return copies[b][8 * a:8 * a + tb]


def _odd_fwd(name, proj, sc_w, cf_w, cf_b, ln_g, ln_b):
    T = proj.shape[0]
    W = sc_w.shape[1]
    K3, K31 = sc_w.shape[0], cf_w.shape[0]
    tb = _tile(T, 256)
    hb = tb // ODD_HALO
    n = tb + ODD_HALO

    def body(cur_ref, prev_ref, w3_ref, w31_ref, cb_ref, lg_ref, lbeta_ref, y_ref, d_ref):
        keep = (pl.program_id(0) > 0).astype(F32)
        sb = cur_ref[:, 0:W]
        p = cur_ref[:, W:2 * W] * cur_ref[:, 2 * W:3 * W]
        glu = cur_ref[:, 3 * W:4 * W] * _sigmoid(cur_ref[:, 4 * W:5 * W])
        p_prev = prev_ref[:, W:2 * W] * prev_ref[:, 2 * W:3 * W] * keep
        glu_prev = prev_ref[:, 3 * W:4 * W] * _sigmoid(prev_ref[:, 4 * W:5 * W]) * keep
        ext = jnp.concatenate([p_prev, p], axis=0)
        cp = w3_ref[K3 - 1:K3, :] * p
        for sh in range(1, K3):
            cp = cp + w3_ref[K3 - 1 - sh:K3 - sh, :] * pltpu.roll(ext, sh, 0)[ODD_HALO:n]
        y_ref[:, 0:W] = (sb * cp).astype(y_ref.dtype)
        glu_past = _row_shifts(jnp.concatenate([glu_prev, glu], axis=0), up=False)
        d = cb_ref[...] + w31_ref[K31 - 1:K31, :] * glu
        for sh in range(1, K31):
            d = d + w31_ref[K31 - 1 - sh:K31 - sh, :] * _past(glu_past, sh, tb)
        d_ref[...] = d
        mu = jnp.mean(d, axis=1, keepdims=True)
        xc = d - mu
        rstd = lax.rsqrt(jnp.mean(xc * xc, axis=1, keepdims=True) + EPS)
        ln = (xc * rstd) * lg_ref[...] + lbeta_ref[...]
        y_ref[:, W:2 * W] = _silu(ln).astype(y_ref.dtype)

    full = lambda shape: pl.BlockSpec(shape, lambda i: tuple(0 for _ in shape))
    return pl.pallas_call(
        body, name=name, grid=(T // tb,),
        in_specs=[pl.BlockSpec((tb, 5 * W), lambda i: (i, 0)),
                  pl.BlockSpec((ODD_HALO, 5 * W), lambda i: (jnp.maximum(i * hb - 1, 0), 0)),
                  full((K3, W)), full((K31, W)), full((1, W)), full((1, W)), full((1, W))],
        out_specs=[pl.BlockSpec((tb, 2 * W), lambda i: (i, 0)), pl.BlockSpec((tb, W), lambda i: (i, 0))],
        out_shape=[jax.ShapeDtypeStruct((T, 2 * W), MXU_DTYPE), jax.ShapeDtypeStruct((T, W), F32)],
        compiler_params=_cp("parallel"),
    )(proj, proj, sc_w, cf_w, cf_b, ln_g, ln_b)


def _odd_bwd(name, proj, dy, dsave, sc_w, cf_w, ln_g, ln_b):
    T = proj.shape[0]
    W = sc_w.shape[1]
    K3, K31 = sc_w.shape[0], cf_w.shape[0]
    tb = _tile(T, 128)
    nb = T // tb
    hb = tb // ODD_HALO
    nh = T // ODD_HALO
    n = tb + ODD_HALO

    def body(cur_ref, prev_ref, next_ref, dy_ref, dyn_ref, d_ref, dn_ref,
             w3_ref, w31_ref, lg_ref, lbeta_ref,
             dp_ref, db_ref, dw3_ref, dw31_ref, dcb_ref, dlg_ref, dlb_ref):
        i = pl.program_id(0)

        @pl.when(i == 0)
        def _():
            for ref in (db_ref, dw3_ref, dw31_ref, dcb_ref, dlg_ref, dlb_ref):
                ref[...] = jnp.zeros_like(ref)

        keep_prev = (i > 0).astype(F32)
        keep_next = (i < nb - 1).astype(F32)
        sb = cur_ref[:, 0:W]
        scv = cur_ref[:, W:2 * W]
        svv = cur_ref[:, 2 * W:3 * W]
        cu = cur_ref[:, 3 * W:4 * W]
        sg = _sigmoid(cur_ref[:, 4 * W:5 * W])
        p = scv * svv
        glu = cu * sg
        p_prev = prev_ref[:, W:2 * W] * prev_ref[:, 2 * W:3 * W] * keep_prev
        glu_prev = prev_ref[:, 3 * W:4 * W] * _sigmoid(prev_ref[:, 4 * W:5 * W]) * keep_prev
        dext = jnp.concatenate([d_ref[...], dn_ref[...]], axis=0)
        dyd = jnp.concatenate([dy_ref[:, W:2 * W], dyn_ref[:, W:2 * W] * keep_next], axis=0)
        mu = jnp.mean(dext, axis=1, keepdims=True)
        xc = dext - mu
        rstd = lax.rsqrt(jnp.mean(xc * xc, axis=1, keepdims=True) + EPS)
        xh = xc * rstd
        lg = lg_ref[...]
        dln = dyd * _dsilu(xh * lg + lbeta_ref[...])
        dxh = dln * lg
        dd = rstd * (dxh - jnp.mean(dxh, axis=1, keepdims=True)
                     - xh * jnp.mean(dxh * xh, axis=1, keepdims=True))
        dlg_ref[...] += jnp.sum((dln * xh)[0:tb], axis=0, keepdims=True)
        dlb_ref[...] += jnp.sum(dln[0:tb], axis=0, keepdims=True)
        ddc = dd[0:tb]
        dcb_ref[...] += jnp.sum(ddc, axis=0, keepdims=True)
        dglu = w31_ref[K31 - 1:K31, :] * ddc
        glu_past = _row_shifts(jnp.concatenate([glu_prev, glu], axis=0), up=False)
        dd_future = _row_shifts(dd, up=True)
        dw31_ref[K31 - 1:K31, :] += jnp.sum(ddc * glu, axis=0, keepdims=True)
        for sh in range(1, K31):
            dglu = dglu + w31_ref[K31 - 1 - sh:K31 - sh, :] * _future(dd_future, sh, tb)
            dw31_ref[K31 - 1 - sh:K31 - sh, :] += jnp.sum(ddc * _past(glu_past, sh, tb), axis=0, keepdims=True)
        _put(dp_ref, db_ref, 3 * W, 4 * W, dglu * sg)
        _put(dp_ref, db_ref, 4 * W, 5 * W, dglu * cu * sg * (1.0 - sg))
        dyc = dy_ref[:, 0:W]
        dcp = jnp.concatenate([dyc * sb, dyn_ref[:, 0:W] * next_ref[:, 0:W] * keep_next], axis=0)
        dcpc = dcp[0:tb]
        ext = jnp.concatenate([p_prev, p], axis=0)
        cp = w3_ref[K3 - 1:K3, :] * p
        dpp = w3_ref[K3 - 1:K3, :] * dcpc
        dw3_ref[K3 - 1:K3, :] += jnp.sum(dcpc * p, axis=0, keepdims=True)
        up = dcp
        for sh in range(1, K3):
            up = pltpu.roll(up, n - 1, 0)
            ext = pltpu.roll(ext, 1, 0)
            shifted = ext[ODD_HALO:n]
            cp = cp + w3_ref[K3 - 1 - sh:K3 - sh, :] * shifted
            dpp = dpp + w3_ref[K3 - 1 - sh:K3 - sh, :] * up[0:tb]
            dw3_ref[K3 - 1 - sh:K3 - sh, :] += jnp.sum(dcpc * shifted, axis=0, keepdims=True)
        _put(dp_ref, db_ref, 0, W, dyc * cp)
        _put(dp_ref, db_ref, W, 2 * W, dpp * svv)
        _put(dp_ref, db_ref, 2 * W, 3 * W, dpp * scv)

    full = lambda shape: pl.BlockSpec(shape, lambda i: tuple(0 for _ in shape))
    prev_map = lambda i: (jnp.maximum(i * hb - 1, 0), 0)
    next_map = lambda i: (jnp.minimum((i + 1) * hb, nh - 1), 0)
    return pl.pallas_call(
        body, name=name, grid=(nb,),
        in_specs=[pl.BlockSpec((tb, 5 * W), lambda i: (i, 0)),
                  pl.BlockSpec((ODD_HALO, 5 * W), prev_map), pl.BlockSpec((ODD_HALO, 5 * W), next_map),
                  pl.BlockSpec((tb, 2 * W), lambda i: (i, 0)), pl.BlockSpec((ODD_HALO, 2 * W), next_map),
                  pl.BlockSpec((tb, W), lambda i: (i, 0)), pl.BlockSpec((ODD_HALO, W), next_map),
                  full((K3, W)), full((K31, W)), full((1, W)), full((1, W))],
        out_specs=[pl.BlockSpec((tb, 5 * W), lambda i: (i, 0)), full((1, 5 * W)), full((K3, W)), full((K31, W)),
                   full((1, W)), full((1, W)), full((1, W))],
        out_shape=[jax.ShapeDtypeStruct((T, 5 * W), MXU_DTYPE), jax.ShapeDtypeStruct((1, 5 * W), F32),
                   jax.ShapeDtypeStruct((K3, W), F32),
                   jax.ShapeDtypeStruct((K31, W), F32)] + [jax.ShapeDtypeStruct((1, W), F32)] * 3,
        compiler_params=_cp("arbitrary"),
    )(proj, proj, proj, dy, dy, dsave, dsave, sc_w, cf_w, ln_g, ln_b)


PACK_WIDTH = 1024


def _lower_bounds(logits):
    sm = jax.nn.softmax(logits.astype(F32), axis=0)
    return jnp.cumsum(sm, axis=0) - sm[0]


def _pack_rows(arrays):
    flat = jnp.concatenate([a.reshape(-1) for a in arrays])
    pad = (-flat.shape[0]) % (8 * PACK_WIDTH)
    return jnp.pad(flat, (0, pad)).reshape(-1, PACK_WIDTH)


def _unpack_rows(packed, shapes):
    flat = packed.reshape(-1)
    out, off = [], 0
    for s in shapes:
        sz = math.prod(s)
        out.append(flat[off:off + sz].reshape(s))
        off += sz
    return out


def _shards_last(a):
    n = a.shape[-1] // N_DEV
    return jnp.moveaxis(a.reshape(a.shape[:-1] + (N_DEV, n)), -2, 0)


def _unshard_last(a):
    a = jnp.moveaxis(a, 0, -2)
    return a.reshape(a.shape[:-2] + (a.shape[-2] * a.shape[-1],))


BIG = ("ev_w_in", "ev_w_out", "od_w_in", "od_w_out", "ffn_w_gate", "ffn_w_up", "ffn_w_down")
COLUMN_MAJOR = ("ffn_w_gate", "ffn_w_up")
SMALL_SHARDED = ("lru_conv_w", "od_b_in", "sc_conv_w", "cf_conv_w", "cf_conv_b", "cf_ln_g", "cf_ln_b")
SMALL_REPL = ("ln_mix_g", "ln_ffn_g", "ln_final_g", "ev_b_in", "lru_conv_b", "lru_wa", "lru_ba", "lru_wx",
              "lru_bx", "lru_lambda", "hgrn_lb_logits", "hgrn_norm_g")
WEIGHTS = ("ln_mix_g", "ln_ffn_g", "ln_final_g", "ev_w_in", "ev_b_in", "lru_conv_w", "lru_conv_b", "lru_wa",
           "lru_ba", "lru_wx", "lru_bx", "lru_lambda", "hgrn_lb_logits", "hgrn_norm_g", "ev_w_out", "od_w_in",
           "od_b_in", "sc_conv_w", "cf_conv_w", "cf_conv_b", "cf_ln_g", "cf_ln_b", "od_w_out", "ffn_w_gate",
           "ffn_w_up", "ffn_w_down")


def _layer_weights(l):
    mix = ("ev_w_in", "ev_w_out") if l % 2 == 0 else ("od_w_in", "od_w_out")
    return [(mix[0], l // 2), (mix[1], l // 2), ("ffn_w_gate", l), ("ffn_w_up", l), ("ffn_w_down", l)]


class _MeshExchange:
    def __init__(self, w, m, v, depth, first):
        self.w, self.m, self.v, self.depth, self.first = w, m, v, depth, tuple(first)
        self.w_bf = {k: w[k].astype(MXU_DTYPE) for k, _ in self._names(0, "mix")}
        self.ready, self.flight, self.rs, self.adam = {}, {}, {}, {}

    @staticmethod
    def _names(l, grp):
        names = _layer_weights(l)
        return names[:2] if grp == "mix" else names[2:]

    def _own_start(self, l, grp, deps):
        names = self._names(l, grp)
        srcs = [self.w_bf[k] for k, _ in names]
        layers = [j for _, j in names]
        plan = _plan_gather_own(layers)
        sems, lands, tok = _split_start(f"ag_own_start_{grp}{l}", plan, srcs, _place_own(srcs, layers), deps)
        self.flight[l, grp] = (plan, sems, srcs, lands)
        return tok

    def _turn(self, l, grp, deps):
        plan, sems, srcs, lands = self.flight[l, grp]
        lands = _split_wait(f"ag_own_wait_{grp}{l}", plan, sems, srcs, lands, deps)
        plan = _plan_gather_pass(len(lands))
        sems, passed, tok = _split_start(f"ag_pass_start_{grp}{l}", plan, [], lands)
        self.flight[l, grp] = (plan, sems, [], passed)
        toks = [tok]
        nl, ng = (l, "ffn") if grp == "mix" else (l + 1, "mix")
        if nl < self.depth:
            toks.append(self._own_start(nl, ng, (tok,)))
        return tuple(toks)

    def _pass_wait(self, l, grp, deps):
        plan, sems, srcs, lands = self.flight.pop((l, grp))
        lands = _split_wait(f"ag_pass_wait_{grp}{l}", plan, sems, srcs, lands, deps)
        self.ready[l, grp] = dict(zip([k for k, _ in self._names(l, grp)], lands))

    def layer_begin(self, l):
        toks = ()
        if l == 0:
            started = self._own_start(0, "mix", self.first)
            for k in BIG:
                if k not in self.w_bf:
                    self.w_bf[k] = lax.optimization_barrier((self.w[k], started))[0].astype(MXU_DTYPE)
            toks = self._turn(0, "mix", tuple(self.w_bf.values()))
            self._pass_wait(0, "mix", ())
        return self.ready.pop((l, "mix")), toks

    def tick(self, l, t, after):
        if t == 2:
            return self._turn(l, "ffn", (after,))
        if t == 3:
            self._pass_wait(l, "ffn", (after,))
        if t == 4 and l + 1 < self.depth:
            return self._turn(l + 1, "mix", (after,))
        if t == 5 and l + 1 < self.depth:
            self._pass_wait(l + 1, "mix", (after,))
        return ()

    def ffn_weights(self, l):
        return self.ready.pop((l, "ffn"))

    def grads(self, tag, named):
        srcs = [g for _, _, g in named]
        lands = [lax.empty((4,) + g.shape[1:], g.dtype) for g in srcs]
        plan = _plan_scatter_pair(len(srcs))
        sems, lands, tok = _split_start(f"rs_pair_start_{tag}", plan, srcs, lands)
        self.rs[tag] = (named, plan, sems, srcs, lands)
        return (tok,)

    def grads_mid(self, tag, after):
        named, plan, sems, srcs, lands = self.rs[tag]
        got = _split_wait(f"rs_pair_wait_{tag}", plan, sems, srcs, lands, (after,))
        both = [_pair_add(f"pair_add_{tag}_{i}", a, b) for i, (a, b) in enumerate(zip(srcs, got))]
        self.rs[tag] = (named, [p for p, _ in both], [f for _, f in both])

    def grads_send(self, tag, deps=()):
        named, parts, fins = self.rs[tag]
        plan = _plan_scatter_chips(len(parts))
        sems, fins, tok = _split_start(f"rs_chip_start_{tag}", plan, parts, fins, tuple(deps))
        self.rs[tag] = (named, plan, sems, parts, fins)
        return (tok,)

    def grads_end(self, tag, after):
        named, plan, sems, parts, fins = self.rs.pop(tag)
        fins = _split_wait(f"rs_chip_wait_{tag}", plan, sems, parts, fins, (after,))
        done = []
        for (k, j, _), fin in zip(named, fins):
            view = (lambda a: a.transpose(0, 2, 1)) if k in COLUMN_MAJOR else (lambda a: a)
            *self.adam[k], token = _adamw(f"adamw_{k}_{j}", view(self.w[k]), view(self.m[k]), view(self.v[k]),
                                          fin, j, self.adam.get(k))
            done.append(token)
        return tuple(done)

    def finish(self, after):
        for tag in list(self.rs):
            self.grads_end(tag, after)

    def results(self, k):
        if k in COLUMN_MAJOR:
            L, r, c = self.w[k].shape
            return [a.reshape(L, c, r).transpose(0, 2, 1) for a in self.adam[k]]
        return [a.reshape(self.w[k].shape) for a in self.adam[k]]


def _local_step(x, tgt, p, ex):
    T, D = x.shape
    depth = p["ln_mix_g"].shape[0]
    lbs = _lower_bounds(p["hgrn_lb_logits"])
    row = lambda a: a.reshape(1, -1)
    saved = []
    h = None
    for l in range(depth):
        j = l // 2
        wl, tok = ex.layer_begin(l)
        s = {"x": x}
        if h is None:
            h = _rmsnorm_fwd(f"norm_mix{l}", x, row(p["ln_mix_g"][l]))
        s["h"] = h
        if l % 2 == 0:
            proj = _proj_in(f"ev_in{l}", h, wl["ev_w_in"], row(p["ev_b_in"][j]), tok)
            wa = p["lru_wa"][j].astype(MXU_DTYPE)
            wx = p["lru_wx"][j].astype(MXU_DTYPE)
            ya, xc, r, ig, hs = _lru_fwd(f"lru_fwd{l}", proj, p["lru_conv_w"][j], row(p["lru_conv_b"][j]),
                                         wa, row(p["lru_ba"][j]), wx, row(p["lru_bx"][j]),
                                         row(p["lru_lambda"][j]))
            y, o, sall = _hgrn_fwd(f"hgrn_fwd{l}", proj, row(lbs[j]), row(p["hgrn_norm_g"][j]), ya)
            s.update(proj=proj, xc=xc, r=r, ig=ig, hs=hs, o=o, sall=sall, wa=wa, wx=wx)
            w_out = wl["ev_w_out"]
        else:
            proj = _proj_in(f"od_in{l}", h, wl["od_w_in"], row(p["od_b_in"][j]), tok)
            y, dsave = _odd_fwd(f"odd_fwd{l}", proj, p["sc_conv_w"][j], p["cf_conv_w"][j],
                                row(p["cf_conv_b"][j]), row(p["cf_ln_g"][j]), row(p["cf_ln_b"][j]))
            s.update(proj=proj, dsave=dsave)
            w_out = wl["od_w_out"]
        x, h2 = _mix_out(f"mix_out{l}", y, w_out, x, row(p["ln_ffn_g"][l]), ex.tick(l, 2, y))
        s["y"] = y
        s["xmid"] = x
        ex.tick(l, 3, x)
        wl = {**wl, **ex.ffn_weights(l)}
        gate, up, hid = _ffn_in(f"ffn_in{l}", h2, wl["ffn_w_gate"], wl["ffn_w_up"])
        tok = ex.tick(l, 4, hid)
        if l + 1 < depth:
            x, h = _proj_out(f"ffn_out{l}", hid, wl["ffn_w_down"], x, tok, row(p["ln_mix_g"][l + 1]))
        else:
            x = _proj_out(f"ffn_out{l}", hid, wl["ffn_w_down"], x, tok)
        ex.tick(l, 5, x)
        s.update(h2=h2, gate=gate, up=up, hid=hid, w=wl)
        saved.append(s)

    loss, dx, dxb, dg_final = _loss_head("loss_head", x, row(p["ln_final_g"]), tgt)

    gs = {k: [None] * p[k].shape[0] for k in SMALL_REPL + SMALL_SHARDED if k not in ("ln_final_g", "hgrn_lb_logits")}
    d_lb = [None] * (depth // 2 + depth % 2)
    tok = ()
    pending = None
    for l in reversed(range(depth)):
        j = l // 2
        s = saved[l]
        wl = s["w"]
        ffn_shape = wl["ffn_w_gate"].shape[1:]
        dwd = _wgrad(f"ffn_dwd{l}", s["hid"], dxb, ffn_shape[1], ffn_shape[0])
        dgate, dup = _ffn_bwd_hidden(f"ffn_bwd_hid{l}", dxb, wl["ffn_w_down"], s["gate"], s["up"], tok)
        dwg = _wgrad(f"ffn_dwg{l}", dgate, s["h2"], ffn_shape[1], ffn_shape[0])
        dwu = _wgrad(f"ffn_dwu{l}", dup, s["h2"], ffn_shape[1], ffn_shape[0])
        tok = ex.grads(f"ffn{l}", [("ffn_w_down", l, dwd), ("ffn_w_gate", l, dwg), ("ffn_w_up", l, dwu)])
        dh2 = _bwd_in(f"ffn_dh{l}", [dgate, dup], [wl["ffn_w_gate"], wl["ffn_w_up"]], tok)
        dx, dxb, dg = _rmsnorm_bwd(f"norm_ffn_bwd{l}", s["xmid"], row(p["ln_ffn_g"][l]), dh2, dx)
        gs["ln_ffn_g"][l] = dg[0]
        ex.grads_mid(f"ffn{l}", dxb)
        done = ex.grads_end(pending, dxb) if pending is not None else ()
        tok = ex.grads_send(f"ffn{l}", done)
        w_in, w_out = ("ev_w_in", "ev_w_out") if l % 2 == 0 else ("od_w_in", "od_w_out")
        dwo = _wgrad(f"mix_dwo{l}", s["y"], dxb, *wl[w_out].shape[1:])
        if l % 2 == 0:
            dy = _bwd_out(f"ev_dy{l}", dxb, wl["ev_w_out"], tok)
            dph, db_h, dlb, dng = _hgrn_bwd(f"hgrn_bwd{l}", s["proj"], dy, s["o"], s["sall"], row(lbs[j]),
                                            row(p["hgrn_norm_g"][j]))
            dproj, db_a, d_cw, d_cb, d_wa, d_ba, d_wx, d_bx, d_lam = _lru_bwd(
                f"lru_bwd{l}", s["proj"], dy, s["xc"], s["r"], s["ig"], s["hs"], p["lru_conv_w"][j],
                s["wa"], s["wx"], row(p["lru_lambda"][j]), dph)
            gs["lru_conv_w"][j], gs["lru_conv_b"][j] = d_cw, d_cb[0]
            gs["lru_wa"][j], gs["lru_ba"][j] = d_wa, d_ba.reshape(p["lru_ba"].shape[1:])
            gs["lru_wx"][j], gs["lru_bx"][j] = d_wx, d_bx.reshape(p["lru_bx"].shape[1:])
            gs["lru_lambda"][j], gs["hgrn_norm_g"][j] = d_lam[0], dng[0]
            d_lb[j] = dlb[0]
            gs["ev_b_in"][j] = jnp.concatenate([db_a[0], db_h[0, db_a.shape[1]:]])
        else:
            dy = _bwd_out(f"od_dy{l}", dxb, wl["od_w_out"], tok)
            dproj, db_in, d_w3, d_w31, d_cfb, d_lg, d_lbeta = _odd_bwd(
                f"odd_bwd{l}", s["proj"], dy, s["dsave"], p["sc_conv_w"][j], p["cf_conv_w"][j],
                row(p["cf_ln_g"][j]), row(p["cf_ln_b"][j]))
            gs["sc_conv_w"][j], gs["cf_conv_w"][j] = d_w3, d_w31
            gs["cf_conv_b"][j], gs["cf_ln_g"][j], gs["cf_ln_b"][j] = d_cfb[0], d_lg[0], d_lbeta[0]
            gs["od_b_in"][j] = db_in[0]
        dwi = _wgrad(f"mix_dwi{l}", s["h"], dproj, *wl[w_in].shape[1:])
        tok = ex.grads(f"mix{l}", [(w_out, j, dwo), (w_in, j, dwi)])
        dh = _bwd_in(f"mix_dh{l}", [dproj], [wl[w_in]], tok)
        dx, dxb, dg = _rmsnorm_bwd(f"norm_mix_bwd{l}", s["x"], row(p["ln_mix_g"][l]), dh, dx)
        gs["ln_mix_g"][l] = dg[0]
        ex.grads_mid(f"mix{l}", dxb)
        if l > 0:
            tok = ex.grads_send(f"mix{l}", ex.grads_end(f"ffn{l}", dxb))
        else:
            tok = ex.grads_send(f"mix{l}")
            ex.grads_end(f"ffn{l}", (tok or (dxb,))[0])
        pending = f"mix{l}"

    small = {k: jnp.stack(v) for k, v in gs.items()}
    small["ln_final_g"] = dg_final[0]
    _, lb_vjp = jax.vjp(_lower_bounds, p["hgrn_lb_logits"])
    small["hgrn_lb_logits"] = lb_vjp(jnp.stack(d_lb))[0]
    if tok:
        small = lax.optimization_barrier((small, tok))[0]
    return loss, dx, small


def kernel(x, ln_mix_g, ln_ffn_g, ln_final_g, ev_w_in, ev_b_in, lru_conv_w, lru_conv_b, lru_wa, lru_ba, lru_wx, lru_bx, lru_lambda, hgrn_lb_logits, hgrn_norm_g, ev_w_out, od_w_in, od_b_in, sc_conv_w, cf_conv_w, cf_conv_b, cf_ln_g, cf_ln_b, od_w_out, ffn_w_gate, ffn_w_up, ffn_w_down, loss_target, m_ln_mix_g, m_ln_ffn_g, m_ln_final_g, m_ev_w_in, m_ev_b_in, m_lru_conv_w, m_lru_conv_b, m_lru_wa, m_lru_ba, m_lru_wx, m_lru_bx, m_lru_lambda, m_hgrn_lb_logits, m_hgrn_norm_g, m_ev_w_out, m_od_w_in, m_od_b_in, m_sc_conv_w, m_cf_conv_w, m_cf_conv_b, m_cf_ln_g, m_cf_ln_b, m_od_w_out, m_ffn_w_gate, m_ffn_w_up, m_ffn_w_down, v_ln_mix_g, v_ln_ffn_g, v_ln_final_g, v_ev_w_in, v_ev_b_in, v_lru_conv_w, v_lru_conv_b, v_lru_wa, v_lru_ba, v_lru_wx, v_lru_bx, v_lru_lambda, v_hgrn_lb_logits, v_hgrn_norm_g, v_ev_w_out, v_od_w_in, v_od_b_in, v_sc_conv_w, v_cf_conv_w, v_cf_conv_b, v_cf_ln_g, v_cf_ln_b, v_od_w_out, v_ffn_w_gate, v_ffn_w_up, v_ffn_w_down):
    args = locals()
    w = {k: args[k] for k in WEIGHTS}
    m = {k: args["m_" + k] for k in WEIGHTS}
    v = {k: args["v_" + k] for k in WEIGHTS}
    assert x.shape[0] == 1
    T, D = x.shape[1:]

    local_shapes = [w[k].shape for k in SMALL_SHARDED]
    gathered = _all_gather("gather_small_params", [_pack_rows([w[k] for k in SMALL_SHARDED])])[0]
    p = {k: w[k] for k in SMALL_REPL}
    per_dev = [_unpack_rows(gathered[s], local_shapes) for s in range(N_DEV)]
    for i, k in enumerate(SMALL_SHARDED):
        p[k] = _unshard_last(jnp.stack([per_dev[s][i] for s in range(N_DEV)]))

    ex = _MeshExchange(w, m, v, ln_mix_g.shape[0], [gathered])
    loss_part, dx, small = _local_step(x[0], loss_target[0], p, ex)
    loss = lax.psum(loss_part[0, 0], ("x", "y", "c"))

    small_sh = jnp.stack([_pack_rows([_shards_last(small[k])[s] for k in SMALL_SHARDED]) for s in range(N_DEV)])
    got = _pair_exchange("small_grads_to_sibling", [small_sh])[0]
    final_small = _chip_exchange("small_grads_to_chips", [_pair_add("pair_add_small", small_sh, got)[0]])[0]
    repl_parts = _all_gather("gather_small_grads", [_pack_rows([small[k] for k in SMALL_REPL])])[0]

    out_g, out_d, out_m, out_v = {}, {}, {}, {}
    res = _adamw("adamw_small_sharded", *[_pack_rows([t[k] for k in SMALL_SHARDED])[None] for t in (w, m, v)],
                 final_small)
    for o, r in zip((out_g, out_d, out_m, out_v), res):
        o.update(zip(SMALL_SHARDED, _unpack_rows(r, local_shapes)))
    res = _adamw("adamw_small_repl", *[_pack_rows([t[k] for k in SMALL_REPL])[None] for t in (w, m, v)], repl_parts)
    for o, r in zip((out_g, out_d, out_m, out_v), res):
        o.update(zip(SMALL_REPL, _unpack_rows(r, [w[k].shape for k in SMALL_REPL])))
    ex.finish(res[0])
    for k in BIG:
        out_g[k], out_d[k], out_m[k], out_v[k] = ex.results(k)

    return (loss, dx[None], *[out_g[k] for k in WEIGHTS], *[out_d[k] for k in WEIGHTS],
            *[out_m[k] for k in WEIGHTS], *[out_v[k] for k in WEIGHTS])
```

```python
import functools
import math

import jax
import jax.numpy as jnp
from jax import lax
from jax.experimental import pallas as pl
from jax.experimental.pallas import tpu as pltpu

F32 = jnp.float32
MXU_DTYPE = jnp.bfloat16
WIRE_DTYPE = jnp.bfloat16
N_DEV = 8
EPS = 1e-6
F_FLOOR = 1e-30
LRU_C = 8.0
HGRN_HEADS = 8
HGRN_SUB = 16
ADAM_LR, ADAM_B1, ADAM_B2, ADAM_EPS, ADAM_WD, ADAM_STEP = 0.001, 0.9, 0.999, 1e-08, 0.01, 10
V7X_VMEM_LIMIT = 48 * 1024 * 1024
MM_ROWS = 1024
WGRAD_ROWS = 2048
MESH = pl.DeviceIdType.MESH
ANY = pl.BlockSpec(memory_space=pl.ANY)


def _cp(*sem):
    return pltpu.CompilerParams(dimension_semantics=sem or None, vmem_limit_bytes=V7X_VMEM_LIMIT)


def _sigmoid(x):
    return 1.0 / (1.0 + jnp.exp(-x))


def _silu(x):
    return x * _sigmoid(x)


def _dsilu(x):
    s = _sigmoid(x)
    return s * (1.0 + x * (1.0 - s))


_GELU_C = math.sqrt(2.0 / math.pi)
LOG2E = 1.0 / math.log(2.0)


def _gelu(x):
    return 0.5 * x * (1.0 + jnp.tanh(_GELU_C * (x + 0.044715 * x * x * x)))


def _dgelu(x):
    t = jnp.tanh(_GELU_C * (x + 0.044715 * x * x * x))
    return 0.5 * (1.0 + t) + 0.5 * x * (1.0 - t * t) * _GELU_C * (1.0 + 3.0 * 0.044715 * x * x)


def _log1p(e):
    return jnp.where(e < 1e-2, e * (1.0 - e * (0.5 - e * (1.0 / 3.0))), jnp.log(1.0 + e))


def _softplus(x):
    return jnp.maximum(x, 0.0) + _log1p(jnp.exp(-jnp.abs(x)))


def _one_minus_exp(x):
    series = -x * (1.0 + x * (0.5 + x * (1.0 / 6.0 + x * (1.0 / 24.0))))
    return jnp.where(x > -0.05, series, 1.0 - jnp.exp(x))


def _rows(n, d=1):
    return lax.broadcasted_iota(jnp.int32, (n, d), 0)


def _dot(a, b):
    return jnp.dot(a.astype(MXU_DTYPE), b.astype(MXU_DTYPE), preferred_element_type=F32)


def _dot_nt(a, b):
    return lax.dot_general(a.astype(MXU_DTYPE), b.astype(MXU_DTYPE), (((1,), (1,)), ((), ())),
                           preferred_element_type=F32)


def _dot_tn(a, b):
    return lax.dot_general(a.astype(MXU_DTYPE), b.astype(MXU_DTYPE), (((0,), (0,)), ((), ())),
                           preferred_element_type=F32)


def _tile(n, want):
    if n <= want:
        return n
    t = want - want % 8
    while n % t:
        t -= 8
    assert t > 0, (n, want)
    return t


def _my_place():
    x, y, c = lax.axis_index("x"), lax.axis_index("y"), lax.axis_index("c")
    return x, y, c


def _all_gather(name, srcs):
    n = len(srcs)

    def body(*refs):
        src_refs, out_refs = refs[:n], refs[n:2 * n]
        send_sems, recv_sems, local_sems = refs[2 * n:]
        x, y, c = _my_place()
        sibling = (x, y, 1 - c)
        chips = [(1 - x, y), (x, 1 - y), (1 - x, 1 - y)]

        def slot(px, py, pc):
            return 4 * px + 2 * py + pc

        def copy(i, k, block, to, src=None):
            dst = out_refs[i].at[slot(*block)]
            return pltpu.make_async_remote_copy(
                src_ref=dst if src is None else src, dst_ref=dst,
                send_sem=send_sems.at[i, k], recv_sem=recv_sems.at[i, k],
                device_id=to, device_id_type=MESH)

        me = (x, y, c)
        sends, own = [], []
        for i in range(n):
            mine = pltpu.make_async_copy(src_refs[i], out_refs[i].at[slot(*me)], local_sems.at[i])
            mine.start()
            own.append(mine)
            first = [copy(i, 0, me, sibling, src=src_refs[i])]
            first += [copy(i, 1 + j, me, (*chip, c), src=src_refs[i]) for j, chip in enumerate(chips)]
            for cp in first:
                cp.start()
            sends += first
        for i in range(n):
            for j, chip in enumerate(chips):
                copy(i, 1 + j, (*chip, c), me).wait_recv()
                passed = copy(i, 4 + j, (*chip, c), sibling)
                passed.start()
                sends.append(passed)
        for i in range(n):
            copy(i, 0, sibling, me).wait_recv()
            for j, chip in enumerate(chips):
                copy(i, 4 + j, (*chip, 1 - c), me).wait_recv()
        for cp in sends:
            cp.wait_send()
        for cp in own:
            cp.wait()

    outs = pl.pallas_call(
        body, name=name,
        out_shape=[jax.ShapeDtypeStruct((N_DEV,) + s.shape, s.dtype) for s in srcs],
        in_specs=[ANY] * n, out_specs=[ANY] * n,
        scratch_shapes=[pltpu.SemaphoreType.DMA((n, 7)), pltpu.SemaphoreType.DMA((n, 7)),
                        pltpu.SemaphoreType.DMA((n,))],
    )(*srcs)
    return list(outs)


def _pair_exchange(name, srcs):
    n = len(srcs)

    def body(*refs):
        src_refs, out_refs = refs[:n], refs[n:2 * n]
        send_sems, recv_sems = refs[2 * n:]
        x, y, c = _my_place()
        copies = []
        for i in range(n):
            for j in range(4):
                cp = pltpu.make_async_remote_copy(
                    src_ref=src_refs[i].at[2 * j + (1 - c)], dst_ref=out_refs[i].at[j],
                    send_sem=send_sems.at[i, j], recv_sem=recv_sems.at[i, j],
                    device_id=(x, y, 1 - c), device_id_type=MESH)
                cp.start()
                copies.append(cp)
        for cp in copies:
            cp.wait()

    outs = pl.pallas_call(
        body, name=name,
        out_shape=[jax.ShapeDtypeStruct((4,) + s.shape[1:], s.dtype) for s in srcs],
        in_specs=[ANY] * n, out_specs=[ANY] * n,
        scratch_shapes=[pltpu.SemaphoreType.DMA((n, 4)), pltpu.SemaphoreType.DMA((n, 4))],
    )(*srcs)
    return list(outs)


def _chip_exchange(name, srcs):
    n = len(srcs)

    def body(*refs):
        src_refs, out_refs = refs[:n], refs[n:2 * n]
        send_sems, recv_sems, local_sems = refs[2 * n:]
        x, y, c = _my_place()
        chip = 2 * x + y
        copies = []
        for i in range(n):
            mine = pltpu.make_async_copy(src_refs[i].at[chip], out_refs[i].at[3], local_sems.at[i])
            mine.start()
            copies.append(mine)
            for k, (fx, fy) in enumerate([(1, 0), (0, 1), (1, 1)]):
                px = x + fx - 2 * x * fx
                py = y + fy - 2 * y * fy
                cp = pltpu.make_async_remote_copy(
                    src_ref=src_refs[i].at[2 * px + py], dst_ref=out_refs[i].at[k],
                    send_sem=send_sems.at[i, k], recv_sem=recv_sems.at[i, k],
                    device_id=(px, py, c), device_id_type=MESH)
                cp.start()
                copies.append(cp)
        for cp in copies:
            cp.wait()

    outs = pl.pallas_call(
        body, name=name,
        out_shape=[jax.ShapeDtypeStruct(s.shape, s.dtype) for s in srcs],
        in_specs=[ANY] * n, out_specs=[ANY] * n,
        scratch_shapes=[pltpu.SemaphoreType.DMA((n, 3)), pltpu.SemaphoreType.DMA((n, 3)),
                        pltpu.SemaphoreType.DMA((n,))],
    )(*srcs)
    return list(outs)


def _pair_add(name, mine, got):
    assert mine.shape[0] == N_DEV and got.shape[0] == 4
    cdim = mine.shape[-1]
    m4 = mine.reshape(4, 2, -1, cdim)
    g3 = got.reshape(4, -1, cdim)
    rows = m4.shape[2]
    tr = _tile(rows, 512)

    def body(m_ref, g_ref, o_ref, fin_ref):
        x, y, _ = _my_place()
        s = (m_ref[...].astype(F32) + g_ref[...].astype(F32)).astype(o_ref.dtype)
        o_ref[...] = s

        @pl.when(pl.program_id(1) == 2 * x + y)
        def _():
            fin_ref[...] = s

    out, fin = pl.pallas_call(
        body, name=name, grid=(rows // tr, 4),
        in_specs=[pl.BlockSpec((None, None, tr, cdim), lambda i, j: (j, lax.axis_index("c"), i, 0)),
                  pl.BlockSpec((None, tr, cdim), lambda i, j: (j, i, 0))],
        out_specs=[pl.BlockSpec((None, tr, cdim), lambda i, j: (j, i, 0)),
                   pl.BlockSpec((None, tr, cdim), lambda i, j: (3, i, 0))],
        out_shape=[jax.ShapeDtypeStruct(g3.shape, got.dtype)] * 2,
        compiler_params=_cp("parallel", "arbitrary"),
    )(m4, g3)
    return out.reshape(got.shape), fin.reshape(got.shape)


HBM = pl.BlockSpec(memory_space=pltpu.HBM)
SEM = pl.BlockSpec(memory_space=pltpu.SEMAPHORE)
EFFECT = pltpu.SideEffectType.DATAFLOW_SIDE_EFFECTING
SLOTS = 4


def _hbm(a):
    return pltpu.with_memory_space_constraint(a, pltpu.HBM)


def _remote(src, dst, sems, i, k, to):
    return pltpu.make_async_remote_copy(src_ref=src, dst_ref=dst, send_sem=sems[0].at[i * SLOTS + k],
                                        recv_sem=sems[1].at[i * SLOTS + k], device_id=to, device_id_type=MESH)


def _slot(px, py, pc):
    return 4 * px + 2 * py + pc


def _other_chips(x, y):
    return [(1 - x, y), (x, 1 - y), (1 - x, 1 - y)]


def _plan_gather_own(layers):
    def plan(srcs, lands, sems):
        x, y, c = _my_place()
        out = []
        for i, j in enumerate(layers):
            dst = lands[i].at[_slot(x, y, c)]
            out.append(_remote(srcs[i].at[j], dst, sems, i, 0, (x, y, 1 - c)))
            for k, (px, py) in enumerate(_other_chips(x, y)):
                out.append(_remote(srcs[i].at[j], dst, sems, i, 1 + k, (px, py, c)))
        return out
    return plan


def _plan_gather_pass(n):
    def plan(srcs, lands, sems):
        x, y, c = _my_place()
        out = []
        for i in range(n):
            for k, (px, py) in enumerate(_other_chips(x, y)):
                blk = lands[i].at[_slot(px, py, c)]
                out.append(_remote(blk, blk, sems, i, k, (x, y, 1 - c)))
        return out
    return plan


def _plan_scatter_pair(n):
    def plan(srcs, lands, sems):
        x, y, c = _my_place()
        return [_remote(srcs[i].at[2 * j + (1 - c)], lands[i].at[j], sems, i, j, (x, y, 1 - c))
                for i in range(n) for j in range(4)]
    return plan


def _plan_scatter_chips(n):
    def plan(srcs, lands, sems):
        x, y, c = _my_place()
        return [_remote(srcs[i].at[2 * px + py], lands[i].at[k], sems, i, k, (px, py, c))
                for i in range(n) for k, (px, py) in enumerate(_other_chips(x, y))]
    return plan


def _split_start(name, plan, srcs, lands, deps=()):
    ns, nl, nd = len(srcs), len(lands), len(deps)
    n = max(ns, nl)

    def body(*refs):
        sems = refs[ns + nl + nd:ns + nl + nd + 2]
        for cp in plan(refs[:ns], refs[ns:ns + nl], sems):
            cp.start()
        refs[-1][...] = jnp.zeros_like(refs[-1])

    outs = pl.pallas_call(
        body, name=name,
        out_shape=(pltpu.SemaphoreType.DMA((n * SLOTS,)), pltpu.SemaphoreType.DMA((n * SLOTS,)),
                   *[pltpu.HBM(a.shape, a.dtype) for a in lands], jax.ShapeDtypeStruct((8, 128), F32)),
        in_specs=[HBM] * (ns + nl) + [ANY] * nd,
        out_specs=(SEM, SEM, *[HBM] * nl, pl.BlockSpec(memory_space=pltpu.VMEM)),
        input_output_aliases={ns + i: 2 + i for i in range(nl)},
        compiler_params=pltpu.CompilerParams(has_side_effects=EFFECT),
    )(*[_hbm(a) for a in srcs], *[_hbm(a) for a in lands], *deps)
    return (outs[0], outs[1]), list(outs[2:2 + nl]), outs[-1]


def _split_wait(name, plan, sems, srcs, lands, deps=()):
    ns, nl, nd = len(srcs), len(lands), len(deps)

    def body(*refs):
        for cp in plan(refs[:ns], refs[ns:ns + nl], refs[ns + nl:ns + nl + 2]):
            cp.wait_send()
            cp.wait_recv()

    outs = pl.pallas_call(
        body, name=name,
        out_shape=tuple(pltpu.HBM(a.shape, a.dtype) for a in lands),
        in_specs=[HBM] * (ns + nl) + [SEM, SEM] + [ANY] * nd,
        out_specs=tuple([HBM] * nl),
        input_output_aliases={ns + i: i for i in range(nl)},
        compiler_params=pltpu.CompilerParams(has_side_effects=EFFECT),
    )(*srcs, *lands, *sems, *deps)
    return list(outs)


def _place_own(srcs, layers):
    x, y, c = _my_place()
    zero = jnp.zeros((), jnp.int32)
    return [lax.dynamic_update_slice(lax.empty((N_DEV,) + a.shape[1:], a.dtype), a[j][None],
                                     (_slot(x, y, c),) + (zero,) * (a.ndim - 1))
            for a, j in zip(srcs, layers)]


def _put(dp_ref, db_ref, lo, hi, val):
    dp_ref[:, lo:hi] = val.astype(dp_ref.dtype)
    db_ref[:, lo:hi] += jnp.sum(val, axis=0, keepdims=True)


def _with_deps(body, n_in, deps):
    nd = len(deps)
    if not nd:
        return body

    def wrapped(*refs):
        body(*refs[:n_in], *refs[n_in + nd:])

    return wrapped


def _rmsnorm_fwd(name, x, g, deps=()):
    T, D = x.shape
    tm = _tile(T, 256)

    def body(x_ref, g_ref, o_ref):
        xv = x_ref[...]
        r = lax.rsqrt(jnp.mean(xv * xv, axis=-1, keepdims=True) + EPS)
        o_ref[...] = ((xv * r) * g_ref[...]).astype(o_ref.dtype)

    return pl.pallas_call(
        _with_deps(body, 2, deps), name=name, grid=(T // tm,),
        in_specs=[pl.BlockSpec((tm, D), lambda i: (i, 0)), pl.BlockSpec((1, D), lambda i: (0, 0))]
        + [ANY] * len(deps),
        out_specs=pl.BlockSpec((tm, D), lambda i: (i, 0)),
        out_shape=jax.ShapeDtypeStruct((T, D), MXU_DTYPE), compiler_params=_cp("parallel"),
    )(x, g, *deps)


def _rmsnorm_bwd(name, x, g, dh, dres):
    T, D = x.shape
    tm = _tile(T, 256)

    def body(x_ref, g_ref, dh_ref, dres_ref, dx_ref, dxb_ref, dg_ref):
        xv = x_ref[...]
        r = lax.rsqrt(jnp.mean(xv * xv, axis=-1, keepdims=True) + EPS)
        xh = xv * r
        dhv = dh_ref[...]

        @pl.when(pl.program_id(0) == 0)
        def _():
            dg_ref[...] = jnp.zeros_like(dg_ref)

        dg_ref[...] += jnp.sum(dhv * xh, axis=0, keepdims=True)
        dxh = dhv * g_ref[...]
        dx = dres_ref[...] + r * (dxh - xh * jnp.mean(dxh * xh, axis=-1, keepdims=True))
        dx_ref[...] = dx
        dxb_ref[...] = dx.astype(dxb_ref.dtype)

    return pl.pallas_call(
        body, name=name, grid=(T // tm,),
        in_specs=[pl.BlockSpec((tm, D), lambda i: (i, 0)), pl.BlockSpec((1, D), lambda i: (0, 0)),
                  pl.BlockSpec((tm, D), lambda i: (i, 0)), pl.BlockSpec((tm, D), lambda i: (i, 0))],
        out_specs=[pl.BlockSpec((tm, D), lambda i: (i, 0)), pl.BlockSpec((tm, D), lambda i: (i, 0)),
                   pl.BlockSpec((1, D), lambda i: (0, 0))],
        out_shape=[jax.ShapeDtypeStruct((T, D), F32), jax.ShapeDtypeStruct((T, D), MXU_DTYPE),
                   jax.ShapeDtypeStruct((1, D), F32)],
        compiler_params=_cp("arbitrary"),
    )(x, g, dh, dres)


def _loss_head(name, x, g, tgt):
    T, D = x.shape
    tm = _tile(T, 256)

    def body(x_ref, g_ref, t_ref, loss_ref, dx_ref, dxb_ref, dg_ref):
        xv = x_ref[...]
        r = lax.rsqrt(jnp.mean(xv * xv, axis=-1, keepdims=True) + EPS)
        xh = xv * r
        gv = g_ref[...]
        diff = xh * gv - t_ref[...]

        @pl.when(pl.program_id(0) == 0)
        def _():
            dg_ref[...] = jnp.zeros_like(dg_ref)
            loss_ref[...] = jnp.zeros_like(loss_ref)

        part = 0.5 * jnp.sum(jnp.mean(diff * diff, axis=-1, keepdims=True), axis=0, keepdims=True)
        loss_ref[...] += jnp.broadcast_to(part, loss_ref.shape)
        dy = diff * (1.0 / D)
        dg_ref[...] += jnp.sum(dy * xh, axis=0, keepdims=True)
        dxh = dy * gv
        dx = r * (dxh - xh * jnp.mean(dxh * xh, axis=-1, keepdims=True))
        dx_ref[...] = dx
        dxb_ref[...] = dx.astype(dxb_ref.dtype)

    return pl.pallas_call(
        body, name=name, grid=(T // tm,),
        in_specs=[pl.BlockSpec((tm, D), lambda i: (i, 0)), pl.BlockSpec((1, D), lambda i: (0, 0)),
                  pl.BlockSpec((tm, D), lambda i: (i, 0))],
        out_specs=[pl.BlockSpec((1, 128), lambda i: (0, 0)), pl.BlockSpec((tm, D), lambda i: (i, 0)),
                   pl.BlockSpec((tm, D), lambda i: (i, 0)), pl.BlockSpec((1, D), lambda i: (0, 0))],
        out_shape=[jax.ShapeDtypeStruct((1, 128), F32), jax.ShapeDtypeStruct((T, D), F32),
                   jax.ShapeDtypeStruct((T, D), MXU_DTYPE), jax.ShapeDtypeStruct((1, D), F32)],
        compiler_params=_cp("arbitrary"),
    )(x, g, tgt)


def _adamw(name, w, m, v, parts, j=0, prev=None):
    L, R, C = w.shape
    P = parts.shape[0]
    tr = _tile(R, 256)
    ob = j * (R // tr)
    w, m, v = (a.reshape(L * R, C) for a in (w, m, v))
    c1 = 1.0 / (1.0 - ADAM_B1 ** ADAM_STEP)
    c2 = 1.0 / (1.0 - ADAM_B2 ** ADAM_STEP)
    chained = L > 1
    if chained and prev is None:
        prev = [lax.empty(w.shape, F32) for _ in range(4)]
    prev = list(prev) if chained else []

    def body(w_ref, m_ref, v_ref, p_ref, *rest):
        g_ref, d_ref, nm_ref, nv_ref, done_ref = rest[len(prev):]
        done_ref[...] = jnp.zeros_like(done_ref)
        g = p_ref[0].astype(F32)
        for s in range(1, P):
            g = g + p_ref[s].astype(F32)
        nm = ADAM_B1 * m_ref[...] + (1.0 - ADAM_B1) * g
        nv = ADAM_B2 * v_ref[...] + (1.0 - ADAM_B2) * (g * g)
        g_ref[...] = g
        nm_ref[...] = nm
        nv_ref[...] = nv
        d_ref[...] = -ADAM_LR * ((nm * c1) / (jnp.sqrt(nv * c2) + ADAM_EPS) + ADAM_WD * w_ref[...])

    blk = pl.BlockSpec((tr, C), lambda i: (i + ob, 0))
    return pl.pallas_call(
        body, name=name, grid=(R // tr,),
        in_specs=[blk, blk, blk, pl.BlockSpec((P, tr, C), lambda i: (0, i, 0))] + [ANY] * len(prev),
        out_specs=[blk, blk, blk, blk, pl.BlockSpec((8, 128), lambda i: (0, 0))],
        out_shape=[jax.ShapeDtypeStruct(w.shape, F32)] * 4 + [jax.ShapeDtypeStruct((8, 128), F32)],
        input_output_aliases={4 + i: i for i in range(len(prev))},
        compiler_params=_cp("arbitrary"),
    )(w, m, v, parts, *prev)


def _proj_in(name, h, wg, bias, deps=()):
    T, K = h.shape
    n = wg.shape[-1]
    tm = _tile(T, MM_ROWS)

    def body(a_ref, w_ref, b_ref, o_ref):
        o_ref[...] = _dot(a_ref[...], w_ref[...]) + b_ref[...]

    return pl.pallas_call(
        _with_deps(body, 3, deps), name=name, grid=(N_DEV, T // tm),
        in_specs=[pl.BlockSpec((tm, K), lambda s, i: (i, 0)),
                  pl.BlockSpec((None, K, n), lambda s, i: (s, 0, 0)),
                  pl.BlockSpec((1, n), lambda s, i: (0, s))] + [ANY] * len(deps),
        out_specs=pl.BlockSpec((tm, n), lambda s, i: (i, s)),
        out_shape=jax.ShapeDtypeStruct((T, N_DEV * n), F32), compiler_params=_cp("parallel", "parallel"),
    )(h, wg, bias, *deps)


def _ffn_in(name, h, wg_gate, wg_up):
    T, K = h.shape
    n = wg_gate.shape[-1]
    tm = _tile(T, MM_ROWS)

    def body(a_ref, wgt_ref, wup_ref, g_ref, u_ref, hid_ref):
        a = a_ref[...]
        g = _dot(a, wgt_ref[...])
        u = _dot(a, wup_ref[...])
        g_ref[...] = g
        u_ref[...] = u
        hid_ref[...] = (_silu(g) * u).astype(hid_ref.dtype)

    wspec = pl.BlockSpec((None, K, n), lambda s, i: (s, 0, 0))
    ospec = pl.BlockSpec((None, tm, n), lambda s, i: (s, i, 0))
    return pl.pallas_call(
        body, name=name, grid=(N_DEV, T // tm),
        in_specs=[pl.BlockSpec((tm, K), lambda s, i: (i, 0)), wspec, wspec],
        out_specs=[ospec, ospec, ospec],
        out_shape=[jax.ShapeDtypeStruct((N_DEV, T, n), F32), jax.ShapeDtypeStruct((N_DEV, T, n), F32),
                   jax.ShapeDtypeStruct((N_DEV, T, n), MXU_DTYPE)],
        compiler_params=_cp("parallel", "parallel"),
    )(h, wg_gate, wg_up)


def _a_spec(a, tm, k):
    if a.ndim == 2:
        return pl.BlockSpec((tm, k), lambda i, s: (i, s))
    return pl.BlockSpec((None, tm, k), lambda i, s: (s, i, 0))


def _proj_out(name, a, wg, res, deps=(), norm_g=None):
    k, N = wg.shape[-2:]
    T = res.shape[0]
    tm = _tile(T, MM_ROWS // 2)
    extra = [] if norm_g is None else [norm_g]

    def body(a_ref, w_ref, r_ref, *rest):
        o_ref = rest[len(extra)]
        p = _dot(a_ref[...], w_ref[...])

        @pl.when(pl.program_id(1) == 0)
        def _():
            o_ref[...] = r_ref[...] + p

        @pl.when(pl.program_id(1) > 0)
        def _():
            o_ref[...] += p

        if extra:
            @pl.when(pl.program_id(1) == N_DEV - 1)
            def _():
                xv = o_ref[...]
                r = lax.rsqrt(jnp.mean(xv * xv, axis=-1, keepdims=True) + EPS)
                rest[2][...] = ((xv * r) * rest[0][...]).astype(rest[2].dtype)

    row_blk = pl.BlockSpec((tm, N), lambda i, s: (i, 0))
    out = pl.pallas_call(
        _with_deps(body, 3 + len(extra), deps), name=name, grid=(T // tm, N_DEV),
        in_specs=[_a_spec(a, tm, k), pl.BlockSpec((None, k, N), lambda i, s: (s, 0, 0)), row_blk]
        + [pl.BlockSpec((1, N), lambda i, s: (0, 0))] * len(extra) + [ANY] * len(deps),
        out_specs=[row_blk] * (1 + len(extra)),
        out_shape=[jax.ShapeDtypeStruct((T, N), F32)] + [jax.ShapeDtypeStruct((T, N), MXU_DTYPE)] * len(extra),
        compiler_params=_cp("parallel", "arbitrary"),
    )(a, wg, res, *extra, *deps)
    return out[0] if norm_g is None else out


def _mix_out(name, y, wg, res, norm_g, deps=()):
    N = wg.shape[-1]
    w2 = wg.reshape(-1, N)
    K = w2.shape[0]
    T = res.shape[0]
    tm = _tile(T, MM_ROWS // 2)

    def body(a_ref, w_ref, r_ref, g_ref, o_ref, h_ref):
        xv = r_ref[...] + _dot(a_ref[...], w_ref[...])
        o_ref[...] = xv
        r = lax.rsqrt(jnp.mean(xv * xv, axis=-1, keepdims=True) + EPS)
        h_ref[...] = ((xv * r) * g_ref[...]).astype(h_ref.dtype)

    row_blk = pl.BlockSpec((tm, N), lambda i: (i, 0))
    return pl.pallas_call(
        _with_deps(body, 4, deps), name=name, grid=(T // tm,),
        in_specs=[pl.BlockSpec((tm, K), lambda i: (i, 0)),
                  pl.BlockSpec((K, N), lambda i: (0, 0), pipeline_mode=pl.Buffered(1)),
                  row_blk, pl.BlockSpec((1, N), lambda i: (0, 0))] + [ANY] * len(deps),
        out_specs=[row_blk, row_blk],
        out_shape=[jax.ShapeDtypeStruct((T, N), F32), jax.ShapeDtypeStruct((T, N), MXU_DTYPE)],
        compiler_params=_cp("parallel"),
    )(y, w2, res, norm_g, *deps)


def _bwd_in(name, das, wgs, deps=()):
    K, n = wgs[0].shape[-2:]
    T = das[0].shape[-2]
    tm = _tile(T, MM_ROWS)
    npair = len(das)

    def body(*refs):
        o_ref = refs[-1]
        p = _dot_nt(refs[0][...], refs[npair][...])
        for q in range(1, npair):
            p = p + _dot_nt(refs[q][...], refs[npair + q][...])

        @pl.when(pl.program_id(1) == 0)
        def _():
            o_ref[...] = p

        @pl.when(pl.program_id(1) > 0)
        def _():
            o_ref[...] += p

    return pl.pallas_call(
        _with_deps(body, 2 * npair, deps), name=name, grid=(T // tm, N_DEV),
        in_specs=[_a_spec(a, tm, n) for a in das]
        + [pl.BlockSpec((None, K, n), lambda i, s: (s, 0, 0)) for _ in wgs] + [ANY] * len(deps),
        out_specs=pl.BlockSpec((tm, K), lambda i, s: (i, 0)),
        out_shape=jax.ShapeDtypeStruct((T, K), F32), compiler_params=_cp("parallel", "arbitrary"),
    )(*das, *wgs, *deps)


def _bwd_out(name, dx, wg, deps=()):
    k, N = wg.shape[-2:]
    T = dx.shape[0]
    tm = _tile(T, MM_ROWS)

    def body(a_ref, w_ref, o_ref):
        o_ref[...] = _dot_nt(a_ref[...], w_ref[...])

    return pl.pallas_call(
        _with_deps(body, 2, deps), name=name, grid=(N_DEV, T // tm),
        in_specs=[pl.BlockSpec((tm, N), lambda s, i: (i, 0)),
                  pl.BlockSpec((None, k, N), lambda s, i: (s, 0, 0))] + [ANY] * len(deps),
        out_specs=pl.BlockSpec((tm, k), lambda s, i: (i, s)),
        out_shape=jax.ShapeDtypeStruct((T, N_DEV * k), F32), compiler_params=_cp("parallel", "parallel"),
    )(dx, wg, *deps)


def _ffn_bwd_hidden(name, dx, wg_down, gate, up, deps=()):
    n, N = wg_down.shape[-2:]
    T = dx.shape[0]
    tm = _tile(T, MM_ROWS)

    def body(a_ref, w_ref, g_ref, u_ref, dg_ref, du_ref):
        dh = _dot_nt(a_ref[...], w_ref[...])
        g = g_ref[...]
        dg_ref[...] = (dh * u_ref[...] * _dsilu(g)).astype(dg_ref.dtype)
        du_ref[...] = (dh * _silu(g)).astype(du_ref.dtype)

    sm = pl.BlockSpec((None, tm, n), lambda s, i: (s, i, 0))
    return pl.pallas_call(
        _with_deps(body, 4, deps), name=name, grid=(N_DEV, T // tm),
        in_specs=[pl.BlockSpec((tm, N), lambda s, i: (i, 0)),
                  pl.BlockSpec((None, n, N), lambda s, i: (s, 0, 0)), sm, sm] + [ANY] * len(deps),
        out_specs=[sm, sm],
        out_shape=[jax.ShapeDtypeStruct((N_DEV, T, n), MXU_DTYPE)] * 2,
        compiler_params=_cp("parallel", "parallel"),
    )(dx, wg_down, gate, up, *deps)


def _wgrad(name, a, c, rows, cols, deps=()):
    T = a.shape[-2]
    tk = _tile(T, WGRAD_ROWS)
    nk = T // tk

    def spec(z, w):
        if z.ndim == 3:
            return pl.BlockSpec((None, tk, w), lambda s, k: (s, k, 0))
        if z.shape[1] == w:
            return pl.BlockSpec((tk, w), lambda s, k: (k, 0))
        return pl.BlockSpec((tk, w), lambda s, k: (k, s))

    def body(a_ref, c_ref, o_ref, acc_ref):
        k = pl.program_id(1)
        p = _dot_tn(a_ref[...], c_ref[...])

        @pl.when(k == 0)
        def _():
            acc_ref[...] = p

        @pl.when(k > 0)
        def _():
            acc_ref[...] += p

        @pl.when(k == nk - 1)
        def _():
            o_ref[...] = acc_ref[...].astype(o_ref.dtype)

    return pl.pallas_call(
        _with_deps(body, 2, deps), name=name, grid=(N_DEV, nk),
        in_specs=[spec(a, rows), spec(c, cols)] + [ANY] * len(deps),
        out_specs=pl.BlockSpec((None, rows, cols), lambda s, k: (s, 0, 0)),
        out_shape=jax.ShapeDtypeStruct((N_DEV, rows, cols), WIRE_DTYPE),
        scratch_shapes=[pltpu.VMEM((rows, cols), F32)],
        compiler_params=_cp("parallel", "arbitrary"),
    )(a, c, *deps)


def _shift_down(cur, prev8, sh):
    n = cur.shape[0]
    rolled = pltpu.roll(cur, sh, 0)
    top = jnp.where(_rows(8) < sh, pltpu.roll(prev8, sh, 0), rolled[0:8])
    return jnp.concatenate([top, rolled[8:n]], axis=0)


def _shift_up(cur, next8, sh):
    n = cur.shape[0]
    rolled = pltpu.roll(cur, n - sh, 0)
    bot = jnp.where(_rows(8) >= 8 - sh, pltpu.roll(next8, 8 - sh, 0), rolled[n - 8:n])
    return jnp.concatenate([rolled[0:n - 8], bot], axis=0)


def _lru_gate_terms(r, lam):
    sp = _softplus(-lam)
    la = -LRU_C * r * sp
    a = jnp.exp(la)
    m2 = _one_minus_exp(2.0 * la)
    return sp, la, a, m2


def _lru_fwd(name, proj, conv_w, conv_b, wa, ba, wx, bx, lam):
    T = proj.shape[0]
    H, hd, _ = wa.shape
    W = H * hd
    K = conv_w.shape[0]
    tb = _tile(T, 256)

    def body(xin_ref, gate_ref, cw_ref, cb_ref, wa_ref, ba_ref, wx_ref, bx_ref, lam_ref,
             ya_ref, xc_ref, r_ref, i_ref, hs_ref, tail_ref, hprev_ref):
        blk = pl.program_id(0)

        @pl.when(blk == 0)
        def _():
            tail_ref[...] = jnp.zeros_like(tail_ref)
            hprev_ref[...] = jnp.zeros_like(hprev_ref)

        xin = xin_ref[...]
        prev8 = tail_ref[...]
        xc = cw_ref[K - 1:K, :] * xin
        for sh in range(1, K):
            xc = xc + cw_ref[K - 1 - sh:K - sh, :] * _shift_down(xin, prev8, sh)
        xc = xc + cb_ref[...]
        tail_ref[...] = xin[tb - 8:tb]
        xc_ref[...] = xc
        for h in range(H):
            cs = slice(h * hd, (h + 1) * hd)
            xh = xc[:, cs]
            r_ref[:, cs] = _sigmoid(_dot(xh, wa_ref[h]) + ba_ref[:, cs])
            i_ref[:, cs] = _sigmoid(_dot(xh, wx_ref[h]) + bx_ref[:, cs])
        r = r_ref[...]
        _, _, a, m2 = _lru_gate_terms(r, lam_ref[...])
        row = _rows(tb)
        mult = jnp.where((row == 0) & (blk == 0), 1.0, jnp.sqrt(jnp.maximum(m2, 0.0)))
        u = mult * i_ref[...] * xc
        d = 1
        while d < tb:
            keep = row >= d
            u = a * jnp.where(keep, pltpu.roll(u, d, 0), 0.0) + u
            a = a * jnp.where(keep, pltpu.roll(a, d, 0), 1.0)
            d *= 2
        hs = u + a * hprev_ref[...]
        hprev_ref[...] = hs[tb - 1:tb]
        hs_ref[...] = hs
        ya_ref[...] = (hs * _gelu(gate_ref[...])).astype(ya_ref.dtype)

    full = lambda shape: pl.BlockSpec(shape, lambda i: tuple(0 for _ in shape))
    blk = pl.BlockSpec((tb, W), lambda i: (i, 0))
    return pl.pallas_call(
        body, name=name, grid=(T // tb,),
        in_specs=[pl.BlockSpec((tb, W), lambda i: (i, 0)), pl.BlockSpec((tb, W), lambda i: (i, 1)),
                  full((K, W)), full((1, W)), full((H, hd, hd)), full((1, W)), full((H, hd, hd)),
                  full((1, W)), full((1, W))],
        out_specs=[blk] * 5,
        out_shape=[jax.ShapeDtypeStruct((T, 2 * W), MXU_DTYPE)] + [jax.ShapeDtypeStruct((T, W), F32)] * 4,
        scratch_shapes=[pltpu.VMEM((8, W), F32), pltpu.VMEM((1, W), F32)],
        compiler_params=_cp("arbitrary"),
    )(proj, proj, conv_w, conv_b, wa, ba, wx, bx, lam)


def _lru_bwd(name, proj, dy, xc, r, ig, hs, conv_w, wa, wx, lam, dpbuf):
    T = proj.shape[0]
    H, hd, _ = wa.shape
    W = H * hd
    K = conv_w.shape[0]
    tb = _tile(T, 256)
    nb = T // tb
    t8 = tb // 8

    def body(xin_ref, xprev_ref, gate_ref, dy_ref, xc_ref, r_ref, i_ref, hs_ref, hsprev_ref,
             cw_ref, wa_ref, wx_ref, lam_ref, dpbuf_ref,
             dp_ref, db_ref, dcw_ref, dcb_ref, dwa_ref, dba_ref, dwx_ref, dbx_ref, dlam_ref,
             cdh_ref, ca_ref, cdxc_ref, dxc_ref):
        del dpbuf_ref
        step = pl.program_id(0)
        blk = nb - 1 - step

        @pl.when(step == 0)
        def _():
            for ref in (db_ref, dcw_ref, dcb_ref, dwa_ref, dba_ref, dwx_ref, dbx_ref, dlam_ref,
                        cdh_ref, ca_ref, cdxc_ref):
                ref[...] = jnp.zeros_like(ref)

        row = _rows(tb)
        first = blk == 0
        gate = gate_ref[...]
        dy_a = dy_ref[...]
        hsv = hs_ref[...]
        _put(dp_ref, db_ref, W, 2 * W, dy_a * hsv * _dgelu(gate))
        d_hs = dy_a * _gelu(gate)
        lam = lam_ref[...]
        rv = r_ref[...]
        sp, la, a, m2 = _lru_gate_terms(rv, lam)
        an = jnp.where(row == tb - 1, ca_ref[...], pltpu.roll(a, tb - 1, 0))
        u = d_hs
        d = 1
        while d < tb:
            keep = row < tb - d
            u = an * jnp.where(keep, pltpu.roll(u, tb - d, 0), 0.0) + u
            an = an * jnp.where(keep, pltpu.roll(an, tb - d, 0), 1.0)
            d *= 2
        dh = u + an * cdh_ref[...]
        cdh_ref[...] = dh[0:1]
        ca_ref[...] = a[0:1]
        hlast = jnp.where(first, 0.0, hsprev_ref[7:8, :])
        hprev = jnp.where(row == 0, hlast, pltpu.roll(hsv, 1, 0))
        da = dh * hprev
        xcv = xc_ref[...]
        iv = i_ref[...]
        t0 = (row == 0) & first
        mult = jnp.sqrt(jnp.maximum(m2, 0.0))
        mult_eff = jnp.where(t0, 1.0, mult)
        d_mult = dh * iv * xcv
        d_i = dh * mult_eff * xcv
        dxc = dh * mult_eff * iv
        e2 = 1.0 - m2
        d_la = da * a + jnp.where(t0 | (m2 <= 0.0), 0.0, -d_mult * e2 / jnp.where(m2 > 0.0, mult, 1.0))
        d_r = d_la * (-LRU_C * sp)
        dlam_ref[...] += jnp.sum(d_la * (-LRU_C * rv), axis=0, keepdims=True) * (-_sigmoid(-lam))
        d_zr = d_r * rv * (1.0 - rv)
        d_zi = d_i * iv * (1.0 - iv)
        dba_ref[...] += jnp.sum(d_zr, axis=0, keepdims=True)
        dbx_ref[...] += jnp.sum(d_zi, axis=0, keepdims=True)
        for h in range(H):
            cs = slice(h * hd, (h + 1) * hd)
            xh = xcv[:, cs]
            zr, zi = d_zr[:, cs], d_zi[:, cs]
            dwa_ref[h] += _dot_tn(xh, zr)
            dwx_ref[h] += _dot_tn(xh, zi)
            dxc_ref[:, cs] = dxc[:, cs] + _dot_nt(zr, wa_ref[h]) + _dot_nt(zi, wx_ref[h])
        dxc = dxc_ref[...]
        dcb_ref[...] += jnp.sum(dxc, axis=0, keepdims=True)
        xin = xin_ref[...]
        prev8 = jnp.where(first, 0.0, xprev_ref[...])
        next8 = cdxc_ref[...]
        dxin = cw_ref[K - 1:K, :] * dxc
        dcw_ref[K - 1:K, :] += jnp.sum(dxc * xin, axis=0, keepdims=True)
        for sh in range(1, K):
            dxin = dxin + cw_ref[K - 1 - sh:K - sh, :] * _shift_up(dxc, next8, sh)
            dcw_ref[K - 1 - sh:K - sh, :] += jnp.sum(dxc * _shift_down(xin, prev8, sh), axis=0, keepdims=True)
        cdxc_ref[...] = dxc[0:8]
        _put(dp_ref, db_ref, 0, W, dxin)

    full = lambda shape: pl.BlockSpec(shape, lambda i: tuple(0 for _ in shape))
    cur = lambda col: pl.BlockSpec((tb, W), lambda i: (nb - 1 - i, col))
    prev = pl.BlockSpec((8, W), lambda i: (jnp.maximum((nb - 1 - i) * t8 - 1, 0), 0))
    return pl.pallas_call(
        body, name=name, grid=(nb,),
        in_specs=[cur(0), prev, cur(1), cur(0), cur(0), cur(0), cur(0), cur(0), prev,
                  full((K, W)), full((H, hd, hd)), full((H, hd, hd)), full((1, W)), ANY],
        out_specs=[pl.BlockSpec((tb, 2 * W), lambda i: (nb - 1 - i, 0)), full((1, 2 * W)), full((K, W)),
                   full((1, W)),
                   full((H, hd, hd)), full((1, W)), full((H, hd, hd)), full((1, W)), full((1, W))],
        out_shape=[jax.ShapeDtypeStruct(dpbuf.shape, dpbuf.dtype), jax.ShapeDtypeStruct((1, 2 * W), F32),
                   jax.ShapeDtypeStruct((K, W), F32),
                   jax.ShapeDtypeStruct((1, W), F32), jax.ShapeDtypeStruct((H, hd, hd), F32),
                   jax.ShapeDtypeStruct((1, W), F32), jax.ShapeDtypeStruct((H, hd, hd), F32),
                   jax.ShapeDtypeStruct((1, W), F32), jax.ShapeDtypeStruct((1, W), F32)],
        scratch_shapes=[pltpu.VMEM((1, W), F32), pltpu.VMEM((1, W), F32), pltpu.VMEM((8, W), F32),
                        pltpu.VMEM((tb, W), F32)],
        input_output_aliases={13: 0}, compiler_params=_cp("arbitrary"),
    )(proj, proj, proj, dy, xc, r, ig, hs, hs, conv_w, wa, wx, lam, dpbuf)


def _chunk_cumsum(g, c):
    n = g.shape[0]
    rc = _rows(n) & (c - 1)
    d = 1
    while d < c:
        g = g + jnp.where(rc >= d, pltpu.roll(g, d, 0), 0.0)
        d *= 2
    return g


def _chunk_rcumsum(g, c):
    n = g.shape[0]
    rc = _rows(n) & (c - 1)
    d = 1
    while d < c:
        g = g + jnp.where(rc < c - d, pltpu.roll(g, n - d, 0), 0.0)
        d *= 2
    return g


def _hgrn_pointwise(qr, fr, lb):
    qf = _silu(qr)
    sig = _sigmoid(fr)
    fg = lb + (1.0 - lb) * sig
    gl = jnp.log(jnp.maximum(fg, F_FLOOR))
    kk = (1.0 - lb) * (1.0 - sig)
    return qf, sig, fg, gl, kk


def _hgrn_fwd(name, proj, lb, norm_g, ybuf):
    T = proj.shape[0]
    W = lb.shape[1]
    H = HGRN_HEADS
    dk = W // H
    c = HGRN_SUB
    R = _tile(T, 128)
    nck = R // c

    def body(q_ref, f_ref, v_ref, g_ref, lb_ref, ng_ref, ybuf_ref, yb_ref, o_ref, sall_ref,
             st_ref, qe_ref, ke_ref, acc_ref):
        del ybuf_ref

        @pl.when(pl.program_id(0) == 0)
        def _():
            st_ref[...] = jnp.zeros_like(st_ref)

        qf, _, _, gl, kk = _hgrn_pointwise(q_ref[...], f_ref[...], lb_ref[...])
        b = _chunk_cumsum(gl * LOG2E, c)
        rc = _rows(R) & (c - 1)
        for h in range(H):
            cs = slice(h * dk, (h + 1) * dk)
            qh, kh, bh, vh = qf[:, cs], kk[:, cs], b[:, cs], v_ref[:, cs]
            acc = jnp.sum(qh * kh, axis=1, keepdims=True) * vh
            for d in range(1, c):
                e = jnp.where(rc >= d, jnp.exp2(bh - pltpu.roll(bh, d, 0)), 0.0)
                s = jnp.sum(qh * pltpu.roll(kh, d, 0) * e, axis=1, keepdims=True)
                acc = acc + s * pltpu.roll(vh, d, 0)
            acc_ref[:, cs] = acc
        qe_ref[...] = qf * jnp.exp2(b)
        for ci in range(nck):
            rs = slice(ci * c, (ci + 1) * c)
            bl = b[ci * c + c - 1:ci * c + c, :]
            ke_ref[rs, :] = kk[rs, :] * jnp.exp2(bl - b[rs, :])
            ebl = jnp.exp2(bl)
            for h in range(H):
                cs = slice(h * dk, (h + 1) * dk)
                st = st_ref[h]
                sall_ref[ci, h] = st
                o_ref[rs, cs] = acc_ref[rs, cs] + _dot_nt(qe_ref[rs, cs], st)
                st_ref[h] = st * ebl[:, cs] + _dot_tn(v_ref[rs, cs], ke_ref[rs, cs])
        ng = ng_ref[...]
        gg = g_ref[...]
        for h in range(H):
            cs = slice(h * dk, (h + 1) * dk)
            oh = o_ref[:, cs]
            rr = lax.rsqrt(jnp.mean(oh * oh, axis=1, keepdims=True) + EPS)
            yb_ref[:, cs] = ((oh * rr) * ng[:, cs] * _silu(gg[:, cs])).astype(yb_ref.dtype)

    full = lambda shape: pl.BlockSpec(shape, lambda i: tuple(0 for _ in shape))
    col = lambda k: pl.BlockSpec((R, W), lambda i: (i, k))
    blk = pl.BlockSpec((R, W), lambda i: (i, 0))
    return pl.pallas_call(
        body, name=name, grid=(T // R,),
        in_specs=[col(2), col(3), col(4), col(5), full((1, W)), full((1, W)), ANY],
        out_specs=[col(1), blk, pl.BlockSpec((nck, H, dk, dk), lambda i: (i, 0, 0, 0))],
        out_shape=[jax.ShapeDtypeStruct((T, 2 * W), MXU_DTYPE), jax.ShapeDtypeStruct((T, W), F32),
                   jax.ShapeDtypeStruct((T // c, H, dk, dk), F32)],
        scratch_shapes=[pltpu.VMEM((H, dk, dk), F32), pltpu.VMEM((R, W), F32), pltpu.VMEM((R, W), F32),
                        pltpu.VMEM((R, W), F32)],
        input_output_aliases={6: 0}, compiler_params=_cp("arbitrary"),
    )(proj, proj, proj, proj, lb, norm_g, ybuf)


def _hgrn_bwd(name, proj, dy, o, sall, lb, norm_g):
    T = proj.shape[0]
    W = lb.shape[1]
    H = HGRN_HEADS
    dk = W // H
    c = HGRN_SUB
    R = _tile(T, 128)
    nck = R // c
    nb = T // R

    def body(q_ref, f_ref, v_ref, g_ref, dy_ref, o_ref, sall_ref, lb_ref, ng_ref,
             dp_ref, db_ref, dlb_ref, dng_ref,
             dst_ref, do_ref, dq_ref, dk_ref, dv_ref, ex_ref, qe_ref, ke_ref):
        @pl.when(pl.program_id(0) == 0)
        def _():
            dst_ref[...] = jnp.zeros_like(dst_ref)
            db_ref[...] = jnp.zeros_like(db_ref)
            dlb_ref[...] = jnp.zeros_like(dlb_ref)
            dng_ref[...] = jnp.zeros_like(dng_ref)

        lbv = lb_ref[...]
        qr = q_ref[...]
        qf, sig, fg, gl, kk = _hgrn_pointwise(qr, f_ref[...], lbv)
        b = _chunk_cumsum(gl * LOG2E, c)
        rc = _rows(R) & (c - 1)
        ng = ng_ref[...]
        gg = g_ref[...]
        dyv = dy_ref[...]
        sg = _silu(gg)
        for h in range(H):
            cs = slice(h * dk, (h + 1) * dk)
            oh = o_ref[:, cs]
            rr = lax.rsqrt(jnp.mean(oh * oh, axis=1, keepdims=True) + EPS)
            ohat = oh * rr
            dyh = dyv[:, cs]
            _put(dp_ref, db_ref, 5 * W + h * dk, 5 * W + (h + 1) * dk,
                 dyh * ohat * ng[:, cs] * _dsilu(gg[:, cs]))
            t = dyh * sg[:, cs]
            dng_ref[:, cs] += jnp.sum(t * ohat, axis=0, keepdims=True)
            dohat = t * ng[:, cs]
            do_ref[:, cs] = rr * (dohat - ohat * jnp.mean(dohat * ohat, axis=1, keepdims=True))
        for h in range(H):
            cs = slice(h * dk, (h + 1) * dk)
            qh, kh, bh, vh, doh = qf[:, cs], kk[:, cs], b[:, cs], v_ref[:, cs], do_ref[:, cs]
            da0 = jnp.sum(doh * vh, axis=1, keepdims=True)
            a0 = jnp.sum(qh * kh, axis=1, keepdims=True)
            dq = da0 * kh
            dkk = da0 * qh
            dv = a0 * doh
            for d in range(1, c):
                e = jnp.where(rc >= d, jnp.exp2(bh - pltpu.roll(bh, d, 0)), 0.0)
                kr = pltpu.roll(kh, d, 0)
                da = jnp.sum(doh * pltpu.roll(vh, d, 0), axis=1, keepdims=True)
                aa = jnp.sum(qh * kr * e, axis=1, keepdims=True)
                dq = dq + da * kr * e
                dkk = dkk + pltpu.roll(da * qh * e, R - d, 0)
                dv = dv + pltpu.roll(aa * doh, R - d, 0)
            dq_ref[:, cs] = dq
            dk_ref[:, cs] = dkk
            dv_ref[:, cs] = dv
        eb = jnp.exp2(b)
        qe_ref[...] = qf * eb
        ex_ref[...] = jnp.zeros_like(ex_ref)
        for ci in reversed(range(nck)):
            rs = slice(ci * c, (ci + 1) * c)
            bl = b[ci * c + c - 1:ci * c + c, :]
            ebl_rows = jnp.exp2(bl - b[rs, :])
            ke_ref[rs, :] = kk[rs, :] * ebl_rows
            ebl = jnp.exp2(bl)
            for h in range(H):
                cs = slice(h * dk, (h + 1) * dk)
                st0 = sall_ref[ci, h]
                dst1 = dst_ref[h]
                doc = do_ref[rs, cs]
                vc = v_ref[rs, cs]
                dq_ref[rs, cs] += _dot(doc, st0) * eb[rs, cs]
                dv_ref[rs, cs] += _dot_nt(ke_ref[rs, cs], dst1)
                dki = _dot(vc, dst1) * ebl_rows[:, cs]
                dk_ref[rs, cs] += dki
                ex_ref[ci * c + c - 1:ci * c + c, cs] = (
                    jnp.sum(dki * kk[rs, cs], axis=0, keepdims=True)
                    + ebl[:, cs] * jnp.sum(st0 * dst1, axis=0, keepdims=True))
                dst_ref[h] = dst1 * ebl[:, cs] + _dot_tn(doc, qe_ref[rs, cs])
        dq = dq_ref[...]
        dkk = dk_ref[...]
        db = qf * dq - kk * dkk + ex_ref[...]
        dgl = _chunk_rcumsum(db, c)
        dfg = jnp.where(fg > F_FLOOR, dgl / jnp.maximum(fg, F_FLOOR), 0.0)
        dsig = (dfg - dkk) * (1.0 - lbv)
        dlb_ref[...] += jnp.sum((dfg - dkk) * (1.0 - sig), axis=0, keepdims=True)
        dp_ref[:, 0:2 * W] = jnp.zeros((R, 2 * W), dp_ref.dtype)
        _put(dp_ref, db_ref, 2 * W, 3 * W, dq * _dsilu(qr))
        _put(dp_ref, db_ref, 3 * W, 4 * W, dsig * sig * (1.0 - sig))
        _put(dp_ref, db_ref, 4 * W, 5 * W, dv_ref[...])

    full = lambda shape: pl.BlockSpec(shape, lambda i: tuple(0 for _ in shape))
    col = lambda k: pl.BlockSpec((R, W), lambda i: (nb - 1 - i, k))
    scr = pltpu.VMEM((R, W), F32)
    return pl.pallas_call(
        body, name=name, grid=(nb,),
        in_specs=[col(2), col(3), col(4), col(5), col(1), col(0),
                  pl.BlockSpec((nck, H, dk, dk), lambda i: (nb - 1 - i, 0, 0, 0)), full((1, W)), full((1, W))],
        out_specs=[pl.BlockSpec((R, 6 * W), lambda i: (nb - 1 - i, 0)), full((1, 6 * W)), full((1, W)),
                   full((1, W))],
        out_shape=[jax.ShapeDtypeStruct((T, 6 * W), MXU_DTYPE), jax.ShapeDtypeStruct((1, 6 * W), F32),
                   jax.ShapeDtypeStruct((1, W), F32), jax.ShapeDtypeStruct((1, W), F32)],
        scratch_shapes=[pltpu.VMEM((H, dk, dk), F32), scr, scr, scr, scr, scr, scr, scr],
        compiler_params=_cp("arbitrary"),
    )(proj, proj, proj, proj, dy, o, sall, lb, norm_g)


ODD_HALO = 32


def _row_shifts(ext, up):
    n = ext.shape[0]
    return [ext] + [pltpu.roll(ext, n - b if up else b, 0) for b in range(1, 8)]


def _past(copies, sh, tb):
    a, b = divmod(sh, 8)
    return copies[b][ODD_HALO - 8 * a:ODD_HALO - 8 * a + tb]


def _future(copies, sh, tb):
    a, b = divmod(sh, 8)
    return copies[b][8 * a:8 * a + tb]


def _odd_fwd(name, proj, sc_w, cf_w, cf_b, ln_g, ln_b):
    T = proj.shape[0]
    W = sc_w.shape[1]
    K3, K31 = sc_w.shape[0], cf_w.shape[0]
    tb = _tile(T, 256)
    hb = tb // ODD_HALO
    n = tb + ODD_HALO

    def body(cur_ref, prev_ref, w3_ref, w31_ref, cb_ref, lg_ref, lbeta_ref, y_ref, d_ref):
        keep = (pl.program_id(0) > 0).astype(F32)
        sb = cur_ref[:, 0:W]
        p = cur_ref[:, W:2 * W] * cur_ref[:, 2 * W:3 * W]
        glu = cur_ref[:, 3 * W:4 * W] * _sigmoid(cur_ref[:, 4 * W:5 * W])
        p_prev = prev_ref[:, W:2 * W] * prev_ref[:, 2 * W:3 * W] * keep
        glu_prev = prev_ref[:, 3 * W:4 * W] * _sigmoid(prev_ref[:, 4 * W:5 * W]) * keep
        ext = jnp.concatenate([p_prev, p], axis=0)
        cp = w3_ref[K3 - 1:K3, :] * p
        for sh in range(1, K3):
            cp = cp + w3_ref[K3 - 1 - sh:K3 - sh, :] * pltpu.roll(ext, sh, 0)[ODD_HALO:n]
        y_ref[:, 0:W] = (sb * cp).astype(y_ref.dtype)
        glu_past = _row_shifts(jnp.concatenate([glu_prev, glu], axis=0), up=False)
        d = cb_ref[...] + w31_ref[K31 - 1:K31, :] * glu
        for sh in range(1, K31):
            d = d + w31_ref[K31 - 1 - sh:K31 - sh, :] * _past(glu_past, sh, tb)
        d_ref[...] = d
        mu = jnp.mean(d, axis=1, keepdims=True)
        xc = d - mu
        rstd = lax.rsqrt(jnp.mean(xc * xc, axis=1, keepdims=True) + EPS)
        ln = (xc * rstd) * lg_ref[...] + lbeta_ref[...]
        y_ref[:, W:2 * W] = _silu(ln).astype(y_ref.dtype)

    full = lambda shape: pl.BlockSpec(shape, lambda i: tuple(0 for _ in shape))
    return pl.pallas_call(
        body, name=name, grid=(T // tb,),
        in_specs=[pl.BlockSpec((tb, 5 * W), lambda i: (i, 0)),
                  pl.BlockSpec((ODD_HALO, 5 * W), lambda i: (jnp.maximum(i * hb - 1, 0), 0)),
                  full((K3, W)), full((K31, W)), full((1, W)), full((1, W)), full((1, W))],
        out_specs=[pl.BlockSpec((tb, 2 * W), lambda i: (i, 0)), pl.BlockSpec((tb, W), lambda i: (i, 0))],
        out_shape=[jax.ShapeDtypeStruct((T, 2 * W), MXU_DTYPE), jax.ShapeDtypeStruct((T, W), F32)],
        compiler_params=_cp("parallel"),
    )(proj, proj, sc_w, cf_w, cf_b, ln_g, ln_b)


def _odd_bwd(name, proj, dy, dsave, sc_w, cf_w, ln_g, ln_b):
    T = proj.shape[0]
    W = sc_w.shape[1]
    K3, K31 = sc_w.shape[0], cf_w.shape[0]
    tb = _tile(T, 128)
    nb = T // tb
    hb = tb // ODD_HALO
    nh = T // ODD_HALO
    n = tb + ODD_HALO

    def body(cur_ref, prev_ref, next_ref, dy_ref, dyn_ref, d_ref, dn_ref,
             w3_ref, w31_ref, lg_ref, lbeta_ref,
             dp_ref, db_ref, dw3_ref, dw31_ref, dcb_ref, dlg_ref, dlb_ref):
        i = pl.program_id(0)

        @pl.when(i == 0)
        def _():
            for ref in (db_ref, dw3_ref, dw31_ref, dcb_ref, dlg_ref, dlb_ref):
                ref[...] = jnp.zeros_like(ref)

        keep_prev = (i > 0).astype(F32)
        keep_next = (i < nb - 1).astype(F32)
        sb = cur_ref[:, 0:W]
        scv = cur_ref[:, W:2 * W]
        svv = cur_ref[:, 2 * W:3 * W]
        cu = cur_ref[:, 3 * W:4 * W]
        sg = _sigmoid(cur_ref[:, 4 * W:5 * W])
        p = scv * svv
        glu = cu * sg
        p_prev = prev_ref[:, W:2 * W] * prev_ref[:, 2 * W:3 * W] * keep_prev
        glu_prev = prev_ref[:, 3 * W:4 * W] * _sigmoid(prev_ref[:, 4 * W:5 * W]) * keep_prev
        dext = jnp.concatenate([d_ref[...], dn_ref[...]], axis=0)
        dyd = jnp.concatenate([dy_ref[:, W:2 * W], dyn_ref[:, W:2 * W] * keep_next], axis=0)
        mu = jnp.mean(dext, axis=1, keepdims=True)
        xc = dext - mu
        rstd = lax.rsqrt(jnp.mean(xc * xc, axis=1, keepdims=True) + EPS)
        xh = xc * rstd
        lg = lg_ref[...]
        dln = dyd * _dsilu(xh * lg + lbeta_ref[...])
        dxh = dln * lg
        dd = rstd * (dxh - jnp.mean(dxh, axis=1, keepdims=True)
                     - xh * jnp.mean(dxh * xh, axis=1, keepdims=True))
        dlg_ref[...] += jnp.sum((dln * xh)[0:tb], axis=0, keepdims=True)
        dlb_ref[...] += jnp.sum(dln[0:tb], axis=0, keepdims=True)
        ddc = dd[0:tb]
        dcb_ref[...] += jnp.sum(ddc, axis=0, keepdims=True)
        dglu = w31_ref[K31 - 1:K31, :] * ddc
        glu_past = _row_shifts(jnp.concatenate([glu_prev, glu], axis=0), up=False)
        dd_future = _row_shifts(dd, up=True)
        dw31_ref[K31 - 1:K31, :] += jnp.sum(ddc * glu, axis=0, keepdims=True)
        for sh in range(1, K31):
            dglu = dglu + w31_ref[K31 - 1 - sh:K31 - sh, :] * _future(dd_future, sh, tb)
            dw31_ref[K31 - 1 - sh:K31 - sh, :] += jnp.sum(ddc * _past(glu_past, sh, tb), axis=0, keepdims=True)
        _put(dp_ref, db_ref, 3 * W, 4 * W, dglu * sg)
        _put(dp_ref, db_ref, 4 * W, 5 * W, dglu * cu * sg * (1.0 - sg))
        dyc = dy_ref[:, 0:W]
        dcp = jnp.concatenate([dyc * sb, dyn_ref[:, 0:W] * next_ref[:, 0:W] * keep_next], axis=0)
        dcpc = dcp[0:tb]
        ext = jnp.concatenate([p_prev, p], axis=0)
        cp = w3_ref[K3 - 1:K3, :] * p
        dpp = w3_ref[K3 - 1:K3, :] * dcpc
        dw3_ref[K3 - 1:K3, :] += jnp.sum(dcpc * p, axis=0, keepdims=True)
        up = dcp
        for sh in range(1, K3):
            up = pltpu.roll(up, n - 1, 0)
            ext = pltpu.roll(ext, 1, 0)
            shifted = ext[ODD_HALO:n]
            cp = cp + w3_ref[K3 - 1 - sh:K3 - sh, :] * shifted
            dpp = dpp + w3_ref[K3 - 1 - sh:K3 - sh, :] * up[0:tb]
            dw3_ref[K3 - 1 - sh:K3 - sh, :] += jnp.sum(dcpc * shifted, axis=0, keepdims=True)
        _put(dp_ref, db_ref, 0, W, dyc * cp)
        _put(dp_ref, db_ref, W, 2 * W, dpp * svv)
        _put(dp_ref, db_ref, 2 * W, 3 * W, dpp * scv)

    full = lambda shape: pl.BlockSpec(shape, lambda i: tuple(0 for _ in shape))
    prev_map = lambda i: (jnp.maximum(i * hb - 1, 0), 0)
    next_map = lambda i: (jnp.minimum((i + 1) * hb, nh - 1), 0)
    return pl.pallas_call(
        body, name=name, grid=(nb,),
        in_specs=[pl.BlockSpec((tb, 5 * W), lambda i: (i, 0)),
                  pl.BlockSpec((ODD_HALO, 5 * W), prev_map), pl.BlockSpec((ODD_HALO, 5 * W), next_map),
                  pl.BlockSpec((tb, 2 * W), lambda i: (i, 0)), pl.BlockSpec((ODD_HALO, 2 * W), next_map),
                  pl.BlockSpec((tb, W), lambda i: (i, 0)), pl.BlockSpec((ODD_HALO, W), next_map),
                  full((K3, W)), full((K31, W)), full((1, W)), full((1, W))],
        out_specs=[pl.BlockSpec((tb, 5 * W), lambda i: (i, 0)), full((1, 5 * W)), full((K3, W)), full((K31, W)),
                   full((1, W)), full((1, W)), full((1, W))],
        out_shape=[jax.ShapeDtypeStruct((T, 5 * W), MXU_DTYPE), jax.ShapeDtypeStruct((1, 5 * W), F32),
                   jax.ShapeDtypeStruct((K3, W), F32),
                   jax.ShapeDtypeStruct((K31, W), F32)] + [jax.ShapeDtypeStruct((1, W), F32)] * 3,
        compiler_params=_cp("arbitrary"),
    )(proj, proj, proj, dy, dy, dsave, dsave, sc_w, cf_w, ln_g, ln_b)


PACK_WIDTH = 1024


def _lower_bounds(logits):
    sm = jax.nn.softmax(logits.astype(F32), axis=0)
    return jnp.cumsum(sm, axis=0) - sm[0]


def _pack_rows(arrays):
    flat = jnp.concatenate([a.reshape(-1) for a in arrays])
    pad = (-flat.shape[0]) % (8 * PACK_WIDTH)
    return jnp.pad(flat, (0, pad)).reshape(-1, PACK_WIDTH)


def _unpack_rows(packed, shapes):
    flat = packed.reshape(-1)
    out, off = [], 0
    for s in shapes:
        sz = math.prod(s)
        out.append(flat[off:off + sz].reshape(s))
        off += sz
    return out


def _shards_last(a):
    n = a.shape[-1] // N_DEV
    return jnp.moveaxis(a.reshape(a.shape[:-1] + (N_DEV, n)), -2, 0)


def _unshard_last(a):
    a = jnp.moveaxis(a, 0, -2)
    return a.reshape(a.shape[:-2] + (a.shape[-2] * a.shape[-1],))


BIG = ("ev_w_in", "ev_w_out", "od_w_in", "od_w_out", "ffn_w_gate", "ffn_w_up", "ffn_w_down")
COLUMN_MAJOR = ("ffn_w_gate", "ffn_w_up")
SMALL_SHARDED = ("lru_conv_w", "od_b_in", "sc_conv_w", "cf_conv_w", "cf_conv_b", "cf_ln_g", "cf_ln_b")
SMALL_REPL = ("ln_mix_g", "ln_ffn_g", "ln_final_g", "ev_b_in", "lru_conv_b", "lru_wa", "lru_ba", "lru_wx",
              "lru_bx", "lru_lambda", "hgrn_lb_logits", "hgrn_norm_g")
WEIGHTS = ("ln_mix_g", "ln_ffn_g", "ln_final_g", "ev_w_in", "ev_b_in", "lru_conv_w", "lru_conv_b", "lru_wa",
           "lru_ba", "lru_wx", "lru_bx", "lru_lambda", "hgrn_lb_logits", "hgrn_norm_g", "ev_w_out", "od_w_in",
           "od_b_in", "sc_conv_w", "cf_conv_w", "cf_conv_b", "cf_ln_g", "cf_ln_b", "od_w_out", "ffn_w_gate",
           "ffn_w_up", "ffn_w_down")


def _layer_weights(l):
    mix = ("ev_w_in", "ev_w_out") if l % 2 == 0 else ("od_w_in", "od_w_out")
    return [(mix[0], l // 2), (mix[1], l // 2), ("ffn_w_gate", l), ("ffn_w_up", l), ("ffn_w_down", l)]


class _MeshExchange:
    def __init__(self, w, m, v, depth, first):
        self.w, self.m, self.v, self.depth, self.first = w, m, v, depth, tuple(first)
        self.w_bf = {k: w[k].astype(MXU_DTYPE) for k, _ in self._names(0, "mix")}
        self.ready, self.flight, self.rs, self.adam = {}, {}, {}, {}

    @staticmethod
    def _names(l, grp):
        names = _layer_weights(l)
        return names[:2] if grp == "mix" else names[2:]

    def _own_start(self, l, grp, deps):
        names = self._names(l, grp)
        srcs = [self.w_bf[k] for k, _ in names]
        layers = [j for _, j in names]
        plan = _plan_gather_own(layers)
        sems, lands, tok = _split_start(f"ag_own_start_{grp}{l}", plan, srcs, _place_own(srcs, layers), deps)
        self.flight[l, grp] = (plan, sems, srcs, lands)
        return tok

    def _turn(self, l, grp, deps):
        plan, sems, srcs, lands = self.flight[l, grp]
        lands = _split_wait(f"ag_own_wait_{grp}{l}", plan, sems, srcs, lands, deps)
        plan = _plan_gather_pass(len(lands))
        sems, passed, tok = _split_start(f"ag_pass_start_{grp}{l}", plan, [], lands)
        self.flight[l, grp] = (plan, sems, [], passed)
        toks = [tok]
        nl, ng = (l, "ffn") if grp == "mix" else (l + 1, "mix")
        if nl < self.depth:
            toks.append(self._own_start(nl, ng, (tok,)))
        return tuple(toks)

    def _pass_wait(self, l, grp, deps):
        plan, sems, srcs, lands = self.flight.pop((l, grp))
        lands = _split_wait(f"ag_pass_wait_{grp}{l}", plan, sems, srcs, lands, deps)
        self.ready[l, grp] = dict(zip([k for k, _ in self._names(l, grp)], lands))

    def layer_begin(self, l):
        toks = ()
        if l == 0:
            started = self._own_start(0, "mix", self.first)
            for k in BIG:
                if k not in self.w_bf:
                    self.w_bf[k] = lax.optimization_barrier((self.w[k], started))[0].astype(MXU_DTYPE)
            toks = self._turn(0, "mix", tuple(self.w_bf.values()))
            self._pass_wait(0, "mix", ())
        return self.ready.pop((l, "mix")), toks

    def tick(self, l, t, after):
        if t == 2:
            return self._turn(l, "ffn", (after,))
        if t == 3:
            self._pass_wait(l, "ffn", (after,))
        if t == 4 and l + 1 < self.depth:
            return self._turn(l + 1, "mix", (after,))
        if t == 5 and l + 1 < self.depth:
            self._pass_wait(l + 1, "mix", (after,))
        return ()

    def ffn_weights(self, l):
        return self.ready.pop((l, "ffn"))

    def grads(self, tag, named):
        srcs = [g for _, _, g in named]
        lands = [lax.empty((4,) + g.shape[1:], g.dtype) for g in srcs]
        plan = _plan_scatter_pair(len(srcs))
        sems, lands, tok = _split_start(f"rs_pair_start_{tag}", plan, srcs, lands)
        self.rs[tag] = (named, plan, sems, srcs, lands)
        return (tok,)

    def grads_mid(self, tag, after):
        named, plan, sems, srcs, lands = self.rs[tag]
        got = _split_wait(f"rs_pair_wait_{tag}", plan, sems, srcs, lands, (after,))
        both = [_pair_add(f"pair_add_{tag}_{i}", a, b) for i, (a, b) in enumerate(zip(srcs, got))]
        self.rs[tag] = (named, [p for p, _ in both], [f for _, f in both])

    def grads_send(self, tag, deps=()):
        named, parts, fins = self.rs[tag]
        plan = _plan_scatter_chips(len(parts))
        sems, fins, tok = _split_start(f"rs_chip_start_{tag}", plan, parts, fins, tuple(deps))
        self.rs[tag] = (named, plan, sems, parts, fins)
        return (tok,)

    def grads_end(self, tag, after):
        named, plan, sems, parts, fins = self.rs.pop(tag)
        fins = _split_wait(f"rs_chip_wait_{tag}", plan, sems, parts, fins, (after,))
        done = []
        for (k, j, _), fin in zip(named, fins):
            view = (lambda a: a.transpose(0, 2, 1)) if k in COLUMN_MAJOR else (lambda a: a)
            *self.adam[k], token = _adamw(f"adamw_{k}_{j}", view(self.w[k]), view(self.m[k]), view(self.v[k]),
                                          fin, j, self.adam.get(k))
            done.append(token)
        return tuple(done)

    def finish(self, after):
        for tag in list(self.rs):
            self.grads_end(tag, after)

    def results(self, k):
        if k in COLUMN_MAJOR:
            L, r, c = self.w[k].shape
            return [a.reshape(L, c, r).transpose(0, 2, 1) for a in self.adam[k]]
        return [a.reshape(self.w[k].shape) for a in self.adam[k]]


def _local_step(x, tgt, p, ex):
    T, D = x.shape
    depth = p["ln_mix_g"].shape[0]
    lbs = _lower_bounds(p["hgrn_lb_logits"])
    row = lambda a: a.reshape(1, -1)
    saved = []
    h = None
    for l in range(depth):
        j = l // 2
        wl, tok = ex.layer_begin(l)
        s = {"x": x}
        if h is None:
            h = _rmsnorm_fwd(f"norm_mix{l}", x, row(p["ln_mix_g"][l]))
        s["h"] = h
        if l % 2 == 0:
            proj = _proj_in(f"ev_in{l}", h, wl["ev_w_in"], row(p["ev_b_in"][j]), tok)
            wa = p["lru_wa"][j].astype(MXU_DTYPE)
            wx = p["lru_wx"][j].astype(MXU_DTYPE)
            ya, xc, r, ig, hs = _lru_fwd(f"lru_fwd{l}", proj, p["lru_conv_w"][j], row(p["lru_conv_b"][j]),
                                         wa, row(p["lru_ba"][j]), wx, row(p["lru_bx"][j]),
                                         row(p["lru_lambda"][j]))
            y, o, sall = _hgrn_fwd(f"hgrn_fwd{l}", proj, row(lbs[j]), row(p["hgrn_norm_g"][j]), ya)
            s.update(proj=proj, xc=xc, r=r, ig=ig, hs=hs, o=o, sall=sall, wa=wa, wx=wx)
            w_out = wl["ev_w_out"]
        else:
            proj = _proj_in(f"od_in{l}", h, wl["od_w_in"], row(p["od_b_in"][j]), tok)
            y, dsave = _odd_fwd(f"odd_fwd{l}", proj, p["sc_conv_w"][j], p["cf_conv_w"][j],
                                row(p["cf_conv_b"][j]), row(p["cf_ln_g"][j]), row(p["cf_ln_b"][j]))
            s.update(proj=proj, dsave=dsave)
            w_out = wl["od_w_out"]
        x, h2 = _mix_out(f"mix_out{l}", y, w_out, x, row(p["ln_ffn_g"][l]), ex.tick(l, 2, y))
        s["y"] = y
        s["xmid"] = x
        ex.tick(l, 3, x)
        wl = {**wl, **ex.ffn_weights(l)}
        gate, up, hid = _ffn_in(f"ffn_in{l}", h2, wl["ffn_w_gate"], wl["ffn_w_up"])
        tok = ex.tick(l, 4, hid)
        if l + 1 < depth:
            x, h = _proj_out(f"ffn_out{l}", hid, wl["ffn_w_down"], x, tok, row(p["ln_mix_g"][l + 1]))
        else:
            x = _proj_out(f"ffn_out{l}", hid, wl["ffn_w_down"], x, tok)
        ex.tick(l, 5, x)
        s.update(h2=h2, gate=gate, up=up, hid=hid, w=wl)
        saved.append(s)

    loss, dx, dxb, dg_final = _loss_head("loss_head", x, row(p["ln_final_g"]), tgt)

    gs = {k: [None] * p[k].shape[0] for k in SMALL_REPL + SMALL_SHARDED if k not in ("ln_final_g", "hgrn_lb_logits")}
    d_lb = [None] * (depth // 2 + depth % 2)
    tok = ()
    pending = None
    for l in reversed(range(depth)):
        j = l // 2
        s = saved[l]
        wl = s["w"]
        ffn_shape = wl["ffn_w_gate"].shape[1:]
        dwd = _wgrad(f"ffn_dwd{l}", s["hid"], dxb, ffn_shape[1], ffn_shape[0])
        dgate, dup = _ffn_bwd_hidden(f"ffn_bwd_hid{l}", dxb, wl["ffn_w_down"], s["gate"], s["up"], tok)
        dwg = _wgrad(f"ffn_dwg{l}", dgate, s["h2"], ffn_shape[1], ffn_shape[0])
        dwu = _wgrad(f"ffn_dwu{l}", dup, s["h2"], ffn_shape[1], ffn_shape[0])
        tok = ex.grads(f"ffn{l}", [("ffn_w_down", l, dwd), ("ffn_w_gate", l, dwg), ("ffn_w_up", l, dwu)])
        dh2 = _bwd_in(f"ffn_dh{l}", [dgate, dup], [wl["ffn_w_gate"], wl["ffn_w_up"]], tok)
        dx, dxb, dg = _rmsnorm_bwd(f"norm_ffn_bwd{l}", s["xmid"], row(p["ln_ffn_g"][l]), dh2, dx)
        gs["ln_ffn_g"][l] = dg[0]
        ex.grads_mid(f"ffn{l}", dxb)
        done = ex.grads_end(pending, dxb) if pending is not None else ()
        tok = ex.grads_send(f"ffn{l}", done)
        w_in, w_out = ("ev_w_in", "ev_w_out") if l % 2 == 0 else ("od_w_in", "od_w_out")
        dwo = _wgrad(f"mix_dwo{l}", s["y"], dxb, *wl[w_out].shape[1:])
        if l % 2 == 0:
            dy = _bwd_out(f"ev_dy{l}", dxb, wl["ev_w_out"], tok)
            dph, db_h, dlb, dng = _hgrn_bwd(f"hgrn_bwd{l}", s["proj"], dy, s["o"], s["sall"], row(lbs[j]),
                                            row(p["hgrn_norm_g"][j]))
            dproj, db_a, d_cw, d_cb, d_wa, d_ba, d_wx, d_bx, d_lam = _lru_bwd(
                f"lru_bwd{l}", s["proj"], dy, s["xc"], s["r"], s["ig"], s["hs"], p["lru_conv_w"][j],
                s["wa"], s["wx"], row(p["lru_lambda"][j]), dph)
            gs["lru_conv_w"][j], gs["lru_conv_b"][j] = d_cw, d_cb[0]
            gs["lru_wa"][j], gs["lru_ba"][j] = d_wa, d_ba.reshape(p["lru_ba"].shape[1:])
            gs["lru_wx"][j], gs["lru_bx"][j] = d_wx, d_bx.reshape(p["lru_bx"].shape[1:])
            gs["lru_lambda"][j], gs["hgrn_norm_g"][j] = d_lam[0], dng[0]
            d_lb[j] = dlb[0]
            gs["ev_b_in"][j] = jnp.concatenate([db_a[0], db_h[0, db_a.shape[1]:]])
        else:
            dy = _bwd_out(f"od_dy{l}", dxb, wl["od_w_out"], tok)
            dproj, db_in, d_w3, d_w31, d_cfb, d_lg, d_lbeta = _odd_bwd(
                f"odd_bwd{l}", s["proj"], dy, s["dsave"], p["sc_conv_w"][j], p["cf_conv_w"][j],
                row(p["cf_ln_g"][j]), row(p["cf_ln_b"][j]))
            gs["sc_conv_w"][j], gs["cf_conv_w"][j] = d_w3, d_w31
            gs["cf_conv_b"][j], gs["cf_ln_g"][j], gs["cf_ln_b"][j] = d_cfb[0], d_lg[0], d_lbeta[0]
            gs["od_b_in"][j] = db_in[0]
        dwi = _wgrad(f"mix_dwi{l}", s["h"], dproj, *wl[w_in].shape[1:])
        tok = ex.grads(f"mix{l}", [(w_out, j, dwo), (w_in, j, dwi)])
        dh = _bwd_in(f"mix_dh{l}", [dproj], [wl[w_in]], tok)
        dx, dxb, dg = _rmsnorm_bwd(f"norm_mix_bwd{l}", s["x"], row(p["ln_mix_g"][l]), dh, dx)
        gs["ln_mix_g"][l] = dg[0]
        ex.grads_mid(f"mix{l}", dxb)
        if l > 0:
            tok = ex.grads_send(f"mix{l}", ex.grads_end(f"ffn{l}", dxb))
        else:
            tok = ex.grads_send(f"mix{l}")
            ex.grads_end(f"ffn{l}", (tok or (dxb,))[0])
        pending = f"mix{l}"

    small = {k: jnp.stack(v) for k, v in gs.items()}
    small["ln_final_g"] = dg_final[0]
    _, lb_vjp = jax.vjp(_lower_bounds, p["hgrn_lb_logits"])
    small["hgrn_lb_logits"] = lb_vjp(jnp.stack(d_lb))[0]
    if tok:
        small = lax.optimization_barrier((small, tok))[0]
    return loss, dx, small


def kernel(x, ln_mix_g, ln_ffn_g, ln_final_g, ev_w_in, ev_b_in, lru_conv_w, lru_conv_b, lru_wa, lru_ba, lru_wx, lru_bx, lru_lambda, hgrn_lb_logits, hgrn_norm_g, ev_w_out, od_w_in, od_b_in, sc_conv_w, cf_conv_w, cf_conv_b, cf_ln_g, cf_ln_b, od_w_out, ffn_w_gate, ffn_w_up, ffn_w_down, loss_target, m_ln_mix_g, m_ln_ffn_g, m_ln_final_g, m_ev_w_in, m_ev_b_in, m_lru_conv_w, m_lru_conv_b, m_lru_wa, m_lru_ba, m_lru_wx, m_lru_bx, m_lru_lambda, m_hgrn_lb_logits, m_hgrn_norm_g, m_ev_w_out, m_od_w_in, m_od_b_in, m_sc_conv_w, m_cf_conv_w, m_cf_conv_b, m_cf_ln_g, m_cf_ln_b, m_od_w_out, m_ffn_w_gate, m_ffn_w_up, m_ffn_w_down, v_ln_mix_g, v_ln_ffn_g, v_ln_final_g, v_ev_w_in, v_ev_b_in, v_lru_conv_w, v_lru_conv_b, v_lru_wa, v_lru_ba, v_lru_wx, v_lru_bx, v_lru_lambda, v_hgrn_lb_logits, v_hgrn_norm_g, v_ev_w_out, v_od_w_in, v_od_b_in, v_sc_conv_w, v_cf_conv_w, v_cf_conv_b, v_cf_ln_g, v_cf_ln_b, v_od_w_out, v_ffn_w_gate, v_ffn_w_up, v_ffn_w_down):
    args = locals()
    w = {k: args[k] for k in WEIGHTS}
    m = {k: args["m_" + k] for k in WEIGHTS}
    v = {k: args["v_" + k] for k in WEIGHTS}
    assert x.shape[0] == 1
    T, D = x.shape[1:]

    local_shapes = [w[k].shape for k in SMALL_SHARDED]
    gathered = _all_gather("gather_small_params", [_pack_rows([w[k] for k in SMALL_SHARDED])])[0]
    p = {k: w[k] for k in SMALL_REPL}
    per_dev = [_unpack_rows(gathered[s], local_shapes) for s in range(N_DEV)]
    for i, k in enumerate(SMALL_SHARDED):
        p[k] = _unshard_last(jnp.stack([per_dev[s][i] for s in range(N_DEV)]))

    ex = _MeshExchange(w, m, v, ln_mix_g.shape[0], [gathered])
    loss_part, dx, small = _local_step(x[0], loss_target[0], p, ex)
    loss = lax.psum(loss_part[0, 0], ("x", "y", "c"))

    small_sh = jnp.stack([_pack_rows([_shards_last(small[k])[s] for k in SMALL_SHARDED]) for s in range(N_DEV)])
    got = _pair_exchange("small_grads_to_sibling", [small_sh])[0]
    final_small = _chip_exchange("small_grads_to_chips", [_pair_add("pair_add_small", small_sh, got)[0]])[0]
    repl_parts = _all_gather("gather_small_grads", [_pack_rows([small[k] for k in SMALL_REPL])])[0]

    out_g, out_d, out_m, out_v = {}, {}, {}, {}
    res = _adamw("adamw_small_sharded", *[_pack_rows([t[k] for k in SMALL_SHARDED])[None] for t in (w, m, v)],
                 final_small)
    for o, r in zip((out_g, out_d, out_m, out_v), res):
        o.update(zip(SMALL_SHARDED, _unpack_rows(r, local_shapes)))
    res = _adamw("adamw_small_repl", *[_pack_rows([t[k] for k in SMALL_REPL])[None] for t in (w, m, v)], repl_parts)
    for o, r in zip((out_g, out_d, out_m, out_v), res):
        o.update(zip(SMALL_REPL, _unpack_rows(r, [w[k].shape for k in SMALL_REPL])))
    ex.finish(res[0])
    for k in BIG:
        out_g[k], out_d[k], out_m[k], out_v[k] = ex.results(k)

    return (loss, dx[None], *[out_g[k] for k in WEIGHTS], *[out_d[k] for k in WEIGHTS],
            *[out_m[k] for k in WEIGHTS], *[out_v[k] for k in WEIGHTS])
```

```python
import functools
import math

import jax
import jax.numpy as jnp
from jax import lax
from jax.experimental import pallas as pl
from jax.experimental.pallas import tpu as pltpu

F32 = jnp.float32
MXU_DTYPE = jnp.bfloat16
WIRE_DTYPE = jnp.bfloat16
N_DEV = 8
EPS = 1e-6
F_FLOOR = 1e-30
LRU_C = 8.0
HGRN_HEADS = 8
HGRN_SUB = 16
ADAM_LR, ADAM_B1, ADAM_B2, ADAM_EPS, ADAM_WD, ADAM_STEP = 0.001, 0.9, 0.999, 1e-08, 0.01, 10
V7X_VMEM_LIMIT = 48 * 1024 * 1024
MM_ROWS = 1024
WGRAD_ROWS = 2048
EPILOGUE_CHUNKS = 4
MESH = pl.DeviceIdType.MESH
ANY = pl.BlockSpec(memory_space=pl.ANY)


def _cp(*sem):
    return pltpu.CompilerParams(dimension_semantics=sem or None, vmem_limit_bytes=V7X_VMEM_LIMIT)


def _sigmoid(x):
    return 1.0 / (1.0 + jnp.exp(-x))


def _silu(x):
    return x * _sigmoid(x)


def _dsilu(x):
    s = _sigmoid(x)
    return s * (1.0 + x * (1.0 - s))


_GELU_C = math.sqrt(2.0 / math.pi)
LOG2E = 1.0 / math.log(2.0)


def _gelu(x):
    return 0.5 * x * (1.0 + jnp.tanh(_GELU_C * (x + 0.044715 * x * x * x)))


def _dgelu(x):
    t = jnp.tanh(_GELU_C * (x + 0.044715 * x * x * x))
    return 0.5 * (1.0 + t) + 0.5 * x * (1.0 - t * t) * _GELU_C * (1.0 + 3.0 * 0.044715 * x * x)


def _log1p(e):
    return jnp.where(e < 1e-2, e * (1.0 - e * (0.5 - e * (1.0 / 3.0))), jnp.log(1.0 + e))


def _softplus(x):
    return jnp.maximum(x, 0.0) + _log1p(jnp.exp(-jnp.abs(x)))


def _one_minus_exp(x):
    series = -x * (1.0 + x * (0.5 + x * (1.0 / 6.0 + x * (1.0 / 24.0))))
    return jnp.where(x > -0.05, series, 1.0 - jnp.exp(x))


def _rows(n, d=1):
    return lax.broadcasted_iota(jnp.int32, (n, d), 0)


def _dot(a, b):
    return jnp.dot(a.astype(MXU_DTYPE), b.astype(MXU_DTYPE), preferred_element_type=F32)


def _dot_nt(a, b):
    return lax.dot_general(a.astype(MXU_DTYPE), b.astype(MXU_DTYPE), (((1,), (1,)), ((), ())),
                           preferred_element_type=F32)


def _dot_tn(a, b):
    return lax.dot_general(a.astype(MXU_DTYPE), b.astype(MXU_DTYPE), (((0,), (0,)), ((), ())),
                           preferred_element_type=F32)


def _tile(n, want):
    if n <= want:
        return n
    t = want - want % 8
    while n % t:
        t -= 8
    assert t > 0, (n, want)
    return t


def _my_place():
    x, y, c = lax.axis_index("x"), lax.axis_index("y"), lax.axis_index("c")
    return x, y, c


def _all_gather(name, srcs):
    n = len(srcs)

    def body(*refs):
        src_refs, out_refs = refs[:n], refs[n:2 * n]
        send_sems, recv_sems, local_sems = refs[2 * n:]
        x, y, c = _my_place()
        sibling = (x, y, 1 - c)
        chips = [(1 - x, y), (x, 1 - y), (1 - x, 1 - y)]

        def slot(px, py, pc):
            return 4 * px + 2 * py + pc

        def copy(i, k, block, to, src=None):
            dst = out_refs[i].at[slot(*block)]
            return pltpu.make_async_remote_copy(
                src_ref=dst if src is None else src, dst_ref=dst,
                send_sem=send_sems.at[i, k], recv_sem=recv_sems.at[i, k],
                device_id=to, device_id_type=MESH)

        me = (x, y, c)
        sends, own = [], []
        for i in range(n):
            mine = pltpu.make_async_copy(src_refs[i], out_refs[i].at[slot(*me)], local_sems.at[i])
            mine.start()
            own.append(mine)
            first = [copy(i, 0, me, sibling, src=src_refs[i])]
            first += [copy(i, 1 + j, me, (*chip, c), src=src_refs[i]) for j, chip in enumerate(chips)]
            for cp in first:
                cp.start()
            sends += first
        for i in range(n):
            for j, chip in enumerate(chips):
                copy(i, 1 + j, (*chip, c), me).wait_recv()
                passed = copy(i, 4 + j, (*chip, c), sibling)
                passed.start()
                sends.append(passed)
        for i in range(n):
            copy(i, 0, sibling, me).wait_recv()
            for j, chip in enumerate(chips):
                copy(i, 4 + j, (*chip, 1 - c), me).wait_recv()
        for cp in sends:
            cp.wait_send()
        for cp in own:
            cp.wait()

    outs = pl.pallas_call(
        body, name=name,
        out_shape=[jax.ShapeDtypeStruct((N_DEV,) + s.shape, s.dtype) for s in srcs],
        in_specs=[ANY] * n, out_specs=[ANY] * n,
        scratch_shapes=[pltpu.SemaphoreType.DMA((n, 7)), pltpu.SemaphoreType.DMA((n, 7)),
                        pltpu.SemaphoreType.DMA((n,))],
    )(*srcs)
    return list(outs)


def _pair_exchange(name, srcs):
    n = len(srcs)

    def body(*refs):
        src_refs, out_refs = refs[:n], refs[n:2 * n]
        send_sems, recv_sems = refs[2 * n:]
        x, y, c = _my_place()
        copies = []
        for i in range(n):
            for j in range(4):
                cp = pltpu.make_async_remote_copy(
                    src_ref=src_refs[i].at[2 * j + (1 - c)], dst_ref=out_refs[i].at[j],
                    send_sem=send_sems.at[i, j], recv_sem=recv_sems.at[i, j],
                    device_id=(x, y, 1 - c), device_id_type=MESH)
                cp.start()
                copies.append(cp)
        for cp in copies:
            cp.wait()

    outs = pl.pallas_call(
        body, name=name,
        out_shape=[jax.ShapeDtypeStruct((4,) + s.shape[1:], s.dtype) for s in srcs],
        in_specs=[ANY] * n, out_specs=[ANY] * n,
        scratch_shapes=[pltpu.SemaphoreType.DMA((n, 4)), pltpu.SemaphoreType.DMA((n, 4))],
    )(*srcs)
    return list(outs)


def _chip_exchange(name, srcs):
    n = len(srcs)

    def body(*refs):
        src_refs, out_refs = refs[:n], refs[n:2 * n]
        send_sems, recv_sems, local_sems = refs[2 * n:]
        x, y, c = _my_place()
        chip = 2 * x + y
        copies = []
        for i in range(n):
            mine = pltpu.make_async_copy(src_refs[i].at[chip], out_refs[i].at[3], local_sems.at[i])
            mine.start()
            copies.append(mine)
            for k, (fx, fy) in enumerate([(1, 0), (0, 1), (1, 1)]):
                px = x + fx - 2 * x * fx
                py = y + fy - 2 * y * fy
                cp = pltpu.make_async_remote_copy(
                    src_ref=src_refs[i].at[2 * px + py], dst_ref=out_refs[i].at[k],
                    send_sem=send_sems.at[i, k], recv_sem=recv_sems.at[i, k],
                    device_id=(px, py, c), device_id_type=MESH)
                cp.start()
                copies.append(cp)
        for cp in copies:
            cp.wait()

    outs = pl.pallas_call(
        body, name=name,
        out_shape=[jax.ShapeDtypeStruct(s.shape, s.dtype) for s in srcs],
        in_specs=[ANY] * n, out_specs=[ANY] * n,
        scratch_shapes=[pltpu.SemaphoreType.DMA((n, 3)), pltpu.SemaphoreType.DMA((n, 3)),
                        pltpu.SemaphoreType.DMA((n,))],
    )(*srcs)
    return list(outs)


def _pair_add(name, mine, got):
    assert mine.shape[0] == N_DEV and got.shape[0] == 4
    cdim = mine.shape[-1]
    m4 = mine.reshape(4, 2, -1, cdim)
    g3 = got.reshape(4, -1, cdim)
    rows = m4.shape[2]
    tr = _tile(rows, 512)

    def body(m_ref, g_ref, o_ref, fin_ref):
        x, y, _ = _my_place()
        s = (m_ref[...].astype(F32) + g_ref[...].astype(F32)).astype(o_ref.dtype)
        o_ref[...] = s

        @pl.when(pl.program_id(1) == 2 * x + y)
        def _():
            fin_ref[...] = s

    out, fin = pl.pallas_call(
        body, name=name, grid=(rows // tr, 4),
        in_specs=[pl.BlockSpec((None, None, tr, cdim), lambda i, j: (j, lax.axis_index("c"), i, 0)),
                  pl.BlockSpec((None, tr, cdim), lambda i, j: (j, i, 0))],
        out_specs=[pl.BlockSpec((None, tr, cdim), lambda i, j: (j, i, 0)),
                   pl.BlockSpec((None, tr, cdim), lambda i, j: (3, i, 0))],
        out_shape=[jax.ShapeDtypeStruct(g3.shape, got.dtype)] * 2,
        compiler_params=_cp("parallel", "arbitrary"),
    )(m4, g3)
    return out.reshape(got.shape), fin.reshape(got.shape)


HBM = pl.BlockSpec(memory_space=pltpu.HBM)
SEM = pl.BlockSpec(memory_space=pltpu.SEMAPHORE)
EFFECT = pltpu.SideEffectType.DATAFLOW_SIDE_EFFECTING
SLOTS = 4


def _hbm(a):
    return pltpu.with_memory_space_constraint(a, pltpu.HBM)


def _remote(src, dst, sems, i, k, to):
    return pltpu.make_async_remote_copy(src_ref=src, dst_ref=dst, send_sem=sems[0].at[i * SLOTS + k],
                                        recv_sem=sems[1].at[i * SLOTS + k], device_id=to, device_id_type=MESH)


def _slot(px, py, pc):
    return 4 * px + 2 * py + pc


def _other_chips(x, y):
    return [(1 - x, y), (x, 1 - y), (1 - x, 1 - y)]


def _plan_gather_own(layers):
    def plan(srcs, lands, sems):
        x, y, c = _my_place()
        out = []
        for i, j in enumerate(layers):
            dst = lands[i].at[_slot(x, y, c)]
            out.append(_remote(srcs[i].at[j], dst, sems, i, 0, (x, y, 1 - c)))
            for k, (px, py) in enumerate(_other_chips(x, y)):
                out.append(_remote(srcs[i].at[j], dst, sems, i, 1 + k, (px, py, c)))
        return out
    return plan


def _plan_gather_pass(n):
    def plan(srcs, lands, sems):
        x, y, c = _my_place()
        out = []
        for i in range(n):
            for k, (px, py) in enumerate(_other_chips(x, y)):
                blk = lands[i].at[_slot(px, py, c)]
                out.append(_remote(blk, blk, sems, i, k, (x, y, 1 - c)))
        return out
    return plan


def _plan_scatter_pair(n):
    def plan(srcs, lands, sems):
        x, y, c = _my_place()
        return [_remote(srcs[i].at[2 * j + (1 - c)], lands[i].at[j], sems, i, j, (x, y, 1 - c))
                for i in range(n) for j in range(4)]
    return plan


def _plan_scatter_chips(n):
    def plan(srcs, lands, sems):
        x, y, c = _my_place()
        return [_remote(srcs[i].at[2 * px + py], lands[i].at[k], sems, i, k, (px, py, c))
                for i in range(n) for k, (px, py) in enumerate(_other_chips(x, y))]
    return plan


def _split_start(name, plan, srcs, lands, deps=()):
    ns, nl, nd = len(srcs), len(lands), len(deps)
    n = max(ns, nl)

    def body(*refs):
        sems = refs[ns + nl + nd:ns + nl + nd + 2]
        for cp in plan(refs[:ns], refs[ns:ns + nl], sems):
            cp.start()
        refs[-1][...] = jnp.zeros_like(refs[-1])

    outs = pl.pallas_call(
        body, name=name,
        out_shape=(pltpu.SemaphoreType.DMA((n * SLOTS,)), pltpu.SemaphoreType.DMA((n * SLOTS,)),
                   *[pltpu.HBM(a.shape, a.dtype) for a in lands], jax.ShapeDtypeStruct((8, 128), F32)),
        in_specs=[HBM] * (ns + nl) + [ANY] * nd,
        out_specs=(SEM, SEM, *[HBM] * nl, pl.BlockSpec(memory_space=pltpu.VMEM)),
        input_output_aliases={ns + i: 2 + i for i in range(nl)},
        compiler_params=pltpu.CompilerParams(has_side_effects=EFFECT),
    )(*[_hbm(a) for a in srcs], *[_hbm(a) for a in lands], *deps)
    return (outs[0], outs[1]), list(outs[2:2 + nl]), outs[-1]


def _split_wait(name, plan, sems, srcs, lands, deps=()):
    ns, nl, nd = len(srcs), len(lands), len(deps)

    def body(*refs):
        for cp in plan(refs[:ns], refs[ns:ns + nl], refs[ns + nl:ns + nl + 2]):
            cp.wait_send()
            cp.wait_recv()

    outs = pl.pallas_call(
        body, name=name,
        out_shape=tuple(pltpu.HBM(a.shape, a.dtype) for a in lands),
        in_specs=[HBM] * (ns + nl) + [SEM, SEM] + [ANY] * nd,
        out_specs=tuple([HBM] * nl),
        input_output_aliases={ns + i: i for i in range(nl)},
        compiler_params=pltpu.CompilerParams(has_side_effects=EFFECT),
    )(*srcs, *lands, *sems, *deps)
    return list(outs)


def _place_own(srcs, layers):
    x, y, c = _my_place()
    zero = jnp.zeros((), jnp.int32)
    return [lax.dynamic_update_slice(lax.empty((N_DEV,) + a.shape[1:], a.dtype), a[j][None],
                                     (_slot(x, y, c),) + (zero,) * (a.ndim - 1))
            for a, j in zip(srcs, layers)]


def _put(dp_ref, db_ref, lo, hi, val):
    dp_ref[:, lo:hi] = val.astype(dp_ref.dtype)
    db_ref[:, lo:hi] += jnp.sum(val, axis=0, keepdims=True)


def _with_deps(body, n_in, deps):
    nd = len(deps)
    if not nd:
        return body

    def wrapped(*refs):
        body(*refs[:n_in], *refs[n_in + nd:])

    return wrapped


def _rmsnorm_fwd(name, x, g, deps=()):
    T, D = x.shape
    tm = _tile(T, 256)

    def body(x_ref, g_ref, o_ref):
        xv = x_ref[...]
        r = lax.rsqrt(jnp.mean(xv * xv, axis=-1, keepdims=True) + EPS)
        o_ref[...] = ((xv * r) * g_ref[...]).astype(o_ref.dtype)

    return pl.pallas_call(
        _with_deps(body, 2, deps), name=name, grid=(T // tm,),
        in_specs=[pl.BlockSpec((tm, D), lambda i: (i, 0)), pl.BlockSpec((1, D), lambda i: (0, 0))]
        + [ANY] * len(deps),
        out_specs=pl.BlockSpec((tm, D), lambda i: (i, 0)),
        out_shape=jax.ShapeDtypeStruct((T, D), MXU_DTYPE), compiler_params=_cp("parallel"),
    )(x, g, *deps)


def _rmsnorm_bwd(name, x, g, dh, dres):
    T, D = x.shape
    tm = _tile(T, 256)

    def body(x_ref, g_ref, dh_ref, dres_ref, dx_ref, dxb_ref, dg_ref):
        xv = x_ref[...]
        r = lax.rsqrt(jnp.mean(xv * xv, axis=-1, keepdims=True) + EPS)
        xh = xv * r
        dhv = dh_ref[...]

        @pl.when(pl.program_id(0) == 0)
        def _():
            dg_ref[...] = jnp.zeros_like(dg_ref)

        dg_ref[...] += jnp.sum(dhv * xh, axis=0, keepdims=True)
        dxh = dhv * g_ref[...]
        dx = dres_ref[...] + r * (dxh - xh * jnp.mean(dxh * xh, axis=-1, keepdims=True))
        dx_ref[...] = dx
        dxb_ref[...] = dx.astype(dxb_ref.dtype)

    return pl.pallas_call(
        body, name=name, grid=(T // tm,),
        in_specs=[pl.BlockSpec((tm, D), lambda i: (i, 0)), pl.BlockSpec((1, D), lambda i: (0, 0)),
                  pl.BlockSpec((tm, D), lambda i: (i, 0)), pl.BlockSpec((tm, D), lambda i: (i, 0))],
        out_specs=[pl.BlockSpec((tm, D), lambda i: (i, 0)), pl.BlockSpec((tm, D), lambda i: (i, 0)),
                   pl.BlockSpec((1, D), lambda i: (0, 0))],
        out_shape=[jax.ShapeDtypeStruct((T, D), F32), jax.ShapeDtypeStruct((T, D), MXU_DTYPE),
                   jax.ShapeDtypeStruct((1, D), F32)],
        compiler_params=_cp("arbitrary"),
    )(x, g, dh, dres)


def _loss_head(name, x, g, tgt):
    T, D = x.shape
    tm = _tile(T, 256)

    def body(x_ref, g_ref, t_ref, loss_ref, dx_ref, dxb_ref, dg_ref):
        xv = x_ref[...]
        r = lax.rsqrt(jnp.mean(xv * xv, axis=-1, keepdims=True) + EPS)
        xh = xv * r
        gv = g_ref[...]
        diff = xh * gv - t_ref[...]

        @pl.when(pl.program_id(0) == 0)
        def _():
            dg_ref[...] = jnp.zeros_like(dg_ref)
            loss_ref[...] = jnp.zeros_like(loss_ref)

        part = 0.5 * jnp.sum(jnp.mean(diff * diff, axis=-1, keepdims=True), axis=0, keepdims=True)
        loss_ref[...] += jnp.broadcast_to(part, loss_ref.shape)
        dy = diff * (1.0 / D)
        dg_ref[...] += jnp.sum(dy * xh, axis=0, keepdims=True)
        dxh = dy * gv
        dx = r * (dxh - xh * jnp.mean(dxh * xh, axis=-1, keepdims=True))
        dx_ref[...] = dx
        dxb_ref[...] = dx.astype(dxb_ref.dtype)

    return pl.pallas_call(
        body, name=name, grid=(T // tm,),
        in_specs=[pl.BlockSpec((tm, D), lambda i: (i, 0)), pl.BlockSpec((1, D), lambda i: (0, 0)),
                  pl.BlockSpec((tm, D), lambda i: (i, 0))],
        out_specs=[pl.BlockSpec((1, 128), lambda i: (0, 0)), pl.BlockSpec((tm, D), lambda i: (i, 0)),
                   pl.BlockSpec((tm, D), lambda i: (i, 0)), pl.BlockSpec((1, D), lambda i: (0, 0))],
        out_shape=[jax.ShapeDtypeStruct((1, 128), F32), jax.ShapeDtypeStruct((T, D), F32),
                   jax.ShapeDtypeStruct((T, D), MXU_DTYPE), jax.ShapeDtypeStruct((1, D), F32)],
        compiler_params=_cp("arbitrary"),
    )(x, g, tgt)


def _adamw(name, w, m, v, parts, j=0, prev=None):
    L, R, C = w.shape
    P = parts.shape[0]
    tr = _tile(R, 256)
    ob = j * (R // tr)
    w, m, v = (a.reshape(L * R, C) for a in (w, m, v))
    c1 = 1.0 / (1.0 - ADAM_B1 ** ADAM_STEP)
    c2 = 1.0 / (1.0 - ADAM_B2 ** ADAM_STEP)
    chained = L > 1
    if chained and prev is None:
        prev = [lax.empty(w.shape, F32) for _ in range(4)]
    prev = list(prev) if chained else []

    def body(w_ref, m_ref, v_ref, p_ref, *rest):
        g_ref, d_ref, nm_ref, nv_ref, done_ref = rest[len(prev):]
        done_ref[...] = jnp.zeros_like(done_ref)
        g = p_ref[0].astype(F32)
        for s in range(1, P):
            g = g + p_ref[s].astype(F32)
        nm = ADAM_B1 * m_ref[...] + (1.0 - ADAM_B1) * g
        nv = ADAM_B2 * v_ref[...] + (1.0 - ADAM_B2) * (g * g)
        g_ref[...] = g
        nm_ref[...] = nm
        nv_ref[...] = nv
        d_ref[...] = -ADAM_LR * ((nm * c1) / (jnp.sqrt(nv * c2) + ADAM_EPS) + ADAM_WD * w_ref[...])

    blk = pl.BlockSpec((tr, C), lambda i: (i + ob, 0))
    return pl.pallas_call(
        body, name=name, grid=(R // tr,),
        in_specs=[blk, blk, blk, pl.BlockSpec((P, tr, C), lambda i: (0, i, 0))] + [ANY] * len(prev),
        out_specs=[blk, blk, blk, blk, pl.BlockSpec((8, 128), lambda i: (0, 0))],
        out_shape=[jax.ShapeDtypeStruct(w.shape, F32)] * 4 + [jax.ShapeDtypeStruct((8, 128), F32)],
        input_output_aliases={4 + i: i for i in range(len(prev))},
        compiler_params=_cp("arbitrary"),
    )(w, m, v, parts, *prev)


def _proj_in(name, h, wg, bias, deps=()):
    T, K = h.shape
    n = wg.shape[-1]
    tm = _tile(T, MM_ROWS)

    def body(a_ref, w_ref, b_ref, o_ref):
        o_ref[...] = _dot(a_ref[...], w_ref[...]) + b_ref[...]

    return pl.pallas_call(
        _with_deps(body, 3, deps), name=name, grid=(N_DEV, T // tm),
        in_specs=[pl.BlockSpec((tm, K), lambda s, i: (i, 0)),
                  pl.BlockSpec((None, K, n), lambda s, i: (s, 0, 0)),
                  pl.BlockSpec((1, n), lambda s, i: (0, s))] + [ANY] * len(deps),
        out_specs=pl.BlockSpec((tm, n), lambda s, i: (i, s)),
        out_shape=jax.ShapeDtypeStruct((T, N_DEV * n), F32), compiler_params=_cp("parallel", "parallel"),
    )(h, wg, bias, *deps)


def _ffn_in(name, h, wg_gate, wg_up):
    T, K = h.shape
    n = wg_gate.shape[-1]
    tm = _tile(T, MM_ROWS)

    def body(a_ref, wgt_ref, wup_ref, g_ref, u_ref, hid_ref):
        for r0 in range(0, tm, tm // EPILOGUE_CHUNKS):
            rs = slice(r0, r0 + tm // EPILOGUE_CHUNKS)
            a = a_ref[rs, :]
            g = _dot(a, wgt_ref[...])
            u = _dot(a, wup_ref[...])
            g_ref[rs, :] = g
            u_ref[rs, :] = u
            hid_ref[rs, :] = (_silu(g) * u).astype(hid_ref.dtype)

    wspec = pl.BlockSpec((None, K, n), lambda s, i: (s, 0, 0))
    ospec = pl.BlockSpec((None, tm, n), lambda s, i: (s, i, 0))
    return pl.pallas_call(
        body, name=name, grid=(N_DEV, T // tm),
        in_specs=[pl.BlockSpec((tm, K), lambda s, i: (i, 0)), wspec, wspec],
        out_specs=[ospec, ospec, ospec],
        out_shape=[jax.ShapeDtypeStruct((N_DEV, T, n), F32), jax.ShapeDtypeStruct((N_DEV, T, n), F32),
                   jax.ShapeDtypeStruct((N_DEV, T, n), MXU_DTYPE)],
        compiler_params=_cp("parallel", "parallel"),
    )(h, wg_gate, wg_up)


def _a_spec(a, tm, k):
    if a.ndim == 2:
        return pl.BlockSpec((tm, k), lambda i, s: (i, s))
    return pl.BlockSpec((None, tm, k), lambda i, s: (s, i, 0))


def _proj_out(name, a, wg, res, deps=(), norm_g=None):
    k, N = wg.shape[-2:]
    T = res.shape[0]
    tm = _tile(T, MM_ROWS // 2)
    extra = [] if norm_g is None else [norm_g]

    def body(a_ref, w_ref, r_ref, *rest):
        o_ref = rest[len(extra)]
        p = _dot(a_ref[...], w_ref[...])

        @pl.when(pl.program_id(1) == 0)
        def _():
            o_ref[...] = r_ref[...] + p

        @pl.when(pl.program_id(1) > 0)
        def _():
            o_ref[...] += p

        if extra:
            @pl.when(pl.program_id(1) == N_DEV - 1)
            def _():
                xv = o_ref[...]
                r = lax.rsqrt(jnp.mean(xv * xv, axis=-1, keepdims=True) + EPS)
                rest[2][...] = ((xv * r) * rest[0][...]).astype(rest[2].dtype)

    row_blk = pl.BlockSpec((tm, N), lambda i, s: (i, 0))
    out = pl.pallas_call(
        _with_deps(body, 3 + len(extra), deps), name=name, grid=(T // tm, N_DEV),
        in_specs=[_a_spec(a, tm, k), pl.BlockSpec((None, k, N), lambda i, s: (s, 0, 0)), row_blk]
        + [pl.BlockSpec((1, N), lambda i, s: (0, 0))] * len(extra) + [ANY] * len(deps),
        out_specs=[row_blk] * (1 + len(extra)),
        out_shape=[jax.ShapeDtypeStruct((T, N), F32)] + [jax.ShapeDtypeStruct((T, N), MXU_DTYPE)] * len(extra),
        compiler_params=_cp("parallel", "arbitrary"),
    )(a, wg, res, *extra, *deps)
    return out[0] if norm_g is None else out


def _mix_out(name, y, wg, res, norm_g, deps=()):
    N = wg.shape[-1]
    w2 = wg.reshape(-1, N)
    K = w2.shape[0]
    T = res.shape[0]
    tm = _tile(T, MM_ROWS // 2)

    def body(a_ref, w_ref, r_ref, g_ref, o_ref, h_ref):
        xv = r_ref[...] + _dot(a_ref[...], w_ref[...])
        o_ref[...] = xv
        r = lax.rsqrt(jnp.mean(xv * xv, axis=-1, keepdims=True) + EPS)
        h_ref[...] = ((xv * r) * g_ref[...]).astype(h_ref.dtype)

    row_blk = pl.BlockSpec((tm, N), lambda i: (i, 0))
    return pl.pallas_call(
        _with_deps(body, 4, deps), name=name, grid=(T // tm,),
        in_specs=[pl.BlockSpec((tm, K), lambda i: (i, 0)),
                  pl.BlockSpec((K, N), lambda i: (0, 0), pipeline_mode=pl.Buffered(1)),
                  row_blk, pl.BlockSpec((1, N), lambda i: (0, 0))] + [ANY] * len(deps),
        out_specs=[row_blk, row_blk],
        out_shape=[jax.ShapeDtypeStruct((T, N), F32), jax.ShapeDtypeStruct((T, N), MXU_DTYPE)],
        compiler_params=_cp("parallel"),
    )(y, w2, res, norm_g, *deps)


def _bwd_in(name, das, wgs, deps=()):
    K, n = wgs[0].shape[-2:]
    T = das[0].shape[-2]
    tm = _tile(T, MM_ROWS)
    npair = len(das)

    def body(*refs):
        o_ref = refs[-1]
        p = _dot_nt(refs[0][...], refs[npair][...])
        for q in range(1, npair):
            p = p + _dot_nt(refs[q][...], refs[npair + q][...])

        @pl.when(pl.program_id(1) == 0)
        def _():
            o_ref[...] = p

        @pl.when(pl.program_id(1) > 0)
        def _():
            o_ref[...] += p

    return pl.pallas_call(
        _with_deps(body, 2 * npair, deps), name=name, grid=(T // tm, N_DEV),
        in_specs=[_a_spec(a, tm, n) for a in das]
        + [pl.BlockSpec((None, K, n), lambda i, s: (s, 0, 0)) for _ in wgs] + [ANY] * len(deps),
        out_specs=pl.BlockSpec((tm, K), lambda i, s: (i, 0)),
        out_shape=jax.ShapeDtypeStruct((T, K), F32), compiler_params=_cp("parallel", "arbitrary"),
    )(*das, *wgs, *deps)


def _bwd_out(name, dx, wg, deps=()):
    k, N = wg.shape[-2:]
    T = dx.shape[0]
    tm = _tile(T, MM_ROWS)

    def body(a_ref, w_ref, o_ref):
        o_ref[...] = _dot_nt(a_ref[...], w_ref[...])

    return pl.pallas_call(
        _with_deps(body, 2, deps), name=name, grid=(N_DEV, T // tm),
        in_specs=[pl.BlockSpec((tm, N), lambda s, i: (i, 0)),
                  pl.BlockSpec((None, k, N), lambda s, i: (s, 0, 0))] + [ANY] * len(deps),
        out_specs=pl.BlockSpec((tm, k), lambda s, i: (i, s)),
        out_shape=jax.ShapeDtypeStruct((T, N_DEV * k), F32), compiler_params=_cp("parallel", "parallel"),
    )(dx, wg, *deps)


def _ffn_bwd_hidden(name, dx, wg_down, gate, up, deps=()):
    n, N = wg_down.shape[-2:]
    T = dx.shape[0]
    tm = _tile(T, MM_ROWS)

    def body(a_ref, w_ref, g_ref, u_ref, dg_ref, du_ref):
        for r0 in range(0, tm, tm // EPILOGUE_CHUNKS):
            rs = slice(r0, r0 + tm // EPILOGUE_CHUNKS)
            dh = _dot_nt(a_ref[rs, :], w_ref[...])
            g = g_ref[rs, :]
            dg_ref[rs, :] = (dh * u_ref[rs, :] * _dsilu(g)).astype(dg_ref.dtype)
            du_ref[rs, :] = (dh * _silu(g)).astype(du_ref.dtype)

    sm = pl.BlockSpec((None, tm, n), lambda s, i: (s, i, 0))
    return pl.pallas_call(
        _with_deps(body, 4, deps), name=name, grid=(N_DEV, T // tm),
        in_specs=[pl.BlockSpec((tm, N), lambda s, i: (i, 0)),
                  pl.BlockSpec((None, n, N), lambda s, i: (s, 0, 0)), sm, sm] + [ANY] * len(deps),
        out_specs=[sm, sm],
        out_shape=[jax.ShapeDtypeStruct((N_DEV, T, n), MXU_DTYPE)] * 2,
        compiler_params=_cp("parallel", "parallel"),
    )(dx, wg_down, gate, up, *deps)


def _wgrad(name, a, c, rows, cols, deps=()):
    T = a.shape[-2]
    tk = _tile(T, WGRAD_ROWS)
    nk = T // tk

    def spec(z, w):
        if z.ndim == 3:
            return pl.BlockSpec((None, tk, w), lambda s, k: (s, k, 0))
        if z.shape[1] == w:
            return pl.BlockSpec((tk, w), lambda s, k: (k, 0))
        return pl.BlockSpec((tk, w), lambda s, k: (k, s))

    def body(a_ref, c_ref, o_ref, acc_ref):
        k = pl.program_id(1)
        p = _dot_tn(a_ref[...], c_ref[...])

        @pl.when(k == 0)
        def _():
            acc_ref[...] = p

        @pl.when(k > 0)
        def _():
            acc_ref[...] += p

        @pl.when(k == nk - 1)
        def _():
            o_ref[...] = acc_ref[...].astype(o_ref.dtype)

    return pl.pallas_call(
        _with_deps(body, 2, deps), name=name, grid=(N_DEV, nk),
        in_specs=[spec(a, rows), spec(c, cols)] + [ANY] * len(deps),
        out_specs=pl.BlockSpec((None, rows, cols), lambda s, k: (s, 0, 0)),
        out_shape=jax.ShapeDtypeStruct((N_DEV, rows, cols), WIRE_DTYPE),
        scratch_shapes=[pltpu.VMEM((rows, cols), F32)],
        compiler_params=_cp("parallel", "arbitrary"),
    )(a, c, *deps)


def _shift_down(cur, prev8, sh):
    n = cur.shape[0]
    rolled = pltpu.roll(cur, sh, 0)
    top = jnp.where(_rows(8) < sh, pltpu.roll(prev8, sh, 0), rolled[0:8])
    return jnp.concatenate([top, rolled[8:n]], axis=0)


def _shift_up(cur, next8, sh):
    n = cur.shape[0]
    rolled = pltpu.roll(cur, n - sh, 0)
    bot = jnp.where(_rows(8) >= 8 - sh, pltpu.roll(next8, 8 - sh, 0), rolled[n - 8:n])
    return jnp.concatenate([rolled[0:n - 8], bot], axis=0)


def _lru_gate_terms(r, lam):
    sp = _softplus(-lam)
    la = -LRU_C * r * sp
    a = jnp.exp(la)
    m2 = _one_minus_exp(2.0 * la)
    return sp, la, a, m2


def _lru_fwd(name, proj, conv_w, conv_b, wa, ba, wx, bx, lam):
    T = proj.shape[0]
    H, hd, _ = wa.shape
    W = H * hd
    K = conv_w.shape[0]
    tb = _tile(T, 256)

    def body(xin_ref, gate_ref, cw_ref, cb_ref, wa_ref, ba_ref, wx_ref, bx_ref, lam_ref,
             ya_ref, xc_ref, r_ref, i_ref, hs_ref, tail_ref, hprev_ref):
        blk = pl.program_id(0)

        @pl.when(blk == 0)
        def _():
            tail_ref[...] = jnp.zeros_like(tail_ref)
            hprev_ref[...] = jnp.zeros_like(hprev_ref)

        xin = xin_ref[...]
        prev8 = tail_ref[...]
        xc = cw_ref[K - 1:K, :] * xin
        for sh in range(1, K):
            xc = xc + cw_ref[K - 1 - sh:K - sh, :] * _shift_down(xin, prev8, sh)
        xc = xc + cb_ref[...]
        tail_ref[...] = xin[tb - 8:tb]
        xc_ref[...] = xc
        for h in range(H):
            cs = slice(h * hd, (h + 1) * hd)
            xh = xc[:, cs]
            r_ref[:, cs] = _sigmoid(_dot(xh, wa_ref[h]) + ba_ref[:, cs])
            i_ref[:, cs] = _sigmoid(_dot(xh, wx_ref[h]) + bx_ref[:, cs])
        r = r_ref[...]
        _, _, a, m2 = _lru_gate_terms(r, lam_ref[...])
        row = _rows(tb)
        mult = jnp.where((row == 0) & (blk == 0), 1.0, jnp.sqrt(jnp.maximum(m2, 0.0)))
        u = mult * i_ref[...] * xc
        d = 1
        while d < tb:
            keep = row >= d
            u = a * jnp.where(keep, pltpu.roll(u, d, 0), 0.0) + u
            a = a * jnp.where(keep, pltpu.roll(a, d, 0), 1.0)
            d *= 2
        hs = u + a * hprev_ref[...]
        hprev_ref[...] = hs[tb - 1:tb]
        hs_ref[...] = hs
        ya_ref[...] = (hs * _gelu(gate_ref[...])).astype(ya_ref.dtype)

    full = lambda shape: pl.BlockSpec(shape, lambda i: tuple(0 for _ in shape))
    blk = pl.BlockSpec((tb, W), lambda i: (i, 0))
    return pl.pallas_call(
        body, name=name, grid=(T // tb,),
        in_specs=[pl.BlockSpec((tb, W), lambda i: (i, 0)), pl.BlockSpec((tb, W), lambda i: (i, 1)),
                  full((K, W)), full((1, W)), full((H, hd, hd)), full((1, W)), full((H, hd, hd)),
                  full((1, W)), full((1, W))],
        out_specs=[blk] * 5,
        out_shape=[jax.ShapeDtypeStruct((T, 2 * W), MXU_DTYPE)] + [jax.ShapeDtypeStruct((T, W), F32)] * 4,
        scratch_shapes=[pltpu.VMEM((8, W), F32), pltpu.VMEM((1, W), F32)],
        compiler_params=_cp("arbitrary"),
    )(proj, proj, conv_w, conv_b, wa, ba, wx, bx, lam)


def _lru_bwd(name, proj, dy, xc, r, ig, hs, conv_w, wa, wx, lam, dpbuf):
    T = proj.shape[0]
    H, hd, _ = wa.shape
    W = H * hd
    K = conv_w.shape[0]
    tb = _tile(T, 256)
    nb = T // tb
    t8 = tb // 8

    def body(xin_ref, xprev_ref, gate_ref, dy_ref, xc_ref, r_ref, i_ref, hs_ref, hsprev_ref,
             cw_ref, wa_ref, wx_ref, lam_ref, dpbuf_ref,
             dp_ref, db_ref, dcw_ref, dcb_ref, dwa_ref, dba_ref, dwx_ref, dbx_ref, dlam_ref,
             cdh_ref, ca_ref, cdxc_ref, dxc_ref):
        del dpbuf_ref
        step = pl.program_id(0)
        blk = nb - 1 - step

        @pl.when(step == 0)
        def _():
            for ref in (db_ref, dcw_ref, dcb_ref, dwa_ref, dba_ref, dwx_ref, dbx_ref, dlam_ref,
                        cdh_ref, ca_ref, cdxc_ref):
                ref[...] = jnp.zeros_like(ref)

        row = _rows(tb)
        first = blk == 0
        gate = gate_ref[...]
        dy_a = dy_ref[...]
        hsv = hs_ref[...]
        _put(dp_ref, db_ref, W, 2 * W, dy_a * hsv * _dgelu(gate))
        d_hs = dy_a * _gelu(gate)
        lam = lam_ref[...]
        rv = r_ref[...]
        sp, la, a, m2 = _lru_gate_terms(rv, lam)
        an = jnp.where(row == tb - 1, ca_ref[...], pltpu.roll(a, tb - 1, 0))
        u = d_hs
        d = 1
        while d < tb:
            keep = row < tb - d
            u = an * jnp.where(keep, pltpu.roll(u, tb - d, 0), 0.0) + u
            an = an * jnp.where(keep, pltpu.roll(an, tb - d, 0), 1.0)
            d *= 2
        dh = u + an * cdh_ref[...]
        cdh_ref[...] = dh[0:1]
        ca_ref[...] = a[0:1]
        hlast = jnp.where(first, 0.0, hsprev_ref[7:8, :])
        hprev = jnp.where(row == 0, hlast, pltpu.roll(hsv, 1, 0))
        da = dh * hprev
        xcv = xc_ref[...]
        iv = i_ref[...]
        t0 = (row == 0) & first
        mult = jnp.sqrt(jnp.maximum(m2, 0.0))
        mult_eff = jnp.where(t0, 1.0, mult)
        d_mult = dh * iv * xcv
        d_i = dh * mult_eff * xcv
        dxc = dh * mult_eff * iv
        e2 = 1.0 - m2
        d_la = da * a + jnp.where(t0 | (m2 <= 0.0), 0.0, -d_mult * e2 / jnp.where(m2 > 0.0, mult, 1.0))
        d_r = d_la * (-LRU_C * sp)
        dlam_ref[...] += jnp.sum(d_la * (-LRU_C * rv), axis=0, keepdims=True) * (-_sigmoid(-lam))
        d_zr = d_r * rv * (1.0 - rv)
        d_zi = d_i * iv * (1.0 - iv)
        dba_ref[...] += jnp.sum(d_zr, axis=0, keepdims=True)
        dbx_ref[...] += jnp.sum(d_zi, axis=0, keepdims=True)
        for h in range(H):
            cs = slice(h * hd, (h + 1) * hd)
            xh = xcv[:, cs]
            zr, zi = d_zr[:, cs], d_zi[:, cs]
            dwa_ref[h] += _dot_tn(xh, zr)
            dwx_ref[h] += _dot_tn(xh, zi)
            dxc_ref[:, cs] = dxc[:, cs] + _dot_nt(zr, wa_ref[h]) + _dot_nt(zi, wx_ref[h])
        dxc = dxc_ref[...]
        dcb_ref[...] += jnp.sum(dxc, axis=0, keepdims=True)
        xin = xin_ref[...]
        prev8 = jnp.where(first, 0.0, xprev_ref[...])
        next8 = cdxc_ref[...]
        dxin = cw_ref[K - 1:K, :] * dxc
        dcw_ref[K - 1:K, :] += jnp.sum(dxc * xin, axis=0, keepdims=True)
        for sh in range(1, K):
            dxin = dxin + cw_ref[K - 1 - sh:K - sh, :] * _shift_up(dxc, next8, sh)
            dcw_ref[K - 1 - sh:K - sh, :] += jnp.sum(dxc * _shift_down(xin, prev8, sh), axis=0, keepdims=True)
        cdxc_ref[...] = dxc[0:8]
        _put(dp_ref, db_ref, 0, W, dxin)

    full = lambda shape: pl.BlockSpec(shape, lambda i: tuple(0 for _ in shape))
    cur = lambda col: pl.BlockSpec((tb, W), lambda i: (nb - 1 - i, col))
    prev = pl.BlockSpec((8, W), lambda i: (jnp.maximum((nb - 1 - i) * t8 - 1, 0), 0))
    return pl.pallas_call(
        body, name=name, grid=(nb,),
        in_specs=[cur(0), prev, cur(1), cur(0), cur(0), cur(0), cur(0), cur(0), prev,
                  full((K, W)), full((H, hd, hd)), full((H, hd, hd)), full((1, W)), ANY],
        out_specs=[pl.BlockSpec((tb, 2 * W), lambda i: (nb - 1 - i, 0)), full((1, 2 * W)), full((K, W)),
                   full((1, W)),
                   full((H, hd, hd)), full((1, W)), full((H, hd, hd)), full((1, W)), full((1, W))],
        out_shape=[jax.ShapeDtypeStruct(dpbuf.shape, dpbuf.dtype), jax.ShapeDtypeStruct((1, 2 * W), F32),
                   jax.ShapeDtypeStruct((K, W), F32),
                   jax.ShapeDtypeStruct((1, W), F32), jax.ShapeDtypeStruct((H, hd, hd), F32),
                   jax.ShapeDtypeStruct((1, W), F32), jax.ShapeDtypeStruct((H, hd, hd), F32),
                   jax.ShapeDtypeStruct((1, W), F32), jax.ShapeDtypeStruct((1, W), F32)],
        scratch_shapes=[pltpu.VMEM((1, W), F32), pltpu.VMEM((1, W), F32), pltpu.VMEM((8, W), F32),
                        pltpu.VMEM((tb, W), F32)],
        input_output_aliases={13: 0}, compiler_params=_cp("arbitrary"),
    )(proj, proj, proj, dy, xc, r, ig, hs, hs, conv_w, wa, wx, lam, dpbuf)


def _chunk_cumsum(g, c):
    n = g.shape[0]
    rc = _rows(n) & (c - 1)
    d = 1
    while d < c:
        g = g + jnp.where(rc >= d, pltpu.roll(g, d, 0), 0.0)
        d *= 2
    return g


def _chunk_rcumsum(g, c):
    n = g.shape[0]
    rc = _rows(n) & (c - 1)
    d = 1
    while d < c:
        g = g + jnp.where(rc < c - d, pltpu.roll(g, n - d, 0), 0.0)
        d *= 2
    return g


def _hgrn_pointwise(qr, fr, lb):
    qf = _silu(qr)
    sig = _sigmoid(fr)
    fg = lb + (1.0 - lb) * sig
    gl = jnp.log(jnp.maximum(fg, F_FLOOR))
    kk = (1.0 - lb) * (1.0 - sig)
    return qf, sig, fg, gl, kk


def _hgrn_fwd(name, proj, lb, norm_g, ybuf):
    T = proj.shape[0]
    W = lb.shape[1]
    H = HGRN_HEADS
    dk = W // H
    c = HGRN_SUB
    R = _tile(T, 128)
    nck = R // c

    def body(q_ref, f_ref, v_ref, g_ref, lb_ref, ng_ref, ybuf_ref, yb_ref, o_ref, sall_ref,
             st_ref, qe_ref, ke_ref, acc_ref):
        del ybuf_ref

        @pl.when(pl.program_id(0) == 0)
        def _():
            st_ref[...] = jnp.zeros_like(st_ref)

        qf, _, _, gl, kk = _hgrn_pointwise(q_ref[...], f_ref[...], lb_ref[...])
        b = _chunk_cumsum(gl * LOG2E, c)
        rc = _rows(R) & (c - 1)
        for h in range(H):
            cs = slice(h * dk, (h + 1) * dk)
            qh, kh, bh, vh = qf[:, cs], kk[:, cs], b[:, cs], v_ref[:, cs]
            acc = jnp.sum(qh * kh, axis=1, keepdims=True) * vh
            for d in range(1, c):
                e = jnp.where(rc >= d, jnp.exp2(bh - pltpu.roll(bh, d, 0)), 0.0)
                s = jnp.sum(qh * pltpu.roll(kh, d, 0) * e, axis=1, keepdims=True)
                acc = acc + s * pltpu.roll(vh, d, 0)
            acc_ref[:, cs] = acc
        qe_ref[...] = qf * jnp.exp2(b)
        for ci in range(nck):
            rs = slice(ci * c, (ci + 1) * c)
            bl = b[ci * c + c - 1:ci * c + c, :]
            ke_ref[rs, :] = kk[rs, :] * jnp.exp2(bl - b[rs, :])
            ebl = jnp.exp2(bl)
            for h in range(H):
                cs = slice(h * dk, (h + 1) * dk)
                st = st_ref[h]
                sall_ref[ci, h] = st
                o_ref[rs, cs] = acc_ref[rs, cs] + _dot_nt(qe_ref[rs, cs], st)
                st_ref[h] = st * ebl[:, cs] + _dot_tn(v_ref[rs, cs], ke_ref[rs, cs])
        ng = ng_ref[...]
        gg = g_ref[...]
        for h in range(H):
            cs = slice(h * dk, (h + 1) * dk)
            oh = o_ref[:, cs]
            rr = lax.rsqrt(jnp.mean(oh * oh, axis=1, keepdims=True) + EPS)
            yb_ref[:, cs] = ((oh * rr) * ng[:, cs] * _silu(gg[:, cs])).astype(yb_ref.dtype)

    full = lambda shape: pl.BlockSpec(shape, lambda i: tuple(0 for _ in shape))
    col = lambda k: pl.BlockSpec((R, W), lambda i: (i, k))
    blk = pl.BlockSpec((R, W), lambda i: (i, 0))
    return pl.pallas_call(
        body, name=name, grid=(T // R,),
        in_specs=[col(2), col(3), col(4), col(5), full((1, W)), full((1, W)), ANY],
        out_specs=[col(1), blk, pl.BlockSpec((nck, H, dk, dk), lambda i: (i, 0, 0, 0))],
        out_shape=[jax.ShapeDtypeStruct((T, 2 * W), MXU_DTYPE), jax.ShapeDtypeStruct((T, W), F32),
                   jax.ShapeDtypeStruct((T // c, H, dk, dk), F32)],
        scratch_shapes=[pltpu.VMEM((H, dk, dk), F32), pltpu.VMEM((R, W), F32), pltpu.VMEM((R, W), F32),
                        pltpu.VMEM((R, W), F32)],
        input_output_aliases={6: 0}, compiler_params=_cp("arbitrary"),
    )(proj, proj, proj, proj, lb, norm_g, ybuf)


def _hgrn_bwd(name, proj, dy, o, sall, lb, norm_g):
    T = proj.shape[0]
    W = lb.shape[1]
    H = HGRN_HEADS
    dk = W // H
    c = HGRN_SUB
    R = _tile(T, 128)
    nck = R // c
    nb = T // R

    def body(q_ref, f_ref, v_ref, g_ref, dy_ref, o_ref, sall_ref, lb_ref, ng_ref,
             dp_ref, db_ref, dlb_ref, dng_ref,
             dst_ref, do_ref, dq_ref, dk_ref, dv_ref, ex_ref, qe_ref, ke_ref):
        @pl.when(pl.program_id(0) == 0)
        def _():
            dst_ref[...] = jnp.zeros_like(dst_ref)
            db_ref[...] = jnp.zeros_like(db_ref)
            dlb_ref[...] = jnp.zeros_like(dlb_ref)
            dng_ref[...] = jnp.zeros_like(dng_ref)

        lbv = lb_ref[...]
        qr = q_ref[...]
        qf, sig, fg, gl, kk = _hgrn_pointwise(qr, f_ref[...], lbv)
        b = _chunk_cumsum(gl * LOG2E, c)
        rc = _rows(R) & (c - 1)
        ng = ng_ref[...]
        gg = g_ref[...]
        dyv = dy_ref[...]
        sg = _silu(gg)
        for h in range(H):
            cs = slice(h * dk, (h + 1) * dk)
            oh = o_ref[:, cs]
            rr = lax.rsqrt(jnp.mean(oh * oh, axis=1, keepdims=True) + EPS)
            ohat = oh * rr
            dyh = dyv[:, cs]
            _put(dp_ref, db_ref, 5 * W + h * dk, 5 * W + (h + 1) * dk,
                 dyh * ohat * ng[:, cs] * _dsilu(gg[:, cs]))
            t = dyh * sg[:, cs]
            dng_ref[:, cs] += jnp.sum(t * ohat, axis=0, keepdims=True)
            dohat = t * ng[:, cs]
            do_ref[:, cs] = rr * (dohat - ohat * jnp.mean(dohat * ohat, axis=1, keepdims=True))
        for h in range(H):
            cs = slice(h * dk, (h + 1) * dk)
            qh, kh, bh, vh, doh = qf[:, cs], kk[:, cs], b[:, cs], v_ref[:, cs], do_ref[:, cs]
            da0 = jnp.sum(doh * vh, axis=1, keepdims=True)
            a0 = jnp.sum(qh * kh, axis=1, keepdims=True)
            dq = da0 * kh
            dkk = da0 * qh
            dv = a0 * doh
            for d in range(1, c):
                e = jnp.where(rc >= d, jnp.exp2(bh - pltpu.roll(bh, d, 0)), 0.0)
                kr = pltpu.roll(kh, d, 0)
                da = jnp.sum(doh * pltpu.roll(vh, d, 0), axis=1, keepdims=True)
                aa = jnp.sum(qh * kr * e, axis=1, keepdims=True)
                dq = dq + da * kr * e
                dkk = dkk + pltpu.roll(da * qh * e, R - d, 0)
                dv = dv + pltpu.roll(aa * doh, R - d, 0)
            dq_ref[:, cs] = dq
            dk_ref[:, cs] = dkk
            dv_ref[:, cs] = dv
        eb = jnp.exp2(b)
        qe_ref[...] = qf * eb
        ex_ref[...] = jnp.zeros_like(ex_ref)
        for ci in reversed(range(nck)):
            rs = slice(ci * c, (ci + 1) * c)
            bl = b[ci * c + c - 1:ci * c + c, :]
            ebl_rows = jnp.exp2(bl - b[rs, :])
            ke_ref[rs, :] = kk[rs, :] * ebl_rows
            ebl = jnp.exp2(bl)
            for h in range(H):
                cs = slice(h * dk, (h + 1) * dk)
                st0 = sall_ref[ci, h]
                dst1 = dst_ref[h]
                doc = do_ref[rs, cs]
                vc = v_ref[rs, cs]
                dq_ref[rs, cs] += _dot(doc, st0) * eb[rs, cs]
                dv_ref[rs, cs] += _dot_nt(ke_ref[rs, cs], dst1)
                dki = _dot(vc, dst1) * ebl_rows[:, cs]
                dk_ref[rs, cs] += dki
                ex_ref[ci * c + c - 1:ci * c + c, cs] = (
                    jnp.sum(dki * kk[rs, cs], axis=0, keepdims=True)
                    + ebl[:, cs] * jnp.sum(st0 * dst1, axis=0, keepdims=True))
                dst_ref[h] = dst1 * ebl[:, cs] + _dot_tn(doc, qe_ref[rs, cs])
        dq = dq_ref[...]
        dkk = dk_ref[...]
        db = qf * dq - kk * dkk + ex_ref[...]
        dgl = _chunk_rcumsum(db, c)
        dfg = jnp.where(fg > F_FLOOR, dgl / jnp.maximum(fg, F_FLOOR), 0.0)
        dsig = (dfg - dkk) * (1.0 - lbv)
        dlb_ref[...] += jnp.sum((dfg - dkk) * (1.0 - sig), axis=0, keepdims=True)
        dp_ref[:, 0:2 * W] = jnp.zeros((R, 2 * W), dp_ref.dtype)
        _put(dp_ref, db_ref, 2 * W, 3 * W, dq * _dsilu(qr))
        _put(dp_ref, db_ref, 3 * W, 4 * W, dsig * sig * (1.0 - sig))
        _put(dp_ref, db_ref, 4 * W, 5 * W, dv_ref[...])

    full = lambda shape: pl.BlockSpec(shape, lambda i: tuple(0 for _ in shape))
    col = lambda k: pl.BlockSpec((R, W), lambda i: (nb - 1 - i, k))
    scr = pltpu.VMEM((R, W), F32)
    return pl.pallas_call(
        body, name=name, grid=(nb,),
        in_specs=[col(2), col(3), col(4), col(5), col(1), col(0),
                  pl.BlockSpec((nck, H, dk, dk), lambda i: (nb - 1 - i, 0, 0, 0)), full((1, W)), full((1, W))],
        out_specs=[pl.BlockSpec((R, 6 * W), lambda i: (nb - 1 - i, 0)), full((1, 6 * W)), full((1, W)),
                   full((1, W))],
        out_shape=[jax.ShapeDtypeStruct((T, 6 * W), MXU_DTYPE), jax.ShapeDtypeStruct((1, 6 * W), F32),
                   jax.ShapeDtypeStruct((1, W), F32), jax.ShapeDtypeStruct((1, W), F32)],
        scratch_shapes=[pltpu.VMEM((H, dk, dk), F32), scr, scr, scr, scr, scr, scr, scr],
        compiler_params=_cp("arbitrary"),
    )(proj, proj, proj, proj, dy, o, sall, lb, norm_g)


ODD_HALO = 32


def _row_shifts(ext, up):
    n = ext.shape[0]
    return [ext] + [pltpu.roll(ext, n - b if up else b, 0) for b in range(1, 8)]


def _past(copies, sh, tb):
    a, b = divmod(sh, 8)
    return copies[b][ODD_HALO - 8 * a:ODD_HALO - 8 * a + tb]


def _future(copies, sh, tb):
    a, b = divmod(sh, 8)
    return copies[b][8 * a:8 * a + tb]


def _odd_fwd(name, proj, sc_w, cf_w, cf_b, ln_g, ln_b):
    T = proj.shape[0]
    W = sc_w.shape[1]
    K3, K31 = sc_w.shape[0], cf_w.shape[0]
    tb = _tile(T, 256)
    hb = tb // ODD_HALO
    n = tb + ODD_HALO

    def body(cur_ref, prev_ref, w3_ref, w31_ref, cb_ref, lg_ref, lbeta_ref, y_ref, d_ref):
        keep = (pl.program_id(0) > 0).astype(F32)
        sb = cur_ref[:, 0:W]
        p = cur_ref[:, W:2 * W] * cur_ref[:, 2 * W:3 * W]
        glu = cur_ref[:, 3 * W:4 * W] * _sigmoid(cur_ref[:, 4 * W:5 * W])
        p_prev = prev_ref[:, W:2 * W] * prev_ref[:, 2 * W:3 * W] * keep
        glu_prev = prev_ref[:, 3 * W:4 * W] * _sigmoid(prev_ref[:, 4 * W:5 * W]) * keep
        ext = jnp.concatenate([p_prev, p], axis=0)
        cp = w3_ref[K3 - 1:K3, :] * p
        for sh in range(1, K3):
            cp = cp + w3_ref[K3 - 1 - sh:K3 - sh, :] * pltpu.roll(ext, sh, 0)[ODD_HALO:n]
        y_ref[:, 0:W] = (sb * cp).astype(y_ref.dtype)
        glu_past = _row_shifts(jnp.concatenate([glu_prev, glu], axis=0), up=False)
        d = cb_ref[...] + w31_ref[K31 - 1:K31, :] * glu
        for sh in range(1, K31):
            d = d + w31_ref[K31 - 1 - sh:K31 - sh, :] * _past(glu_past, sh, tb)
        d_ref[...] = d
        mu = jnp.mean(d, axis=1, keepdims=True)
        xc = d - mu
        rstd = lax.rsqrt(jnp.mean(xc * xc, axis=1, keepdims=True) + EPS)
        ln = (xc * rstd) * lg_ref[...] + lbeta_ref[...]
        y_ref[:, W:2 * W] = _silu(ln).astype(y_ref.dtype)

    full = lambda shape: pl.BlockSpec(shape, lambda i: tuple(0 for _ in shape))
    return pl.pallas_call(
        body, name=name, grid=(T // tb,),
        in_specs=[pl.BlockSpec((tb, 5 * W), lambda i: (i, 0)),
                  pl.BlockSpec((ODD_HALO, 5 * W), lambda i: (jnp.maximum(i * hb - 1, 0), 0)),
                  full((K3, W)), full((K31, W)), full((1, W)), full((1, W)), full((1, W))],
        out_specs=[pl.BlockSpec((tb, 2 * W), lambda i: (i, 0)), pl.BlockSpec((tb, W), lambda i: (i, 0))],
        out_shape=[jax.ShapeDtypeStruct((T, 2 * W), MXU_DTYPE), jax.ShapeDtypeStruct((T, W), F32)],
        compiler_params=_cp("parallel"),
    )(proj, proj, sc_w, cf_w, cf_b, ln_g, ln_b)


def _odd_bwd(name, proj, dy, dsave, sc_w, cf_w, ln_g, ln_b):
    T = proj.shape[0]
    W = sc_w.shape[1]
    K3, K31 = sc_w.shape[0], cf_w.shape[0]
    tb = _tile(T, 128)
    nb = T // tb
    hb = tb // ODD_HALO
    nh = T // ODD_HALO
    n = tb + ODD_HALO

    def body(cur_ref, prev_ref, next_ref, dy_ref, dyn_ref, d_ref, dn_ref,
             w3_ref, w31_ref, lg_ref, lbeta_ref,
             dp_ref, db_ref, dw3_ref, dw31_ref, dcb_ref, dlg_ref, dlb_ref):
        i = pl.program_id(0)

        @pl.when(i == 0)
        def _():
            for ref in (db_ref, dw3_ref, dw31_ref, dcb_ref, dlg_ref, dlb_ref):
                ref[...] = jnp.zeros_like(ref)

        keep_prev = (i > 0).astype(F32)
        keep_next = (i < nb - 1).astype(F32)
        sb = cur_ref[:, 0:W]
        scv = cur_ref[:, W:2 * W]
        svv = cur_ref[:, 2 * W:3 * W]
        cu = cur_ref[:, 3 * W:4 * W]
        sg = _sigmoid(cur_ref[:, 4 * W:5 * W])
        p = scv * svv
        glu = cu * sg
        p_prev = prev_ref[:, W:2 * W] * prev_ref[:, 2 * W:3 * W] * keep_prev
        glu_prev = prev_ref[:, 3 * W:4 * W] * _sigmoid(prev_ref[:, 4 * W:5 * W]) * keep_prev
        dext = jnp.concatenate([d_ref[...], dn_ref[...]], axis=0)
        dyd = jnp.concatenate([dy_ref[:, W:2 * W], dyn_ref[:, W:2 * W] * keep_next], axis=0)
        mu = jnp.mean(dext, axis=1, keepdims=True)
        xc = dext - mu
        rstd = lax.rsqrt(jnp.mean(xc * xc, axis=1, keepdims=True) + EPS)
        xh = xc * rstd
        lg = lg_ref[...]
        dln = dyd * _dsilu(xh * lg + lbeta_ref[...])
        dxh = dln * lg
        dd = rstd * (dxh - jnp.mean(dxh, axis=1, keepdims=True)
                     - xh * jnp.mean(dxh * xh, axis=1, keepdims=True))
        dlg_ref[...] += jnp.sum((dln * xh)[0:tb], axis=0, keepdims=True)
        dlb_ref[...] += jnp.sum(dln[0:tb], axis=0, keepdims=True)
        ddc = dd[0:tb]
        dcb_ref[...] += jnp.sum(ddc, axis=0, keepdims=True)
        dglu = w31_ref[K31 - 1:K31, :] * ddc
        glu_past = _row_shifts(jnp.concatenate([glu_prev, glu], axis=0), up=False)
        dd_future = _row_shifts(dd, up=True)
        dw31_ref[K31 - 1:K31, :] += jnp.sum(ddc * glu, axis=0, keepdims=True)
        for sh in range(1, K31):
            dglu = dglu + w31_ref[K31 - 1 - sh:K31 - sh, :] * _future(dd_future, sh, tb)
            dw31_ref[K31 - 1 - sh:K31 - sh, :] += jnp.sum(ddc * _past(glu_past, sh, tb), axis=0, keepdims=True)
        _put(dp_ref, db_ref, 3 * W, 4 * W, dglu * sg)
        _put(dp_ref, db_ref, 4 * W, 5 * W, dglu * cu * sg * (1.0 - sg))
        dyc = dy_ref[:, 0:W]
        dcp = jnp.concatenate([dyc * sb, dyn_ref[:, 0:W] * next_ref[:, 0:W] * keep_next], axis=0)
        dcpc = dcp[0:tb]
        ext = jnp.concatenate([p_prev, p], axis=0)
        cp = w3_ref[K3 - 1:K3, :] * p
        dpp = w3_ref[K3 - 1:K3, :] * dcpc
        dw3_ref[K3 - 1:K3, :] += jnp.sum(dcpc * p, axis=0, keepdims=True)
        up = dcp
        for sh in range(1, K3):
            up = pltpu.roll(up, n - 1, 0)
            ext = pltpu.roll(ext, 1, 0)
            shifted = ext[ODD_HALO:n]
            cp = cp + w3_ref[K3 - 1 - sh:K3 - sh, :] * shifted
            dpp = dpp + w3_ref[K3 - 1 - sh:K3 - sh, :] * up[0:tb]
            dw3_ref[K3 - 1 - sh:K3 - sh, :] += jnp.sum(dcpc * shifted, axis=0, keepdims=True)
        _put(dp_ref, db_ref, 0, W, dyc * cp)
        _put(dp_ref, db_ref, W, 2 * W, dpp * svv)
        _put(dp_ref, db_ref, 2 * W, 3 * W, dpp * scv)

    full = lambda shape: pl.BlockSpec(shape, lambda i: tuple(0 for _ in shape))
    prev_map = lambda i: (jnp.maximum(i * hb - 1, 0), 0)
    next_map = lambda i: (jnp.minimum((i + 1) * hb, nh - 1), 0)
    return pl.pallas_call(
        body, name=name, grid=(nb,),
        in_specs=[pl.BlockSpec((tb, 5 * W), lambda i: (i, 0)),
                  pl.BlockSpec((ODD_HALO, 5 * W), prev_map), pl.BlockSpec((ODD_HALO, 5 * W), next_map),
                  pl.BlockSpec((tb, 2 * W), lambda i: (i, 0)), pl.BlockSpec((ODD_HALO, 2 * W), next_map),
                  pl.BlockSpec((tb, W), lambda i: (i, 0)), pl.BlockSpec((ODD_HALO, W), next_map),
                  full((K3, W)), full((K31, W)), full((1, W)), full((1, W))],
        out_specs=[pl.BlockSpec((tb, 5 * W), lambda i: (i, 0)), full((1, 5 * W)), full((K3, W)), full((K31, W)),
                   full((1, W)), full((1, W)), full((1, W))],
        out_shape=[jax.ShapeDtypeStruct((T, 5 * W), MXU_DTYPE), jax.ShapeDtypeStruct((1, 5 * W), F32),
                   jax.ShapeDtypeStruct((K3, W), F32),
                   jax.ShapeDtypeStruct((K31, W), F32)] + [jax.ShapeDtypeStruct((1, W), F32)] * 3,
        compiler_params=_cp("arbitrary"),
    )(proj, proj, proj, dy, dy, dsave, dsave, sc_w, cf_w, ln_g, ln_b)


PACK_WIDTH = 1024


def _lower_bounds(logits):
    sm = jax.nn.softmax(logits.astype(F32), axis=0)
    return jnp.cumsum(sm, axis=0) - sm[0]


def _pack_rows(arrays):
    flat = jnp.concatenate([a.reshape(-1) for a in arrays])
    pad = (-flat.shape[0]) % (8 * PACK_WIDTH)
    return jnp.pad(flat, (0, pad)).reshape(-1, PACK_WIDTH)


def _unpack_rows(packed, shapes):
    flat = packed.reshape(-1)
    out, off = [], 0
    for s in shapes:
        sz = math.prod(s)
        out.append(flat[off:off + sz].reshape(s))
        off += sz
    return out


def _shards_last(a):
    n = a.shape[-1] // N_DEV
    return jnp.moveaxis(a.reshape(a.shape[:-1] + (N_DEV, n)), -2, 0)


def _unshard_last(a):
    a = jnp.moveaxis(a, 0, -2)
    return a.reshape(a.shape[:-2] + (a.shape[-2] * a.shape[-1],))


BIG = ("ev_w_in", "ev_w_out", "od_w_in", "od_w_out", "ffn_w_gate", "ffn_w_up", "ffn_w_down")
COLUMN_MAJOR = ("ffn_w_gate", "ffn_w_up")
SMALL_SHARDED = ("lru_conv_w", "od_b_in", "sc_conv_w", "cf_conv_w", "cf_conv_b", "cf_ln_g", "cf_ln_b")
SMALL_REPL = ("ln_mix_g", "ln_ffn_g", "ln_final_g", "ev_b_in", "lru_conv_b", "lru_wa", "lru_ba", "lru_wx",
              "lru_bx", "lru_lambda", "hgrn_lb_logits", "hgrn_norm_g")
WEIGHTS = ("ln_mix_g", "ln_ffn_g", "ln_final_g", "ev_w_in", "ev_b_in", "lru_conv_w", "lru_conv_b", "lru_wa",
           "lru_ba", "lru_wx", "lru_bx", "lru_lambda", "hgrn_lb_logits", "hgrn_norm_g", "ev_w_out", "od_w_in",
           "od_b_in", "sc_conv_w", "cf_conv_w", "cf_conv_b", "cf_ln_g", "cf_ln_b", "od_w_out", "ffn_w_gate",
           "ffn_w_up", "ffn_w_down")


def _layer_weights(l):
    mix = ("ev_w_in", "ev_w_out") if l % 2 == 0 else ("od_w_in", "od_w_out")
    return [(mix[0], l // 2), (mix[1], l // 2), ("ffn_w_gate", l), ("ffn_w_up", l), ("ffn_w_down", l)]


class _MeshExchange:
    def __init__(self, w, m, v, depth, first):
        self.w, self.m, self.v, self.depth, self.first = w, m, v, depth, tuple(first)
        self.w_bf = {k: w[k].astype(MXU_DTYPE) for k, _ in self._names(0, "mix")}
        self.ready, self.flight, self.rs, self.adam = {}, {}, {}, {}

    @staticmethod
    def _names(l, grp):
        names = _layer_weights(l)
        return names[:2] if grp == "mix" else names[2:]

    def _own_start(self, l, grp, deps):
        names = self._names(l, grp)
        srcs = [self.w_bf[k] for k, _ in names]
        layers = [j for _, j in names]
        plan = _plan_gather_own(layers)
        sems, lands, tok = _split_start(f"ag_own_start_{grp}{l}", plan, srcs, _place_own(srcs, layers), deps)
        self.flight[l, grp] = (plan, sems, srcs, lands)
        return tok

    def _turn(self, l, grp, deps):
        plan, sems, srcs, lands = self.flight[l, grp]
        lands = _split_wait(f"ag_own_wait_{grp}{l}", plan, sems, srcs, lands, deps)
        plan = _plan_gather_pass(len(lands))
        sems, passed, tok = _split_start(f"ag_pass_start_{grp}{l}", plan, [], lands)
        self.flight[l, grp] = (plan, sems, [], passed)
        toks = [tok]
        nl, ng = (l, "ffn") if grp == "mix" else (l + 1, "mix")
        if nl < self.depth:
            toks.append(self._own_start(nl, ng, (tok,)))
        return tuple(toks)

    def _pass_wait(self, l, grp, deps):
        plan, sems, srcs, lands = self.flight.pop((l, grp))
        lands = _split_wait(f"ag_pass_wait_{grp}{l}", plan, sems, srcs, lands, deps)
        self.ready[l, grp] = dict(zip([k for k, _ in self._names(l, grp)], lands))

    def layer_begin(self, l):
        toks = ()
        if l == 0:
            started = self._own_start(0, "mix", self.first)
            for k in BIG:
                if k not in self.w_bf:
                    self.w_bf[k] = lax.optimization_barrier((self.w[k], started))[0].astype(MXU_DTYPE)
            toks = self._turn(0, "mix", tuple(self.w_bf.values()))
            self._pass_wait(0, "mix", ())
        return self.ready.pop((l, "mix")), toks

    def tick(self, l, t, after):
        if t == 2:
            return self._turn(l, "ffn", (after,))
        if t == 3:
            self._pass_wait(l, "ffn", (after,))
        if t == 4 and l + 1 < self.depth:
            return self._turn(l + 1, "mix", (after,))
        if t == 5 and l + 1 < self.depth:
            self._pass_wait(l + 1, "mix", (after,))
        return ()

    def ffn_weights(self, l):
        return self.ready.pop((l, "ffn"))

    def grads(self, tag, named):
        srcs = [g for _, _, g in named]
        lands = [lax.empty((4,) + g.shape[1:], g.dtype) for g in srcs]
        plan = _plan_scatter_pair(len(srcs))
        sems, lands, tok = _split_start(f"rs_pair_start_{tag}", plan, srcs, lands)
        self.rs[tag] = (named, plan, sems, srcs, lands)
        return (tok,)

    def grads_mid(self, tag, after):
        named, plan, sems, srcs, lands = self.rs[tag]
        got = _split_wait(f"rs_pair_wait_{tag}", plan, sems, srcs, lands, (after,))
        both = [_pair_add(f"pair_add_{tag}_{i}", a, b) for i, (a, b) in enumerate(zip(srcs, got))]
        self.rs[tag] = (named, [p for p, _ in both], [f for _, f in both])

    def grads_send(self, tag, deps=()):
        named, parts, fins = self.rs[tag]
        plan = _plan_scatter_chips(len(parts))
        sems, fins, tok = _split_start(f"rs_chip_start_{tag}", plan, parts, fins, tuple(deps))
        self.rs[tag] = (named, plan, sems, parts, fins)
        return (tok,)

    def grads_end(self, tag, after):
        named, plan, sems, parts, fins = self.rs.pop(tag)
        fins = _split_wait(f"rs_chip_wait_{tag}", plan, sems, parts, fins, (after,))
        done = []
        for (k, j, _), fin in zip(named, fins):
            view = (lambda a: a.transpose(0, 2, 1)) if k in COLUMN_MAJOR else (lambda a: a)
            *self.adam[k], token = _adamw(f"adamw_{k}_{j}", view(self.w[k]), view(self.m[k]), view(self.v[k]),
                                          fin, j, self.adam.get(k))
            done.append(token)
        return tuple(done)

    def finish(self, after):
        for tag in list(self.rs):
            self.grads_end(tag, after)

    def results(self, k):
        if k in COLUMN_MAJOR:
            L, r, c = self.w[k].shape
            return [a.reshape(L, c, r).transpose(0, 2, 1) for a in self.adam[k]]
        return [a.reshape(self.w[k].shape) for a in self.adam[k]]


def _local_step(x, tgt, p, ex):
    T, D = x.shape
    depth = p["ln_mix_g"].shape[0]
    lbs = _lower_bounds(p["hgrn_lb_logits"])
    row = lambda a: a.reshape(1, -1)
    saved = []
    h = None
    for l in range(depth):
        j = l // 2
        wl, tok = ex.layer_begin(l)
        s = {"x": x}
        if h is None:
            h = _rmsnorm_fwd(f"norm_mix{l}", x, row(p["ln_mix_g"][l]))
        s["h"] = h
        if l % 2 == 0:
            proj = _proj_in(f"ev_in{l}", h, wl["ev_w_in"], row(p["ev_b_in"][j]), tok)
            wa = p["lru_wa"][j].astype(MXU_DTYPE)
            wx = p["lru_wx"][j].astype(MXU_DTYPE)
            ya, xc, r, ig, hs = _lru_fwd(f"lru_fwd{l}", proj, p["lru_conv_w"][j], row(p["lru_conv_b"][j]),
                                         wa, row(p["lru_ba"][j]), wx, row(p["lru_bx"][j]),
                                         row(p["lru_lambda"][j]))
            y, o, sall = _hgrn_fwd(f"hgrn_fwd{l}", proj, row(lbs[j]), row(p["hgrn_norm_g"][j]), ya)
            s.update(proj=proj, xc=xc, r=r, ig=ig, hs=hs, o=o, sall=sall, wa=wa, wx=wx)
            w_out = wl["ev_w_out"]
        else:
            proj = _proj_in(f"od_in{l}", h, wl["od_w_in"], row(p["od_b_in"][j]), tok)
            y, dsave = _odd_fwd(f"odd_fwd{l}", proj, p["sc_conv_w"][j], p["cf_conv_w"][j],
                                row(p["cf_conv_b"][j]), row(p["cf_ln_g"][j]), row(p["cf_ln_b"][j]))
            s.update(proj=proj, dsave=dsave)
            w_out = wl["od_w_out"]
        x, h2 = _mix_out(f"mix_out{l}", y, w_out, x, row(p["ln_ffn_g"][l]), ex.tick(l, 2, y))
        s["y"] = y
        s["xmid"] = x
        ex.tick(l, 3, x)
        wl = {**wl, **ex.ffn_weights(l)}
        gate, up, hid = _ffn_in(f"ffn_in{l}", h2, wl["ffn_w_gate"], wl["ffn_w_up"])
        tok = ex.tick(l, 4, hid)
        if l + 1 < depth:
            x, h = _proj_out(f"ffn_out{l}", hid, wl["ffn_w_down"], x, tok, row(p["ln_mix_g"][l + 1]))
        else:
            x = _proj_out(f"ffn_out{l}", hid, wl["ffn_w_down"], x, tok)
        ex.tick(l, 5, x)
        s.update(h2=h2, gate=gate, up=up, hid=hid, w=wl)
        saved.append(s)

    loss, dx, dxb, dg_final = _loss_head("loss_head", x, row(p["ln_final_g"]), tgt)

    gs = {k: [None] * p[k].shape[0] for k in SMALL_REPL + SMALL_SHARDED if k not in ("ln_final_g", "hgrn_lb_logits")}
    d_lb = [None] * (depth // 2 + depth % 2)
    tok = ()
    pending = None
    for l in reversed(range(depth)):
        j = l // 2
        s = saved[l]
        wl = s["w"]
        ffn_shape = wl["ffn_w_gate"].shape[1:]
        dwd = _wgrad(f"ffn_dwd{l}", s["hid"], dxb, ffn_shape[1], ffn_shape[0])
        dgate, dup = _ffn_bwd_hidden(f"ffn_bwd_hid{l}", dxb, wl["ffn_w_down"], s["gate"], s["up"], tok)
        dwg = _wgrad(f"ffn_dwg{l}", dgate, s["h2"], ffn_shape[1], ffn_shape[0])
        dwu = _wgrad(f"ffn_dwu{l}", dup, s["h2"], ffn_shape[1], ffn_shape[0])
        tok = ex.grads(f"ffn{l}", [("ffn_w_down", l, dwd), ("ffn_w_gate", l, dwg), ("ffn_w_up", l, dwu)])
        dh2 = _bwd_in(f"ffn_dh{l}", [dgate, dup], [wl["ffn_w_gate"], wl["ffn_w_up"]], tok)
        dx, dxb, dg = _rmsnorm_bwd(f"norm_ffn_bwd{l}", s["xmid"], row(p["ln_ffn_g"][l]), dh2, dx)
        gs["ln_ffn_g"][l] = dg[0]
        ex.grads_mid(f"ffn{l}", dxb)
        done = ex.grads_end(pending, dxb) if pending is not None else ()
        tok = ex.grads_send(f"ffn{l}", done)
        w_in, w_out = ("ev_w_in", "ev_w_out") if l % 2 == 0 else ("od_w_in", "od_w_out")
        dwo = _wgrad(f"mix_dwo{l}", s["y"], dxb, *wl[w_out].shape[1:])
        if l % 2 == 0:
            dy = _bwd_out(f"ev_dy{l}", dxb, wl["ev_w_out"], tok)
            dph, db_h, dlb, dng = _hgrn_bwd(f"hgrn_bwd{l}", s["proj"], dy, s["o"], s["sall"], row(lbs[j]),
                                            row(p["hgrn_norm_g"][j]))
            dproj, db_a, d_cw, d_cb, d_wa, d_ba, d_wx, d_bx, d_lam = _lru_bwd(
                f"lru_bwd{l}", s["proj"], dy, s["xc"], s["r"], s["ig"], s["hs"], p["lru_conv_w"][j],
                s["wa"], s["wx"], row(p["lru_lambda"][j]), dph)
            gs["lru_conv_w"][j], gs["lru_conv_b"][j] = d_cw, d_cb[0]
            gs["lru_wa"][j], gs["lru_ba"][j] = d_wa, d_ba.reshape(p["lru_ba"].shape[1:])
            gs["lru_wx"][j], gs["lru_bx"][j] = d_wx, d_bx.reshape(p["lru_bx"].shape[1:])
            gs["lru_lambda"][j], gs["hgrn_norm_g"][j] = d_lam[0], dng[0]
            d_lb[j] = dlb[0]
            gs["ev_b_in"][j] = jnp.concatenate([db_a[0], db_h[0, db_a.shape[1]:]])
        else:
            dy = _bwd_out(f"od_dy{l}", dxb, wl["od_w_out"], tok)
            dproj, db_in, d_w3, d_w31, d_cfb, d_lg, d_lbeta = _odd_bwd(
                f"odd_bwd{l}", s["proj"], dy, s["dsave"], p["sc_conv_w"][j], p["cf_conv_w"][j],
                row(p["cf_ln_g"][j]), row(p["cf_ln_b"][j]))
            gs["sc_conv_w"][j], gs["cf_conv_w"][j] = d_w3, d_w31
            gs["cf_conv_b"][j], gs["cf_ln_g"][j], gs["cf_ln_b"][j] = d_cfb[0], d_lg[0], d_lbeta[0]
            gs["od_b_in"][j] = db_in[0]
        dwi = _wgrad(f"mix_dwi{l}", s["h"], dproj, *wl[w_in].shape[1:])
        tok = ex.grads(f"mix{l}", [(w_out, j, dwo), (w_in, j, dwi)])
        dh = _bwd_in(f"mix_dh{l}", [dproj], [wl[w_in]], tok)
        dx, dxb, dg = _rmsnorm_bwd(f"norm_mix_bwd{l}", s["x"], row(p["ln_mix_g"][l]), dh, dx)
        gs["ln_mix_g"][l] = dg[0]
        ex.grads_mid(f"mix{l}", dxb)
        if l > 0:
            tok = ex.grads_send(f"mix{l}", ex.grads_end(f"ffn{l}", dxb))
        else:
            tok = ex.grads_send(f"mix{l}")
            ex.grads_end(f"ffn{l}", (tok or (dxb,))[0])
        pending = f"mix{l}"

    small = {k: jnp.stack(v) for k, v in gs.items()}
    small["ln_final_g"] = dg_final[0]
    _, lb_vjp = jax.vjp(_lower_bounds, p["hgrn_lb_logits"])
    small["hgrn_lb_logits"] = lb_vjp(jnp.stack(d_lb))[0]
    if tok:
        small = lax.optimization_barrier((small, tok))[0]
    return loss, dx, small


def kernel(x, ln_mix_g, ln_ffn_g, ln_final_g, ev_w_in, ev_b_in, lru_conv_w, lru_conv_b, lru_wa, lru_ba, lru_wx, lru_bx, lru_lambda, hgrn_lb_logits, hgrn_norm_g, ev_w_out, od_w_in, od_b_in, sc_conv_w, cf_conv_w, cf_conv_b, cf_ln_g, cf_ln_b, od_w_out, ffn_w_gate, ffn_w_up, ffn_w_down, loss_target, m_ln_mix_g, m_ln_ffn_g, m_ln_final_g, m_ev_w_in, m_ev_b_in, m_lru_conv_w, m_lru_conv_b, m_lru_wa, m_lru_ba, m_lru_wx, m_lru_bx, m_lru_lambda, m_hgrn_lb_logits, m_hgrn_norm_g, m_ev_w_out, m_od_w_in, m_od_b_in, m_sc_conv_w, m_cf_conv_w, m_cf_conv_b, m_cf_ln_g, m_cf_ln_b, m_od_w_out, m_ffn_w_gate, m_ffn_w_up, m_ffn_w_down, v_ln_mix_g, v_ln_ffn_g, v_ln_final_g, v_ev_w_in, v_ev_b_in, v_lru_conv_w, v_lru_conv_b, v_lru_wa, v_lru_ba, v_lru_wx, v_lru_bx, v_lru_lambda, v_hgrn_lb_logits, v_hgrn_norm_g, v_ev_w_out, v_od_w_in, v_od_b_in, v_sc_conv_w, v_cf_conv_w, v_cf_conv_b, v_cf_ln_g, v_cf_ln_b, v_od_w_out, v_ffn_w_gate, v_ffn_w_up, v_ffn_w_down):
    args = locals()
    w = {k: args[k] for k in WEIGHTS}
    m = {k: args["m_" + k] for k in WEIGHTS}
    v = {k: args["v_" + k] for k in WEIGHTS}
    assert x.shape[0] == 1
    T, D = x.shape[1:]

    local_shapes = [w[k].shape for k in SMALL_SHARDED]
    gathered = _all_gather("gather_small_params", [_pack_rows([w[k] for k in SMALL_SHARDED])])[0]
    p = {k: w[k] for k in SMALL_REPL}
    per_dev = [_unpack_rows(gathered[s], local_shapes) for s in range(N_DEV)]
    for i, k in enumerate(SMALL_SHARDED):
        p[k] = _unshard_last(jnp.stack([per_dev[s][i] for s in range(N_DEV)]))

    ex = _MeshExchange(w, m, v, ln_mix_g.shape[0], [gathered])
    loss_part, dx, small = _local_step(x[0], loss_target[0], p, ex)
    loss = lax.psum(loss_part[0, 0], ("x", "y", "c"))

    small_sh = jnp.stack([_pack_rows([_shards_last(small[k])[s] for k in SMALL_SHARDED]) for s in range(N_DEV)])
    got = _pair_exchange("small_grads_to_sibling", [small_sh])[0]
    final_small = _chip_exchange("small_grads_to_chips", [_pair_add("pair_add_small", small_sh, got)[0]])[0]
    repl_parts = _all_gather("gather_small_grads", [_pack_rows([small[k] for k in SMALL_REPL])])[0]

    out_g, out_d, out_m, out_v = {}, {}, {}, {}
    res = _adamw("adamw_small_sharded", *[_pack_rows([t[k] for k in SMALL_SHARDED])[None] for t in (w, m, v)],
                 final_small)
    for o, r in zip((out_g, out_d, out_m, out_v), res):
        o.update(zip(SMALL_SHARDED, _unpack_rows(r, local_shapes)))
    res = _adamw("adamw_small_repl", *[_pack_rows([t[k] for k in SMALL_REPL])[None] for t in (w, m, v)], repl_parts)
    for o, r in zip((out_g, out_d, out_m, out_v), res):
        o.update(zip(SMALL_REPL, _unpack_rows(r, [w[k].shape for k in SMALL_REPL])))
    ex.finish(res[0])
    for k in BIG:
        out_g[k], out_d[k], out_m[k], out_v[k] = ex.results(k)

    return (loss, dx[None], *[out_g[k] for k in WEIGHTS], *[out_d[k] for k in WEIGHTS],
            *[out_m[k] for k in WEIGHTS], *[out_v[k] for k in WEIGHTS])
```
